```python
import jax, jax.numpy as jnp
from jax import lax
import numpy as np

D_MODEL = 1024
BATCH = 8
SEQ = 2048
DEPTH = 4
DEC_BATCH = 128
DEC_SEQ = 4
PAST_LEN = 16384
PAGE_SIZE = 128

RWKV_HEADS = 8
RWKV_HEAD_DIM = 64
RWKV_WIDTH = RWKV_HEADS * RWKV_HEAD_DIM
LORA_W = 64
LORA_A = 64
LORA_V = 32
LORA_G = 128
RWKV_SPLITS = (RWKV_WIDTH, 2 * RWKV_WIDTH, 3 * RWKV_WIDTH, 3 * RWKV_WIDTH + LORA_W, 3 * RWKV_WIDTH + LORA_W + LORA_A)
RWKV_COLS = 3 * RWKV_WIDTH + LORA_W + LORA_A + LORA_G
GN_EPS = RWKV_HEAD_DIM * 1e-5
CONV_WIDTH = 31
CONV_CH = 512
CONV_BUF = CONV_WIDTH - 1
N_MEM = 256
MEM_HEADS = 4
MEM_HEAD_DIM = 128
MEM_WIDTH = MEM_HEADS * MEM_HEAD_DIM
N_BRANCH = 3
OFF_GLU = RWKV_COLS
OFF_Q = OFF_GLU + 2 * CONV_CH
OFF_GATE = OFF_Q + MEM_WIDTH
IN_COLS = OFF_GATE + N_BRANCH * D_MODEL
N_EXPERTS = 128
TOP_K = 8
D_EXPERT = 256
D_SHARED = 256
ROUTED_SCALE = 2.5
MOE_BLOCK = 64
LN_EPS = 1e-5
ALPHA = (2 * DEPTH) ** 0.25
BETA = (8 * DEPTH) ** -0.25

kernel_name = 'rwkv7_conformer_memattn_moe_deepnorm_step'


def layer_norm(x, g, b, eps):
    xf = x.astype(jnp.float32)
    mu = jnp.mean(xf, axis=-1, keepdims=True)
    var = jnp.mean(jnp.square(xf - mu), axis=-1, keepdims=True)
    y = (xf - mu) * lax.rsqrt(var + eps) * g.astype(jnp.float32) + b.astype(jnp.float32)
    return y.astype(x.dtype)


def wkv7_scan(S0, r, decay, k, v, kk, a):
    def step(S, inp):
        r_t, w_t, k_t, v_t, kk_t, a_t = inp
        sa = jnp.einsum('bhvk,bhk->bhv', S, -kk_t)
        S = (S * w_t[:, :, None, :] + sa[..., None] * (kk_t * a_t)[:, :, None, :]
             + v_t[..., None] * k_t[:, :, None, :])
        return S, jnp.einsum('bhvk,bhk->bhv', S, r_t)
    xs = tuple(jnp.swapaxes(z, 0, 1) for z in (r, decay, k, v, kk, a))
    S, y = lax.scan(step, S0, xs)
    return S, jnp.swapaxes(y, 0, 1)


def rwkv7_branch(p_rwkv, p_vd, v_first, S0, prm, layer_idx):
    B, T, _ = p_rwkv.shape
    r, k, v, wd, ad, gd = jnp.split(p_rwkv, list(RWKV_SPLITS), axis=-1)
    log_w = -jax.nn.softplus(-(prm['w0'] + jnp.tanh(wd) @ prm['w_up'])) - 0.5
    decay = jnp.exp(-jnp.exp(log_w.astype(jnp.float32)))
    a = jax.nn.sigmoid(prm['a0'] + ad @ prm['a_up'])
    g = jax.nn.sigmoid(gd) @ prm['g_up']
    if layer_idx == 0:
        v_first = v
    else:
        v = v + (v_first - v) * jax.nn.sigmoid(prm['v0'] + p_vd @ prm['v_up'])

    def heads(z):
        return z.reshape(B, T, RWKV_HEADS, RWKV_HEAD_DIM).astype(jnp.float32)

    kk = heads(k * prm['k_k'])
    kk = kk / jnp.maximum(jnp.linalg.norm(kk, axis=-1, keepdims=True), 1e-12)
    k = k * (1.0 + (a - 1.0) * prm['k_a'])
    rh, kh, vh, ah = heads(r), heads(k), heads(v), heads(a)
    S, y = wkv7_scan(S0, rh, heads(decay), kh, vh, kk, ah)
    mu = jnp.mean(y, axis=-1, keepdims=True)
    var = jnp.mean(jnp.square(y - mu), axis=-1, keepdims=True)
    y = ((y - mu) * lax.rsqrt(var + GN_EPS)).reshape(B, T, RWKV_WIDTH)
    y = y * prm['lnx_g'].astype(jnp.float32) + prm['lnx_b'].astype(jnp.float32)
    bonus = (jnp.sum(rh * kh * prm['r_k'].astype(jnp.float32), axis=-1, keepdims=True) * vh).reshape(B, T, RWKV_WIDTH)
    out = ((y + bonus).astype(g.dtype) * g) @ prm['w_o_a']
    return out, S, v_first


def conv_module(p_glu, conv_buf, prm):
    glu = p_glu[..., :CONV_CH] * jax.nn.sigmoid(p_glu[..., CONV_CH:])
    ext = jnp.concatenate([conv_buf.astype(glu.dtype), glu], axis=1)
    h = lax.conv_general_dilated(ext, prm['conv_w'].astype(ext.dtype)[:, None, :],
                                 window_strides=(1,), padding='VALID',
                                 dimension_numbers=('NWC', 'WIO', 'NWC'),
                                 feature_group_count=CONV_CH)
    h = jax.nn.silu(layer_norm(h + prm['conv_b'], prm['cn_g'], prm['cn_b'], LN_EPS))
    return h @ prm['w_o_b'], ext[:, -CONV_BUF:]


def memory_attention(q, mem_k, mem_v, w_o_c):
    B, T, _ = q.shape
    q = q.reshape(B, T, MEM_HEADS, MEM_HEAD_DIM)
    s = jnp.einsum('bthd,bmhd->bhtm', q, mem_k.astype(q.dtype)).astype(jnp.float32) * (MEM_HEAD_DIM ** -0.5)
    pr = jax.nn.softmax(s, axis=-1).astype(q.dtype)
    o = jnp.einsum('bhtm,bmhd->bthd', pr, mem_v.astype(q.dtype)).reshape(B, T, MEM_WIDTH)
    return o @ w_o_c


def moe_ffn(x, prm):
    B, T, D = x.shape
    n_tok = B * T
    xt = x.reshape(n_tok, D)
    scores = jax.nn.sigmoid((xt @ prm['w_router']).astype(jnp.float32))
    _, idx = lax.top_k(scores + prm['b_router'].astype(jnp.float32), TOP_K)
    sel = jnp.take_along_axis(scores, idx, axis=-1)
    gate = sel / jnp.sum(sel, axis=-1, keepdims=True) * ROUTED_SCALE
    n_assign = n_tok * TOP_K
    flat_e = idx.reshape(-1)
    order = jnp.argsort(flat_e)
    e_sorted = flat_e[order]
    counts = jnp.zeros((N_EXPERTS,), jnp.int32).at[flat_e].add(1)
    padded = (counts + MOE_BLOCK - 1) // MOE_BLOCK * MOE_BLOCK
    start_sorted = jnp.cumsum(counts) - counts
    end_padded = jnp.cumsum(padded)
    start_padded = end_padded - padded
    dest = start_padded[e_sorted] + jnp.arange(n_assign, dtype=jnp.int32) - start_sorted[e_sorted]
    n_blocks = (n_assign + MOE_BLOCK - 1) // MOE_BLOCK + N_EXPERTS
    n_slots = n_blocks * MOE_BLOCK
    slot_tok = jnp.zeros((n_slots,), jnp.int32).at[dest].set((order // TOP_K).astype(jnp.int32))
    slot_w = jnp.zeros((n_slots,), jnp.float32).at[dest].set(gate.reshape(-1)[order])
    block_e = jnp.minimum(jnp.searchsorted(end_padded, jnp.arange(n_blocks, dtype=jnp.int32) * MOE_BLOCK, side='right'), N_EXPERTS - 1)

    def block_step(acc, blk):
        tok, w, e = blk
        xb = xt[tok]
        h = jax.nn.silu(xb @ prm['w_e1'][e]) * (xb @ prm['w_e3'][e])
        yb = (h @ prm['w_e2'][e]).astype(jnp.float32) * w[:, None]
        return acc.at[tok].add(yb), None

    routed, _ = lax.scan(block_step, jnp.zeros((n_tok, D), jnp.float32),
                         (slot_tok.reshape(n_blocks, MOE_BLOCK), slot_w.reshape(n_blocks, MOE_BLOCK), block_e))
    shared = (jax.nn.silu(xt @ prm['w_s1']) * (xt @ prm['w_s3'])) @ prm['w_s2']
    return (shared.astype(jnp.float32) + routed).astype(x.dtype).reshape(B, T, D)


def trunk_layer(x, mem_k, mem_v, S0, x_prev, conv_buf, v_first, prm, layer_idx):
    B, T, D = x.shape
    w_in = prm['w_in']
    if layer_idx > 0:
        w_in = jnp.concatenate([w_in, prm['w_in_vmix']], axis=1)
    proj = x @ w_in
    prev = x_prev.astype(x.dtype) @ w_in

    def shift_lerp(cur, before, mu):
        shifted = jnp.concatenate([before[:, None], cur[:, :-1]], axis=1)
        return cur + (shifted - cur) * mu

    p_rwkv = shift_lerp(proj[..., :RWKV_COLS], prev[:, :RWKV_COLS], prm['mu_shift'])
    p_vd = shift_lerp(proj[..., IN_COLS:], prev[:, IN_COLS:], prm['mu_vmix']) if layer_idx > 0 else None
    y_a, S_new, v_first = rwkv7_branch(p_rwkv, p_vd, v_first, S0, prm, layer_idx)
    y_b, conv_new = conv_module(proj[..., OFF_GLU:OFF_Q], conv_buf, prm)
    y_c = memory_attention(proj[..., OFF_Q:OFF_GATE], mem_k, mem_v, prm['w_o_c'])
    gates = jax.nn.sigmoid(proj[..., OFF_GATE:IN_COLS]).reshape(B, T, N_BRANCH, D)
    merged = gates[:, :, 0] * y_a + gates[:, :, 1] * y_b + gates[:, :, 2] * y_c
    h = layer_norm(ALPHA * x + merged @ prm['w_out'], prm['ln1_g'], prm['ln1_b'], LN_EPS)
    y = layer_norm(ALPHA * h + moe_ffn(h, prm), prm['ln2_g'], prm['ln2_b'], LN_EPS)
    return y, S_new, x[:, -1], conv_new, v_first


def setup_inputs(seed: int = 0) -> dict:
    key = jax.random.key(seed)
    keys = jax.random.split(key, 48)
    cnt = [0]

    def nxt():
        k = keys[cnt[0]]
        cnt[0] += 1
        return k

    def nrm(shape, scale):
        return scale * jax.random.normal(nxt(), shape, jnp.float32)

    def uni(shape, lo, hi):
        return jax.random.uniform(nxt(), shape, jnp.float32, lo, hi)

    L, Lm, D = DEPTH, DEPTH - 1, D_MODEL
    return {
        'x_prompt': nrm((BATCH, SEQ, D), 1.0),
        'x_sample': nrm((DEC_BATCH, DEC_SEQ, D), 1.0),
        'state_wkv': nrm((L, DEC_BATCH, RWKV_HEADS, RWKV_HEAD_DIM, RWKV_HEAD_DIM), 0.1),
        'state_shift': nrm((L, DEC_BATCH, D), 1.0),
        'state_conv': nrm((L, DEC_BATCH, CONV_BUF, CONV_CH), 0.5),
        'cache_mem_k': nrm((L, DEC_BATCH, N_MEM, MEM_HEADS, MEM_HEAD_DIM), 1.0),
        'cache_mem_v': nrm((L, DEC_BATCH, N_MEM, MEM_HEADS, MEM_HEAD_DIM), 1.0),
        'mem_prompt': nrm((BATCH, N_MEM, D), 1.0),
        'w_in': nrm((L, D, IN_COLS), D ** -0.5),
        'w_in_vmix': nrm((Lm, D, LORA_V), D ** -0.5),
        'mu_shift': uni((L, RWKV_COLS), 0.0, 1.0),
        'mu_vmix': uni((Lm, LORA_V), 0.0, 1.0),
        'w0': uni((L, RWKV_WIDTH), -6.0, -1.0),
        'w_up': nrm((L, LORA_W, RWKV_WIDTH), 0.1),
        'a0': nrm((L, RWKV_WIDTH), 0.1),
        'a_up': nrm((L, LORA_A, RWKV_WIDTH), 0.1),
        'v0': nrm((Lm, RWKV_WIDTH), 0.5),
        'v_up': nrm((Lm, LORA_V, RWKV_WIDTH), 0.1),
        'g_up': nrm((L, LORA_G, RWKV_WIDTH), LORA_G ** -0.5),
        'k_k': 0.85 + nrm((L, RWKV_WIDTH), 0.05),
        'k_a': 1.0 + nrm((L, RWKV_WIDTH), 0.05),
        'r_k': nrm((L, RWKV_HEADS, RWKV_HEAD_DIM), 0.1),
        'lnx_g': 1.0 + nrm((L, RWKV_WIDTH), 0.05),
        'lnx_b': nrm((L, RWKV_WIDTH), 0.02),
        'w_o_a': nrm((L, RWKV_WIDTH, D), RWKV_WIDTH ** -0.5),
        'conv_w': nrm((L, CONV_WIDTH, CONV_CH), CONV_WIDTH ** -0.5),
        'conv_b': nrm((L, CONV_CH), 0.02),
        'cn_g': 1.0 + nrm((L, CONV_CH), 0.05),
        'cn_b': nrm((L, CONV_CH), 0.02),
        'w_o_b': nrm((L, CONV_CH, D), CONV_CH ** -0.5),
        'w_mem_kv': nrm((L, D, 2 * MEM_WIDTH), D ** -0.5),
        'w_o_c': nrm((L, MEM_WIDTH, D), MEM_WIDTH ** -0.5),
        'w_out': nrm((L, D, D), BETA * D ** -0.5),
        'ln1_g': 1.0 + nrm((L, D), 0.05),
        'ln1_b': nrm((L, D), 0.02),
        'ln2_g': 1.0 + nrm((L, D), 0.05),
        'ln2_b': nrm((L, D), 0.02),
        'w_router': nrm((L, D, N_EXPERTS), D ** -0.5),
        'b_router': nrm((L, N_EXPERTS), 0.01),
        'w_e1': nrm((L, N_EXPERTS, D, D_EXPERT), D ** -0.5),
        'w_e3': nrm((L, N_EXPERTS, D, D_EXPERT), D ** -0.5),
        'w_e2': nrm((L, N_EXPERTS, D_EXPERT, D), BETA * D_EXPERT ** -0.5),
        'w_s1': nrm((L, D, D_SHARED), D ** -0.5),
        'w_s3': nrm((L, D, D_SHARED), D ** -0.5),
        'w_s2': nrm((L, D_SHARED, D), BETA * D_SHARED ** -0.5),
    }


def reference(x_prompt, x_sample, state_wkv, state_shift, state_conv, cache_mem_k, cache_mem_v, mem_prompt,
              w_in, w_in_vmix, mu_shift, mu_vmix, w0, w_up, a0, a_up, v0, v_up, g_up, k_k, k_a, r_k,
              lnx_g, lnx_b, w_o_a, conv_w, conv_b, cn_g, cn_b, w_o_b, w_mem_kv, w_o_c, w_out,
              ln1_g, ln1_b, ln2_g, ln2_b, w_router, b_router, w_e1, w_e3, w_e2, w_s1, w_s3, w_s2):
    Bp, Bs = x_prompt.shape[0], x_sample.shape[0]
    n_mem = mem_prompt.shape[1]
    xp, xs = x_prompt, x_sample
    vf_p, vf_s = None, None
    wkv_p, shift_p, conv_p, mk_list, mv_list = [], [], [], [], []
    wkv_s, shift_s, conv_s = [], [], []
    for l in range(DEPTH):
        prm = {'w_in': w_in[l], 'mu_shift': mu_shift[l], 'w0': w0[l], 'w_up': w_up[l], 'a0': a0[l],
               'a_up': a_up[l], 'g_up': g_up[l], 'k_k': k_k[l], 'k_a': k_a[l], 'r_k': r_k[l],
               'lnx_g': lnx_g[l], 'lnx_b': lnx_b[l], 'w_o_a': w_o_a[l], 'conv_w': conv_w[l],
               'conv_b': conv_b[l], 'cn_g': cn_g[l], 'cn_b': cn_b[l], 'w_o_b': w_o_b[l],
               'w_o_c': w_o_c[l], 'w_out': w_out[l], 'ln1_g': ln1_g[l], 'ln1_b': ln1_b[l],
               'ln2_g': ln2_g[l], 'ln2_b': ln2_b[l], 'w_router': w_router[l], 'b_router': b_router[l],
               'w_e1': w_e1[l], 'w_e3': w_e3[l], 'w_e2': w_e2[l],
               'w_s1': w_s1[l], 'w_s3': w_s3[l], 'w_s2': w_s2[l]}
        if l > 0:
            prm['w_in_vmix'] = w_in_vmix[l - 1]
            prm['mu_vmix'] = mu_vmix[l - 1]
            prm['v0'] = v0[l - 1]
            prm['v_up'] = v_up[l - 1]
        kv = mem_prompt.astype(xp.dtype) @ w_mem_kv[l]
        mk_p = kv[..., :MEM_WIDTH].reshape(Bp, n_mem, MEM_HEADS, MEM_HEAD_DIM)
        mv_p = kv[..., MEM_WIDTH:].reshape(Bp, n_mem, MEM_HEADS, MEM_HEAD_DIM)
        S0_p = jnp.zeros((Bp, RWKV_HEADS, RWKV_HEAD_DIM, RWKV_HEAD_DIM), jnp.float32)
        prev_p = jnp.zeros((Bp, D_MODEL), xp.dtype)
        buf_p = jnp.zeros((Bp, CONV_BUF, CONV_CH), xp.dtype)
        xp, S_p, sh_p, cb_p, vf_p = trunk_layer(xp, mk_p, mv_p, S0_p, prev_p, buf_p, vf_p, prm, l)
        xs, S_s, sh_s, cb_s, vf_s = trunk_layer(xs, cache_mem_k[l], cache_mem_v[l], state_wkv[l].astype(jnp.float32),
                                                state_shift[l], state_conv[l], vf_s, prm, l)
        wkv_p.append(S_p.astype(state_wkv.dtype))
        shift_p.append(sh_p.astype(state_shift.dtype))
        conv_p.append(cb_p.astype(state_conv.dtype))
        mk_list.append(mk_p.astype(cache_mem_k.dtype))
        mv_list.append(mv_p.astype(cache_mem_v.dtype))
        wkv_s.append(S_s.astype(state_wkv.dtype))
        shift_s.append(sh_s.astype(state_shift.dtype))
        conv_s.append(cb_s.astype(state_conv.dtype))
    return (xp, xs, jnp.stack(wkv_p), jnp.stack(shift_p), jnp.stack(conv_p), jnp.stack(mk_list), jnp.stack(mv_list),
            jnp.stack(wkv_s), jnp.stack(shift_s), jnp.stack(conv_s))
```

```python
import functools
import math

import jax
import jax.numpy as jnp
from jax import lax
from jax.experimental import pallas as pl
from jax.experimental.pallas import tpu as pltpu

F32 = jnp.float32
BF16 = jnp.bfloat16

HEADS = 8
HEAD_DIM = 64
RW = HEADS * HEAD_DIM
LORA_W = 64
LORA_A = 64
LORA_V = 32
LORA_G = 128
RWKV_COLS = 3 * RW + LORA_W + LORA_A + LORA_G
GN_EPS = HEAD_DIM * 1e-5
CONV_WIDTH = 31
CONV_BUF = CONV_WIDTH - 1
CONV_CH = 512
MEM_HEADS = 4
MEM_HEAD_DIM = 128
MEM_WIDTH = MEM_HEADS * MEM_HEAD_DIM
TOP_K = 8
ROUTED_SCALE = 2.5
LN_EPS = 1e-5
EXP_M05 = math.exp(-0.5)

LANES = 128
ROW_TILE = 512
MERGE_TILE = 256
CHUNK = 64
NEUMANN_BLOCK = 16
MOE_BLOCK = 256
DISPATCH_TILE = 256
COMBINE_TILE = 128
VMEM_LIMIT = 48 * 1024 * 1024


def _cparams(*sem, vmem=VMEM_LIMIT):
    return pltpu.CompilerParams(dimension_semantics=tuple(sem), vmem_limit_bytes=vmem)


def _bdot(a, b):
    return jnp.dot(a.astype(BF16), b.astype(BF16), preferred_element_type=F32)


def _bdot_nt(a, b):
    return lax.dot_general(a.astype(BF16), b.astype(BF16), (((1,), (1,)), ((), ())),
                           preferred_element_type=F32)


def _sigmoid(x):
    return 1.0 / (1.0 + jnp.exp(-x))


def _silu(x):
    return x * _sigmoid(x)


def _layer_norm(x, g, b):
    mu = jnp.mean(x, axis=-1, keepdims=True)
    d = x - mu
    var = jnp.mean(d * d, axis=-1, keepdims=True)
    return d * lax.rsqrt(var + LN_EPS) * g + b


def _seg_sum(x):
    ax = x.ndim - 1
    n = x.shape[ax]
    lane = lax.broadcasted_iota(jnp.int32, x.shape, ax)
    s = 1
    while s < HEAD_DIM:
        up = pltpu.roll(x, s, axis=ax)
        dn = pltpu.roll(x, n - s, axis=ax)
        x = x + jnp.where((lane & s) != 0, up, dn)
        s *= 2
    return x


def _mm_kernel(x_ref, w_ref, o_ref):
    o_ref[...] = _bdot(x_ref[...], w_ref[...]).astype(o_ref.dtype)


def _matmul(x, w, tm, tn):
    m, k = x.shape
    n_cols = w.shape[1]
    assert m % tm == 0 and n_cols % tn == 0
    return pl.pallas_call(
        _mm_kernel,
        grid=(m // tm, n_cols // tn),
        in_specs=[pl.BlockSpec((tm, k), lambda i, j: (i, 0)),
                  pl.BlockSpec((k, tn), lambda i, j: (0, j))],
        out_specs=pl.BlockSpec((tm, tn), lambda i, j: (i, j)),
        out_shape=jax.ShapeDtypeStruct((m, n_cols), F32),
        compiler_params=_cparams("parallel", "parallel"),
    )(x, w)


def _pre_kernel(first, npt, tps, *refs):
    if first:
        (p_ref, halo_ref, prev_ref, mu_ref,
         w0_ref, wup_ref, a0_ref, aup_ref, gup_ref, kk_ref, ka_ref, rk_ref,
         r_o, lw_o, k_o, v_o, nkk_o, bb_o, g_o, bon_o, sh_ref) = refs
    else:
        (p_ref, halo_ref, prev_ref, mu_ref, pv_ref, halov_ref, prevv_ref, muv_ref, vf_ref, v0_ref, vup_ref,
         w0_ref, wup_ref, a0_ref, aup_ref, gup_ref, kk_ref, ka_ref, rk_ref,
         r_o, lw_o, k_o, v_o, nkk_o, bb_o, g_o, bon_o, sh_ref, shv_ref) = refs
    i = pl.program_id(0)
    rows_n = p_ref.shape[0]

    def build_shift(src_ref, halo_r, prev_r, dst_ref):
        @pl.when(i < npt)
        def _():
            x = src_ref[...]
            rolled = pltpu.roll(x, 1, axis=0)
            last = halo_r[7:8, :]
            row0 = jnp.where((i % tps) == 0, jnp.zeros_like(last), last)
            rows = lax.broadcasted_iota(jnp.int32, x.shape, 0)
            dst_ref[...] = jnp.where(rows == 0, row0, rolled)

        @pl.when(i >= npt)
        def _():
            nb = prev_r.shape[0]
            dst_ref[0:nb, :] = prev_r[...]
            dst_ref[nb:rows_n, :] = src_ref[0:rows_n - nb, :]

    build_shift(p_ref, halo_ref, prev_ref, sh_ref)
    p = p_ref[...]
    p = p + (sh_ref[...] - p) * mu_ref[...]
    r = p[:, 0:RW]
    k = p[:, RW:2 * RW]
    v = p[:, 2 * RW:3 * RW]
    u = p[:, 3 * RW:3 * RW + LORA_W + LORA_A]
    gd = p[:, 3 * RW + LORA_W + LORA_A:RWKV_COLS]

    z = w0_ref[...] + _bdot(jnp.tanh(u), wup_ref[...])
    lw = -EXP_M05 * _sigmoid(z)
    a = _sigmoid(a0_ref[...] + _bdot(u, aup_ref[...]))
    g = _bdot(_sigmoid(gd), gup_ref[...])
    if not first:
        build_shift(pv_ref, halov_ref, prevv_ref, shv_ref)
        pv = pv_ref[...]
        pv = pv + (shv_ref[...] - pv) * muv_ref[...]
        mix = _sigmoid(v0_ref[...] + _bdot(pv, vup_ref[...]))
        v = v + (vf_ref[...] - v) * mix
    kk = k * kk_ref[...]
    nrm = jnp.sqrt(_seg_sum(kk * kk))
    kkn = kk / jnp.maximum(nrm, 1e-12)
    k2 = k * (1.0 + (a - 1.0) * ka_ref[...])
    bonus = _seg_sum(r * k2 * rk_ref[...]) * v
    r_o[...] = r
    lw_o[...] = lw
    k_o[...] = k2
    v_o[...] = v
    nkk_o[...] = -kkn
    bb_o[...] = kkn * a
    g_o[...] = g
    bon_o[...] = bonus


def _rwkv_pre(first, npt, tps, p_rwkv, prev_rwkv, mu, lora, p_vd=None, prev_vd=None, mu_v=None, v_first=None,
              v0=None, vup=None):
    n = p_rwkv.shape[0]
    R = ROW_TILE
    nt = n // R

    def row_spec(w):
        return pl.BlockSpec((R, w), lambda i: (i, 0))

    def halo_spec(w):
        return pl.BlockSpec((8, w), lambda i: (jnp.maximum(i * (R // 8) - 1, 0), 0))

    def full(a):
        return pl.BlockSpec(a.shape, lambda i: (0,) * a.ndim)

    ins = [p_rwkv, p_rwkv, prev_rwkv, mu]
    specs = [row_spec(RWKV_COLS), halo_spec(RWKV_COLS), full(prev_rwkv), full(mu)]
    scratch = [pltpu.VMEM((R, RWKV_COLS), F32)]
    if not first:
        ins += [p_vd, p_vd, prev_vd, mu_v, v_first, v0, vup]
        specs += [row_spec(LANES), halo_spec(LANES), full(prev_vd), full(mu_v), row_spec(RW), full(v0), full(vup)]
        scratch.append(pltpu.VMEM((R, LANES), F32))
    ins += list(lora)
    specs += [full(a) for a in lora]
    outs = pl.pallas_call(
        functools.partial(_pre_kernel, first, npt, tps),
        grid=(nt,),
        in_specs=specs,
        out_specs=[row_spec(RW)] * 8,
        out_shape=[jax.ShapeDtypeStruct((n, RW), F32)] * 8,
        scratch_shapes=scratch,
        compiler_params=_cparams("parallel"),
    )(*ins)
    return outs


def _wkv_chunk_kernel(r_ref, lw_ref, k_ref, v_ref, nkk_ref, bb_ref, y_ref, s_out_ref, s_ref):
    c = pl.program_id(1)
    C = r_ref.shape[0]
    G = 4 * HEAD_DIM
    assert 4 * C == G

    @pl.when(c == 0)
    def _():
        s_ref[...] = jnp.zeros_like(s_ref)

    lw = lw_ref[...]
    rows = lax.broadcasted_iota(jnp.int32, lw.shape, 0)
    cs = lw
    s = 1
    while s < C:
        cs = cs + jnp.where(rows >= s, pltpu.roll(cs, s, axis=0), 0.0)
        s *= 2
    cs_last = cs[C - 1:C, :]
    e_incl = jnp.exp(cs)
    e_excl = jnp.exp(cs - lw)
    e_neg = jnp.exp(-cs)
    e_tail = jnp.exp(cs_last - cs)
    kk_ = k_ref[...]
    bb_ = bb_ref[...]
    at = nkk_ref[...] * e_excl
    rt = r_ref[...] * e_incl
    bt = bb_ * e_neg
    kt = kk_ * e_neg
    bw = bb_ * e_tail
    kw = kk_ * e_tail
    vv = v_ref[...]
    w_end = jnp.exp(cs_last)

    ri = lax.broadcasted_iota(jnp.int32, (G, G), 0)
    ci = lax.broadcasted_iota(jnp.int32, (G, G), 1)
    same_head = (ri // C) == (ci // HEAD_DIM)
    tril_s = same_head & ((ci % C) < (ri % C))
    tril_i = same_head & ((ci % C) <= (ri % C))
    diag_blk = (ri // NEUMANN_BLOCK) == (ci // NEUMANN_BLOCK)
    eye = (ri == ci).astype(F32)

    def stack(x):
        return jnp.where(same_head, jnp.concatenate([x, x, x, x], axis=0), 0.0)

    def collapse(x4):
        return x4[0:C] + x4[C:2 * C] + x4[2 * C:3 * C] + x4[3 * C:4 * C]

    for g in range(RW // G):
        sl = slice(g * G, (g + 1) * G)
        at4, rt4, v4 = stack(at[:, sl]), stack(rt[:, sl]), stack(vv[:, sl])
        bt4 = jnp.concatenate([bt[:, sl]] * 4, axis=0)
        kt4 = jnp.concatenate([kt[:, sl]] * 4, axis=0)
        a_ab = jnp.where(tril_s, _bdot_nt(at4, bt4), 0.0)
        a_ak = jnp.where(tril_s, _bdot_nt(at4, kt4), 0.0)
        a_rb = jnp.where(tril_i, _bdot_nt(rt4, bt4), 0.0)
        a_rk = jnp.where(tril_i, _bdot_nt(rt4, kt4), 0.0)
        a_d = jnp.where(diag_blk, a_ab, 0.0)
        a_o = a_ab - a_d
        x2 = _bdot(a_d, a_d)
        x4 = _bdot(x2, x2)
        x8 = _bdot(x4, x4)
        t_d = eye + a_d
        t_d = t_d + _bdot(t_d, x2)
        t_d = t_d + _bdot(t_d, x4)
        t_d = t_d + _bdot(t_d, x8)
        nn = _bdot(t_d, a_o)
        n2 = _bdot(nn, nn)
        m1 = t_d + _bdot(nn, t_d)
        t_m = m1 + _bdot(n2, m1)
        p4 = _bdot(t_m, at4)
        q4 = _bdot(t_m, _bdot(a_ak, v4))
        s4 = s_ref[g]
        u4 = _bdot_nt(p4, s4) + q4
        y4 = _bdot_nt(rt4, s4) + _bdot(a_rb, u4) + _bdot(a_rk, v4)
        y_ref[:, sl] = collapse(y4)
        u_c = collapse(u4)
        upd = _bdot(u_c.T, bw[:, sl]) + _bdot(vv[:, sl].T, kw[:, sl])
        s_new = s4 * w_end[:, sl] + jnp.where((ri // HEAD_DIM) == (ci // HEAD_DIM), upd, 0.0)
        s_ref[g] = s_new

    @pl.when(c == pl.num_programs(1) - 1)
    def _():
        s_out_ref[0] = s_ref[...]


def _wkv_prompt(B, T, r, lw, k, v, nkk, bb):
    C = CHUNK
    nc = T // C
    G = 4 * HEAD_DIM
    spec = pl.BlockSpec((C, RW), lambda b, c: (b * nc + c, 0))
    y, s = pl.pallas_call(
        _wkv_chunk_kernel,
        grid=(B, nc),
        in_specs=[spec] * 6,
        out_specs=[spec, pl.BlockSpec((1, RW // G, G, G), lambda b, c: (b, 0, 0, 0))],
        out_shape=[jax.ShapeDtypeStruct((B * T, RW), F32),
                   jax.ShapeDtypeStruct((B, RW // G, G, G), F32)],
        scratch_shapes=[pltpu.VMEM((RW // G, G, G), F32)],
        compiler_params=_cparams("parallel", "arbitrary"),
    )(r, lw, k, v, nkk, bb)
    s = s.reshape(B, RW // G, 4, HEAD_DIM, 4, HEAD_DIM)
    idx = jnp.arange(4)
    s = s[:, :, idx, :, idx, :]
    s = jnp.transpose(s, (1, 2, 0, 3, 4)).reshape(B, HEADS, HEAD_DIM, HEAD_DIM)
    return y, s


def _wkv_seq_kernel(s0_ref, r_ref, lw_ref, k_ref, v_ref, nkk_ref, bb_ref, y_ref, s_ref):
    steps = r_ref.shape[0]

    def body(vi, carry):
        st = s0_ref[0, vi]
        for t in range(steps):
            sa = jnp.sum(st * nkk_ref[t, 0], axis=0, keepdims=True)
            st = st * jnp.exp(lw_ref[t, 0]) + sa * bb_ref[t, 0] + v_ref[t, 0, vi] * k_ref[t, 0]
            y_ref[t, 0, vi] = jnp.sum(st * r_ref[t, 0], axis=0, keepdims=True)
        s_ref[0, vi] = st
        return carry

    lax.fori_loop(0, HEAD_DIM, body, 0)


def _wkv_sample(s0, r, lw, k, v, nkk, bb):
    steps, _, _, nb = r.shape
    vec = pl.BlockSpec((steps, 1, HEAD_DIM, nb), lambda h: (0, h, 0, 0))
    vcol = pl.BlockSpec((steps, 1, HEAD_DIM, 1, nb), lambda h: (0, h, 0, 0, 0))
    st = pl.BlockSpec((1, HEAD_DIM, HEAD_DIM, nb), lambda h: (h, 0, 0, 0))
    y, s = pl.pallas_call(
        _wkv_seq_kernel,
        grid=(HEADS,),
        in_specs=[st, vec, vec, vec, vcol, vec, vec],
        out_specs=[vcol, st],
        out_shape=[jax.ShapeDtypeStruct((steps, HEADS, HEAD_DIM, 1, nb), F32),
                   jax.ShapeDtypeStruct(s0.shape, F32)],
        compiler_params=_cparams("parallel"),
    )(s0, r, lw, k, v, nkk, bb)
    return y, s


def _post_kernel(npt, yp_ref, ys_ref, bon_ref, g_ref, lg_ref, lb_ref, o_ref):
    i = pl.program_id(0)
    y = jnp.where(i < npt, yp_ref[...], ys_ref[...])
    mu = _seg_sum(y) * (1.0 / HEAD_DIM)
    d = y - mu
    var = _seg_sum(d * d) * (1.0 / HEAD_DIM)
    yn = d * lax.rsqrt(var + GN_EPS) * lg_ref[...] + lb_ref[...]
    o_ref[...] = (yn + bon_ref[...]) * g_ref[...]


def _rwkv_post(npt, y_p, y_s, bonus, g, lnx_g, lnx_b):
    n = bonus.shape[0]
    R = ROW_TILE
    row = pl.BlockSpec((R, RW), lambda i: (i, 0))
    vec = pl.BlockSpec((1, RW), lambda i: (0, 0))
    return pl.pallas_call(
        functools.partial(_post_kernel, npt),
        grid=(n // R,),
        in_specs=[pl.BlockSpec((R, RW), lambda i: (jnp.minimum(i, npt - 1), 0)),
                  pl.BlockSpec((R, RW), lambda i: (0, 0)), row, row, vec, vec],
        out_specs=row,
        out_shape=jax.ShapeDtypeStruct((n, RW), F32),
        compiler_params=_cparams("parallel"),
    )(y_p, y_s, bonus, g, lnx_g, lnx_b)


def _glu(p):
    return p[..., :CONV_CH] * _sigmoid(p[..., CONV_CH:])


def _conv_prompt_kernel(p_ref, halo_ref, w_ref, cb_ref, g_ref, b_ref, o_ref, tail_ref, ext_ref):
    j = pl.program_id(1)
    R = p_ref.shape[0]
    H = halo_ref.shape[0]
    glu = _glu(p_ref[...])
    halo = _glu(halo_ref[...])
    ext_ref[0:H, :] = jnp.where(j == 0, jnp.zeros_like(halo), halo)
    ext_ref[H:H + R, :] = glu
    acc = jnp.zeros((R, CONV_CH), F32)
    for t in range(CONV_WIDTH):
        acc = acc + ext_ref[pl.ds(H - CONV_BUF + t, R), :] * w_ref[t:t + 1, :]
    h = _layer_norm(acc + cb_ref[...], g_ref[...], b_ref[...])
    o_ref[...] = _silu(h)

    @pl.when(j == pl.num_programs(1) - 1)
    def _():
        tail_ref[0] = glu[R - H:R, :]


def _conv_prompt(B, T, p_glu, conv_w, conv_b, cn_g, cn_b):
    R = ROW_TILE
    H = 32
    tps = T // R
    vec = pl.BlockSpec((1, CONV_CH), lambda b, j: (0, 0))
    z, tail = pl.pallas_call(
        _conv_prompt_kernel,
        grid=(B, tps),
        in_specs=[pl.BlockSpec((R, 2 * CONV_CH), lambda b, j: (b * tps + j, 0)),
                  pl.BlockSpec((H, 2 * CONV_CH), lambda b, j: (jnp.maximum((b * tps + j) * (R // H) - 1, 0), 0)),
                  pl.BlockSpec((CONV_WIDTH, CONV_CH), lambda b, j: (0, 0)), vec, vec, vec],
        out_specs=[pl.BlockSpec((R, CONV_CH), lambda b, j: (b * tps + j, 0)),
                   pl.BlockSpec((1, H, CONV_CH), lambda b, j: (b, 0, 0))],
        out_shape=[jax.ShapeDtypeStruct((B * T, CONV_CH), F32),
                   jax.ShapeDtypeStruct((B, H, CONV_CH), F32)],
        scratch_shapes=[pltpu.VMEM((H + R, CONV_CH), F32)],
        compiler_params=_cparams("parallel", "arbitrary"),
    )(p_glu, p_glu, conv_w, conv_b, cn_g, cn_b)
    return z, tail[:, H - CONV_BUF:, :]


def _conv_sample_kernel(st_ref, p_ref, w_ref, cb_ref, g_ref, b_ref, o_ref, glu_ref):
    steps = p_ref.shape[0]
    glu = [_glu(p_ref[t]) for t in range(steps)]
    for t in range(steps):
        glu_ref[t] = glu[t]
    for t in range(steps):
        acc = jnp.zeros(glu[0].shape, F32)
        for j in range(CONV_WIDTH):
            src = t + j
            x = st_ref[src] if src < CONV_BUF else glu[src - CONV_BUF]
            acc = acc + x * w_ref[j:j + 1, :]
        h = _layer_norm(acc + cb_ref[...], g_ref[...], b_ref[...])
        o_ref[t] = _silu(h)


def _conv_sample(state_t, p_glu, conv_w, conv_b, cn_g, cn_b):
    steps, nb, _ = p_glu.shape

    def full(a):
        return pl.BlockSpec(a.shape, lambda i: (0,) * a.ndim)

    ins = (state_t, p_glu, conv_w, conv_b, cn_g, cn_b)
    return pl.pallas_call(
        _conv_sample_kernel,
        grid=(1,),
        in_specs=[full(a) for a in ins],
        out_specs=[pl.BlockSpec((steps, nb, CONV_CH), lambda i: (0, 0, 0))] * 2,
        out_shape=[jax.ShapeDtypeStruct((steps, nb, CONV_CH), F32)] * 2,
        compiler_params=_cparams("arbitrary"),
    )(*ins)


def _attn_kernel(q_ref, k_ref, v_ref, o_ref):
    nb = q_ref.shape[0]
    scale = MEM_HEAD_DIM ** -0.5
    for bi in range(nb):
        for h in range(MEM_HEADS):
            sl = slice(h * MEM_HEAD_DIM, (h + 1) * MEM_HEAD_DIM)
            s = _bdot_nt(q_ref[bi, :, sl], k_ref[bi, :, sl]) * scale
            m = jnp.max(s, axis=-1, keepdims=True)
            e = jnp.exp(s - m)
            pr = e / jnp.sum(e, axis=-1, keepdims=True)
            o_ref[bi, :, sl] = _bdot(pr, v_ref[bi, :, sl])


def _attention(q, mk, mv, kcol, vcol, bb, tq):
    B, T, W = q.shape
    M = mk.shape[1]
    return pl.pallas_call(
        _attn_kernel,
        grid=(B // bb, T // tq),
        in_specs=[pl.BlockSpec((bb, tq, W), lambda b, j: (b, j, 0)),
                  pl.BlockSpec((bb, M, W), lambda b, j: (b, 0, kcol)),
                  pl.BlockSpec((bb, M, W), lambda b, j: (b, 0, vcol))],
        out_specs=pl.BlockSpec((bb, tq, W), lambda b, j: (b, j, 0)),
        out_shape=jax.ShapeDtypeStruct((B, T, W), F32),
        compiler_params=_cparams("parallel", "parallel"),
    )(q, mk, mv)


def _merge_kernel(npt, alpha, za_ref, zbp_ref, zbs_ref, zcp_ref, zcs_ref, gate_ref, x_ref,
                  woa_ref, wob_ref, woc_ref, wout_ref, g_ref, b_ref, o_ref):
    i = pl.program_id(0)
    D = x_ref.shape[1]
    is_p = i < npt
    zb = jnp.where(is_p, zbp_ref[...], zbs_ref[...])
    zc = jnp.where(is_p, zcp_ref[...], zcs_ref[...])
    merged = (_sigmoid(gate_ref[:, 0:D]) * _bdot(za_ref[...], woa_ref[...])
              + _sigmoid(gate_ref[:, D:2 * D]) * _bdot(zb, wob_ref[...])
              + _sigmoid(gate_ref[:, 2 * D:3 * D]) * _bdot(zc, woc_ref[...]))
    h = alpha * x_ref[...] + _bdot(merged, wout_ref[...])
    o_ref[...] = _layer_norm(h, g_ref[...], b_ref[...])


def _merge(n_p, alpha, za, zb_p, zb_s, zc_p, zc_s, gates, x, woa, wob, woc, wout, ln_g, ln_b):
    n, D = x.shape
    R = MERGE_TILE
    npt = n_p // R

    def row(w):
        return pl.BlockSpec((R, w), lambda i: (i, 0))

    def prow(w):
        return pl.BlockSpec((R, w), lambda i: (jnp.minimum(i, npt - 1), 0))

    def srow(w):
        return pl.BlockSpec((R, w), lambda i: (jnp.maximum(i - npt, 0), 0))

    def full(a):
        return pl.BlockSpec(a.shape, lambda i: (0,) * a.ndim)

    return pl.pallas_call(
        functools.partial(_merge_kernel, npt, alpha),
        grid=(n // R,),
        in_specs=[row(RW), prow(CONV_CH), srow(CONV_CH), prow(MEM_WIDTH), srow(MEM_WIDTH), row(3 * D), row(D),
                  full(woa), full(wob), full(woc), full(wout), full(ln_g), full(ln_b)],
        out_specs=row(D),
        out_shape=jax.ShapeDtypeStruct((n, D), F32),
        compiler_params=_cparams("parallel"),
    )(za, zb_p, zb_s, zc_p, zc_s, gates, x, woa, wob, woc, wout, ln_g, ln_b)


def _router_kernel(h_ref, wr_ref, br_ref, idx_o, gate_o, rank_o, cnt_o, carry_ref):
    i = pl.program_id(0)
    R = h_ref.shape[0]
    E = wr_ref.shape[1]

    @pl.when(i == 0)
    def _():
        carry_ref[...] = jnp.zeros_like(carry_ref)

    scores = _sigmoid(_bdot(h_ref[...], wr_ref[...]))
    work = scores + br_ref[...]
    lane = lax.broadcasted_iota(jnp.int32, (R, E), 1).astype(F32)
    sel_mask = jnp.zeros((R, E), F32)
    idx_cols, sel_cols = [], []
    for _ in range(TOP_K):
        m = jnp.max(work, axis=-1, keepdims=True)
        first = jnp.min(jnp.where(work == m, lane, float(E)), axis=-1, keepdims=True)
        onehot = lane == first
        sel_cols.append(jnp.sum(jnp.where(onehot, scores, 0.0), axis=-1, keepdims=True))
        idx_cols.append(first)
        sel_mask = jnp.where(onehot, 1.0, sel_mask)
        work = jnp.where(onehot, -jnp.inf, work)
    total = sel_cols[0]
    for c in sel_cols[1:]:
        total = total + c
    ri = lax.broadcasted_iota(jnp.int32, (R, R), 0)
    ci = lax.broadcasted_iota(jnp.int32, (R, R), 1)
    before = jnp.dot((ci < ri).astype(BF16), sel_mask.astype(BF16), preferred_element_type=F32) + carry_ref[0:1, :]
    idx_out = jnp.zeros((R, E), F32)
    gate_out = jnp.zeros((R, E), F32)
    rank_out = jnp.zeros((R, E), F32)
    for kk in range(TOP_K):
        onehot = lane == idx_cols[kk]
        rank = jnp.sum(jnp.where(onehot, before, 0.0), axis=-1, keepdims=True)
        col = lane == float(kk)
        idx_out = jnp.where(col, idx_cols[kk], idx_out)
        gate_out = jnp.where(col, sel_cols[kk] / total * ROUTED_SCALE, gate_out)
        rank_out = jnp.where(col, rank, rank_out)
    idx_o[...] = idx_out
    gate_o[...] = gate_out
    rank_o[...] = rank_out
    carry_ref[0:1, :] = carry_ref[0:1, :] + jnp.sum(sel_mask, axis=0, keepdims=True)
    cnt_o[...] = carry_ref[...]


def _router(h, w_router, b_router):
    n, D = h.shape
    E = w_router.shape[1]
    assert E == LANES
    R = 256
    row = pl.BlockSpec((R, E), lambda i: (i, 0))
    return pl.pallas_call(
        _router_kernel,
        grid=(n // R,),
        in_specs=[pl.BlockSpec((R, D), lambda i: (i, 0)),
                  pl.BlockSpec((D, E), lambda i: (0, 0)),
                  pl.BlockSpec((1, E), lambda i: (0, 0))],
        out_specs=[row, row, row, pl.BlockSpec((8, E), lambda i: (0, 0))],
        out_shape=[jax.ShapeDtypeStruct((n, E), F32), jax.ShapeDtypeStruct((n, E), F32),
                   jax.ShapeDtypeStruct((n, E), F32), jax.ShapeDtypeStruct((8, E), F32)],
        scratch_shapes=[pltpu.VMEM((8, E), F32)],
        compiler_params=_cparams("arbitrary"),
    )(h, w_router, b_router)


def _dest_kernel(idx_ref, rank_ref, start_ref, o_ref):
    R, E = idx_ref.shape
    lane = lax.broadcasted_iota(jnp.int32, (R, E), 1).astype(F32)
    idx = idx_ref[...]
    out = jnp.zeros((R, E), F32)
    start = start_ref[...]
    for kk in range(TOP_K):
        e_k = jnp.sum(jnp.where(lane == float(kk), idx, 0.0), axis=-1, keepdims=True)
        s_k = jnp.sum(jnp.where(lane == e_k, start, 0.0), axis=-1, keepdims=True)
        out = jnp.where(lane == float(kk), s_k, out)
    o_ref[...] = (out + rank_ref[...]).astype(jnp.int32)


def _dest(idx, rank, start):
    n, E = idx.shape
    R = 256
    row = pl.BlockSpec((R, E), lambda i: (i, 0))
    return pl.pallas_call(
        _dest_kernel,
        grid=(n // R,),
        in_specs=[row, row, pl.BlockSpec((1, E), lambda i: (0, 0))],
        out_specs=row,
        out_shape=jax.ShapeDtypeStruct((n, E), jnp.int32),
        compiler_params=_cparams("parallel"),
    )(idx, rank, start)


def _dispatch_kernel(h_ref, dest_hbm, xg_in, xg_hbm, dest_smem, sem_idx, sem):
    del xg_in
    i = pl.program_id(0)
    R = h_ref.shape[0]
    n_idx = R * TOP_K
    cp = pltpu.make_async_copy(dest_hbm.at[pl.ds(i * n_idx, n_idx)], dest_smem, sem_idx)
    cp.start()
    cp.wait()

    def row_copy(t, kk):
        d = dest_smem[t * TOP_K + kk]
        return pltpu.make_async_copy(h_ref.at[pl.ds(t, 1), :], xg_hbm.at[pl.ds(d, 1), :], sem)

    def issue(t, c):
        for kk in range(TOP_K):
            row_copy(t, kk).start()
        return c

    lax.fori_loop(0, R, issue, 0)

    def drain(t, c):
        for kk in range(TOP_K):
            row_copy(t, kk).wait()
        return c

    lax.fori_loop(0, R, drain, 0)


def _dispatch(h, dest_flat, n_slots):
    n, D = h.shape
    R = DISPATCH_TILE
    xg0 = jnp.zeros((n_slots, D), F32)
    return pl.pallas_call(
        _dispatch_kernel,
        grid=(n // R,),
        in_specs=[pl.BlockSpec((R, D), lambda i: (i, 0)),
                  pl.BlockSpec(memory_space=pl.ANY),
                  pl.BlockSpec(memory_space=pl.ANY)],
        out_specs=pl.BlockSpec(memory_space=pl.ANY),
        out_shape=jax.ShapeDtypeStruct((n_slots, D), F32),
        scratch_shapes=[pltpu.SMEM((R * TOP_K,), jnp.int32), pltpu.SemaphoreType.DMA, pltpu.SemaphoreType.DMA],
        input_output_aliases={2: 0},
        compiler_params=_cparams("arbitrary"),
    )(h, dest_flat, xg0)


def _expert_kernel(be_ref, nu_ref, x_ref, w1_ref, w3_ref, w2_ref, o_ref, w13_s, w2_s):
    i = pl.program_id(0)
    F = w1_ref.shape[1]

    @pl.when(i < nu_ref[0])
    def _():
        prev = be_ref[jnp.maximum(i - 1, 0)]

        @pl.when((i == 0) | (be_ref[i] != prev))
        def _():
            w13_s[:, 0:F] = w1_ref[...].astype(BF16)
            w13_s[:, F:2 * F] = w3_ref[...].astype(BF16)
            w2_s[...] = w2_ref[...].astype(BF16)

        h13 = jnp.dot(x_ref[...].astype(BF16), w13_s[...], preferred_element_type=F32)
        hh = _silu(h13[:, 0:F]) * h13[:, F:2 * F]
        o_ref[...] = jnp.dot(hh.astype(BF16), w2_s[...], preferred_element_type=F32)


def _experts(layer, block_e, n_used, xg, w_e1, w_e3, w_e2):
    n_slots, D = xg.shape
    F = w_e1.shape[3]
    BM = MOE_BLOCK
    nb = n_slots // BM

    def blk(i, be, nu):
        return (jnp.minimum(i, nu[0] - 1), 0)

    grid_spec = pltpu.PrefetchScalarGridSpec(
        num_scalar_prefetch=2,
        grid=(nb,),
        in_specs=[pl.BlockSpec((BM, D), blk),
                  pl.BlockSpec((None, None, D, F), lambda i, be, nu: (layer, be[i], 0, 0)),
                  pl.BlockSpec((None, None, D, F), lambda i, be, nu: (layer, be[i], 0, 0)),
                  pl.BlockSpec((None, None, F, D), lambda i, be, nu: (layer, be[i], 0, 0))],
        out_specs=pl.BlockSpec((BM, D), blk),
        scratch_shapes=[pltpu.VMEM((D, 2 * F), BF16), pltpu.VMEM((F, D), BF16)],
    )
    return pl.pallas_call(
        _expert_kernel,
        grid_spec=grid_spec,
        out_shape=jax.ShapeDtypeStruct((n_slots, D), F32),
        compiler_params=_cparams("arbitrary"),
    )(block_e, n_used, xg, w_e1, w_e3, w_e2)


def _combine_kernel(alpha, h_ref, gate_ref, dest_hbm, yg_hbm, ws13_ref, ws2_ref, g_ref, b_ref, o_ref,
                    buf, dest_smem, sem_idx, sem):
    i = pl.program_id(0)
    R = h_ref.shape[0]
    F = ws2_ref.shape[0]
    n_idx = R * TOP_K
    cp = pltpu.make_async_copy(dest_hbm.at[pl.ds(i * n_idx, n_idx)], dest_smem, sem_idx)
    cp.start()
    cp.wait()

    def row_copy(t, kk):
        d = dest_smem[t * TOP_K + kk]
        return pltpu.make_async_copy(yg_hbm.at[pl.ds(d, 1), :], buf.at[kk, pl.ds(t, 1), :], sem)

    def issue(t, c):
        for kk in range(TOP_K):
            row_copy(t, kk).start()
        return c

    lax.fori_loop(0, R, issue, 0)

    h = h_ref[...]
    h13 = _bdot(h, ws13_ref[...])
    shared = _bdot(_silu(h13[:, 0:F]) * h13[:, F:2 * F], ws2_ref[...])

    def drain(t, c):
        for kk in range(TOP_K):
            row_copy(t, kk).wait()
        return c

    lax.fori_loop(0, R, drain, 0)

    routed = jnp.zeros_like(h)
    for kk in range(TOP_K):
        routed = routed + buf[kk] * gate_ref[:, kk:kk + 1]
    o_ref[...] = _layer_norm(alpha * h + (shared + routed), g_ref[...], b_ref[...])


def _combine(alpha, h, gate, dest_flat, yg, ws13, ws2, ln_g, ln_b):
    n, D = h.shape
    R = COMBINE_TILE

    def full(a):
        return pl.BlockSpec(a.shape, lambda i: (0,) * a.ndim)

    return pl.pallas_call(
        functools.partial(_combine_kernel, alpha),
        grid=(n // R,),
        in_specs=[pl.BlockSpec((R, D), lambda i: (i, 0)),
                  pl.BlockSpec((R, LANES), lambda i: (i, 0)),
                  pl.BlockSpec(memory_space=pl.ANY),
                  pl.BlockSpec(memory_space=pl.ANY),
                  full(ws13), full(ws2), full(ln_g), full(ln_b)],
        out_specs=pl.BlockSpec((R, D), lambda i: (i, 0)),
        out_shape=jax.ShapeDtypeStruct((n, D), F32),
        scratch_shapes=[pltpu.VMEM((TOP_K, R, D), F32), pltpu.SMEM((R * TOP_K,), jnp.int32),
                        pltpu.SemaphoreType.DMA, pltpu.SemaphoreType.DMA],
        compiler_params=_cparams("arbitrary"),
    )(h, gate, dest_flat, yg, ws13, ws2, ln_g, ln_b)


def _moe(layer, alpha, h, w_router, b_router, w_e1, w_e3, w_e2, ws13, ws2, ln_g, ln_b):
    n, D = h.shape
    E = w_router.shape[1]
    BM = MOE_BLOCK
    idx, gate, rank, cnt = _router(h, w_router, b_router)
    counts = cnt[0].astype(jnp.int32)
    padded = (counts + BM - 1) // BM * BM
    end = jnp.cumsum(padded)
    start = (end - padded).astype(jnp.int32)
    n_blocks = (n * TOP_K + BM - 1) // BM + E
    block_e = jnp.minimum(jnp.searchsorted(end, jnp.arange(n_blocks, dtype=jnp.int32) * BM, side='right'),
                          E - 1).astype(jnp.int32)
    n_used = (end[-1] // BM).astype(jnp.int32).reshape(1)
    dest = _dest(idx, rank, start.astype(F32).reshape(1, E))
    dest_flat = dest[:, :TOP_K].reshape(-1)
    xg = _dispatch(h, dest_flat, n_blocks * BM)
    yg = _experts(layer, block_e, n_used, xg, w_e1, w_e3, w_e2)
    return _combine(alpha, h, gate, dest_flat, yg, ws13, ws2, ln_g, ln_b)


def kernel(x_prompt, x_sample, state_wkv, state_shift, state_conv, cache_mem_k, cache_mem_v, mem_prompt, w_in, w_in_vmix, mu_shift, mu_vmix, w0, w_up, a0, a_up, v0, v_up, g_up, k_k, k_a, r_k, lnx_g, lnx_b, w_o_a, conv_w, conv_b, cn_g, cn_b, w_o_b, w_mem_kv, w_o_c, w_out, ln1_g, ln1_b, ln2_g, ln2_b, w_router, b_router, w_e1, w_e3, w_e2, w_s1, w_s3, w_s2):
    B, T, D = x_prompt.shape
    SB, ST, _ = x_sample.shape
    L = w_in.shape[0]
    M = mem_prompt.shape[1]
    R = ROW_TILE
    n_p, n_s = B * T, SB * ST
    n = n_p + n_s
    assert n_s == R and SB == LANES and T % R == 0 and T % CHUNK == 0 and D % LANES == 0
    npt, tps = n_p // R, T // R
    alpha = (2.0 * L) ** 0.25
    off_glu = RWKV_COLS
    off_q = off_glu + 2 * CONV_CH
    off_gate = off_q + MEM_WIDTH
    in_cols = off_gate + 3 * D

    x = jnp.concatenate([x_prompt.reshape(n_p, D), jnp.swapaxes(x_sample, 0, 1).reshape(n_s, D)], axis=0)
    mem_rows = mem_prompt.reshape(B * M, D)
    row2 = lambda a: a.reshape(1, -1)
    v_first = None
    outs = {k: [] for k in ('wkv_p', 'shift_p', 'conv_p', 'mk', 'mv', 'wkv_s', 'shift_s', 'conv_s')}

    for l in range(L):
        first = l == 0
        w_in_l = w_in[l].astype(BF16)
        w_rwkv = w_in_l[:, :off_glu]
        p_rwkv = _matmul(x, w_rwkv, R, RWKV_COLS // 2)
        p_glu = _matmul(x, w_in_l[:, off_glu:off_q], R, 2 * CONV_CH)
        p_q = _matmul(x, w_in_l[:, off_q:off_gate], R, MEM_WIDTH)
        p_gate = _matmul(x, w_in_l[:, off_gate:in_cols], R, D)
        x_prev_s = state_shift[l]
        prev_rwkv = _matmul(x_prev_s, w_rwkv, SB, RWKV_COLS // 2)

        wup_pad = jnp.concatenate([w_up[l], jnp.zeros((LORA_A, RW), F32)], axis=0).astype(BF16)
        aup_pad = jnp.concatenate([jnp.zeros((LORA_W, RW), F32), a_up[l]], axis=0).astype(BF16)
        lora = (row2(w0[l]), wup_pad, row2(a0[l]), aup_pad, g_up[l].astype(BF16), row2(k_k[l]), row2(k_a[l]),
                row2(r_k[l]))
        if first:
            pre = _rwkv_pre(True, npt, tps, p_rwkv, prev_rwkv, row2(mu_shift[l]), lora)
        else:
            wv_pad = jnp.pad(w_in_vmix[l - 1], ((0, 0), (0, LANES - LORA_V))).astype(BF16)
            p_vd = _matmul(x, wv_pad, R, LANES)
            prev_vd = _matmul(x_prev_s, wv_pad, SB, LANES)
            mu_v = jnp.pad(mu_vmix[l - 1], (0, LANES - LORA_V)).reshape(1, LANES)
            vup_pad = jnp.pad(v_up[l - 1], ((0, LANES - LORA_V), (0, 0))).astype(BF16)
            pre = _rwkv_pre(False, npt, tps, p_rwkv, prev_rwkv, row2(mu_shift[l]), lora, p_vd, prev_vd, mu_v,
                            v_first, row2(v0[l - 1]), vup_pad)
        r_, lw_, k_, v_, nkk_, bb_, g_, bonus_ = pre
        if first:
            v_first = v_

        y_p, s_p = _wkv_prompt(B, T, r_, lw_, k_, v_, nkk_, bb_)

        def lanes_b(a):
            return jnp.transpose(a[n_p:].reshape(ST, SB, RW), (0, 2, 1)).reshape(ST, HEADS, HEAD_DIM, SB)

        s0 = jnp.transpose(state_wkv[l].astype(F32), (1, 2, 3, 0))
        y_s, s_s = _wkv_sample(s0, lanes_b(r_), lanes_b(lw_), lanes_b(k_),
                               lanes_b(v_).reshape(ST, HEADS, HEAD_DIM, 1, SB), lanes_b(nkk_), lanes_b(bb_))
        y_s = jnp.transpose(y_s.reshape(ST, RW, SB), (0, 2, 1)).reshape(n_s, RW)
        s_s = jnp.transpose(s_s, (3, 0, 1, 2))
        z_a = _rwkv_post(npt, y_p, y_s, bonus_, g_, row2(lnx_g[l]), row2(lnx_b[l]))

        cvec = (conv_w[l], row2(conv_b[l]), row2(cn_g[l]), row2(cn_b[l]))
        zb_p, conv_p = _conv_prompt(B, T, p_glu, *cvec)
        st_t = jnp.swapaxes(state_conv[l], 0, 1)
        zb_s, glu_s = _conv_sample(st_t, p_glu[n_p:].reshape(ST, SB, 2 * CONV_CH), *cvec)
        zb_s = zb_s.reshape(n_s, CONV_CH)
        conv_s = jnp.concatenate([state_conv[l][:, ST:], jnp.swapaxes(glu_s, 0, 1)], axis=1)

        kv_p = _matmul(mem_rows, w_mem_kv[l].astype(BF16), M, 2 * MEM_WIDTH)
        kv3 = kv_p.reshape(B, M, 2 * MEM_WIDTH)
        zc_p = _attention(p_q[:n_p].reshape(B, T, MEM_WIDTH), kv3, kv3, 0, 1, 1, R).reshape(n_p, MEM_WIDTH)
        q_s = jnp.swapaxes(p_q[n_p:].reshape(ST, SB, MEM_WIDTH), 0, 1)
        q_s = jnp.pad(q_s, ((0, 0), (0, 8 - ST), (0, 0)))
        ck = cache_mem_k[l].reshape(SB, M, MEM_WIDTH)
        cv = cache_mem_v[l].reshape(SB, M, MEM_WIDTH)
        zc_s = _attention(q_s, ck, cv, 0, 0, 8, 8)[:, :ST]
        zc_s = jnp.swapaxes(zc_s, 0, 1).reshape(n_s, MEM_WIDTH)

        h = _merge(n_p, alpha, z_a, zb_p, zb_s, zc_p, zc_s, p_gate, x,
                   w_o_a[l].astype(BF16), w_o_b[l].astype(BF16), w_o_c[l].astype(BF16), w_out[l].astype(BF16),
                   row2(ln1_g[l]), row2(ln1_b[l]))

        ws13 = jnp.concatenate([w_s1[l], w_s3[l]], axis=1).astype(BF16)
        x_new = _moe(l, alpha, h, w_router[l].astype(BF16), row2(b_router[l]), w_e1, w_e3, w_e2,
                     ws13, w_s2[l].astype(BF16), row2(ln2_g[l]), row2(ln2_b[l]))

        outs['wkv_p'].append(s_p.astype(state_wkv.dtype))
        outs['shift_p'].append(x[:n_p].reshape(B, T, D)[:, -1].astype(state_shift.dtype))
        outs['conv_p'].append(conv_p.astype(state_conv.dtype))
        outs['mk'].append(kv3[..., :MEM_WIDTH].reshape(B, M, MEM_HEADS, MEM_HEAD_DIM).astype(cache_mem_k.dtype))
        outs['mv'].append(kv3[..., MEM_WIDTH:].reshape(B, M, MEM_HEADS, MEM_HEAD_DIM).astype(cache_mem_v.dtype))
        outs['wkv_s'].append(s_s.astype(state_wkv.dtype))
        outs['shift_s'].append(x[n_p + (ST - 1) * SB:].astype(state_shift.dtype))
        outs['conv_s'].append(conv_s.astype(state_conv.dtype))
        x = x_new

    y_p = x[:n_p].reshape(B, T, D)
    y_s = jnp.swapaxes(x[n_p:].reshape(ST, SB, D), 0, 1)
    return (y_p, y_s, jnp.stack(outs['wkv_p']), jnp.stack(outs['shift_p']), jnp.stack(outs['conv_p']),
            jnp.stack(outs['mk']), jnp.stack(outs['mv']), jnp.stack(outs['wkv_s']), jnp.stack(outs['shift_s']),
            jnp.stack(outs['conv_s']))
```

```python
import functools
import math

import jax
import jax.numpy as jnp
from jax import lax
from jax.experimental import pallas as pl
from jax.experimental.pallas import tpu as pltpu

F32 = jnp.float32
BF16 = jnp.bfloat16

HEADS = 8
HEAD_DIM = 64
RW = HEADS * HEAD_DIM
LORA_W = 64
LORA_A = 64
LORA_V = 32
LORA_G = 128
RWKV_COLS = 3 * RW + LORA_W + LORA_A + LORA_G
GN_EPS = HEAD_DIM * 1e-5
CONV_WIDTH = 31
CONV_BUF = CONV_WIDTH - 1
CONV_CH = 512
MEM_HEADS = 4
MEM_HEAD_DIM = 128
MEM_WIDTH = MEM_HEADS * MEM_HEAD_DIM
TOP_K = 8
ROUTED_SCALE = 2.5
LN_EPS = 1e-5
EXP_M05 = math.exp(-0.5)

LANES = 128
ROW_TILE = 512
MERGE_TILE = 256
CHUNK = 64
NEUMANN_BLOCK = 16
MOE_BLOCK = 256
DISPATCH_TILE = 256
COMBINE_TILE = 128
VMEM_LIMIT = 48 * 1024 * 1024


def _cparams(*sem, vmem=VMEM_LIMIT):
    return pltpu.CompilerParams(dimension_semantics=tuple(sem), vmem_limit_bytes=vmem)


def _bdot(a, b):
    return jnp.dot(a.astype(BF16), b.astype(BF16), preferred_element_type=F32)


def _bdot_nt(a, b):
    return lax.dot_general(a.astype(BF16), b.astype(BF16), (((1,), (1,)), ((), ())),
                           preferred_element_type=F32)


def _sigmoid(x):
    return 1.0 / (1.0 + jnp.exp(-x))


def _silu(x):
    return x * _sigmoid(x)


def _layer_norm(x, g, b):
    mu = jnp.mean(x, axis=-1, keepdims=True)
    d = x - mu
    var = jnp.mean(d * d, axis=-1, keepdims=True)
    return d * lax.rsqrt(var + LN_EPS) * g + b


SEG_LANES = 256


def _seg_ones():
    i = jnp.arange(SEG_LANES)
    return ((i[:, None] // HEAD_DIM) == (i[None, :] // HEAD_DIM)).astype(BF16)


def _seg_sum(x, seg):
    hi = x.astype(BF16)
    lo = (x - hi.astype(F32)).astype(BF16)
    parts = []
    for j in range(x.shape[-1] // SEG_LANES):
        sl = slice(j * SEG_LANES, (j + 1) * SEG_LANES)
        parts.append(jnp.dot(hi[:, sl], seg, preferred_element_type=F32)
                     + jnp.dot(lo[:, sl], seg, preferred_element_type=F32))
    return jnp.concatenate(parts, axis=-1)


def _pack_pair(lo, hi):
    lo_b = lax.bitcast_convert_type(lo.astype(BF16).astype(F32), jnp.int32)
    hi_b = lax.bitcast_convert_type(hi.astype(BF16).astype(F32), jnp.int32)
    return lax.shift_right_logical(lo_b, jnp.full(lo_b.shape, 16, jnp.int32)) | (hi_b & jnp.int32(-65536))


def _unpack_pair(p):
    lo = lax.bitcast_convert_type(lax.shift_left(p, jnp.full(p.shape, 16, jnp.int32)), F32)
    hi = lax.bitcast_convert_type(p & jnp.int32(-65536), F32)
    return lo, hi


def _mm_kernel(x_ref, w_ref, o_ref):
    o_ref[...] = _bdot(x_ref[...], w_ref[...]).astype(o_ref.dtype)


def _matmul(x, w, tm, tn):
    m, k = x.shape
    n_cols = w.shape[1]
    assert m % tm == 0 and n_cols % tn == 0
    return pl.pallas_call(
        _mm_kernel,
        grid=(m // tm, n_cols // tn),
        in_specs=[pl.BlockSpec((tm, k), lambda i, j: (i, 0)),
                  pl.BlockSpec((k, tn), lambda i, j: (0, j))],
        out_specs=pl.BlockSpec((tm, tn), lambda i, j: (i, j)),
        out_shape=jax.ShapeDtypeStruct((m, n_cols), F32),
        compiler_params=_cparams("parallel", "parallel"),
        name=f"matmul_{n_cols}",
    )(x, w)


def _pre_kernel(first, npt, tps, *refs):
    if first:
        (p_ref, halo_ref, prev_ref, mu_ref,
         w0_ref, wup_ref, a0_ref, aup_ref, gup_ref, kk_ref, ka_ref, rk_ref, seg_ref,
         r_o, lw_o, k_o, v_o, nkk_o, bb_o, g_o, bon_o, sh_ref) = refs
    else:
        (p_ref, halo_ref, prev_ref, mu_ref, pv_ref, halov_ref, prevv_ref, muv_ref, vf_ref, v0_ref, vup_ref,
         w0_ref, wup_ref, a0_ref, aup_ref, gup_ref, kk_ref, ka_ref, rk_ref, seg_ref,
         r_o, lw_o, k_o, v_o, nkk_o, bb_o, g_o, bon_o, sh_ref, shv_ref) = refs
    i = pl.program_id(0)
    rows_n = p_ref.shape[0]

    def build_shift(src_ref, halo_r, prev_r, dst_ref):
        @pl.when(i < npt)
        def _():
            x = src_ref[...]
            rolled = pltpu.roll(x, 1, axis=0)
            last = halo_r[7:8, :]
            row0 = jnp.where((i % tps) == 0, jnp.zeros_like(last), last)
            rows = lax.broadcasted_iota(jnp.int32, x.shape, 0)
            dst_ref[...] = jnp.where(rows == 0, row0, rolled)

        @pl.when(i >= npt)
        def _():
            nb = prev_r.shape[0]
            dst_ref[0:nb, :] = prev_r[...]
            dst_ref[nb:rows_n, :] = src_ref[0:rows_n - nb, :]

    build_shift(p_ref, halo_ref, prev_ref, sh_ref)
    p = p_ref[...]
    p = p + (sh_ref[...] - p) * mu_ref[...]
    r = p[:, 0:RW]
    k = p[:, RW:2 * RW]
    v = p[:, 2 * RW:3 * RW]
    u = p[:, 3 * RW:3 * RW + LORA_W + LORA_A]
    gd = p[:, 3 * RW + LORA_W + LORA_A:RWKV_COLS]

    z = w0_ref[...] + _bdot(jnp.tanh(u), wup_ref[...])
    lw = -EXP_M05 * _sigmoid(z)
    a = _sigmoid(a0_ref[...] + _bdot(u, aup_ref[...]))
    g = _bdot(_sigmoid(gd), gup_ref[...])
    if not first:
        build_shift(pv_ref, halov_ref, prevv_ref, shv_ref)
        pv = pv_ref[...]
        pv = pv + (shv_ref[...] - pv) * muv_ref[...]
        mix = _sigmoid(v0_ref[...] + _bdot(pv, vup_ref[...]))
        v = v + (vf_ref[...] - v) * mix
    kk = k * kk_ref[...]
    seg = seg_ref[...]
    nrm = jnp.sqrt(_seg_sum(kk * kk, seg))
    kkn = kk / jnp.maximum(nrm, 1e-12)
    k2 = k * (1.0 + (a - 1.0) * ka_ref[...])
    bonus = _seg_sum(r * k2 * rk_ref[...], seg) * v
    r_o[...] = r
    lw_o[...] = lw
    k_o[...] = k2
    v_o[...] = v
    nkk_o[...] = -kkn
    bb_o[...] = kkn * a
    g_o[...] = g
    bon_o[...] = bonus


def _rwkv_pre(first, npt, tps, p_rwkv, prev_rwkv, mu, lora, p_vd=None, prev_vd=None, mu_v=None, v_first=None,
              v0=None, vup=None):
    n = p_rwkv.shape[0]
    R = ROW_TILE
    nt = n // R

    def row_spec(w):
        return pl.BlockSpec((R, w), lambda i: (i, 0))

    def halo_spec(w):
        return pl.BlockSpec((8, w), lambda i: (jnp.maximum(i * (R // 8) - 1, 0), 0))

    def full(a):
        return pl.BlockSpec(a.shape, lambda i: (0,) * a.ndim)

    ins = [p_rwkv, p_rwkv, prev_rwkv, mu]
    specs = [row_spec(RWKV_COLS), halo_spec(RWKV_COLS), full(prev_rwkv), full(mu)]
    scratch = [pltpu.VMEM((R, RWKV_COLS), F32)]
    if not first:
        ins += [p_vd, p_vd, prev_vd, mu_v, v_first, v0, vup]
        specs += [row_spec(LANES), halo_spec(LANES), full(prev_vd), full(mu_v), row_spec(RW), full(v0), full(vup)]
        scratch.append(pltpu.VMEM((R, LANES), F32))
    ins += list(lora)
    specs += [full(a) for a in lora]
    outs = pl.pallas_call(
        functools.partial(_pre_kernel, first, npt, tps),
        grid=(nt,),
        in_specs=specs,
        out_specs=[row_spec(RW)] * 8,
        out_shape=[jax.ShapeDtypeStruct((n, RW), F32)] * 8,
        scratch_shapes=scratch,
        compiler_params=_cparams("parallel"),
        name="rwkv_pre",
    )(*ins)
    return outs


def _wkv_chunk_kernel(nbat, *refs):
    in_refs = refs[:6 * nbat]
    y_ref, s_out_ref, s_ref = refs[6 * nbat:]
    c = pl.program_id(1)
    C = in_refs[0].shape[0]
    G = 4 * HEAD_DIM
    assert 4 * C == G

    @pl.when(c == 0)
    def _():
        s_ref[...] = jnp.zeros_like(s_ref)

    ri = lax.broadcasted_iota(jnp.int32, (G, G), 0)
    ci = lax.broadcasted_iota(jnp.int32, (G, G), 1)
    same_head = (ri // C) == (ci // HEAD_DIM)
    tril_s = same_head & ((ci % C) < (ri % C))
    tril_i = same_head & ((ci % C) <= (ri % C))
    diag_blk = (ri // NEUMANN_BLOCK) == (ci // NEUMANN_BLOCK)
    eye = (ri == ci).astype(F32)

    def stack(x):
        return jnp.where(same_head, jnp.concatenate([x, x, x, x], axis=0), 0.0).astype(BF16)

    def tile4(x):
        return jnp.concatenate([x, x, x, x], axis=0).astype(BF16)

    def collapse(x4):
        return x4[0:C] + x4[C:2 * C] + x4[2 * C:3 * C] + x4[3 * C:4 * C]

    for b in range(nbat):
        r_ref, lw_ref, k_ref, v_ref, nkk_ref, bb_ref = in_refs[6 * b:6 * b + 6]
        lw = lw_ref[...]
        rows = lax.broadcasted_iota(jnp.int32, lw.shape, 0)
        cs = lw
        s = 1
        while s < C:
            cs = cs + jnp.where(rows >= s, pltpu.roll(cs, s, axis=0), 0.0)
            s *= 2
        cs_last = cs[C - 1:C, :]
        e_neg = jnp.exp(-cs)
        e_tail = jnp.exp(cs_last - cs)
        kk_ = k_ref[...]
        bb_ = bb_ref[...]
        at = nkk_ref[...] * jnp.exp(cs - lw)
        rt = r_ref[...] * jnp.exp(cs)
        bt = bb_ * e_neg
        kt = kk_ * e_neg
        bw = bb_ * e_tail
        kw = kk_ * e_tail
        vv = v_ref[...]
        w_end = jnp.exp(cs_last)

        for g in range(RW // G):
            sl = slice(g * G, (g + 1) * G)
            at4, rt4, v4 = stack(at[:, sl]), stack(rt[:, sl]), stack(vv[:, sl])
            bt4, kt4 = tile4(bt[:, sl]), tile4(kt[:, sl])
            a_ab = jnp.where(tril_s, _bdot_nt(at4, bt4), 0.0)
            a_ak = jnp.where(tril_s, _bdot_nt(at4, kt4), 0.0).astype(BF16)
            a_rb = jnp.where(tril_i, _bdot_nt(rt4, bt4), 0.0).astype(BF16)
            a_rk = jnp.where(tril_i, _bdot_nt(rt4, kt4), 0.0).astype(BF16)
            a_d = jnp.where(diag_blk, a_ab, 0.0)
            a_o = a_ab - a_d
            a_db = a_d.astype(BF16)
            x2 = _bdot(a_db, a_db).astype(BF16)
            x4 = _bdot(x2, x2).astype(BF16)
            x8 = _bdot(x4, x4)
            t_d = eye + a_d
            t_d = t_d + _bdot(t_d, x2)
            t_d = t_d + _bdot(t_d, x4)
            t_d = t_d + _bdot(t_d, x8)
            t_db = t_d.astype(BF16)
            nn = _bdot(t_db, a_o).astype(BF16)
            n2 = _bdot(nn, nn)
            m1 = t_d + _bdot(nn, t_db)
            t_m = m1 + _bdot(n2, m1)
            s4 = s_ref[b, g]
            s4b = s4.astype(BF16)
            u4 = _bdot(t_m, _bdot_nt(at4, s4b) + _bdot(a_ak, v4))
            y4 = _bdot_nt(rt4, s4b) + _bdot(jnp.concatenate([a_rb, a_rk], axis=1),
                                            jnp.concatenate([u4.astype(BF16), v4], axis=0))
            y_ref[b, :, sl] = collapse(y4)
            uv_t = jnp.concatenate([collapse(u4), vv[:, sl]], axis=0).T
            upd = _bdot(uv_t, jnp.concatenate([bw[:, sl], kw[:, sl]], axis=0))
            s_ref[b, g] = s4 * w_end[:, sl] + jnp.where((ri // HEAD_DIM) == (ci // HEAD_DIM), upd, 0.0)

    @pl.when(c == pl.num_programs(1) - 1)
    def _():
        s_out_ref[...] = s_ref[...]


def _wkv_prompt(B, T, r, lw, k, v, nkk, bb):
    C = CHUNK
    nc = T // C
    G = 4 * HEAD_DIM
    nbat = 2 if B % 2 == 0 else 1
    specs = []
    for j in range(nbat):
        specs += [pl.BlockSpec((C, RW), lambda i, c, j=j: ((nbat * i + j) * nc + c, 0))] * 6
    y, s = pl.pallas_call(
        functools.partial(_wkv_chunk_kernel, nbat),
        grid=(B // nbat, nc),
        in_specs=specs,
        out_specs=[pl.BlockSpec((nbat, C, RW), lambda i, c: (i, c, 0)),
                   pl.BlockSpec((nbat, RW // G, G, G), lambda i, c: (i, 0, 0, 0))],
        out_shape=[jax.ShapeDtypeStruct((B, T, RW), F32),
                   jax.ShapeDtypeStruct((B, RW // G, G, G), F32)],
        scratch_shapes=[pltpu.VMEM((nbat, RW // G, G, G), F32)],
        compiler_params=_cparams("parallel", "arbitrary"),
        name="wkv_chunk",
    )(*([r, lw, k, v, nkk, bb] * nbat))
    y = y.reshape(B * T, RW)
    s = s.reshape(B, RW // G, 4, HEAD_DIM, 4, HEAD_DIM)
    idx = jnp.arange(4)
    s = s[:, :, idx, :, idx, :]
    s = jnp.transpose(s, (1, 2, 0, 3, 4)).reshape(B, HEADS, HEAD_DIM, HEAD_DIM)
    return y, s


def _wkv_seq_kernel(s0_ref, r_ref, lw_ref, k_ref, v_ref, nkk_ref, bb_ref, y_ref, s_ref):
    steps = r_ref.shape[0]

    def body(vi, carry):
        st = s0_ref[0, vi]
        for t in range(steps):
            sa = jnp.sum(st * nkk_ref[t, 0], axis=0, keepdims=True)
            st = st * jnp.exp(lw_ref[t, 0]) + sa * bb_ref[t, 0] + v_ref[t, 0, vi] * k_ref[t, 0]
            y_ref[t, 0, vi] = jnp.sum(st * r_ref[t, 0], axis=0, keepdims=True)
        s_ref[0, vi] = st
        return carry

    lax.fori_loop(0, HEAD_DIM, body, 0)


def _wkv_sample(s0, r, lw, k, v, nkk, bb):
    steps, _, _, nb = r.shape
    vec = pl.BlockSpec((steps, 1, HEAD_DIM, nb), lambda h: (0, h, 0, 0))
    vcol = pl.BlockSpec((steps, 1, HEAD_DIM, 1, nb), lambda h: (0, h, 0, 0, 0))
    st = pl.BlockSpec((1, HEAD_DIM, HEAD_DIM, nb), lambda h: (h, 0, 0, 0))
    y, s = pl.pallas_call(
        _wkv_seq_kernel,
        grid=(HEADS,),
        in_specs=[st, vec, vec, vec, vcol, vec, vec],
        out_specs=[vcol, st],
        out_shape=[jax.ShapeDtypeStruct((steps, HEADS, HEAD_DIM, 1, nb), F32),
                   jax.ShapeDtypeStruct(s0.shape, F32)],
        compiler_params=_cparams("parallel"),
        name="wkv_seq",
    )(s0, r, lw, k, v, nkk, bb)
    return y, s


def _post_kernel(npt, yp_ref, ys_ref, bon_ref, g_ref, lg_ref, lb_ref, seg_ref, o_ref):
    i = pl.program_id(0)
    y = jnp.where(i < npt, yp_ref[...], ys_ref[...])
    seg = seg_ref[...]
    mu = _seg_sum(y, seg) * (1.0 / HEAD_DIM)
    d = y - mu
    var = _seg_sum(d * d, seg) * (1.0 / HEAD_DIM)
    yn = d * lax.rsqrt(var + GN_EPS) * lg_ref[...] + lb_ref[...]
    o_ref[...] = (yn + bon_ref[...]) * g_ref[...]


def _rwkv_post(npt, y_p, y_s, bonus, g, lnx_g, lnx_b):
    n = bonus.shape[0]
    R = ROW_TILE
    row = pl.BlockSpec((R, RW), lambda i: (i, 0))
    vec = pl.BlockSpec((1, RW), lambda i: (0, 0))
    return pl.pallas_call(
        functools.partial(_post_kernel, npt),
        grid=(n // R,),
        in_specs=[pl.BlockSpec((R, RW), lambda i: (jnp.minimum(i, npt - 1), 0)),
                  pl.BlockSpec((R, RW), lambda i: (0, 0)), row, row, vec, vec,
                  pl.BlockSpec((SEG_LANES, SEG_LANES), lambda i: (0, 0))],
        out_specs=row,
        out_shape=jax.ShapeDtypeStruct((n, RW), F32),
        compiler_params=_cparams("parallel"),
        name="rwkv_post",
    )(y_p, y_s, bonus, g, lnx_g, lnx_b, _seg_ones())


def _glu(p):
    return p[..., :CONV_CH] * _sigmoid(p[..., CONV_CH:])


def _conv_prompt_kernel(p_ref, halo_ref, w_ref, cb_ref, g_ref, b_ref, o_ref, tail_ref, ext_ref):
    j = pl.program_id(1)
    R = p_ref.shape[0]
    H = halo_ref.shape[0]
    glu = _glu(p_ref[...])
    halo = _glu(halo_ref[...])
    ext_ref[0:H, :] = jnp.where(j == 0, jnp.zeros_like(halo), halo)
    ext_ref[H:H + R, :] = glu
    acc = jnp.zeros((R, CONV_CH), F32)
    for t in range(CONV_WIDTH):
        acc = acc + ext_ref[pl.ds(H - CONV_BUF + t, R), :] * w_ref[t:t + 1, :]
    h = _layer_norm(acc + cb_ref[...], g_ref[...], b_ref[...])
    o_ref[...] = _silu(h)

    @pl.when(j == pl.num_programs(1) - 1)
    def _():
        tail_ref[0] = glu[R - H:R, :]


def _conv_prompt(B, T, p_glu, conv_w, conv_b, cn_g, cn_b):
    R = ROW_TILE
    H = 32
    tps = T // R
    vec = pl.BlockSpec((1, CONV_CH), lambda b, j: (0, 0))
    z, tail = pl.pallas_call(
        _conv_prompt_kernel,
        grid=(B, tps),
        in_specs=[pl.BlockSpec((R, 2 * CONV_CH), lambda b, j: (b * tps + j, 0)),
                  pl.BlockSpec((H, 2 * CONV_CH), lambda b, j: (jnp.maximum((b * tps + j) * (R // H) - 1, 0), 0)),
                  pl.BlockSpec((CONV_WIDTH, CONV_CH), lambda b, j: (0, 0)), vec, vec, vec],
        out_specs=[pl.BlockSpec((R, CONV_CH), lambda b, j: (b * tps + j, 0)),
                   pl.BlockSpec((1, H, CONV_CH), lambda b, j: (b, 0, 0))],
        out_shape=[jax.ShapeDtypeStruct((B * T, CONV_CH), F32),
                   jax.ShapeDtypeStruct((B, H, CONV_CH), F32)],
        scratch_shapes=[pltpu.VMEM((H + R, CONV_CH), F32)],
        compiler_params=_cparams("parallel", "arbitrary"),
        name="conv_prompt",
    )(p_glu, p_glu, conv_w, conv_b, cn_g, cn_b)
    return z, tail[:, H - CONV_BUF:, :]


def _conv_sample_kernel(st_ref, p_ref, w_ref, cb_ref, g_ref, b_ref, o_ref, glu_ref):
    steps = p_ref.shape[0]
    glu = [_glu(p_ref[t]) for t in range(steps)]
    for t in range(steps):
        glu_ref[t] = glu[t]
    for t in range(steps):
        acc = jnp.zeros(glu[0].shape, F32)
        for j in range(CONV_WIDTH):
            src = t + j
            x = st_ref[src] if src < CONV_BUF else glu[src - CONV_BUF]
            acc = acc + x * w_ref[j:j + 1, :]
        h = _layer_norm(acc + cb_ref[...], g_ref[...], b_ref[...])
        o_ref[t] = _silu(h)


def _conv_sample(state_t, p_glu, conv_w, conv_b, cn_g, cn_b):
    steps, nb, _ = p_glu.shape

    def full(a):
        return pl.BlockSpec(a.shape, lambda i: (0,) * a.ndim)

    ins = (state_t, p_glu, conv_w, conv_b, cn_g, cn_b)
    return pl.pallas_call(
        _conv_sample_kernel,
        grid=(1,),
        in_specs=[full(a) for a in ins],
        out_specs=[pl.BlockSpec((steps, nb, CONV_CH), lambda i: (0, 0, 0))] * 2,
        out_shape=[jax.ShapeDtypeStruct((steps, nb, CONV_CH), F32)] * 2,
        compiler_params=_cparams("arbitrary"),
        name="conv_sample",
    )(*ins)


def _attn_kernel(q_ref, k_ref, v_ref, o_ref):
    nb = q_ref.shape[0]
    scale = MEM_HEAD_DIM ** -0.5
    for bi in range(nb):
        for h in range(MEM_HEADS):
            sl = slice(h * MEM_HEAD_DIM, (h + 1) * MEM_HEAD_DIM)
            s = _bdot_nt(q_ref[bi, :, sl], k_ref[bi, :, sl]) * scale
            m = jnp.max(s, axis=-1, keepdims=True)
            e = jnp.exp(s - m)
            pr = e / jnp.sum(e, axis=-1, keepdims=True)
            o_ref[bi, :, sl] = _bdot(pr, v_ref[bi, :, sl])


def _attention(q, mk, mv, kcol, vcol, bb, tq):
    B, T, W = q.shape
    M = mk.shape[1]
    return pl.pallas_call(
        _attn_kernel,
        grid=(B // bb, T // tq),
        in_specs=[pl.BlockSpec((bb, tq, W), lambda b, j: (b, j, 0)),
                  pl.BlockSpec((bb, M, W), lambda b, j: (b, 0, kcol)),
                  pl.BlockSpec((bb, M, W), lambda b, j: (b, 0, vcol))],
        out_specs=pl.BlockSpec((bb, tq, W), lambda b, j: (b, j, 0)),
        out_shape=jax.ShapeDtypeStruct((B, T, W), F32),
        compiler_params=_cparams("parallel", "parallel"),
        name=f"mem_attention_{tq}",
    )(q, mk, mv)


def _merge_kernel(npt, alpha, za_ref, zbp_ref, zbs_ref, zcp_ref, zcs_ref, gate_ref, x_ref,
                  woa_ref, wob_ref, woc_ref, wout_ref, g_ref, b_ref, o_ref):
    i = pl.program_id(0)
    D = x_ref.shape[1]
    is_p = i < npt
    zb = jnp.where(is_p, zbp_ref[...], zbs_ref[...])
    zc = jnp.where(is_p, zcp_ref[...], zcs_ref[...])
    merged = (_sigmoid(gate_ref[:, 0:D]) * _bdot(za_ref[...], woa_ref[...])
              + _sigmoid(gate_ref[:, D:2 * D]) * _bdot(zb, wob_ref[...])
              + _sigmoid(gate_ref[:, 2 * D:3 * D]) * _bdot(zc, woc_ref[...]))
    h = alpha * x_ref[...] + _bdot(merged, wout_ref[...])
    o_ref[...] = _layer_norm(h, g_ref[...], b_ref[...])


def _merge(n_p, alpha, za, zb_p, zb_s, zc_p, zc_s, gates, x, woa, wob, woc, wout, ln_g, ln_b):
    n, D = x.shape
    R = MERGE_TILE
    npt = n_p // R

    def row(w):
        return pl.BlockSpec((R, w), lambda i: (i, 0))

    def prow(w):
        return pl.BlockSpec((R, w), lambda i: (jnp.minimum(i, npt - 1), 0))

    def srow(w):
        return pl.BlockSpec((R, w), lambda i: (jnp.maximum(i - npt, 0), 0))

    def full(a):
        return pl.BlockSpec(a.shape, lambda i: (0,) * a.ndim)

    return pl.pallas_call(
        functools.partial(_merge_kernel, npt, alpha),
        grid=(n // R,),
        in_specs=[row(RW), prow(CONV_CH), srow(CONV_CH), prow(MEM_WIDTH), srow(MEM_WIDTH), row(3 * D), row(D),
                  full(woa), full(wob), full(woc), full(wout), full(ln_g), full(ln_b)],
        out_specs=row(D),
        out_shape=jax.ShapeDtypeStruct((n, D), F32),
        compiler_params=_cparams("parallel"),
        name="branch_merge",
    )(za, zb_p, zb_s, zc_p, zc_s, gates, x, woa, wob, woc, wout, ln_g, ln_b)


def _router_kernel(h_ref, wr_ref, br_ref, idx_o, gate_o, rank_o, cnt_o, carry_ref):
    i = pl.program_id(0)
    R = h_ref.shape[0]
    E = wr_ref.shape[1]

    @pl.when(i == 0)
    def _():
        carry_ref[...] = jnp.zeros_like(carry_ref)

    scores = _sigmoid(_bdot(h_ref[...], wr_ref[...]))
    work = scores + br_ref[...]
    lane = lax.broadcasted_iota(jnp.int32, (R, E), 1).astype(F32)
    sel_mask = jnp.zeros((R, E), F32)
    idx_cols, sel_cols = [], []
    for _ in range(TOP_K):
        m = jnp.max(work, axis=-1, keepdims=True)
        first = jnp.min(jnp.where(work == m, lane, float(E)), axis=-1, keepdims=True)
        onehot = lane == first
        sel_cols.append(jnp.sum(jnp.where(onehot, scores, 0.0), axis=-1, keepdims=True))
        idx_cols.append(first)
        sel_mask = jnp.where(onehot, 1.0, sel_mask)
        work = jnp.where(onehot, -jnp.inf, work)
    total = sel_cols[0]
    for c in sel_cols[1:]:
        total = total + c
    ri = lax.broadcasted_iota(jnp.int32, (R, R), 0)
    ci = lax.broadcasted_iota(jnp.int32, (R, R), 1)
    before = jnp.dot((ci < ri).astype(BF16), sel_mask.astype(BF16), preferred_element_type=F32) + carry_ref[0:1, :]
    idx_out = jnp.zeros((R, E), F32)
    gate_out = jnp.zeros((R, E), F32)
    rank_out = jnp.zeros((R, E), F32)
    for kk in range(TOP_K):
        onehot = lane == idx_cols[kk]
        rank = jnp.sum(jnp.where(onehot, before, 0.0), axis=-1, keepdims=True)
        col = lane == float(kk)
        idx_out = jnp.where(col, idx_cols[kk], idx_out)
        gate_out = jnp.where(col, sel_cols[kk] / total * ROUTED_SCALE, gate_out)
        rank_out = jnp.where(col, rank, rank_out)
    idx_o[...] = idx_out
    gate_o[...] = gate_out
    rank_o[...] = rank_out
    carry_ref[0:1, :] = carry_ref[0:1, :] + jnp.sum(sel_mask, axis=0, keepdims=True)
    cnt_o[...] = carry_ref[...]


def _router(h, w_router, b_router):
    n, D = h.shape
    E = w_router.shape[1]
    assert E == LANES
    R = 256
    row = pl.BlockSpec((R, E), lambda i: (i, 0))
    return pl.pallas_call(
        _router_kernel,
        grid=(n // R,),
        in_specs=[pl.BlockSpec((R, D), lambda i: (i, 0)),
                  pl.BlockSpec((D, E), lambda i: (0, 0)),
                  pl.BlockSpec((1, E), lambda i: (0, 0))],
        out_specs=[row, row, row, pl.BlockSpec((8, E), lambda i: (0, 0))],
        out_shape=[jax.ShapeDtypeStruct((n, E), F32), jax.ShapeDtypeStruct((n, E), F32),
                   jax.ShapeDtypeStruct((n, E), F32), jax.ShapeDtypeStruct((8, E), F32)],
        scratch_shapes=[pltpu.VMEM((8, E), F32)],
        compiler_params=_cparams("arbitrary"),
        name="moe_router",
    )(h, w_router, b_router)


def _dest_kernel(idx_ref, rank_ref, start_ref, o_ref):
    R, E = idx_ref.shape
    lane = lax.broadcasted_iota(jnp.int32, (R, E), 1).astype(F32)
    idx = idx_ref[...]
    out = jnp.zeros((R, E), F32)
    start = start_ref[...]
    for kk in range(TOP_K):
        e_k = jnp.sum(jnp.where(lane == float(kk), idx, 0.0), axis=-1, keepdims=True)
        s_k = jnp.sum(jnp.where(lane == e_k, start, 0.0), axis=-1, keepdims=True)
        out = jnp.where(lane == float(kk), s_k, out)
    o_ref[...] = (out + rank_ref[...]).astype(jnp.int32)


def _dest(idx, rank, start):
    n, E = idx.shape
    R = 256
    row = pl.BlockSpec((R, E), lambda i: (i, 0))
    return pl.pallas_call(
        _dest_kernel,
        grid=(n // R,),
        in_specs=[row, row, pl.BlockSpec((1, E), lambda i: (0, 0))],
        out_specs=row,
        out_shape=jax.ShapeDtypeStruct((n, E), jnp.int32),
        compiler_params=_cparams("parallel"),
        name="moe_dest",
    )(idx, rank, start)


def _zero_tail_kernel(eb_ref, o_ref):
    o_ref[...] = jnp.zeros_like(o_ref)


def _zero_tails(end_blocks, n_slots, width):
    BM = MOE_BLOCK
    grid_spec = pltpu.PrefetchScalarGridSpec(
        num_scalar_prefetch=1,
        grid=(end_blocks.shape[0],),
        in_specs=[],
        out_specs=pl.BlockSpec((BM, width), lambda e, eb: (jnp.maximum(eb[e] - 1, 0), 0)),
    )
    return pl.pallas_call(
        _zero_tail_kernel,
        grid_spec=grid_spec,
        out_shape=jax.ShapeDtypeStruct((n_slots, width), jnp.int32),
        compiler_params=_cparams("arbitrary"),
        name="moe_zero_tails",
    )(end_blocks)


def _dispatch_kernel(h_ref, dest_hbm, xg_in, xg_hbm, pk_ref, dest_smem, sem_idx, sem):
    del xg_in
    i = pl.program_id(0)
    R, D = h_ref.shape
    n_idx = R * TOP_K
    cp = pltpu.make_async_copy(dest_hbm.at[pl.ds(i * n_idx, n_idx)], dest_smem, sem_idx)
    cp.start()
    pk_ref[...] = _pack_pair(h_ref[:, 0:D // 2], h_ref[:, D // 2:D])
    cp.wait()

    def row_copy(t, kk):
        d = dest_smem[t * TOP_K + kk]
        return pltpu.make_async_copy(pk_ref.at[pl.ds(t, 1), :], xg_hbm.at[pl.ds(d, 1), :], sem)

    def issue(t, c):
        for kk in range(TOP_K):
            row_copy(t, kk).start()
        return c

    lax.fori_loop(0, R, issue, 0)

    def drain(t, c):
        for kk in range(TOP_K):
            row_copy(t, kk).wait()
        return c

    lax.fori_loop(0, R, drain, 0)


def _dispatch(h, dest_flat, xg0):
    n, D = h.shape
    R = DISPATCH_TILE
    return pl.pallas_call(
        _dispatch_kernel,
        grid=(n // R,),
        in_specs=[pl.BlockSpec((R, D), lambda i: (i, 0)),
                  pl.BlockSpec(memory_space=pl.ANY),
                  pl.BlockSpec(memory_space=pl.ANY)],
        out_specs=pl.BlockSpec(memory_space=pl.ANY),
        out_shape=jax.ShapeDtypeStruct(xg0.shape, xg0.dtype),
        scratch_shapes=[pltpu.VMEM((R, D // 2), jnp.int32), pltpu.SMEM((R * TOP_K,), jnp.int32),
                        pltpu.SemaphoreType.DMA, pltpu.SemaphoreType.DMA],
        input_output_aliases={2: 0},
        compiler_params=_cparams("arbitrary"),
        name="moe_dispatch",
    )(h, dest_flat, xg0)


def _expert_kernel(be_ref, nu_ref, x_ref, w1_ref, w3_ref, w2_ref, o_ref, w13_s, w2_s):
    i = pl.program_id(0)
    F = w1_ref.shape[1]

    @pl.when(i < nu_ref[0])
    def _():
        prev = be_ref[jnp.maximum(i - 1, 0)]

        @pl.when((i == 0) | (be_ref[i] != prev))
        def _():
            w13_s[:, 0:F] = w1_ref[...].astype(BF16)
            w13_s[:, F:2 * F] = w3_ref[...].astype(BF16)
            w2_s[...] = w2_ref[...].astype(BF16)

        x_lo, x_hi = _unpack_pair(x_ref[...])
        half = x_ref.shape[1]
        h13 = (jnp.dot(x_lo.astype(BF16), w13_s[0:half, :], preferred_element_type=F32)
               + jnp.dot(x_hi.astype(BF16), w13_s[half:2 * half, :], preferred_element_type=F32))
        hh = _silu(h13[:, 0:F]) * h13[:, F:2 * F]
        y = jnp.dot(hh.astype(BF16), w2_s[...], preferred_element_type=F32)
        o_ref[...] = _pack_pair(y[:, 0:half], y[:, half:2 * half])


def _experts(layer, block_e, n_used, xg, w_e1, w_e3, w_e2):
    n_slots, half = xg.shape
    D = 2 * half
    F = w_e1.shape[3]
    BM = MOE_BLOCK
    nb = n_slots // BM

    def blk(i, be, nu):
        return (jnp.minimum(i, nu[0] - 1), 0)

    grid_spec = pltpu.PrefetchScalarGridSpec(
        num_scalar_prefetch=2,
        grid=(nb,),
        in_specs=[pl.BlockSpec((BM, half), blk),
                  pl.BlockSpec((None, None, D, F), lambda i, be, nu: (layer, be[i], 0, 0)),
                  pl.BlockSpec((None, None, D, F), lambda i, be, nu: (layer, be[i], 0, 0)),
                  pl.BlockSpec((None, None, F, D), lambda i, be, nu: (layer, be[i], 0, 0))],
        out_specs=pl.BlockSpec((BM, half), blk),
        scratch_shapes=[pltpu.VMEM((D, 2 * F), BF16), pltpu.VMEM((F, D), BF16)],
    )
    return pl.pallas_call(
        _expert_kernel,
        grid_spec=grid_spec,
        out_shape=jax.ShapeDtypeStruct((n_slots, half), jnp.int32),
        compiler_params=_cparams("arbitrary"),
        name="moe_experts",
    )(block_e, n_used, xg, w_e1, w_e3, w_e2)


def _combine_kernel(alpha, h_ref, gate_ref, dest_hbm, yg_hbm, ws13_ref, ws2_ref, g_ref, b_ref, o_ref,
                    buf, dest_smem, sem_idx, sem):
    i = pl.program_id(0)
    R = h_ref.shape[0]
    F = ws2_ref.shape[0]
    n_idx = R * TOP_K
    cp = pltpu.make_async_copy(dest_hbm.at[pl.ds(i * n_idx, n_idx)], dest_smem, sem_idx)
    cp.start()
    cp.wait()

    def row_copy(t, kk):
        d = dest_smem[t * TOP_K + kk]
        return pltpu.make_async_copy(yg_hbm.at[pl.ds(d, 1), :], buf.at[kk, pl.ds(t, 1), :], sem)

    def issue(t, c):
        for kk in range(TOP_K):
            row_copy(t, kk).start()
        return c

    lax.fori_loop(0, R, issue, 0)

    h = h_ref[...]
    h13 = _bdot(h, ws13_ref[...])
    shared = _bdot(_silu(h13[:, 0:F]) * h13[:, F:2 * F], ws2_ref[...])

    def drain(t, c):
        for kk in range(TOP_K):
            row_copy(t, kk).wait()
        return c

    lax.fori_loop(0, R, drain, 0)

    r_lo = jnp.zeros(buf.shape[1:], F32)
    r_hi = jnp.zeros(buf.shape[1:], F32)
    for kk in range(TOP_K):
        y_lo, y_hi = _unpack_pair(buf[kk])
        gk = gate_ref[:, kk:kk + 1]
        r_lo = r_lo + y_lo * gk
        r_hi = r_hi + y_hi * gk
    routed = jnp.concatenate([r_lo, r_hi], axis=-1)
    o_ref[...] = _layer_norm(alpha * h + (shared + routed), g_ref[...], b_ref[...])


def _combine(alpha, h, gate, dest_flat, yg, ws13, ws2, ln_g, ln_b):
    n, D = h.shape
    R = COMBINE_TILE

    def full(a):
        return pl.BlockSpec(a.shape, lambda i: (0,) * a.ndim)

    return pl.pallas_call(
        functools.partial(_combine_kernel, alpha),
        grid=(n // R,),
        in_specs=[pl.BlockSpec((R, D), lambda i: (i, 0)),
                  pl.BlockSpec((R, LANES), lambda i: (i, 0)),
                  pl.BlockSpec(memory_space=pl.ANY),
                  pl.BlockSpec(memory_space=pl.ANY),
                  full(ws13), full(ws2), full(ln_g), full(ln_b)],
        out_specs=pl.BlockSpec((R, D), lambda i: (i, 0)),
        out_shape=jax.ShapeDtypeStruct((n, D), F32),
        scratch_shapes=[pltpu.VMEM((TOP_K, R, D // 2), jnp.int32), pltpu.SMEM((R * TOP_K,), jnp.int32),
                        pltpu.SemaphoreType.DMA, pltpu.SemaphoreType.DMA],
        compiler_params=_cparams("arbitrary"),
        name="moe_combine",
    )(h, gate, dest_flat, yg, ws13, ws2, ln_g, ln_b)


def _moe(layer, alpha, h, w_router, b_router, w_e1, w_e3, w_e2, ws13, ws2, ln_g, ln_b):
    n, D = h.shape
    E = w_router.shape[1]
    BM = MOE_BLOCK
    idx, gate, rank, cnt = _router(h, w_router, b_router)
    counts = cnt[0].astype(jnp.int32)
    padded = (counts + BM - 1) // BM * BM
    end = jnp.cumsum(padded)
    start = (end - padded).astype(jnp.int32)
    n_blocks = (n * TOP_K + BM - 1) // BM + E
    block_e = jnp.minimum(jnp.searchsorted(end, jnp.arange(n_blocks, dtype=jnp.int32) * BM, side='right'),
                          E - 1).astype(jnp.int32)
    n_used = (end[-1] // BM).astype(jnp.int32).reshape(1)
    dest = _dest(idx, rank, start.astype(F32).reshape(1, E))
    dest_flat = dest[:, :TOP_K].reshape(-1)
    xg = _dispatch(h, dest_flat, _zero_tails((end // BM).astype(jnp.int32), n_blocks * BM, D // 2))
    yg = _experts(layer, block_e, n_used, xg, w_e1, w_e3, w_e2)
    return _combine(alpha, h, gate, dest_flat, yg, ws13, ws2, ln_g, ln_b)


def kernel(x_prompt, x_sample, state_wkv, state_shift, state_conv, cache_mem_k, cache_mem_v, mem_prompt, w_in, w_in_vmix, mu_shift, mu_vmix, w0, w_up, a0, a_up, v0, v_up, g_up, k_k, k_a, r_k, lnx_g, lnx_b, w_o_a, conv_w, conv_b, cn_g, cn_b, w_o_b, w_mem_kv, w_o_c, w_out, ln1_g, ln1_b, ln2_g, ln2_b, w_router, b_router, w_e1, w_e3, w_e2, w_s1, w_s3, w_s2):
    B, T, D = x_prompt.shape
    SB, ST, _ = x_sample.shape
    L = w_in.shape[0]
    M = mem_prompt.shape[1]
    R = ROW_TILE
    n_p, n_s = B * T, SB * ST
    n = n_p + n_s
    assert n_s == R and SB == LANES and T % R == 0 and T % CHUNK == 0 and D % LANES == 0
    npt, tps = n_p // R, T // R
    alpha = (2.0 * L) ** 0.25
    off_glu = RWKV_COLS
    off_q = off_glu + 2 * CONV_CH
    off_gate = off_q + MEM_WIDTH
    in_cols = off_gate + 3 * D

    x = jnp.concatenate([x_prompt.reshape(n_p, D), jnp.swapaxes(x_sample, 0, 1).reshape(n_s, D)], axis=0)
    mem_rows = mem_prompt.reshape(B * M, D)
    row2 = lambda a: a.reshape(1, -1)
    v_first = None
    outs = {k: [] for k in ('wkv_p', 'shift_p', 'conv_p', 'mk', 'mv', 'wkv_s', 'shift_s', 'conv_s')}

    for l in range(L):
        first = l == 0
        w_in_l = w_in[l].astype(BF16)
        w_rwkv = w_in_l[:, :off_glu]
        p_rwkv = _matmul(x, w_rwkv, R, RWKV_COLS // 2)
        p_glu = _matmul(x, w_in_l[:, off_glu:off_q], R, 2 * CONV_CH)
        p_q = _matmul(x, w_in_l[:, off_q:off_gate], R, MEM_WIDTH)
        p_gate = _matmul(x, w_in_l[:, off_gate:in_cols], R, D)
        x_prev_s = state_shift[l]
        prev_rwkv = _matmul(x_prev_s, w_rwkv, SB, RWKV_COLS // 2)

        wup_pad = jnp.concatenate([w_up[l], jnp.zeros((LORA_A, RW), F32)], axis=0).astype(BF16)
        aup_pad = jnp.concatenate([jnp.zeros((LORA_W, RW), F32), a_up[l]], axis=0).astype(BF16)
        lora = (row2(w0[l]), wup_pad, row2(a0[l]), aup_pad, g_up[l].astype(BF16), row2(k_k[l]), row2(k_a[l]),
                row2(r_k[l]), _seg_ones())
        if first:
            pre = _rwkv_pre(True, npt, tps, p_rwkv, prev_rwkv, row2(mu_shift[l]), lora)
        else:
            wv_pad = jnp.pad(w_in_vmix[l - 1], ((0, 0), (0, LANES - LORA_V))).astype(BF16)
            p_vd = _matmul(x, wv_pad, R, LANES)
            prev_vd = _matmul(x_prev_s, wv_pad, SB, LANES)
            mu_v = jnp.pad(mu_vmix[l - 1], (0, LANES - LORA_V)).reshape(1, LANES)
            vup_pad = jnp.pad(v_up[l - 1], ((0, LANES - LORA_V), (0, 0))).astype(BF16)
            pre = _rwkv_pre(False, npt, tps, p_rwkv, prev_rwkv, row2(mu_shift[l]), lora, p_vd, prev_vd, mu_v,
                            v_first, row2(v0[l - 1]), vup_pad)
        r_, lw_, k_, v_, nkk_, bb_, g_, bonus_ = pre
        if first:
            v_first = v_

        y_p, s_p = _wkv_prompt(B, T, r_, lw_, k_, v_, nkk_, bb_)

        def lanes_b(a):
            return jnp.transpose(a[n_p:].reshape(ST, SB, RW), (0, 2, 1)).reshape(ST, HEADS, HEAD_DIM, SB)

        s0 = jnp.transpose(state_wkv[l].astype(F32), (1, 2, 3, 0))
        y_s, s_s = _wkv_sample(s0, lanes_b(r_), lanes_b(lw_), lanes_b(k_),
                               lanes_b(v_).reshape(ST, HEADS, HEAD_DIM, 1, SB), lanes_b(nkk_), lanes_b(bb_))
        y_s = jnp.transpose(y_s.reshape(ST, RW, SB), (0, 2, 1)).reshape(n_s, RW)
        s_s = jnp.transpose(s_s, (3, 0, 1, 2))
        z_a = _rwkv_post(npt, y_p, y_s, bonus_, g_, row2(lnx_g[l]), row2(lnx_b[l]))

        cvec = (conv_w[l], row2(conv_b[l]), row2(cn_g[l]), row2(cn_b[l]))
        zb_p, conv_p = _conv_prompt(B, T, p_glu, *cvec)
        st_t = jnp.swapaxes(state_conv[l], 0, 1)
        zb_s, glu_s = _conv_sample(st_t, p_glu[n_p:].reshape(ST, SB, 2 * CONV_CH), *cvec)
        zb_s = zb_s.reshape(n_s, CONV_CH)
        conv_s = jnp.concatenate([state_conv[l][:, ST:], jnp.swapaxes(glu_s, 0, 1)], axis=1)

        kv_p = _matmul(mem_rows, w_mem_kv[l].astype(BF16), M, 2 * MEM_WIDTH)
        kv3 = kv_p.reshape(B, M, 2 * MEM_WIDTH)
        zc_p = _attention(p_q[:n_p].reshape(B, T, MEM_WIDTH), kv3, kv3, 0, 1, 1, R).reshape(n_p, MEM_WIDTH)
        q_s = jnp.swapaxes(p_q[n_p:].reshape(ST, SB, MEM_WIDTH), 0, 1)
        q_s = jnp.pad(q_s, ((0, 0), (0, 8 - ST), (0, 0)))
        ck = cache_mem_k[l].reshape(SB, M, MEM_WIDTH)
        cv = cache_mem_v[l].reshape(SB, M, MEM_WIDTH)
        zc_s = _attention(q_s, ck, cv, 0, 0, 8, 8)[:, :ST]
        zc_s = jnp.swapaxes(zc_s, 0, 1).reshape(n_s, MEM_WIDTH)

        h = _merge(n_p, alpha, z_a, zb_p, zb_s, zc_p, zc_s, p_gate, x,
                   w_o_a[l].astype(BF16), w_o_b[l].astype(BF16), w_o_c[l].astype(BF16), w_out[l].astype(BF16),
                   row2(ln1_g[l]), row2(ln1_b[l]))

        ws13 = jnp.concatenate([w_s1[l], w_s3[l]], axis=1).astype(BF16)
        x_new = _moe(l, alpha, h, w_router[l].astype(BF16), row2(b_router[l]), w_e1, w_e3, w_e2,
                     ws13, w_s2[l].astype(BF16), row2(ln2_g[l]), row2(ln2_b[l]))

        outs['wkv_p'].append(s_p.astype(state_wkv.dtype))
        outs['shift_p'].append(x[:n_p].reshape(B, T, D)[:, -1].astype(state_shift.dtype))
        outs['conv_p'].append(conv_p.astype(state_conv.dtype))
        outs['mk'].append(kv3[..., :MEM_WIDTH].reshape(B, M, MEM_HEADS, MEM_HEAD_DIM).astype(cache_mem_k.dtype))
        outs['mv'].append(kv3[..., MEM_WIDTH:].reshape(B, M, MEM_HEADS, MEM_HEAD_DIM).astype(cache_mem_v.dtype))
        outs['wkv_s'].append(s_s.astype(state_wkv.dtype))
        outs['shift_s'].append(x[n_p + (ST - 1) * SB:].astype(state_shift.dtype))
        outs['conv_s'].append(conv_s.astype(state_conv.dtype))
        x = x_new

    y_p = x[:n_p].reshape(B, T, D)
    y_s = jnp.swapaxes(x[n_p:].reshape(ST, SB, D), 0, 1)
    return (y_p, y_s, jnp.stack(outs['wkv_p']), jnp.stack(outs['shift_p']), jnp.stack(outs['conv_p']),
            jnp.stack(outs['mk']), jnp.stack(outs['mv']), jnp.stack(outs['wkv_s']), jnp.stack(outs['shift_s']),
            jnp.stack(outs['conv_s']))
```

```python
import functools
import math

import jax
import jax.numpy as jnp
from jax import lax
from jax.experimental import pallas as pl
from jax.experimental.pallas import tpu as pltpu

F32 = jnp.float32
BF16 = jnp.bfloat16

HEADS = 8
HEAD_DIM = 64
RW = HEADS * HEAD_DIM
LORA_W = 64
LORA_A = 64
LORA_V = 32
LORA_G = 128
RWKV_COLS = 3 * RW + LORA_W + LORA_A + LORA_G
GN_EPS = HEAD_DIM * 1e-5
CONV_WIDTH = 31
CONV_BUF = CONV_WIDTH - 1
CONV_CH = 512
MEM_HEADS = 4
MEM_HEAD_DIM = 128
MEM_WIDTH = MEM_HEADS * MEM_HEAD_DIM
TOP_K = 8
ROUTED_SCALE = 2.5
LN_EPS = 1e-5
EXP_M05 = math.exp(-0.5)

LANES = 128
ROW_TILE = 512
MERGE_TILE = 256
CHUNK = 64
NEUMANN_BLOCK = 16
MOE_BLOCK = 512
DMA_QUEUES = 2
DISPATCH_TILE = 256
COMBINE_TILE = 128
VMEM_LIMIT = 48 * 1024 * 1024


def _cparams(*sem, vmem=VMEM_LIMIT):
    return pltpu.CompilerParams(dimension_semantics=tuple(sem), vmem_limit_bytes=vmem)


def _bdot(a, b):
    return jnp.dot(a.astype(BF16), b.astype(BF16), preferred_element_type=F32)


def _bdot_nt(a, b):
    return lax.dot_general(a.astype(BF16), b.astype(BF16), (((1,), (1,)), ((), ())),
                           preferred_element_type=F32)


def _sigmoid(x):
    return 1.0 / (1.0 + jnp.exp(-x))


def _silu(x):
    return x * _sigmoid(x)


def _layer_norm(x, g, b):
    mu = jnp.mean(x, axis=-1, keepdims=True)
    d = x - mu
    var = jnp.mean(d * d, axis=-1, keepdims=True)
    return d * lax.rsqrt(var + LN_EPS) * g + b


SEG_LANES = 256


def _seg_ones():
    i = jnp.arange(SEG_LANES)
    return ((i[:, None] // HEAD_DIM) == (i[None, :] // HEAD_DIM)).astype(BF16)


def _seg_sum(x, seg):
    hi = x.astype(BF16)
    lo = (x - hi.astype(F32)).astype(BF16)
    parts = []
    for j in range(x.shape[-1] // SEG_LANES):
        sl = slice(j * SEG_LANES, (j + 1) * SEG_LANES)
        parts.append(jnp.dot(hi[:, sl], seg, preferred_element_type=F32)
                     + jnp.dot(lo[:, sl], seg, preferred_element_type=F32))
    return jnp.concatenate(parts, axis=-1)


def _pack_pair(lo, hi):
    lo_b = lax.bitcast_convert_type(lo.astype(BF16).astype(F32), jnp.int32)
    hi_b = lax.bitcast_convert_type(hi.astype(BF16).astype(F32), jnp.int32)
    return lax.shift_right_logical(lo_b, jnp.full(lo_b.shape, 16, jnp.int32)) | (hi_b & jnp.int32(-65536))


def _unpack_pair(p):
    lo = lax.bitcast_convert_type(lax.shift_left(p, jnp.full(p.shape, 16, jnp.int32)), F32)
    hi = lax.bitcast_convert_type(p & jnp.int32(-65536), F32)
    return lo, hi


def _mm_kernel(x_ref, w_ref, o_ref):
    o_ref[...] = _bdot(x_ref[...], w_ref[...]).astype(o_ref.dtype)


def _matmul(x, w, tm, tn):
    m, k = x.shape
    n_cols = w.shape[1]
    assert m % tm == 0 and n_cols % tn == 0
    return pl.pallas_call(
        _mm_kernel,
        grid=(m // tm, n_cols // tn),
        in_specs=[pl.BlockSpec((tm, k), lambda i, j: (i, 0)),
                  pl.BlockSpec((k, tn), lambda i, j: (0, j))],
        out_specs=pl.BlockSpec((tm, tn), lambda i, j: (i, j)),
        out_shape=jax.ShapeDtypeStruct((m, n_cols), F32),
        compiler_params=_cparams("parallel", "parallel"),
        name=f"matmul_{n_cols}",
    )(x, w)


def _pre_kernel(first, npt, tps, *refs):
    if first:
        (p_ref, halo_ref, prev_ref, mu_ref,
         w0_ref, wup_ref, a0_ref, aup_ref, gup_ref, kk_ref, ka_ref, rk_ref, seg_ref,
         r_o, lw_o, k_o, v_o, nkk_o, bb_o, g_o, bon_o, sh_ref) = refs
    else:
        (p_ref, halo_ref, prev_ref, mu_ref, pv_ref, halov_ref, prevv_ref, muv_ref, vf_ref, v0_ref, vup_ref,
         w0_ref, wup_ref, a0_ref, aup_ref, gup_ref, kk_ref, ka_ref, rk_ref, seg_ref,
         r_o, lw_o, k_o, v_o, nkk_o, bb_o, g_o, bon_o, sh_ref, shv_ref) = refs
    i = pl.program_id(0)
    rows_n = p_ref.shape[0]

    def build_shift(src_ref, halo_r, prev_r, dst_ref):
        @pl.when(i < npt)
        def _():
            x = src_ref[...]
            rolled = pltpu.roll(x, 1, axis=0)
            last = halo_r[7:8, :]
            row0 = jnp.where((i % tps) == 0, jnp.zeros_like(last), last)
            rows = lax.broadcasted_iota(jnp.int32, x.shape, 0)
            dst_ref[...] = jnp.where(rows == 0, row0, rolled)

        @pl.when(i >= npt)
        def _():
            nb = prev_r.shape[0]
            dst_ref[0:nb, :] = prev_r[...]
            dst_ref[nb:rows_n, :] = src_ref[0:rows_n - nb, :]

    build_shift(p_ref, halo_ref, prev_ref, sh_ref)
    p = p_ref[...]
    p = p + (sh_ref[...] - p) * mu_ref[...]
    r = p[:, 0:RW]
    k = p[:, RW:2 * RW]
    v = p[:, 2 * RW:3 * RW]
    u = p[:, 3 * RW:3 * RW + LORA_W + LORA_A]
    gd = p[:, 3 * RW + LORA_W + LORA_A:RWKV_COLS]

    z = w0_ref[...] + _bdot(jnp.tanh(u), wup_ref[...])
    lw = -EXP_M05 * _sigmoid(z)
    a = _sigmoid(a0_ref[...] + _bdot(u, aup_ref[...]))
    g = _bdot(_sigmoid(gd), gup_ref[...])
    if not first:
        build_shift(pv_ref, halov_ref, prevv_ref, shv_ref)
        pv = pv_ref[...]
        pv = pv + (shv_ref[...] - pv) * muv_ref[...]
        mix = _sigmoid(v0_ref[...] + _bdot(pv, vup_ref[...]))
        v = v + (vf_ref[...] - v) * mix
    kk = k * kk_ref[...]
    seg = seg_ref[...]
    nrm = jnp.sqrt(_seg_sum(kk * kk, seg))
    kkn = kk / jnp.maximum(nrm, 1e-12)
    k2 = k * (1.0 + (a - 1.0) * ka_ref[...])
    bonus = _seg_sum(r * k2 * rk_ref[...], seg) * v
    r_o[...] = r
    lw_o[...] = lw
    k_o[...] = k2
    v_o[...] = v
    nkk_o[...] = -kkn
    bb_o[...] = kkn * a
    g_o[...] = g
    bon_o[...] = bonus


def _rwkv_pre(first, npt, tps, p_rwkv, prev_rwkv, mu, lora, p_vd=None, prev_vd=None, mu_v=None, v_first=None,
              v0=None, vup=None):
    n = p_rwkv.shape[0]
    R = ROW_TILE
    nt = n // R

    def row_spec(w):
        return pl.BlockSpec((R, w), lambda i: (i, 0))

    def halo_spec(w):
        return pl.BlockSpec((8, w), lambda i: (jnp.maximum(i * (R // 8) - 1, 0), 0))

    def full(a):
        return pl.BlockSpec(a.shape, lambda i: (0,) * a.ndim)

    ins = [p_rwkv, p_rwkv, prev_rwkv, mu]
    specs = [row_spec(RWKV_COLS), halo_spec(RWKV_COLS), full(prev_rwkv), full(mu)]
    scratch = [pltpu.VMEM((R, RWKV_COLS), F32)]
    if not first:
        ins += [p_vd, p_vd, prev_vd, mu_v, v_first, v0, vup]
        specs += [row_spec(LANES), halo_spec(LANES), full(prev_vd), full(mu_v), row_spec(RW), full(v0), full(vup)]
        scratch.append(pltpu.VMEM((R, LANES), F32))
    ins += list(lora)
    specs += [full(a) for a in lora]
    outs = pl.pallas_call(
        functools.partial(_pre_kernel, first, npt, tps),
        grid=(nt,),
        in_specs=specs,
        out_specs=[row_spec(RW)] * 8,
        out_shape=[jax.ShapeDtypeStruct((n, RW), F32)] * 8,
        scratch_shapes=scratch,
        compiler_params=_cparams("parallel"),
        name="rwkv_pre",
    )(*ins)
    return outs


def _wkv_chunk_kernel(nbat, *refs):
    in_refs = refs[:6 * nbat]
    y_ref, s_out_ref, s_ref = refs[6 * nbat:]
    c = pl.program_id(1)
    C = in_refs[0].shape[0]
    G = 4 * HEAD_DIM
    assert 4 * C == G

    @pl.when(c == 0)
    def _():
        s_ref[...] = jnp.zeros_like(s_ref)

    ri = lax.broadcasted_iota(jnp.int32, (G, G), 0)
    ci = lax.broadcasted_iota(jnp.int32, (G, G), 1)
    same_head = (ri // C) == (ci // HEAD_DIM)
    tril_s = same_head & ((ci % C) < (ri % C))
    tril_i = same_head & ((ci % C) <= (ri % C))
    diag_blk = (ri // NEUMANN_BLOCK) == (ci // NEUMANN_BLOCK)
    eye = (ri == ci).astype(F32)

    def stack(x):
        return jnp.where(same_head, jnp.concatenate([x, x, x, x], axis=0), 0.0).astype(BF16)

    def tile4(x):
        return jnp.concatenate([x, x, x, x], axis=0).astype(BF16)

    def collapse(x4):
        return x4[0:C] + x4[C:2 * C] + x4[2 * C:3 * C] + x4[3 * C:4 * C]

    for b in range(nbat):
        r_ref, lw_ref, k_ref, v_ref, nkk_ref, bb_ref = in_refs[6 * b:6 * b + 6]
        lw = lw_ref[...]
        rows = lax.broadcasted_iota(jnp.int32, lw.shape, 0)
        cs = lw
        s = 1
        while s < C:
            cs = cs + jnp.where(rows >= s, pltpu.roll(cs, s, axis=0), 0.0)
            s *= 2
        cs_last = cs[C - 1:C, :]
        e_neg = jnp.exp(-cs)
        e_tail = jnp.exp(cs_last - cs)
        kk_ = k_ref[...]
        bb_ = bb_ref[...]
        at = nkk_ref[...] * jnp.exp(cs - lw)
        rt = r_ref[...] * jnp.exp(cs)
        bt = bb_ * e_neg
        kt = kk_ * e_neg
        bw = bb_ * e_tail
        kw = kk_ * e_tail
        vv = v_ref[...]
        w_end = jnp.exp(cs_last)

        for g in range(RW // G):
            sl = slice(g * G, (g + 1) * G)
            at4, rt4, v4 = stack(at[:, sl]), stack(rt[:, sl]), stack(vv[:, sl])
            bt4, kt4 = tile4(bt[:, sl]), tile4(kt[:, sl])
            a_ab = jnp.where(tril_s, _bdot_nt(at4, bt4), 0.0)
            a_ak = jnp.where(tril_s, _bdot_nt(at4, kt4), 0.0).astype(BF16)
            a_rb = jnp.where(tril_i, _bdot_nt(rt4, bt4), 0.0).astype(BF16)
            a_rk = jnp.where(tril_i, _bdot_nt(rt4, kt4), 0.0).astype(BF16)
            a_d = jnp.where(diag_blk, a_ab, 0.0)
            a_o = a_ab - a_d
            a_db = a_d.astype(BF16)
            x2 = _bdot(a_db, a_db).astype(BF16)
            x4 = _bdot(x2, x2).astype(BF16)
            x8 = _bdot(x4, x4)
            t_d = eye + a_d
            t_d = t_d + _bdot(t_d, x2)
            t_d = t_d + _bdot(t_d, x4)
            t_d = t_d + _bdot(t_d, x8)
            t_db = t_d.astype(BF16)
            nn = _bdot(t_db, a_o).astype(BF16)
            n2 = _bdot(nn, nn)
            m1 = t_d + _bdot(nn, t_db)
            t_m = m1 + _bdot(n2, m1)
            s4 = s_ref[b, g]
            s4b = s4.astype(BF16)
            u4 = _bdot(t_m, _bdot_nt(at4, s4b) + _bdot(a_ak, v4))
            y4 = _bdot_nt(rt4, s4b) + _bdot(jnp.concatenate([a_rb, a_rk], axis=1),
                                            jnp.concatenate([u4.astype(BF16), v4], axis=0))
            y_ref[b, :, sl] = collapse(y4)
            uv_t = jnp.concatenate([collapse(u4), vv[:, sl]], axis=0).T
            upd = _bdot(uv_t, jnp.concatenate([bw[:, sl], kw[:, sl]], axis=0))
            s_ref[b, g] = s4 * w_end[:, sl] + jnp.where((ri // HEAD_DIM) == (ci // HEAD_DIM), upd, 0.0)

    @pl.when(c == pl.num_programs(1) - 1)
    def _():
        s_out_ref[...] = s_ref[...]


def _wkv_prompt(B, T, r, lw, k, v, nkk, bb):
    C = CHUNK
    nc = T // C
    G = 4 * HEAD_DIM
    nbat = 2 if B % 2 == 0 else 1
    specs = []
    for j in range(nbat):
        specs += [pl.BlockSpec((C, RW), lambda i, c, j=j: ((nbat * i + j) * nc + c, 0))] * 6
    y, s = pl.pallas_call(
        functools.partial(_wkv_chunk_kernel, nbat),
        grid=(B // nbat, nc),
        in_specs=specs,
        out_specs=[pl.BlockSpec((nbat, C, RW), lambda i, c: (i, c, 0)),
                   pl.BlockSpec((nbat, RW // G, G, G), lambda i, c: (i, 0, 0, 0))],
        out_shape=[jax.ShapeDtypeStruct((B, T, RW), F32),
                   jax.ShapeDtypeStruct((B, RW // G, G, G), F32)],
        scratch_shapes=[pltpu.VMEM((nbat, RW // G, G, G), F32)],
        compiler_params=_cparams("parallel", "arbitrary"),
        name="wkv_chunk",
    )(*([r, lw, k, v, nkk, bb] * nbat))
    y = y.reshape(B * T, RW)
    s = s.reshape(B, RW // G, 4, HEAD_DIM, 4, HEAD_DIM)
    idx = jnp.arange(4)
    s = s[:, :, idx, :, idx, :]
    s = jnp.transpose(s, (1, 2, 0, 3, 4)).reshape(B, HEADS, HEAD_DIM, HEAD_DIM)
    return y, s


def _wkv_seq_kernel(s0_ref, r_ref, lw_ref, k_ref, v_ref, nkk_ref, bb_ref, y_ref, s_ref):
    steps = r_ref.shape[0]

    def body(vi, carry):
        st = s0_ref[0, vi]
        for t in range(steps):
            sa = jnp.sum(st * nkk_ref[t, 0], axis=0, keepdims=True)
            st = st * jnp.exp(lw_ref[t, 0]) + sa * bb_ref[t, 0] + v_ref[t, 0, vi] * k_ref[t, 0]
            y_ref[t, 0, vi] = jnp.sum(st * r_ref[t, 0], axis=0, keepdims=True)
        s_ref[0, vi] = st
        return carry

    lax.fori_loop(0, HEAD_DIM, body, 0)


def _wkv_sample(s0, r, lw, k, v, nkk, bb):
    steps, _, _, nb = r.shape
    vec = pl.BlockSpec((steps, 1, HEAD_DIM, nb), lambda h: (0, h, 0, 0))
    vcol = pl.BlockSpec((steps, 1, HEAD_DIM, 1, nb), lambda h: (0, h, 0, 0, 0))
    st = pl.BlockSpec((1, HEAD_DIM, HEAD_DIM, nb), lambda h: (h, 0, 0, 0))
    y, s = pl.pallas_call(
        _wkv_seq_kernel,
        grid=(HEADS,),
        in_specs=[st, vec, vec, vec, vcol, vec, vec],
        out_specs=[vcol, st],
        out_shape=[jax.ShapeDtypeStruct((steps, HEADS, HEAD_DIM, 1, nb), F32),
                   jax.ShapeDtypeStruct(s0.shape, F32)],
        compiler_params=_cparams("parallel"),
        name="wkv_seq",
    )(s0, r, lw, k, v, nkk, bb)
    return y, s


def _tr_kernel(x_ref, o_ref):
    o_ref[...] = x_ref[...].T


def _transpose_layer(x3, layer, tc):
    _, a, c = x3.shape
    return pl.pallas_call(
        _tr_kernel,
        grid=(c // tc,),
        in_specs=[pl.BlockSpec((None, a, tc), lambda j: (layer, 0, j))],
        out_specs=pl.BlockSpec((tc, a), lambda j: (j, 0)),
        out_shape=jax.ShapeDtypeStruct((c, a), x3.dtype),
        compiler_params=_cparams("parallel"),
        name="transpose_in",
    )(x3)


def _transpose_2d(x, tc):
    c, a = x.shape
    return pl.pallas_call(
        _tr_kernel,
        grid=(c // tc,),
        in_specs=[pl.BlockSpec((tc, a), lambda j: (j, 0))],
        out_specs=pl.BlockSpec((a, tc), lambda j: (0, j)),
        out_shape=jax.ShapeDtypeStruct((a, c), x.dtype),
        compiler_params=_cparams("parallel"),
        name="transpose_out",
    )(x)


def _post_kernel(npt, yp_ref, ys_ref, bon_ref, g_ref, lg_ref, lb_ref, seg_ref, o_ref):
    i = pl.program_id(0)
    y = jnp.where(i < npt, yp_ref[...], ys_ref[...])
    seg = seg_ref[...]
    mu = _seg_sum(y, seg) * (1.0 / HEAD_DIM)
    d = y - mu
    var = _seg_sum(d * d, seg) * (1.0 / HEAD_DIM)
    yn = d * lax.rsqrt(var + GN_EPS) * lg_ref[...] + lb_ref[...]
    o_ref[...] = (yn + bon_ref[...]) * g_ref[...]


def _rwkv_post(npt, y_p, y_s, bonus, g, lnx_g, lnx_b):
    n = bonus.shape[0]
    R = ROW_TILE
    row = pl.BlockSpec((R, RW), lambda i: (i, 0))
    vec = pl.BlockSpec((1, RW), lambda i: (0, 0))
    return pl.pallas_call(
        functools.partial(_post_kernel, npt),
        grid=(n // R,),
        in_specs=[pl.BlockSpec((R, RW), lambda i: (jnp.minimum(i, npt - 1), 0)),
                  pl.BlockSpec((R, RW), lambda i: (0, 0)), row, row, vec, vec,
                  pl.BlockSpec((SEG_LANES, SEG_LANES), lambda i: (0, 0))],
        out_specs=row,
        out_shape=jax.ShapeDtypeStruct((n, RW), F32),
        compiler_params=_cparams("parallel"),
        name="rwkv_post",
    )(y_p, y_s, bonus, g, lnx_g, lnx_b, _seg_ones())


def _glu(p):
    return p[..., :CONV_CH] * _sigmoid(p[..., CONV_CH:])


def _conv_prompt_kernel(p_ref, halo_ref, w_ref, cb_ref, g_ref, b_ref, o_ref, tail_ref, ext_ref):
    j = pl.program_id(1)
    R = p_ref.shape[0]
    H = halo_ref.shape[0]
    glu = _glu(p_ref[...])
    halo = _glu(halo_ref[...])
    ext_ref[0:H, :] = jnp.where(j == 0, jnp.zeros_like(halo), halo)
    ext_ref[H:H + R, :] = glu
    acc = jnp.zeros((R, CONV_CH), F32)
    for t in range(CONV_WIDTH):
        acc = acc + ext_ref[pl.ds(H - CONV_BUF + t, R), :] * w_ref[t:t + 1, :]
    h = _layer_norm(acc + cb_ref[...], g_ref[...], b_ref[...])
    o_ref[...] = _silu(h)

    @pl.when(j == pl.num_programs(1) - 1)
    def _():
        tail_ref[0] = glu[R - H:R, :]


def _conv_prompt(B, T, p_glu, conv_w, conv_b, cn_g, cn_b):
    R = ROW_TILE
    H = 32
    tps = T // R
    vec = pl.BlockSpec((1, CONV_CH), lambda b, j: (0, 0))
    z, tail = pl.pallas_call(
        _conv_prompt_kernel,
        grid=(B, tps),
        in_specs=[pl.BlockSpec((R, 2 * CONV_CH), lambda b, j: (b * tps + j, 0)),
                  pl.BlockSpec((H, 2 * CONV_CH), lambda b, j: (jnp.maximum((b * tps + j) * (R // H) - 1, 0), 0)),
                  pl.BlockSpec((CONV_WIDTH, CONV_CH), lambda b, j: (0, 0)), vec, vec, vec],
        out_specs=[pl.BlockSpec((R, CONV_CH), lambda b, j: (b * tps + j, 0)),
                   pl.BlockSpec((1, H, CONV_CH), lambda b, j: (b, 0, 0))],
        out_shape=[jax.ShapeDtypeStruct((B * T, CONV_CH), F32),
                   jax.ShapeDtypeStruct((B, H, CONV_CH), F32)],
        scratch_shapes=[pltpu.VMEM((H + R, CONV_CH), F32)],
        compiler_params=_cparams("parallel", "arbitrary"),
        name="conv_prompt",
    )(p_glu, p_glu, conv_w, conv_b, cn_g, cn_b)
    return z, tail[:, H - CONV_BUF:, :]


def _conv_sample_kernel(st_ref, p_ref, w_ref, cb_ref, g_ref, b_ref, o_ref, glu_ref):
    steps = p_ref.shape[0]
    glu = [_glu(p_ref[t]) for t in range(steps)]
    for t in range(steps):
        glu_ref[t] = glu[t]
    for t in range(steps):
        acc = jnp.zeros(glu[0].shape, F32)
        for j in range(CONV_WIDTH):
            src = t + j
            x = st_ref[src] if src < CONV_BUF else glu[src - CONV_BUF]
            acc = acc + x * w_ref[j:j + 1, :]
        h = _layer_norm(acc + cb_ref[...], g_ref[...], b_ref[...])
        o_ref[t] = _silu(h)


def _conv_sample(state_t, p_glu, conv_w, conv_b, cn_g, cn_b):
    steps, nb, _ = p_glu.shape

    def full(a):
        return pl.BlockSpec(a.shape, lambda i: (0,) * a.ndim)

    ins = (state_t, p_glu, conv_w, conv_b, cn_g, cn_b)
    return pl.pallas_call(
        _conv_sample_kernel,
        grid=(1,),
        in_specs=[full(a) for a in ins],
        out_specs=[pl.BlockSpec((steps, nb, CONV_CH), lambda i: (0, 0, 0))] * 2,
        out_shape=[jax.ShapeDtypeStruct((steps, nb, CONV_CH), F32)] * 2,
        compiler_params=_cparams("arbitrary"),
        name="conv_sample",
    )(*ins)


def _attn_kernel(q_ref, k_ref, v_ref, o_ref):
    nb = q_ref.shape[0]
    scale = MEM_HEAD_DIM ** -0.5
    for bi in range(nb):
        for h in range(MEM_HEADS):
            sl = slice(h * MEM_HEAD_DIM, (h + 1) * MEM_HEAD_DIM)
            s = _bdot_nt(q_ref[bi, :, sl], k_ref[bi, :, sl]) * scale
            m = jnp.max(s, axis=-1, keepdims=True)
            e = jnp.exp(s - m)
            pr = e / jnp.sum(e, axis=-1, keepdims=True)
            o_ref[bi, :, sl] = _bdot(pr, v_ref[bi, :, sl])


def _attention(q, mk, mv, kcol, vcol, bb, tq):
    B, T, W = q.shape
    M = mk.shape[1]
    return pl.pallas_call(
        _attn_kernel,
        grid=(B // bb, T // tq),
        in_specs=[pl.BlockSpec((bb, tq, W), lambda b, j: (b, j, 0)),
                  pl.BlockSpec((bb, M, W), lambda b, j: (b, 0, kcol)),
                  pl.BlockSpec((bb, M, W), lambda b, j: (b, 0, vcol))],
        out_specs=pl.BlockSpec((bb, tq, W), lambda b, j: (b, j, 0)),
        out_shape=jax.ShapeDtypeStruct((B, T, W), F32),
        compiler_params=_cparams("parallel", "parallel"),
        name=f"mem_attention_{tq}",
    )(q, mk, mv)


def _merge_kernel(npt, alpha, za_ref, zbp_ref, zbs_ref, zcp_ref, zcs_ref, gate_ref, x_ref,
                  woa_ref, wob_ref, woc_ref, wout_ref, g_ref, b_ref, o_ref):
    i = pl.program_id(0)
    D = x_ref.shape[1]
    is_p = i < npt
    zb = jnp.where(is_p, zbp_ref[...], zbs_ref[...])
    zc = jnp.where(is_p, zcp_ref[...], zcs_ref[...])
    merged = (_sigmoid(gate_ref[:, 0:D]) * _bdot(za_ref[...], woa_ref[...])
              + _sigmoid(gate_ref[:, D:2 * D]) * _bdot(zb, wob_ref[...])
              + _sigmoid(gate_ref[:, 2 * D:3 * D]) * _bdot(zc, woc_ref[...]))
    h = alpha * x_ref[...] + _bdot(merged, wout_ref[...])
    o_ref[...] = _layer_norm(h, g_ref[...], b_ref[...])


def _merge(n_p, alpha, za, zb_p, zb_s, zc_p, zc_s, gates, x, woa, wob, woc, wout, ln_g, ln_b):
    n, D = x.shape
    R = MERGE_TILE
    npt = n_p // R

    def row(w):
        return pl.BlockSpec((R, w), lambda i: (i, 0))

    def prow(w):
        return pl.BlockSpec((R, w), lambda i: (jnp.minimum(i, npt - 1), 0))

    def srow(w):
        return pl.BlockSpec((R, w), lambda i: (jnp.maximum(i - npt, 0), 0))

    def full(a):
        return pl.BlockSpec(a.shape, lambda i: (0,) * a.ndim)

    return pl.pallas_call(
        functools.partial(_merge_kernel, npt, alpha),
        grid=(n // R,),
        in_specs=[row(RW), prow(CONV_CH), srow(CONV_CH), prow(MEM_WIDTH), srow(MEM_WIDTH), row(3 * D), row(D),
                  full(woa), full(wob), full(woc), full(wout), full(ln_g), full(ln_b)],
        out_specs=row(D),
        out_shape=jax.ShapeDtypeStruct((n, D), F32),
        compiler_params=_cparams("parallel"),
        name="branch_merge",
    )(za, zb_p, zb_s, zc_p, zc_s, gates, x, woa, wob, woc, wout, ln_g, ln_b)


def _router_kernel(h_ref, wr_ref, br_ref, idx_o, gate_o, rank_o, cnt_o, carry_ref):
    i = pl.program_id(0)
    R = h_ref.shape[0]
    E = wr_ref.shape[1]

    @pl.when(i == 0)
    def _():
        carry_ref[...] = jnp.zeros_like(carry_ref)

    scores = _sigmoid(_bdot(h_ref[...], wr_ref[...]))
    work = scores + br_ref[...]
    lane = lax.broadcasted_iota(jnp.int32, (R, E), 1).astype(F32)
    sel_mask = jnp.zeros((R, E), F32)
    idx_cols, sel_cols = [], []
    for _ in range(TOP_K):
        m = jnp.max(work, axis=-1, keepdims=True)
        first = jnp.min(jnp.where(work == m, lane, float(E)), axis=-1, keepdims=True)
        onehot = lane == first
        sel_cols.append(jnp.sum(jnp.where(onehot, scores, 0.0), axis=-1, keepdims=True))
        idx_cols.append(first)
        sel_mask = jnp.where(onehot, 1.0, sel_mask)
        work = jnp.where(onehot, -jnp.inf, work)
    total = sel_cols[0]
    for c in sel_cols[1:]:
        total = total + c
    ri = lax.broadcasted_iota(jnp.int32, (R, R), 0)
    ci = lax.broadcasted_iota(jnp.int32, (R, R), 1)
    before = jnp.dot((ci < ri).astype(BF16), sel_mask.astype(BF16), preferred_element_type=F32) + carry_ref[0:1, :]
    idx_out = jnp.zeros((R, E), F32)
    gate_out = jnp.zeros((R, E), F32)
    rank_out = jnp.zeros((R, E), F32)
    for kk in range(TOP_K):
        onehot = lane == idx_cols[kk]
        rank = jnp.sum(jnp.where(onehot, before, 0.0), axis=-1, keepdims=True)
        col = lane == float(kk)
        idx_out = jnp.where(col, idx_cols[kk], idx_out)
        gate_out = jnp.where(col, sel_cols[kk] / total * ROUTED_SCALE, gate_out)
        rank_out = jnp.where(col, rank, rank_out)
    idx_o[...] = idx_out
    gate_o[...] = gate_out
    rank_o[...] = rank_out
    carry_ref[0:1, :] = carry_ref[0:1, :] + jnp.sum(sel_mask, axis=0, keepdims=True)
    cnt_o[...] = carry_ref[...]


def _router(h, w_router, b_router):
    n, D = h.shape
    E = w_router.shape[1]
    assert E == LANES
    R = 256
    row = pl.BlockSpec((R, E), lambda i: (i, 0))
    return pl.pallas_call(
        _router_kernel,
        grid=(n // R,),
        in_specs=[pl.BlockSpec((R, D), lambda i: (i, 0)),
                  pl.BlockSpec((D, E), lambda i: (0, 0)),
                  pl.BlockSpec((1, E), lambda i: (0, 0))],
        out_specs=[row, row, row, pl.BlockSpec((8, E), lambda i: (0, 0))],
        out_shape=[jax.ShapeDtypeStruct((n, E), F32), jax.ShapeDtypeStruct((n, E), F32),
                   jax.ShapeDtypeStruct((n, E), F32), jax.ShapeDtypeStruct((8, E), F32)],
        scratch_shapes=[pltpu.VMEM((8, E), F32)],
        compiler_params=_cparams("arbitrary"),
        name="moe_router",
    )(h, w_router, b_router)


def _dest_kernel(idx_ref, rank_ref, start_ref, o_ref):
    R, E = idx_ref.shape
    lane = lax.broadcasted_iota(jnp.int32, (R, E), 1).astype(F32)
    idx = idx_ref[...]
    out = jnp.zeros((R, E), F32)
    start = start_ref[...]
    for kk in range(TOP_K):
        e_k = jnp.sum(jnp.where(lane == float(kk), idx, 0.0), axis=-1, keepdims=True)
        s_k = jnp.sum(jnp.where(lane == e_k, start, 0.0), axis=-1, keepdims=True)
        out = jnp.where(lane == float(kk), s_k, out)
    o_ref[...] = (out + rank_ref[...]).astype(jnp.int32)


def _dest(idx, rank, start):
    n, E = idx.shape
    R = 256
    row = pl.BlockSpec((R, E), lambda i: (i, 0))
    return pl.pallas_call(
        _dest_kernel,
        grid=(n // R,),
        in_specs=[row, row, pl.BlockSpec((1, E), lambda i: (0, 0))],
        out_specs=row,
        out_shape=jax.ShapeDtypeStruct((n, E), jnp.int32),
        compiler_params=_cparams("parallel"),
        name="moe_dest",
    )(idx, rank, start)


def _zero_tail_kernel(eb_ref, o_ref):
    o_ref[...] = jnp.zeros_like(o_ref)


def _zero_tails(end_blocks, n_slots, width):
    BM = MOE_BLOCK
    grid_spec = pltpu.PrefetchScalarGridSpec(
        num_scalar_prefetch=1,
        grid=(end_blocks.shape[0],),
        in_specs=[],
        out_specs=pl.BlockSpec((BM, width), lambda e, eb: (jnp.maximum(eb[e] - 1, 0), 0)),
    )
    return pl.pallas_call(
        _zero_tail_kernel,
        grid_spec=grid_spec,
        out_shape=jax.ShapeDtypeStruct((n_slots, width), jnp.int32),
        compiler_params=_cparams("arbitrary"),
        name="moe_zero_tails",
    )(end_blocks)


def _dispatch_kernel(h_ref, dest_hbm, xg_in, xg_hbm, pk_ref, dest_smem, sem_idx, sem):
    del xg_in
    i = pl.program_id(0)
    R, D = h_ref.shape
    n_idx = R * TOP_K
    cp = pltpu.make_async_copy(dest_hbm.at[pl.ds(i * n_idx, n_idx)], dest_smem, sem_idx)
    cp.start()
    pk_ref[...] = _pack_pair(h_ref[:, 0:D // 2], h_ref[:, D // 2:D])
    cp.wait()

    def row_copy(t, kk):
        d = dest_smem[t * TOP_K + kk]
        return pltpu.make_async_copy(pk_ref.at[pl.ds(t, 1), :], xg_hbm.at[pl.ds(d, 1), :], sem)

    def issue(t, c):
        for kk in range(TOP_K):
            row_copy(t, kk).start(priority=kk % DMA_QUEUES)
        return c

    lax.fori_loop(0, R, issue, 0)

    def drain(t, c):
        for kk in range(TOP_K):
            row_copy(t, kk).wait()
        return c

    lax.fori_loop(0, R, drain, 0)


def _dispatch(h, dest_flat, xg0):
    n, D = h.shape
    R = DISPATCH_TILE
    return pl.pallas_call(
        _dispatch_kernel,
        grid=(n // R,),
        in_specs=[pl.BlockSpec((R, D), lambda i: (i, 0)),
                  pl.BlockSpec(memory_space=pl.ANY),
                  pl.BlockSpec(memory_space=pl.ANY)],
        out_specs=pl.BlockSpec(memory_space=pl.ANY),
        out_shape=jax.ShapeDtypeStruct(xg0.shape, xg0.dtype),
        scratch_shapes=[pltpu.VMEM((R, D // 2), jnp.int32), pltpu.SMEM((R * TOP_K,), jnp.int32),
                        pltpu.SemaphoreType.DMA, pltpu.SemaphoreType.DMA],
        input_output_aliases={2: 0},
        compiler_params=_cparams("arbitrary"),
        name="moe_dispatch",
    )(h, dest_flat, xg0)


def _expert_kernel(be_ref, nu_ref, x_ref, w1_ref, w3_ref, w2_ref, o_ref, w13_s, w2_s):
    i = pl.program_id(0)
    F = w1_ref.shape[1]

    @pl.when(i < nu_ref[0])
    def _():
        prev = be_ref[jnp.maximum(i - 1, 0)]

        @pl.when((i == 0) | (be_ref[i] != prev))
        def _():
            w13_s[:, 0:F] = w1_ref[...].astype(BF16)
            w13_s[:, F:2 * F] = w3_ref[...].astype(BF16)
            w2_s[...] = w2_ref[...].astype(BF16)

        x_lo, x_hi = _unpack_pair(x_ref[...])
        half = x_ref.shape[1]
        h13 = (jnp.dot(x_lo.astype(BF16), w13_s[0:half, :], preferred_element_type=F32)
               + jnp.dot(x_hi.astype(BF16), w13_s[half:2 * half, :], preferred_element_type=F32))
        hh = _silu(h13[:, 0:F]) * h13[:, F:2 * F]
        y = jnp.dot(hh.astype(BF16), w2_s[...], preferred_element_type=F32)
        o_ref[...] = _pack_pair(y[:, 0:half], y[:, half:2 * half])


def _experts(layer, block_e, n_used, xg, w_e1, w_e3, w_e2):
    n_slots, half = xg.shape
    D = 2 * half
    F = w_e1.shape[3]
    BM = MOE_BLOCK
    nb = n_slots // BM

    def blk(i, be, nu):
        return (jnp.minimum(i, nu[0] - 1), 0)

    grid_spec = pltpu.PrefetchScalarGridSpec(
        num_scalar_prefetch=2,
        grid=(nb,),
        in_specs=[pl.BlockSpec((BM, half), blk),
                  pl.BlockSpec((None, None, D, F), lambda i, be, nu: (layer, be[i], 0, 0)),
                  pl.BlockSpec((None, None, D, F), lambda i, be, nu: (layer, be[i], 0, 0)),
                  pl.BlockSpec((None, None, F, D), lambda i, be, nu: (layer, be[i], 0, 0))],
        out_specs=pl.BlockSpec((BM, half), blk),
        scratch_shapes=[pltpu.VMEM((D, 2 * F), BF16), pltpu.VMEM((F, D), BF16)],
    )
    return pl.pallas_call(
        _expert_kernel,
        grid_spec=grid_spec,
        out_shape=jax.ShapeDtypeStruct((n_slots, half), jnp.int32),
        compiler_params=_cparams("arbitrary"),
        name="moe_experts",
    )(block_e, n_used, xg, w_e1, w_e3, w_e2)


def _combine_kernel(alpha, h_ref, gate_ref, dest_hbm, yg_hbm, ws13_ref, ws2_ref, g_ref, b_ref, o_ref,
                    buf, dest_smem, sem_idx, sem):
    i = pl.program_id(0)
    R = h_ref.shape[0]
    F = ws2_ref.shape[0]
    n_idx = R * TOP_K
    cp = pltpu.make_async_copy(dest_hbm.at[pl.ds(i * n_idx, n_idx)], dest_smem, sem_idx)
    cp.start()
    cp.wait()

    def row_copy(t, kk):
        d = dest_smem[t * TOP_K + kk]
        return pltpu.make_async_copy(yg_hbm.at[pl.ds(d, 1), :], buf.at[kk, pl.ds(t, 1), :], sem)

    def issue(t, c):
        for kk in range(TOP_K):
            row_copy(t, kk).start(priority=kk % DMA_QUEUES)
        return c

    lax.fori_loop(0, R, issue, 0)

    h = h_ref[...]
    h13 = _bdot(h, ws13_ref[...])
    shared = _bdot(_silu(h13[:, 0:F]) * h13[:, F:2 * F], ws2_ref[...])

    def drain(t, c):
        for kk in range(TOP_K):
            row_copy(t, kk).wait()
        return c

    lax.fori_loop(0, R, drain, 0)

    r_lo = jnp.zeros(buf.shape[1:], F32)
    r_hi = jnp.zeros(buf.shape[1:], F32)
    for kk in range(TOP_K):
        y_lo, y_hi = _unpack_pair(buf[kk])
        gk = gate_ref[:, kk:kk + 1]
        r_lo = r_lo + y_lo * gk
        r_hi = r_hi + y_hi * gk
    routed = jnp.concatenate([r_lo, r_hi], axis=-1)
    o_ref[...] = _layer_norm(alpha * h + (shared + routed), g_ref[...], b_ref[...])


def _combine(alpha, h, gate, dest_flat, yg, ws13, ws2, ln_g, ln_b):
    n, D = h.shape
    R = COMBINE_TILE

    def full(a):
        return pl.BlockSpec(a.shape, lambda i: (0,) * a.ndim)

    return pl.pallas_call(
        functools.partial(_combine_kernel, alpha),
        grid=(n // R,),
        in_specs=[pl.BlockSpec((R, D), lambda i: (i, 0)),
                  pl.BlockSpec((R, LANES), lambda i: (i, 0)),
                  pl.BlockSpec(memory_space=pl.ANY),
                  pl.BlockSpec(memory_space=pl.ANY),
                  full(ws13), full(ws2), full(ln_g), full(ln_b)],
        out_specs=pl.BlockSpec((R, D), lambda i: (i, 0)),
        out_shape=jax.ShapeDtypeStruct((n, D), F32),
        scratch_shapes=[pltpu.VMEM((TOP_K, R, D // 2), jnp.int32), pltpu.SMEM((R * TOP_K,), jnp.int32),
                        pltpu.SemaphoreType.DMA, pltpu.SemaphoreType.DMA],
        compiler_params=_cparams("arbitrary"),
        name="moe_combine",
    )(h, gate, dest_flat, yg, ws13, ws2, ln_g, ln_b)


def _moe(layer, alpha, h, w_router, b_router, w_e1, w_e3, w_e2, ws13, ws2, ln_g, ln_b):
    n, D = h.shape
    E = w_router.shape[1]
    BM = MOE_BLOCK
    idx, gate, rank, cnt = _router(h, w_router, b_router)
    counts = cnt[0].astype(jnp.int32)
    padded = (counts + BM - 1) // BM * BM
    end = jnp.cumsum(padded)
    start = (end - padded).astype(jnp.int32)
    n_blocks = (n * TOP_K + BM - 1) // BM + E
    block_e = jnp.minimum(jnp.searchsorted(end, jnp.arange(n_blocks, dtype=jnp.int32) * BM, side='right'),
                          E - 1).astype(jnp.int32)
    n_used = (end[-1] // BM).astype(jnp.int32).reshape(1)
    dest = _dest(idx, rank, start.astype(F32).reshape(1, E))
    dest_flat = dest[:, :TOP_K].reshape(-1)
    xg = _dispatch(h, dest_flat, _zero_tails((end // BM).astype(jnp.int32), n_blocks * BM, D // 2))
    yg = _experts(layer, block_e, n_used, xg, w_e1, w_e3, w_e2)
    return _combine(alpha, h, gate, dest_flat, yg, ws13, ws2, ln_g, ln_b)


def kernel(x_prompt, x_sample, state_wkv, state_shift, state_conv, cache_mem_k, cache_mem_v, mem_prompt, w_in, w_in_vmix, mu_shift, mu_vmix, w0, w_up, a0, a_up, v0, v_up, g_up, k_k, k_a, r_k, lnx_g, lnx_b, w_o_a, conv_w, conv_b, cn_g, cn_b, w_o_b, w_mem_kv, w_o_c, w_out, ln1_g, ln1_b, ln2_g, ln2_b, w_router, b_router, w_e1, w_e3, w_e2, w_s1, w_s3, w_s2):
    B, T, D = x_prompt.shape
    SB, ST, _ = x_sample.shape
    L = w_in.shape[0]
    M = mem_prompt.shape[1]
    R = ROW_TILE
    n_p, n_s = B * T, SB * ST
    n = n_p + n_s
    assert n_s == R and SB == LANES and T % R == 0 and T % CHUNK == 0 and D % LANES == 0
    npt, tps = n_p // R, T // R
    alpha = (2.0 * L) ** 0.25
    off_glu = RWKV_COLS
    off_q = off_glu + 2 * CONV_CH
    off_gate = off_q + MEM_WIDTH
    in_cols = off_gate + 3 * D

    x = jnp.concatenate([x_prompt.reshape(n_p, D), jnp.swapaxes(x_sample, 0, 1).reshape(n_s, D)], axis=0)
    mem_rows = mem_prompt.reshape(B * M, D)
    row2 = lambda a: a.reshape(1, -1)
    v_first = None
    outs = {k: [] for k in ('wkv_p', 'shift_p', 'conv_p', 'mk', 'mv', 'wkv_s', 'shift_s', 'conv_s')}

    for l in range(L):
        first = l == 0
        w_in_l = w_in[l].astype(BF16)
        w_rwkv = w_in_l[:, :off_glu]
        p_rwkv = _matmul(x, w_rwkv, R, RWKV_COLS // 2)
        p_glu = _matmul(x, w_in_l[:, off_glu:off_q], R, 2 * CONV_CH)
        p_q = _matmul(x, w_in_l[:, off_q:off_gate], R, MEM_WIDTH)
        p_gate = _matmul(x, w_in_l[:, off_gate:in_cols], R, D)
        x_prev_s = state_shift[l]
        prev_rwkv = _matmul(x_prev_s, w_rwkv, SB, RWKV_COLS // 2)

        wup_pad = jnp.concatenate([w_up[l], jnp.zeros((LORA_A, RW), F32)], axis=0).astype(BF16)
        aup_pad = jnp.concatenate([jnp.zeros((LORA_W, RW), F32), a_up[l]], axis=0).astype(BF16)
        lora = (row2(w0[l]), wup_pad, row2(a0[l]), aup_pad, g_up[l].astype(BF16), row2(k_k[l]), row2(k_a[l]),
                row2(r_k[l]), _seg_ones())
        if first:
            pre = _rwkv_pre(True, npt, tps, p_rwkv, prev_rwkv, row2(mu_shift[l]), lora)
        else:
            wv_pad = jnp.pad(w_in_vmix[l - 1], ((0, 0), (0, LANES - LORA_V))).astype(BF16)
            p_vd = _matmul(x, wv_pad, R, LANES)
            prev_vd = _matmul(x_prev_s, wv_pad, SB, LANES)
            mu_v = jnp.pad(mu_vmix[l - 1], (0, LANES - LORA_V)).reshape(1, LANES)
            vup_pad = jnp.pad(v_up[l - 1], ((0, LANES - LORA_V), (0, 0))).astype(BF16)
            pre = _rwkv_pre(False, npt, tps, p_rwkv, prev_rwkv, row2(mu_shift[l]), lora, p_vd, prev_vd, mu_v,
                            v_first, row2(v0[l - 1]), vup_pad)
        r_, lw_, k_, v_, nkk_, bb_, g_, bonus_ = pre
        if first:
            v_first = v_

        y_p, s_p = _wkv_prompt(B, T, r_, lw_, k_, v_, nkk_, bb_)

        def lanes_b(a):
            return jnp.transpose(a[n_p:].reshape(ST, SB, RW), (0, 2, 1)).reshape(ST, HEADS, HEAD_DIM, SB)

        state_sz = HEADS * HEAD_DIM * HEAD_DIM
        s0 = _transpose_layer(state_wkv.astype(F32).reshape(L, SB, state_sz), l, 2048)
        s0 = s0.reshape(HEADS, HEAD_DIM, HEAD_DIM, SB)
        y_s, s_s = _wkv_sample(s0, lanes_b(r_), lanes_b(lw_), lanes_b(k_),
                               lanes_b(v_).reshape(ST, HEADS, HEAD_DIM, 1, SB), lanes_b(nkk_), lanes_b(bb_))
        y_s = jnp.transpose(y_s.reshape(ST, RW, SB), (0, 2, 1)).reshape(n_s, RW)
        s_s = _transpose_2d(s_s.reshape(state_sz, SB), 2048).reshape(SB, HEADS, HEAD_DIM, HEAD_DIM)
        z_a = _rwkv_post(npt, y_p, y_s, bonus_, g_, row2(lnx_g[l]), row2(lnx_b[l]))

        cvec = (conv_w[l], row2(conv_b[l]), row2(cn_g[l]), row2(cn_b[l]))
        zb_p, conv_p = _conv_prompt(B, T, p_glu, *cvec)
        st_t = jnp.swapaxes(state_conv[l], 0, 1)
        zb_s, glu_s = _conv_sample(st_t, p_glu[n_p:].reshape(ST, SB, 2 * CONV_CH), *cvec)
        zb_s = zb_s.reshape(n_s, CONV_CH)
        conv_s = jnp.concatenate([state_conv[l][:, ST:], jnp.swapaxes(glu_s, 0, 1)], axis=1)

        kv_p = _matmul(mem_rows, w_mem_kv[l].astype(BF16), M, 2 * MEM_WIDTH)
        kv3 = kv_p.reshape(B, M, 2 * MEM_WIDTH)
        zc_p = _attention(p_q[:n_p].reshape(B, T, MEM_WIDTH), kv3, kv3, 0, 1, 1, R).reshape(n_p, MEM_WIDTH)
        q_s = jnp.swapaxes(p_q[n_p:].reshape(ST, SB, MEM_WIDTH), 0, 1)
        q_s = jnp.pad(q_s, ((0, 0), (0, 8 - ST), (0, 0)))
        ck = cache_mem_k[l].reshape(SB, M, MEM_WIDTH)
        cv = cache_mem_v[l].reshape(SB, M, MEM_WIDTH)
        zc_s = _attention(q_s, ck, cv, 0, 0, 8, 8)[:, :ST]
        zc_s = jnp.swapaxes(zc_s, 0, 1).reshape(n_s, MEM_WIDTH)

        h = _merge(n_p, alpha, z_a, zb_p, zb_s, zc_p, zc_s, p_gate, x,
                   w_o_a[l].astype(BF16), w_o_b[l].astype(BF16), w_o_c[l].astype(BF16), w_out[l].astype(BF16),
                   row2(ln1_g[l]), row2(ln1_b[l]))

        ws13 = jnp.concatenate([w_s1[l], w_s3[l]], axis=1).astype(BF16)
        x_new = _moe(l, alpha, h, w_router[l].astype(BF16), row2(b_router[l]), w_e1, w_e3, w_e2,
                     ws13, w_s2[l].astype(BF16), row2(ln2_g[l]), row2(ln2_b[l]))

        outs['wkv_p'].append(s_p.astype(state_wkv.dtype))
        outs['shift_p'].append(x[:n_p].reshape(B, T, D)[:, -1].astype(state_shift.dtype))
        outs['conv_p'].append(conv_p.astype(state_conv.dtype))
        outs['mk'].append(kv3[..., :MEM_WIDTH].reshape(B, M, MEM_HEADS, MEM_HEAD_DIM).astype(cache_mem_k.dtype))
        outs['mv'].append(kv3[..., MEM_WIDTH:].reshape(B, M, MEM_HEADS, MEM_HEAD_DIM).astype(cache_mem_v.dtype))
        outs['wkv_s'].append(s_s.astype(state_wkv.dtype))
        outs['shift_s'].append(x[n_p + (ST - 1) * SB:].astype(state_shift.dtype))
        outs['conv_s'].append(conv_s.astype(state_conv.dtype))
        x = x_new

    y_p = x[:n_p].reshape(B, T, D)
    y_s = jnp.swapaxes(x[n_p:].reshape(ST, SB, D), 0, 1)
    return (y_p, y_s, jnp.stack(outs['wkv_p']), jnp.stack(outs['shift_p']), jnp.stack(outs['conv_p']),
            jnp.stack(outs['mk']), jnp.stack(outs['mv']), jnp.stack(outs['wkv_s']), jnp.stack(outs['shift_s']),
            jnp.stack(outs['conv_s']))
```

```python
import functools
import math

import jax
import jax.numpy as jnp
from jax import lax
from jax.experimental import pallas as pl
from jax.experimental.pallas import tpu as pltpu
from jax.experimental.pallas import tpu_sc as plsc

F32 = jnp.float32
BF16 = jnp.bfloat16

HEADS = 8
HEAD_DIM = 64
RW = HEADS * HEAD_DIM
LORA_W = 64
LORA_A = 64
LORA_V = 32
LORA_G = 128
RWKV_COLS = 3 * RW + LORA_W + LORA_A + LORA_G
GN_EPS = HEAD_DIM * 1e-5
CONV_WIDTH = 31
CONV_BUF = CONV_WIDTH - 1
CONV_CH = 512
MEM_HEADS = 4
MEM_HEAD_DIM = 128
MEM_WIDTH = MEM_HEADS * MEM_HEAD_DIM
TOP_K = 8
ROUTED_SCALE = 2.5
LN_EPS = 1e-5
EXP_M05 = math.exp(-0.5)

LANES = 128
ROW_TILE = 512
MERGE_TILE = 256
CHUNK = 64
NEUMANN_BLOCK = 16
MOE_BLOCK = 512
DMA_QUEUES = 2
SC_CORES = 2
SC_SUBCORES = 16
SC_WINDOW = 128
DISPATCH_TILE = 256
COMBINE_TILE = 128
VMEM_LIMIT = 48 * 1024 * 1024


def _cparams(*sem, vmem=VMEM_LIMIT):
    return pltpu.CompilerParams(dimension_semantics=tuple(sem), vmem_limit_bytes=vmem)


def _bdot(a, b):
    return jnp.dot(a.astype(BF16), b.astype(BF16), preferred_element_type=F32)


def _bdot_nt(a, b):
    return lax.dot_general(a.astype(BF16), b.astype(BF16), (((1,), (1,)), ((), ())),
                           preferred_element_type=F32)


def _sigmoid(x):
    return 1.0 / (1.0 + jnp.exp(-x))


def _silu(x):
    return x * _sigmoid(x)


def _layer_norm(x, g, b):
    mu = jnp.mean(x, axis=-1, keepdims=True)
    d = x - mu
    var = jnp.mean(d * d, axis=-1, keepdims=True)
    return d * lax.rsqrt(var + LN_EPS) * g + b


SEG_LANES = 256


def _seg_ones():
    i = jnp.arange(SEG_LANES)
    return ((i[:, None] // HEAD_DIM) == (i[None, :] // HEAD_DIM)).astype(BF16)


def _seg_sum(x, seg):
    hi = x.astype(BF16)
    lo = (x - hi.astype(F32)).astype(BF16)
    parts = []
    for j in range(x.shape[-1] // SEG_LANES):
        sl = slice(j * SEG_LANES, (j + 1) * SEG_LANES)
        parts.append(jnp.dot(hi[:, sl], seg, preferred_element_type=F32)
                     + jnp.dot(lo[:, sl], seg, preferred_element_type=F32))
    return jnp.concatenate(parts, axis=-1)


def _pack_pair(lo, hi):
    lo_b = lax.bitcast_convert_type(lo.astype(BF16).astype(F32), jnp.int32)
    hi_b = lax.bitcast_convert_type(hi.astype(BF16).astype(F32), jnp.int32)
    return lax.shift_right_logical(lo_b, jnp.full(lo_b.shape, 16, jnp.int32)) | (hi_b & jnp.int32(-65536))


def _unpack_pair(p):
    lo = lax.bitcast_convert_type(lax.shift_left(p, jnp.full(p.shape, 16, jnp.int32)), F32)
    hi = lax.bitcast_convert_type(p & jnp.int32(-65536), F32)
    return lo, hi


def _mm_kernel(x_ref, w_ref, o_ref):
    o_ref[...] = _bdot(x_ref[...], w_ref[...]).astype(o_ref.dtype)


def _matmul(x, w, tm, tn):
    m, k = x.shape
    n_cols = w.shape[1]
    assert m % tm == 0 and n_cols % tn == 0
    return pl.pallas_call(
        _mm_kernel,
        grid=(m // tm, n_cols // tn),
        in_specs=[pl.BlockSpec((tm, k), lambda i, j: (i, 0)),
                  pl.BlockSpec((k, tn), lambda i, j: (0, j))],
        out_specs=pl.BlockSpec((tm, tn), lambda i, j: (i, j)),
        out_shape=jax.ShapeDtypeStruct((m, n_cols), F32),
        compiler_params=_cparams("parallel", "parallel"),
        name=f"matmul_{n_cols}",
    )(x, w)


def _pre_kernel(first, npt, tps, *refs):
    if first:
        (p_ref, halo_ref, prev_ref, mu_ref,
         w0_ref, wup_ref, a0_ref, aup_ref, gup_ref, kk_ref, ka_ref, rk_ref, seg_ref,
         r_o, lw_o, k_o, v_o, nkk_o, bb_o, g_o, bon_o, sh_ref) = refs
    else:
        (p_ref, halo_ref, prev_ref, mu_ref, pv_ref, halov_ref, prevv_ref, muv_ref, vf_ref, v0_ref, vup_ref,
         w0_ref, wup_ref, a0_ref, aup_ref, gup_ref, kk_ref, ka_ref, rk_ref, seg_ref,
         r_o, lw_o, k_o, v_o, nkk_o, bb_o, g_o, bon_o, sh_ref, shv_ref) = refs
    i = pl.program_id(0)
    rows_n = p_ref.shape[0]

    def build_shift(src_ref, halo_r, prev_r, dst_ref):
        @pl.when(i < npt)
        def _():
            x = src_ref[...]
            rolled = pltpu.roll(x, 1, axis=0)
            last = halo_r[7:8, :]
            row0 = jnp.where((i % tps) == 0, jnp.zeros_like(last), last)
            rows = lax.broadcasted_iota(jnp.int32, x.shape, 0)
            dst_ref[...] = jnp.where(rows == 0, row0, rolled)

        @pl.when(i >= npt)
        def _():
            nb = prev_r.shape[0]
            dst_ref[0:nb, :] = prev_r[...]
            dst_ref[nb:rows_n, :] = src_ref[0:rows_n - nb, :]

    build_shift(p_ref, halo_ref, prev_ref, sh_ref)
    p = p_ref[...]
    p = p + (sh_ref[...] - p) * mu_ref[...]
    r = p[:, 0:RW]
    k = p[:, RW:2 * RW]
    v = p[:, 2 * RW:3 * RW]
    u = p[:, 3 * RW:3 * RW + LORA_W + LORA_A]
    gd = p[:, 3 * RW + LORA_W + LORA_A:RWKV_COLS]

    z = w0_ref[...] + _bdot(jnp.tanh(u), wup_ref[...])
    lw = -EXP_M05 * _sigmoid(z)
    a = _sigmoid(a0_ref[...] + _bdot(u, aup_ref[...]))
    g = _bdot(_sigmoid(gd), gup_ref[...])
    if not first:
        build_shift(pv_ref, halov_ref, prevv_ref, shv_ref)
        pv = pv_ref[...]
        pv = pv + (shv_ref[...] - pv) * muv_ref[...]
        mix = _sigmoid(v0_ref[...] + _bdot(pv, vup_ref[...]))
        v = v + (vf_ref[...] - v) * mix
    kk = k * kk_ref[...]
    seg = seg_ref[...]
    nrm = jnp.sqrt(_seg_sum(kk * kk, seg))
    kkn = kk / jnp.maximum(nrm, 1e-12)
    k2 = k * (1.0 + (a - 1.0) * ka_ref[...])
    bonus = _seg_sum(r * k2 * rk_ref[...], seg) * v
    r_o[...] = r
    lw_o[...] = lw
    k_o[...] = k2
    v_o[...] = v
    nkk_o[...] = -kkn
    bb_o[...] = kkn * a
    g_o[...] = g
    bon_o[...] = bonus


def _rwkv_pre(first, npt, tps, p_rwkv, prev_rwkv, mu, lora, p_vd=None, prev_vd=None, mu_v=None, v_first=None,
              v0=None, vup=None):
    n = p_rwkv.shape[0]
    R = ROW_TILE
    nt = n // R

    def row_spec(w):
        return pl.BlockSpec((R, w), lambda i: (i, 0))

    def halo_spec(w):
        return pl.BlockSpec((8, w), lambda i: (jnp.maximum(i * (R // 8) - 1, 0), 0))

    def full(a):
        return pl.BlockSpec(a.shape, lambda i: (0,) * a.ndim)

    ins = [p_rwkv, p_rwkv, prev_rwkv, mu]
    specs = [row_spec(RWKV_COLS), halo_spec(RWKV_COLS), full(prev_rwkv), full(mu)]
    scratch = [pltpu.VMEM((R, RWKV_COLS), F32)]
    if not first:
        ins += [p_vd, p_vd, prev_vd, mu_v, v_first, v0, vup]
        specs += [row_spec(LANES), halo_spec(LANES), full(prev_vd), full(mu_v), row_spec(RW), full(v0), full(vup)]
        scratch.append(pltpu.VMEM((R, LANES), F32))
    ins += list(lora)
    specs += [full(a) for a in lora]
    outs = pl.pallas_call(
        functools.partial(_pre_kernel, first, npt, tps),
        grid=(nt,),
        in_specs=specs,
        out_specs=[row_spec(RW)] * 8,
        out_shape=[jax.ShapeDtypeStruct((n, RW), F32)] * 8,
        scratch_shapes=scratch,
        compiler_params=_cparams("parallel"),
        name="rwkv_pre",
    )(*ins)
    return outs


def _wkv_chunk_kernel(nbat, *refs):
    in_refs = refs[:6 * nbat]
    y_ref, s_out_ref, s_ref = refs[6 * nbat:]
    c = pl.program_id(1)
    C = in_refs[0].shape[0]
    G = 4 * HEAD_DIM
    assert 4 * C == G

    @pl.when(c == 0)
    def _():
        s_ref[...] = jnp.zeros_like(s_ref)

    ri = lax.broadcasted_iota(jnp.int32, (G, G), 0)
    ci = lax.broadcasted_iota(jnp.int32, (G, G), 1)
    same_head = (ri // C) == (ci // HEAD_DIM)
    tril_s = same_head & ((ci % C) < (ri % C))
    tril_i = same_head & ((ci % C) <= (ri % C))
    diag_blk = (ri // NEUMANN_BLOCK) == (ci // NEUMANN_BLOCK)
    eye = (ri == ci).astype(F32)

    def stack(x):
        return jnp.where(same_head, jnp.concatenate([x, x, x, x], axis=0), 0.0).astype(BF16)

    def tile4(x):
        return jnp.concatenate([x, x, x, x], axis=0).astype(BF16)

    def collapse(x4):
        return x4[0:C] + x4[C:2 * C] + x4[2 * C:3 * C] + x4[3 * C:4 * C]

    for b in range(nbat):
        r_ref, lw_ref, k_ref, v_ref, nkk_ref, bb_ref = in_refs[6 * b:6 * b + 6]
        lw = lw_ref[...]
        rows = lax.broadcasted_iota(jnp.int32, lw.shape, 0)
        cs = lw
        s = 1
        while s < C:
            cs = cs + jnp.where(rows >= s, pltpu.roll(cs, s, axis=0), 0.0)
            s *= 2
        cs_last = cs[C - 1:C, :]
        e_neg = jnp.exp(-cs)
        e_tail = jnp.exp(cs_last - cs)
        kk_ = k_ref[...]
        bb_ = bb_ref[...]
        at = nkk_ref[...] * jnp.exp(cs - lw)
        rt = r_ref[...] * jnp.exp(cs)
        bt = bb_ * e_neg
        kt = kk_ * e_neg
        bw = bb_ * e_tail
        kw = kk_ * e_tail
        vv = v_ref[...]
        w_end = jnp.exp(cs_last)

        for g in range(RW // G):
            sl = slice(g * G, (g + 1) * G)
            at4, rt4, v4 = stack(at[:, sl]), stack(rt[:, sl]), stack(vv[:, sl])
            bt4, kt4 = tile4(bt[:, sl]), tile4(kt[:, sl])
            a_ab = jnp.where(tril_s, _bdot_nt(at4, bt4), 0.0)
            a_ak = jnp.where(tril_s, _bdot_nt(at4, kt4), 0.0).astype(BF16)
            a_rb = jnp.where(tril_i, _bdot_nt(rt4, bt4), 0.0).astype(BF16)
            a_rk = jnp.where(tril_i, _bdot_nt(rt4, kt4), 0.0).astype(BF16)
            a_d = jnp.where(diag_blk, a_ab, 0.0)
            a_o = a_ab - a_d
            a_db = a_d.astype(BF16)
            x2 = _bdot(a_db, a_db).astype(BF16)
            x4 = _bdot(x2, x2).astype(BF16)
            x8 = _bdot(x4, x4)
            t_d = eye + a_d
            t_d = t_d + _bdot(t_d, x2)
            t_d = t_d + _bdot(t_d, x4)
            t_d = t_d + _bdot(t_d, x8)
            t_db = t_d.astype(BF16)
            nn = _bdot(t_db, a_o).astype(BF16)
            n2 = _bdot(nn, nn)
            m1 = t_d + _bdot(nn, t_db)
            t_m = m1 + _bdot(n2, m1)
            s4 = s_ref[b, g]
            s4b = s4.astype(BF16)
            u4 = _bdot(t_m, _bdot_nt(at4, s4b) + _bdot(a_ak, v4))
            y4 = _bdot_nt(rt4, s4b) + _bdot(jnp.concatenate([a_rb, a_rk], axis=1),
                                            jnp.concatenate([u4.astype(BF16), v4], axis=0))
            y_ref[b, :, sl] = collapse(y4)
            uv_t = jnp.concatenate([collapse(u4), vv[:, sl]], axis=0).T
            upd = _bdot(uv_t, jnp.concatenate([bw[:, sl], kw[:, sl]], axis=0))
            s_ref[b, g] = s4 * w_end[:, sl] + jnp.where((ri // HEAD_DIM) == (ci // HEAD_DIM), upd, 0.0)

    @pl.when(c == pl.num_programs(1) - 1)
    def _():
        s_out_ref[...] = s_ref[...]


def _wkv_prompt(B, T, r, lw, k, v, nkk, bb):
    C = CHUNK
    nc = T // C
    G = 4 * HEAD_DIM
    nbat = 2 if B % 2 == 0 else 1
    specs = []
    for j in range(nbat):
        specs += [pl.BlockSpec((C, RW), lambda i, c, j=j: ((nbat * i + j) * nc + c, 0))] * 6
    y, s = pl.pallas_call(
        functools.partial(_wkv_chunk_kernel, nbat),
        grid=(B // nbat, nc),
        in_specs=specs,
        out_specs=[pl.BlockSpec((nbat, C, RW), lambda i, c: (i, c, 0)),
                   pl.BlockSpec((nbat, RW // G, G, G), lambda i, c: (i, 0, 0, 0))],
        out_shape=[jax.ShapeDtypeStruct((B, T, RW), F32),
                   jax.ShapeDtypeStruct((B, RW // G, G, G), F32)],
        scratch_shapes=[pltpu.VMEM((nbat, RW // G, G, G), F32)],
        compiler_params=_cparams("parallel", "arbitrary"),
        name="wkv_chunk",
    )(*([r, lw, k, v, nkk, bb] * nbat))
    y = y.reshape(B * T, RW)
    s = s.reshape(B, RW // G, 4, HEAD_DIM, 4, HEAD_DIM)
    idx = jnp.arange(4)
    s = s[:, :, idx, :, idx, :]
    s = jnp.transpose(s, (1, 2, 0, 3, 4)).reshape(B, HEADS, HEAD_DIM, HEAD_DIM)
    return y, s


def _wkv_seq_kernel(s0_ref, r_ref, lw_ref, k_ref, v_ref, nkk_ref, bb_ref, y_ref, s_ref):
    steps = r_ref.shape[0]

    def body(vi, carry):
        st = s0_ref[0, vi]
        for t in range(steps):
            sa = jnp.sum(st * nkk_ref[t, 0], axis=0, keepdims=True)
            st = st * jnp.exp(lw_ref[t, 0]) + sa * bb_ref[t, 0] + v_ref[t, 0, vi] * k_ref[t, 0]
            y_ref[t, 0, vi] = jnp.sum(st * r_ref[t, 0], axis=0, keepdims=True)
        s_ref[0, vi] = st
        return carry

    lax.fori_loop(0, HEAD_DIM, body, 0)


def _wkv_sample(s0, r, lw, k, v, nkk, bb):
    steps, _, _, nb = r.shape
    vec = pl.BlockSpec((steps, 1, HEAD_DIM, nb), lambda h: (0, h, 0, 0))
    vcol = pl.BlockSpec((steps, 1, HEAD_DIM, 1, nb), lambda h: (0, h, 0, 0, 0))
    st = pl.BlockSpec((1, HEAD_DIM, HEAD_DIM, nb), lambda h: (h, 0, 0, 0))
    y, s = pl.pallas_call(
        _wkv_seq_kernel,
        grid=(HEADS,),
        in_specs=[st, vec, vec, vec, vcol, vec, vec],
        out_specs=[vcol, st],
        out_shape=[jax.ShapeDtypeStruct((steps, HEADS, HEAD_DIM, 1, nb), F32),
                   jax.ShapeDtypeStruct(s0.shape, F32)],
        compiler_params=_cparams("parallel"),
        name="wkv_seq",
    )(s0, r, lw, k, v, nkk, bb)
    return y, s


def _tr_kernel(x_ref, o_ref):
    o_ref[...] = x_ref[...].T


def _transpose_layer(x3, layer, tc):
    _, a, c = x3.shape
    return pl.pallas_call(
        _tr_kernel,
        grid=(c // tc,),
        in_specs=[pl.BlockSpec((None, a, tc), lambda j: (layer, 0, j))],
        out_specs=pl.BlockSpec((tc, a), lambda j: (j, 0)),
        out_shape=jax.ShapeDtypeStruct((c, a), x3.dtype),
        compiler_params=_cparams("parallel"),
        name="transpose_in",
    )(x3)


def _transpose_2d(x, tc):
    c, a = x.shape
    return pl.pallas_call(
        _tr_kernel,
        grid=(c // tc,),
        in_specs=[pl.BlockSpec((tc, a), lambda j: (j, 0))],
        out_specs=pl.BlockSpec((a, tc), lambda j: (0, j)),
        out_shape=jax.ShapeDtypeStruct((a, c), x.dtype),
        compiler_params=_cparams("parallel"),
        name="transpose_out",
    )(x)


def _post_kernel(npt, yp_ref, ys_ref, bon_ref, g_ref, lg_ref, lb_ref, seg_ref, o_ref):
    i = pl.program_id(0)
    y = jnp.where(i < npt, yp_ref[...], ys_ref[...])
    seg = seg_ref[...]
    mu = _seg_sum(y, seg) * (1.0 / HEAD_DIM)
    d = y - mu
    var = _seg_sum(d * d, seg) * (1.0 / HEAD_DIM)
    yn = d * lax.rsqrt(var + GN_EPS) * lg_ref[...] + lb_ref[...]
    o_ref[...] = (yn + bon_ref[...]) * g_ref[...]


def _rwkv_post(npt, y_p, y_s, bonus, g, lnx_g, lnx_b):
    n = bonus.shape[0]
    R = ROW_TILE
    row = pl.BlockSpec((R, RW), lambda i: (i, 0))
    vec = pl.BlockSpec((1, RW), lambda i: (0, 0))
    return pl.pallas_call(
        functools.partial(_post_kernel, npt),
        grid=(n // R,),
        in_specs=[pl.BlockSpec((R, RW), lambda i: (jnp.minimum(i, npt - 1), 0)),
                  pl.BlockSpec((R, RW), lambda i: (0, 0)), row, row, vec, vec,
                  pl.BlockSpec((SEG_LANES, SEG_LANES), lambda i: (0, 0))],
        out_specs=row,
        out_shape=jax.ShapeDtypeStruct((n, RW), F32),
        compiler_params=_cparams("parallel"),
        name="rwkv_post",
    )(y_p, y_s, bonus, g, lnx_g, lnx_b, _seg_ones())


def _glu(p):
    return p[..., :CONV_CH] * _sigmoid(p[..., CONV_CH:])


def _conv_prompt_kernel(p_ref, halo_ref, w_ref, cb_ref, g_ref, b_ref, o_ref, tail_ref, ext_ref):
    j = pl.program_id(1)
    R = p_ref.shape[0]
    H = halo_ref.shape[0]
    glu = _glu(p_ref[...])
    halo = _glu(halo_ref[...])
    ext_ref[0:H, :] = jnp.where(j == 0, jnp.zeros_like(halo), halo)
    ext_ref[H:H + R, :] = glu
    acc = jnp.zeros((R, CONV_CH), F32)
    for t in range(CONV_WIDTH):
        acc = acc + ext_ref[pl.ds(H - CONV_BUF + t, R), :] * w_ref[t:t + 1, :]
    h = _layer_norm(acc + cb_ref[...], g_ref[...], b_ref[...])
    o_ref[...] = _silu(h)

    @pl.when(j == pl.num_programs(1) - 1)
    def _():
        tail_ref[0] = glu[R - H:R, :]


def _conv_prompt(B, T, p_glu, conv_w, conv_b, cn_g, cn_b):
    R = ROW_TILE
    H = 32
    tps = T // R
    vec = pl.BlockSpec((1, CONV_CH), lambda b, j: (0, 0))
    z, tail = pl.pallas_call(
        _conv_prompt_kernel,
        grid=(B, tps),
        in_specs=[pl.BlockSpec((R, 2 * CONV_CH), lambda b, j: (b * tps + j, 0)),
                  pl.BlockSpec((H, 2 * CONV_CH), lambda b, j: (jnp.maximum((b * tps + j) * (R // H) - 1, 0), 0)),
                  pl.BlockSpec((CONV_WIDTH, CONV_CH), lambda b, j: (0, 0)), vec, vec, vec],
        out_specs=[pl.BlockSpec((R, CONV_CH), lambda b, j: (b * tps + j, 0)),
                   pl.BlockSpec((1, H, CONV_CH), lambda b, j: (b, 0, 0))],
        out_shape=[jax.ShapeDtypeStruct((B * T, CONV_CH), F32),
                   jax.ShapeDtypeStruct((B, H, CONV_CH), F32)],
        scratch_shapes=[pltpu.VMEM((H + R, CONV_CH), F32)],
        compiler_params=_cparams("parallel", "arbitrary"),
        name="conv_prompt",
    )(p_glu, p_glu, conv_w, conv_b, cn_g, cn_b)
    return z, tail[:, H - CONV_BUF:, :]


def _conv_sample_kernel(st_ref, p_ref, w_ref, cb_ref, g_ref, b_ref, o_ref, glu_ref):
    steps = p_ref.shape[0]
    glu = [_glu(p_ref[t]) for t in range(steps)]
    for t in range(steps):
        glu_ref[t] = glu[t]
    for t in range(steps):
        acc = jnp.zeros(glu[0].shape, F32)
        for j in range(CONV_WIDTH):
            src = t + j
            x = st_ref[src] if src < CONV_BUF else glu[src - CONV_BUF]
            acc = acc + x * w_ref[j:j + 1, :]
        h = _layer_norm(acc + cb_ref[...], g_ref[...], b_ref[...])
        o_ref[t] = _silu(h)


def _conv_sample(state_t, p_glu, conv_w, conv_b, cn_g, cn_b):
    steps, nb, _ = p_glu.shape

    def full(a):
        return pl.BlockSpec(a.shape, lambda i: (0,) * a.ndim)

    ins = (state_t, p_glu, conv_w, conv_b, cn_g, cn_b)
    return pl.pallas_call(
        _conv_sample_kernel,
        grid=(1,),
        in_specs=[full(a) for a in ins],
        out_specs=[pl.BlockSpec((steps, nb, CONV_CH), lambda i: (0, 0, 0))] * 2,
        out_shape=[jax.ShapeDtypeStruct((steps, nb, CONV_CH), F32)] * 2,
        compiler_params=_cparams("arbitrary"),
        name="conv_sample",
    )(*ins)


def _attn_kernel(q_ref, k_ref, v_ref, o_ref):
    nb = q_ref.shape[0]
    scale = MEM_HEAD_DIM ** -0.5
    for bi in range(nb):
        for h in range(MEM_HEADS):
            sl = slice(h * MEM_HEAD_DIM, (h + 1) * MEM_HEAD_DIM)
            s = _bdot_nt(q_ref[bi, :, sl], k_ref[bi, :, sl]) * scale
            m = jnp.max(s, axis=-1, keepdims=True)
            e = jnp.exp(s - m)
            pr = e / jnp.sum(e, axis=-1, keepdims=True)
            o_ref[bi, :, sl] = _bdot(pr, v_ref[bi, :, sl])


def _attention(q, mk, mv, kcol, vcol, bb, tq):
    B, T, W = q.shape
    M = mk.shape[1]
    return pl.pallas_call(
        _attn_kernel,
        grid=(B // bb, T // tq),
        in_specs=[pl.BlockSpec((bb, tq, W), lambda b, j: (b, j, 0)),
                  pl.BlockSpec((bb, M, W), lambda b, j: (b, 0, kcol)),
                  pl.BlockSpec((bb, M, W), lambda b, j: (b, 0, vcol))],
        out_specs=pl.BlockSpec((bb, tq, W), lambda b, j: (b, j, 0)),
        out_shape=jax.ShapeDtypeStruct((B, T, W), F32),
        compiler_params=_cparams("parallel", "parallel"),
        name=f"mem_attention_{tq}",
    )(q, mk, mv)


def _merge_kernel(npt, alpha, za_ref, zbp_ref, zbs_ref, zcp_ref, zcs_ref, gate_ref, x_ref,
                  woa_ref, wob_ref, woc_ref, wout_ref, g_ref, b_ref, o_ref):
    i = pl.program_id(0)
    D = x_ref.shape[1]
    is_p = i < npt
    zb = jnp.where(is_p, zbp_ref[...], zbs_ref[...])
    zc = jnp.where(is_p, zcp_ref[...], zcs_ref[...])
    merged = (_sigmoid(gate_ref[:, 0:D]) * _bdot(za_ref[...], woa_ref[...])
              + _sigmoid(gate_ref[:, D:2 * D]) * _bdot(zb, wob_ref[...])
              + _sigmoid(gate_ref[:, 2 * D:3 * D]) * _bdot(zc, woc_ref[...]))
    h = alpha * x_ref[...] + _bdot(merged, wout_ref[...])
    o_ref[...] = _layer_norm(h, g_ref[...], b_ref[...])


def _merge(n_p, alpha, za, zb_p, zb_s, zc_p, zc_s, gates, x, woa, wob, woc, wout, ln_g, ln_b):
    n, D = x.shape
    R = MERGE_TILE
    npt = n_p // R

    def row(w):
        return pl.BlockSpec((R, w), lambda i: (i, 0))

    def prow(w):
        return pl.BlockSpec((R, w), lambda i: (jnp.minimum(i, npt - 1), 0))

    def srow(w):
        return pl.BlockSpec((R, w), lambda i: (jnp.maximum(i - npt, 0), 0))

    def full(a):
        return pl.BlockSpec(a.shape, lambda i: (0,) * a.ndim)

    return pl.pallas_call(
        functools.partial(_merge_kernel, npt, alpha),
        grid=(n // R,),
        in_specs=[row(RW), prow(CONV_CH), srow(CONV_CH), prow(MEM_WIDTH), srow(MEM_WIDTH), row(3 * D), row(D),
                  full(woa), full(wob), full(woc), full(wout), full(ln_g), full(ln_b)],
        out_specs=row(D),
        out_shape=jax.ShapeDtypeStruct((n, D), F32),
        compiler_params=_cparams("parallel"),
        name="branch_merge",
    )(za, zb_p, zb_s, zc_p, zc_s, gates, x, woa, wob, woc, wout, ln_g, ln_b)


def _router_kernel(h_ref, wr_ref, br_ref, idx_o, gate_o, rank_o, cnt_o, carry_ref):
    i = pl.program_id(0)
    R = h_ref.shape[0]
    E = wr_ref.shape[1]

    @pl.when(i == 0)
    def _():
        carry_ref[...] = jnp.zeros_like(carry_ref)

    scores = _sigmoid(_bdot(h_ref[...], wr_ref[...]))
    work = scores + br_ref[...]
    lane = lax.broadcasted_iota(jnp.int32, (R, E), 1).astype(F32)
    sel_mask = jnp.zeros((R, E), F32)
    idx_cols, sel_cols = [], []
    for _ in range(TOP_K):
        m = jnp.max(work, axis=-1, keepdims=True)
        first = jnp.min(jnp.where(work == m, lane, float(E)), axis=-1, keepdims=True)
        onehot = lane == first
        sel_cols.append(jnp.sum(jnp.where(onehot, scores, 0.0), axis=-1, keepdims=True))
        idx_cols.append(first)
        sel_mask = jnp.where(onehot, 1.0, sel_mask)
        work = jnp.where(onehot, -jnp.inf, work)
    total = sel_cols[0]
    for c in sel_cols[1:]:
        total = total + c
    ri = lax.broadcasted_iota(jnp.int32, (R, R), 0)
    ci = lax.broadcasted_iota(jnp.int32, (R, R), 1)
    before = jnp.dot((ci < ri).astype(BF16), sel_mask.astype(BF16), preferred_element_type=F32) + carry_ref[0:1, :]
    idx_out = jnp.zeros((R, E), F32)
    gate_out = jnp.zeros((R, E), F32)
    rank_out = jnp.zeros((R, E), F32)
    for kk in range(TOP_K):
        onehot = lane == idx_cols[kk]
        rank = jnp.sum(jnp.where(onehot, before, 0.0), axis=-1, keepdims=True)
        col = lane == float(kk)
        idx_out = jnp.where(col, idx_cols[kk], idx_out)
        gate_out = jnp.where(col, sel_cols[kk] / total * ROUTED_SCALE, gate_out)
        rank_out = jnp.where(col, rank, rank_out)
    idx_o[...] = idx_out
    gate_o[...] = gate_out
    rank_o[...] = rank_out
    carry_ref[0:1, :] = carry_ref[0:1, :] + jnp.sum(sel_mask, axis=0, keepdims=True)
    cnt_o[...] = carry_ref[...]


def _router(h, w_router, b_router):
    n, D = h.shape
    E = w_router.shape[1]
    assert E == LANES
    R = 256
    row = pl.BlockSpec((R, E), lambda i: (i, 0))
    return pl.pallas_call(
        _router_kernel,
        grid=(n // R,),
        in_specs=[pl.BlockSpec((R, D), lambda i: (i, 0)),
                  pl.BlockSpec((D, E), lambda i: (0, 0)),
                  pl.BlockSpec((1, E), lambda i: (0, 0))],
        out_specs=[row, row, row, pl.BlockSpec((8, E), lambda i: (0, 0))],
        out_shape=[jax.ShapeDtypeStruct((n, E), F32), jax.ShapeDtypeStruct((n, E), F32),
                   jax.ShapeDtypeStruct((n, E), F32), jax.ShapeDtypeStruct((8, E), F32)],
        scratch_shapes=[pltpu.VMEM((8, E), F32)],
        compiler_params=_cparams("arbitrary"),
        name="moe_router",
    )(h, w_router, b_router)


def _dest_kernel(idx_ref, rank_ref, start_ref, o_ref):
    R, E = idx_ref.shape
    lane = lax.broadcasted_iota(jnp.int32, (R, E), 1).astype(F32)
    idx = idx_ref[...]
    out = jnp.zeros((R, E), F32)
    start = start_ref[...]
    for kk in range(TOP_K):
        e_k = jnp.sum(jnp.where(lane == float(kk), idx, 0.0), axis=-1, keepdims=True)
        s_k = jnp.sum(jnp.where(lane == e_k, start, 0.0), axis=-1, keepdims=True)
        out = jnp.where(lane == float(kk), s_k, out)
    o_ref[...] = (out + rank_ref[...]).astype(jnp.int32)


def _dest(idx, rank, start):
    n, E = idx.shape
    R = 256
    row = pl.BlockSpec((R, E), lambda i: (i, 0))
    return pl.pallas_call(
        _dest_kernel,
        grid=(n // R,),
        in_specs=[row, row, pl.BlockSpec((1, E), lambda i: (0, 0))],
        out_specs=row,
        out_shape=jax.ShapeDtypeStruct((n, E), jnp.int32),
        compiler_params=_cparams("parallel"),
        name="moe_dest",
    )(idx, rank, start)


def _zero_tail_kernel(eb_ref, o_ref):
    o_ref[...] = jnp.zeros_like(o_ref)


def _zero_tails(end_blocks, n_slots, width):
    BM = MOE_BLOCK
    grid_spec = pltpu.PrefetchScalarGridSpec(
        num_scalar_prefetch=1,
        grid=(end_blocks.shape[0],),
        in_specs=[],
        out_specs=pl.BlockSpec((BM, width), lambda e, eb: (jnp.maximum(eb[e] - 1, 0), 0)),
    )
    return pl.pallas_call(
        _zero_tail_kernel,
        grid_spec=grid_spec,
        out_shape=jax.ShapeDtypeStruct((n_slots, width), jnp.int32),
        compiler_params=_cparams("arbitrary"),
        name="moe_zero_tails",
    )(end_blocks)


def _dispatch_kernel(h_ref, dest_hbm, xg_in, xg_hbm, pk_ref, dest_smem, sem_idx, sem):
    del xg_in
    i = pl.program_id(0)
    R, D = h_ref.shape
    n_idx = R * TOP_K
    cp = pltpu.make_async_copy(dest_hbm.at[pl.ds(i * n_idx, n_idx)], dest_smem, sem_idx)
    cp.start()
    pk_ref[...] = _pack_pair(h_ref[:, 0:D // 2], h_ref[:, D // 2:D])
    cp.wait()

    def row_copy(t, kk):
        d = dest_smem[t * TOP_K + kk]
        return pltpu.make_async_copy(pk_ref.at[pl.ds(t, 1), :], xg_hbm.at[pl.ds(d, 1), :], sem)

    def issue(t, c):
        for kk in range(TOP_K):
            row_copy(t, kk).start(priority=kk % DMA_QUEUES)
        return c

    lax.fori_loop(0, R, issue, 0)

    def drain(t, c):
        for kk in range(TOP_K):
            row_copy(t, kk).wait()
        return c

    lax.fori_loop(0, R, drain, 0)


def _dispatch(h, dest_flat, xg0):
    n, D = h.shape
    R = DISPATCH_TILE
    return pl.pallas_call(
        _dispatch_kernel,
        grid=(n // R,),
        in_specs=[pl.BlockSpec((R, D), lambda i: (i, 0)),
                  pl.BlockSpec(memory_space=pl.ANY),
                  pl.BlockSpec(memory_space=pl.ANY)],
        out_specs=pl.BlockSpec(memory_space=pl.ANY),
        out_shape=jax.ShapeDtypeStruct(xg0.shape, xg0.dtype),
        scratch_shapes=[pltpu.VMEM((R, D // 2), jnp.int32), pltpu.SMEM((R * TOP_K,), jnp.int32),
                        pltpu.SemaphoreType.DMA, pltpu.SemaphoreType.DMA],
        input_output_aliases={2: 0},
        compiler_params=_cparams("arbitrary"),
        name="moe_dispatch",
    )(h, dest_flat, xg0)


def _expert_kernel(be_ref, nu_ref, x_ref, w1_ref, w3_ref, w2_ref, o_ref, w13_s, w2_s):
    i = pl.program_id(0)
    F = w1_ref.shape[1]

    @pl.when(i < nu_ref[0])
    def _():
        prev = be_ref[jnp.maximum(i - 1, 0)]

        @pl.when((i == 0) | (be_ref[i] != prev))
        def _():
            w13_s[:, 0:F] = w1_ref[...].astype(BF16)
            w13_s[:, F:2 * F] = w3_ref[...].astype(BF16)
            w2_s[...] = w2_ref[...].astype(BF16)

        x_lo, x_hi = _unpack_pair(x_ref[...])
        half = x_ref.shape[1]
        h13 = (jnp.dot(x_lo.astype(BF16), w13_s[0:half, :], preferred_element_type=F32)
               + jnp.dot(x_hi.astype(BF16), w13_s[half:2 * half, :], preferred_element_type=F32))
        hh = _silu(h13[:, 0:F]) * h13[:, F:2 * F]
        y = jnp.dot(hh.astype(BF16), w2_s[...], preferred_element_type=F32)
        o_ref[...] = _pack_pair(y[:, 0:half], y[:, half:2 * half])


def _experts(layer, block_e, n_used, xg, w_e1, w_e3, w_e2):
    n_slots, half = xg.shape
    D = 2 * half
    F = w_e1.shape[3]
    BM = MOE_BLOCK
    nb = n_slots // BM

    def blk(i, be, nu):
        return (jnp.minimum(i, nu[0] - 1), 0)

    grid_spec = pltpu.PrefetchScalarGridSpec(
        num_scalar_prefetch=2,
        grid=(nb,),
        in_specs=[pl.BlockSpec((BM, half), blk),
                  pl.BlockSpec((None, None, D, F), lambda i, be, nu: (layer, be[i], 0, 0)),
                  pl.BlockSpec((None, None, D, F), lambda i, be, nu: (layer, be[i], 0, 0)),
                  pl.BlockSpec((None, None, F, D), lambda i, be, nu: (layer, be[i], 0, 0))],
        out_specs=pl.BlockSpec((BM, half), blk),
        scratch_shapes=[pltpu.VMEM((D, 2 * F), BF16), pltpu.VMEM((F, D), BF16)],
    )
    return pl.pallas_call(
        _expert_kernel,
        grid_spec=grid_spec,
        out_shape=jax.ShapeDtypeStruct((n_slots, half), jnp.int32),
        compiler_params=_cparams("arbitrary"),
        name="moe_experts",
    )(block_e, n_used, xg, w_e1, w_e3, w_e2)


def _sc_gather(table, idx):
    n_idx = idx.shape[0]
    width = table.shape[1]
    workers = SC_CORES * SC_SUBCORES
    per_w = n_idx // workers
    assert per_w * workers == n_idx and per_w % SC_WINDOW == 0
    mesh = plsc.VectorSubcoreMesh(core_axis_name="c", subcore_axis_name="s",
                                  num_cores=SC_CORES, num_subcores=SC_SUBCORES)

    @functools.partial(
        pl.kernel, mesh=mesh,
        out_type=jax.ShapeDtypeStruct((n_idx, width), table.dtype),
        scratch_types=[pltpu.VMEM((SC_WINDOW,), jnp.int32),
                       pltpu.VMEM((SC_WINDOW, width), table.dtype),
                       pltpu.SemaphoreType.DMA],
        name="moe_sc_gather",
    )
    def gather(table_hbm, idx_hbm, out_hbm, idx_v, rows_v, sem):
        wid = lax.axis_index("s") * SC_CORES + lax.axis_index("c")

        @pl.loop(0, per_w // SC_WINDOW)
        def _(j):
            base = pl.multiple_of(wid * per_w + j * SC_WINDOW, SC_WINDOW)
            pltpu.sync_copy(idx_hbm.at[pl.ds(base, SC_WINDOW)], idx_v)
            pltpu.async_copy(table_hbm.at[idx_v], rows_v, sem).wait()
            pltpu.sync_copy(rows_v, out_hbm.at[pl.ds(base, SC_WINDOW)])

    return gather(table, idx)


def _combine_kernel(alpha, h_ref, gate_ref, yt_ref, ws13_ref, ws2_ref, g_ref, b_ref, o_ref):
    F = ws2_ref.shape[0]
    half = h_ref.shape[1] // 2
    h = h_ref[...]
    h13 = _bdot(h, ws13_ref[...])
    shared = _bdot(_silu(h13[:, 0:F]) * h13[:, F:2 * F], ws2_ref[...])
    r_lo = jnp.zeros((h.shape[0], half), F32)
    r_hi = jnp.zeros((h.shape[0], half), F32)
    for kk in range(TOP_K):
        y_lo, y_hi = _unpack_pair(yt_ref[:, kk * half:(kk + 1) * half])
        gk = gate_ref[:, kk:kk + 1]
        r_lo = r_lo + y_lo * gk
        r_hi = r_hi + y_hi * gk
    routed = jnp.concatenate([r_lo, r_hi], axis=-1)
    o_ref[...] = _layer_norm(alpha * h + (shared + routed), g_ref[...], b_ref[...])


def _combine(alpha, h, gate, yt, ws13, ws2, ln_g, ln_b):
    n, D = h.shape
    R = COMBINE_TILE

    def full(a):
        return pl.BlockSpec(a.shape, lambda i: (0,) * a.ndim)

    return pl.pallas_call(
        functools.partial(_combine_kernel, alpha),
        grid=(n // R,),
        in_specs=[pl.BlockSpec((R, D), lambda i: (i, 0)),
                  pl.BlockSpec((R, LANES), lambda i: (i, 0)),
                  pl.BlockSpec((R, TOP_K * D // 2), lambda i: (i, 0)),
                  full(ws13), full(ws2), full(ln_g), full(ln_b)],
        out_specs=pl.BlockSpec((R, D), lambda i: (i, 0)),
        out_shape=jax.ShapeDtypeStruct((n, D), F32),
        compiler_params=_cparams("parallel"),
        name="moe_combine",
    )(h, gate, yt, ws13, ws2, ln_g, ln_b)


def _moe(layer, alpha, h, w_router, b_router, w_e1, w_e3, w_e2, ws13, ws2, ln_g, ln_b):
    n, D = h.shape
    E = w_router.shape[1]
    BM = MOE_BLOCK
    idx, gate, rank, cnt = _router(h, w_router, b_router)
    counts = cnt[0].astype(jnp.int32)
    padded = (counts + BM - 1) // BM * BM
    end = jnp.cumsum(padded)
    start = (end - padded).astype(jnp.int32)
    n_blocks = (n * TOP_K + BM - 1) // BM + E
    block_e = jnp.minimum(jnp.searchsorted(end, jnp.arange(n_blocks, dtype=jnp.int32) * BM, side='right'),
                          E - 1).astype(jnp.int32)
    n_used = (end[-1] // BM).astype(jnp.int32).reshape(1)
    dest = _dest(idx, rank, start.astype(F32).reshape(1, E))
    dest_flat = dest[:, :TOP_K].reshape(-1)
    xg = _dispatch(h, dest_flat, _zero_tails((end // BM).astype(jnp.int32), n_blocks * BM, D // 2))
    yg = _experts(layer, block_e, n_used, xg, w_e1, w_e3, w_e2)
    yt = _sc_gather(yg, dest_flat).reshape(n, TOP_K * D // 2)
    return _combine(alpha, h, gate, yt, ws13, ws2, ln_g, ln_b)


def kernel(x_prompt, x_sample, state_wkv, state_shift, state_conv, cache_mem_k, cache_mem_v, mem_prompt, w_in, w_in_vmix, mu_shift, mu_vmix, w0, w_up, a0, a_up, v0, v_up, g_up, k_k, k_a, r_k, lnx_g, lnx_b, w_o_a, conv_w, conv_b, cn_g, cn_b, w_o_b, w_mem_kv, w_o_c, w_out, ln1_g, ln1_b, ln2_g, ln2_b, w_router, b_router, w_e1, w_e3, w_e2, w_s1, w_s3, w_s2):
    B, T, D = x_prompt.shape
    SB, ST, _ = x_sample.shape
    L = w_in.shape[0]
    M = mem_prompt.shape[1]
    R = ROW_TILE
    n_p, n_s = B * T, SB * ST
    n = n_p + n_s
    assert n_s == R and SB == LANES and T % R == 0 and T % CHUNK == 0 and D % LANES == 0
    npt, tps = n_p // R, T // R
    alpha = (2.0 * L) ** 0.25
    off_glu = RWKV_COLS
    off_q = off_glu + 2 * CONV_CH
    off_gate = off_q + MEM_WIDTH
    in_cols = off_gate + 3 * D

    x = jnp.concatenate([x_prompt.reshape(n_p, D), jnp.swapaxes(x_sample, 0, 1).reshape(n_s, D)], axis=0)
    mem_rows = mem_prompt.reshape(B * M, D)
    row2 = lambda a: a.reshape(1, -1)
    v_first = None
    outs = {k: [] for k in ('wkv_p', 'shift_p', 'conv_p', 'mk', 'mv', 'wkv_s', 'shift_s', 'conv_s')}

    for l in range(L):
        first = l == 0
        w_in_l = w_in[l].astype(BF16)
        w_rwkv = w_in_l[:, :off_glu]
        p_rwkv = _matmul(x, w_rwkv, R, RWKV_COLS // 2)
        p_glu = _matmul(x, w_in_l[:, off_glu:off_q], R, 2 * CONV_CH)
        p_q = _matmul(x, w_in_l[:, off_q:off_gate], R, MEM_WIDTH)
        p_gate = _matmul(x, w_in_l[:, off_gate:in_cols], R, D)
        x_prev_s = state_shift[l]
        prev_rwkv = _matmul(x_prev_s, w_rwkv, SB, RWKV_COLS // 2)

        wup_pad = jnp.concatenate([w_up[l], jnp.zeros((LORA_A, RW), F32)], axis=0).astype(BF16)
        aup_pad = jnp.concatenate([jnp.zeros((LORA_W, RW), F32), a_up[l]], axis=0).astype(BF16)
        lora = (row2(w0[l]), wup_pad, row2(a0[l]), aup_pad, g_up[l].astype(BF16), row2(k_k[l]), row2(k_a[l]),
                row2(r_k[l]), _seg_ones())
        if first:
            pre = _rwkv_pre(True, npt, tps, p_rwkv, prev_rwkv, row2(mu_shift[l]), lora)
        else:
            wv_pad = jnp.pad(w_in_vmix[l - 1], ((0, 0), (0, LANES - LORA_V))).astype(BF16)
            p_vd = _matmul(x, wv_pad, R, LANES)
            prev_vd = _matmul(x_prev_s, wv_pad, SB, LANES)
            mu_v = jnp.pad(mu_vmix[l - 1], (0, LANES - LORA_V)).reshape(1, LANES)
            vup_pad = jnp.pad(v_up[l - 1], ((0, LANES - LORA_V), (0, 0))).astype(BF16)
            pre = _rwkv_pre(False, npt, tps, p_rwkv, prev_rwkv, row2(mu_shift[l]), lora, p_vd, prev_vd, mu_v,
                            v_first, row2(v0[l - 1]), vup_pad)
        r_, lw_, k_, v_, nkk_, bb_, g_, bonus_ = pre
        if first:
            v_first = v_

        y_p, s_p = _wkv_prompt(B, T, r_, lw_, k_, v_, nkk_, bb_)

        def lanes_b(a):
            return jnp.transpose(a[n_p:].reshape(ST, SB, RW), (0, 2, 1)).reshape(ST, HEADS, HEAD_DIM, SB)

        state_sz = HEADS * HEAD_DIM * HEAD_DIM
        s0 = _transpose_layer(state_wkv.astype(F32).reshape(L, SB, state_sz), l, 2048)
        s0 = s0.reshape(HEADS, HEAD_DIM, HEAD_DIM, SB)
        y_s, s_s = _wkv_sample(s0, lanes_b(r_), lanes_b(lw_), lanes_b(k_),
                               lanes_b(v_).reshape(ST, HEADS, HEAD_DIM, 1, SB), lanes_b(nkk_), lanes_b(bb_))
        y_s = jnp.transpose(y_s.reshape(ST, RW, SB), (0, 2, 1)).reshape(n_s, RW)
        s_s = _transpose_2d(s_s.reshape(state_sz, SB), 2048).reshape(SB, HEADS, HEAD_DIM, HEAD_DIM)
        z_a = _rwkv_post(npt, y_p, y_s, bonus_, g_, row2(lnx_g[l]), row2(lnx_b[l]))

        cvec = (conv_w[l], row2(conv_b[l]), row2(cn_g[l]), row2(cn_b[l]))
        zb_p, conv_p = _conv_prompt(B, T, p_glu, *cvec)
        st_t = jnp.swapaxes(state_conv[l], 0, 1)
        zb_s, glu_s = _conv_sample(st_t, p_glu[n_p:].reshape(ST, SB, 2 * CONV_CH), *cvec)
        zb_s = zb_s.reshape(n_s, CONV_CH)
        conv_s = jnp.concatenate([state_conv[l][:, ST:], jnp.swapaxes(glu_s, 0, 1)], axis=1)

        kv_p = _matmul(mem_rows, w_mem_kv[l].astype(BF16), M, 2 * MEM_WIDTH)
        kv3 = kv_p.reshape(B, M, 2 * MEM_WIDTH)
        zc_p = _attention(p_q[:n_p].reshape(B, T, MEM_WIDTH), kv3, kv3, 0, 1, 1, R).reshape(n_p, MEM_WIDTH)
        q_s = jnp.swapaxes(p_q[n_p:].reshape(ST, SB, MEM_WIDTH), 0, 1)
        q_s = jnp.pad(q_s, ((0, 0), (0, 8 - ST), (0, 0)))
        ck = cache_mem_k[l].reshape(SB, M, MEM_WIDTH)
        cv = cache_mem_v[l].reshape(SB, M, MEM_WIDTH)
        zc_s = _attention(q_s, ck, cv, 0, 0, 8, 8)[:, :ST]
        zc_s = jnp.swapaxes(zc_s, 0, 1).reshape(n_s, MEM_WIDTH)

        h = _merge(n_p, alpha, z_a, zb_p, zb_s, zc_p, zc_s, p_gate, x,
                   w_o_a[l].astype(BF16), w_o_b[l].astype(BF16), w_o_c[l].astype(BF16), w_out[l].astype(BF16),
                   row2(ln1_g[l]), row2(ln1_b[l]))

        ws13 = jnp.concatenate([w_s1[l], w_s3[l]], axis=1).astype(BF16)
        x_new = _moe(l, alpha, h, w_router[l].astype(BF16), row2(b_router[l]), w_e1, w_e3, w_e2,
                     ws13, w_s2[l].astype(BF16), row2(ln2_g[l]), row2(ln2_b[l]))

        outs['wkv_p'].append(s_p.astype(state_wkv.dtype))
        outs['shift_p'].append(x[:n_p].reshape(B, T, D)[:, -1].astype(state_shift.dtype))
        outs['conv_p'].append(conv_p.astype(state_conv.dtype))
        outs['mk'].append(kv3[..., :MEM_WIDTH].reshape(B, M, MEM_HEADS, MEM_HEAD_DIM).astype(cache_mem_k.dtype))
        outs['mv'].append(kv3[..., MEM_WIDTH:].reshape(B, M, MEM_HEADS, MEM_HEAD_DIM).astype(cache_mem_v.dtype))
        outs['wkv_s'].append(s_s.astype(state_wkv.dtype))
        outs['shift_s'].append(x[n_p + (ST - 1) * SB:].astype(state_shift.dtype))
        outs['conv_s'].append(conv_s.astype(state_conv.dtype))
        x = x_new

    y_p = x[:n_p].reshape(B, T, D)
    y_s = jnp.swapaxes(x[n_p:].reshape(ST, SB, D), 0, 1)
    return (y_p, y_s, jnp.stack(outs['wkv_p']), jnp.stack(outs['shift_p']), jnp.stack(outs['conv_p']),
            jnp.stack(outs['mk']), jnp.stack(outs['mv']), jnp.stack(outs['wkv_s']), jnp.stack(outs['shift_s']),
            jnp.stack(outs['conv_s']))
```

```python
import functools
import math

import jax
import jax.numpy as jnp
from jax import lax
from jax.experimental import pallas as pl
from jax.experimental.pallas import tpu as pltpu
from jax.experimental.pallas import tpu_sc as plsc

F32 = jnp.float32
BF16 = jnp.bfloat16

HEADS = 8
HEAD_DIM = 64
RW = HEADS * HEAD_DIM
LORA_W = 64
LORA_A = 64
LORA_V = 32
LORA_G = 128
RWKV_COLS = 3 * RW + LORA_W + LORA_A + LORA_G
GN_EPS = HEAD_DIM * 1e-5
CONV_WIDTH = 31
CONV_BUF = CONV_WIDTH - 1
CONV_CH = 512
MEM_HEADS = 4
MEM_HEAD_DIM = 128
MEM_WIDTH = MEM_HEADS * MEM_HEAD_DIM
TOP_K = 8
ROUTED_SCALE = 2.5
LN_EPS = 1e-5
EXP_M05 = math.exp(-0.5)

LANES = 128
ROW_TILE = 512
MERGE_TILE = 256
CHUNK = 64
NEUMANN_BLOCK = 16
MOE_BLOCK = 512
DMA_QUEUES = 2
SC_CORES = 2
SC_SUBCORES = 16
SC_WINDOW = 128
DISPATCH_TILE = 256
COMBINE_TILE = 128
VMEM_LIMIT = 48 * 1024 * 1024


def _cparams(*sem, vmem=VMEM_LIMIT):
    return pltpu.CompilerParams(dimension_semantics=tuple(sem), vmem_limit_bytes=vmem)


def _bdot(a, b):
    return jnp.dot(a.astype(BF16), b.astype(BF16), preferred_element_type=F32)


def _bdot_nt(a, b):
    return lax.dot_general(a.astype(BF16), b.astype(BF16), (((1,), (1,)), ((), ())),
                           preferred_element_type=F32)


def _sigmoid(x):
    return 1.0 / (1.0 + jnp.exp(-x))


def _silu(x):
    return x * _sigmoid(x)


def _layer_norm(x, g, b):
    mu = jnp.mean(x, axis=-1, keepdims=True)
    d = x - mu
    var = jnp.mean(d * d, axis=-1, keepdims=True)
    return d * lax.rsqrt(var + LN_EPS) * g + b


SEG_LANES = 256


def _seg_ones():
    i = jnp.arange(SEG_LANES)
    return ((i[:, None] // HEAD_DIM) == (i[None, :] // HEAD_DIM)).astype(BF16)


def _seg_sum(x, seg):
    hi = x.astype(BF16)
    lo = (x - hi.astype(F32)).astype(BF16)
    parts = []
    for j in range(x.shape[-1] // SEG_LANES):
        sl = slice(j * SEG_LANES, (j + 1) * SEG_LANES)
        parts.append(jnp.dot(hi[:, sl], seg, preferred_element_type=F32)
                     + jnp.dot(lo[:, sl], seg, preferred_element_type=F32))
    return jnp.concatenate(parts, axis=-1)


def _pack_pair(lo, hi):
    lo_b = lax.bitcast_convert_type(lo.astype(BF16).astype(F32), jnp.int32)
    hi_b = lax.bitcast_convert_type(hi.astype(BF16).astype(F32), jnp.int32)
    return lax.shift_right_logical(lo_b, jnp.full(lo_b.shape, 16, jnp.int32)) | (hi_b & jnp.int32(-65536))


def _unpack_pair(p):
    lo = lax.bitcast_convert_type(lax.shift_left(p, jnp.full(p.shape, 16, jnp.int32)), F32)
    hi = lax.bitcast_convert_type(p & jnp.int32(-65536), F32)
    return lo, hi


def _mm_kernel(x_ref, w_ref, o_ref):
    o_ref[...] = _bdot(x_ref[...], w_ref[...]).astype(o_ref.dtype)


def _matmul(x, w, tm, tn):
    m, k = x.shape
    n_cols = w.shape[1]
    assert m % tm == 0 and n_cols % tn == 0
    return pl.pallas_call(
        _mm_kernel,
        grid=(m // tm, n_cols // tn),
        in_specs=[pl.BlockSpec((tm, k), lambda i, j: (i, 0)),
                  pl.BlockSpec((k, tn), lambda i, j: (0, j))],
        out_specs=pl.BlockSpec((tm, tn), lambda i, j: (i, j)),
        out_shape=jax.ShapeDtypeStruct((m, n_cols), F32),
        compiler_params=_cparams("parallel", "parallel"),
        name=f"matmul_{n_cols}",
    )(x, w)


def _pre_kernel(first, npt, tps, *refs):
    if first:
        (p_ref, halo_ref, prev_ref, mu_ref,
         w0_ref, wup_ref, a0_ref, aup_ref, gup_ref, kk_ref, ka_ref, rk_ref, seg_ref,
         r_o, lw_o, k_o, v_o, nkk_o, bb_o, g_o, bon_o, sh_ref) = refs
    else:
        (p_ref, halo_ref, prev_ref, mu_ref, pv_ref, halov_ref, prevv_ref, muv_ref, vf_ref, v0_ref, vup_ref,
         w0_ref, wup_ref, a0_ref, aup_ref, gup_ref, kk_ref, ka_ref, rk_ref, seg_ref,
         r_o, lw_o, k_o, v_o, nkk_o, bb_o, g_o, bon_o, sh_ref, shv_ref) = refs
    i = pl.program_id(0)
    rows_n = p_ref.shape[0]

    def build_shift(src_ref, halo_r, prev_r, dst_ref):
        @pl.when(i < npt)
        def _():
            x = src_ref[...]
            rolled = pltpu.roll(x, 1, axis=0)
            last = halo_r[7:8, :]
            row0 = jnp.where((i % tps) == 0, jnp.zeros_like(last), last)
            rows = lax.broadcasted_iota(jnp.int32, x.shape, 0)
            dst_ref[...] = jnp.where(rows == 0, row0, rolled)

        @pl.when(i >= npt)
        def _():
            nb = prev_r.shape[0]
            dst_ref[0:nb, :] = prev_r[...]
            dst_ref[nb:rows_n, :] = src_ref[0:rows_n - nb, :]

    build_shift(p_ref, halo_ref, prev_ref, sh_ref)
    p = p_ref[...]
    p = p + (sh_ref[...] - p) * mu_ref[...]
    r = p[:, 0:RW]
    k = p[:, RW:2 * RW]
    v = p[:, 2 * RW:3 * RW]
    u = p[:, 3 * RW:3 * RW + LORA_W + LORA_A]
    gd = p[:, 3 * RW + LORA_W + LORA_A:RWKV_COLS]

    z = w0_ref[...] + _bdot(jnp.tanh(u), wup_ref[...])
    lw = -EXP_M05 * _sigmoid(z)
    a = _sigmoid(a0_ref[...] + _bdot(u, aup_ref[...]))
    g = _bdot(_sigmoid(gd), gup_ref[...])
    if not first:
        build_shift(pv_ref, halov_ref, prevv_ref, shv_ref)
        pv = pv_ref[...]
        pv = pv + (shv_ref[...] - pv) * muv_ref[...]
        mix = _sigmoid(v0_ref[...] + _bdot(pv, vup_ref[...]))
        v = v + (vf_ref[...] - v) * mix
    kk = k * kk_ref[...]
    seg = seg_ref[...]
    nrm = jnp.sqrt(_seg_sum(kk * kk, seg))
    kkn = kk / jnp.maximum(nrm, 1e-12)
    k2 = k * (1.0 + (a - 1.0) * ka_ref[...])
    bonus = _seg_sum(r * k2 * rk_ref[...], seg) * v
    r_o[...] = r
    lw_o[...] = lw
    k_o[...] = k2
    v_o[...] = v
    nkk_o[...] = -kkn
    bb_o[...] = kkn * a
    g_o[...] = g
    bon_o[...] = bonus


def _rwkv_pre(first, npt, tps, p_rwkv, prev_rwkv, mu, lora, p_vd=None, prev_vd=None, mu_v=None, v_first=None,
              v0=None, vup=None):
    n = p_rwkv.shape[0]
    R = ROW_TILE
    nt = n // R

    def row_spec(w):
        return pl.BlockSpec((R, w), lambda i: (i, 0))

    def halo_spec(w):
        return pl.BlockSpec((8, w), lambda i: (jnp.maximum(i * (R // 8) - 1, 0), 0))

    def full(a):
        return pl.BlockSpec(a.shape, lambda i: (0,) * a.ndim)

    ins = [p_rwkv, p_rwkv, prev_rwkv, mu]
    specs = [row_spec(RWKV_COLS), halo_spec(RWKV_COLS), full(prev_rwkv), full(mu)]
    scratch = [pltpu.VMEM((R, RWKV_COLS), F32)]
    if not first:
        ins += [p_vd, p_vd, prev_vd, mu_v, v_first, v0, vup]
        specs += [row_spec(LANES), halo_spec(LANES), full(prev_vd), full(mu_v), row_spec(RW), full(v0), full(vup)]
        scratch.append(pltpu.VMEM((R, LANES), F32))
    ins += list(lora)
    specs += [full(a) for a in lora]
    outs = pl.pallas_call(
        functools.partial(_pre_kernel, first, npt, tps),
        grid=(nt,),
        in_specs=specs,
        out_specs=[row_spec(RW)] * 8,
        out_shape=[jax.ShapeDtypeStruct((n, RW), F32)] * 8,
        scratch_shapes=scratch,
        compiler_params=_cparams("parallel"),
        name="rwkv_pre",
    )(*ins)
    return outs


def _wkv_chunk_kernel(nbat, *refs):
    in_refs = refs[:6 * nbat]
    y_ref, s_out_ref, s_ref = refs[6 * nbat:]
    c = pl.program_id(1)
    C = in_refs[0].shape[0]
    G = 4 * HEAD_DIM
    assert 4 * C == G

    @pl.when(c == 0)
    def _():
        s_ref[...] = jnp.zeros_like(s_ref)

    ri = lax.broadcasted_iota(jnp.int32, (G, G), 0)
    ci = lax.broadcasted_iota(jnp.int32, (G, G), 1)
    same_head = (ri // C) == (ci // HEAD_DIM)
    tril_s = same_head & ((ci % C) < (ri % C))
    tril_i = same_head & ((ci % C) <= (ri % C))
    diag_blk = (ri // NEUMANN_BLOCK) == (ci // NEUMANN_BLOCK)
    eye = (ri == ci).astype(F32)

    def stack(x):
        return jnp.where(same_head, jnp.concatenate([x, x, x, x], axis=0), 0.0).astype(BF16)

    def tile4(x):
        return jnp.concatenate([x, x, x, x], axis=0).astype(BF16)

    def collapse(x4):
        return x4[0:C] + x4[C:2 * C] + x4[2 * C:3 * C] + x4[3 * C:4 * C]

    for b in range(nbat):
        r_ref, lw_ref, k_ref, v_ref, nkk_ref, bb_ref = in_refs[6 * b:6 * b + 6]
        lw = lw_ref[...]
        rows = lax.broadcasted_iota(jnp.int32, lw.shape, 0)
        cs = lw
        s = 1
        while s < C:
            cs = cs + jnp.where(rows >= s, pltpu.roll(cs, s, axis=0), 0.0)
            s *= 2
        cs_last = cs[C - 1:C, :]
        e_neg = jnp.exp(-cs)
        e_tail = jnp.exp(cs_last - cs)
        kk_ = k_ref[...]
        bb_ = bb_ref[...]
        at = nkk_ref[...] * jnp.exp(cs - lw)
        rt = r_ref[...] * jnp.exp(cs)
        bt = bb_ * e_neg
        kt = kk_ * e_neg
        bw = bb_ * e_tail
        kw = kk_ * e_tail
        vv = v_ref[...]
        w_end = jnp.exp(cs_last)

        for g in range(RW // G):
            sl = slice(g * G, (g + 1) * G)
            at4, rt4, v4 = stack(at[:, sl]), stack(rt[:, sl]), stack(vv[:, sl])
            bt4, kt4 = tile4(bt[:, sl]), tile4(kt[:, sl])
            a_ab = jnp.where(tril_s, _bdot_nt(at4, bt4), 0.0)
            a_ak = jnp.where(tril_s, _bdot_nt(at4, kt4), 0.0).astype(BF16)
            a_rb = jnp.where(tril_i, _bdot_nt(rt4, bt4), 0.0).astype(BF16)
            a_rk = jnp.where(tril_i, _bdot_nt(rt4, kt4), 0.0).astype(BF16)
            a_d = jnp.where(diag_blk, a_ab, 0.0)
            a_o = a_ab - a_d
            a_db = a_d.astype(BF16)
            x2 = _bdot(a_db, a_db).astype(BF16)
            x4 = _bdot(x2, x2).astype(BF16)
            x8 = _bdot(x4, x4)
            t_d = eye + a_d
            t_d = t_d + _bdot(t_d, x2)
            t_d = t_d + _bdot(t_d, x4)
            t_d = t_d + _bdot(t_d, x8)
            t_db = t_d.astype(BF16)
            nn = _bdot(t_db, a_o).astype(BF16)
            n2 = _bdot(nn, nn)
            m1 = t_d + _bdot(nn, t_db)
            t_m = m1 + _bdot(n2, m1)
            s4 = s_ref[b, g]
            s4b = s4.astype(BF16)
            u4 = _bdot(t_m, _bdot_nt(at4, s4b) + _bdot(a_ak, v4))
            y4 = _bdot_nt(rt4, s4b) + _bdot(jnp.concatenate([a_rb, a_rk], axis=1),
                                            jnp.concatenate([u4.astype(BF16), v4], axis=0))
            y_ref[b, :, sl] = collapse(y4)
            uv_t = jnp.concatenate([collapse(u4), vv[:, sl]], axis=0).T
            upd = _bdot(uv_t, jnp.concatenate([bw[:, sl], kw[:, sl]], axis=0))
            s_ref[b, g] = s4 * w_end[:, sl] + jnp.where((ri // HEAD_DIM) == (ci // HEAD_DIM), upd, 0.0)

    @pl.when(c == pl.num_programs(1) - 1)
    def _():
        s_out_ref[...] = s_ref[...]


def _wkv_prompt(B, T, r, lw, k, v, nkk, bb):
    C = CHUNK
    nc = T // C
    G = 4 * HEAD_DIM
    nbat = 2 if B % 2 == 0 else 1
    specs = []
    for j in range(nbat):
        specs += [pl.BlockSpec((C, RW), lambda i, c, j=j: ((nbat * i + j) * nc + c, 0))] * 6
    y, s = pl.pallas_call(
        functools.partial(_wkv_chunk_kernel, nbat),
        grid=(B // nbat, nc),
        in_specs=specs,
        out_specs=[pl.BlockSpec((nbat, C, RW), lambda i, c: (i, c, 0)),
                   pl.BlockSpec((nbat, RW // G, G, G), lambda i, c: (i, 0, 0, 0))],
        out_shape=[jax.ShapeDtypeStruct((B, T, RW), F32),
                   jax.ShapeDtypeStruct((B, RW // G, G, G), F32)],
        scratch_shapes=[pltpu.VMEM((nbat, RW // G, G, G), F32)],
        compiler_params=_cparams("parallel", "arbitrary"),
        name="wkv_chunk",
    )(*([r, lw, k, v, nkk, bb] * nbat))
    y = y.reshape(B * T, RW)
    s = s.reshape(B, RW // G, 4, HEAD_DIM, 4, HEAD_DIM)
    idx = jnp.arange(4)
    s = s[:, :, idx, :, idx, :]
    s = jnp.transpose(s, (1, 2, 0, 3, 4)).reshape(B, HEADS, HEAD_DIM, HEAD_DIM)
    return y, s


def _wkv_seq_kernel(s0_ref, r_ref, lw_ref, k_ref, v_ref, nkk_ref, bb_ref, y_ref, s_ref):
    steps = r_ref.shape[0]

    def body(vi, carry):
        st = s0_ref[0, vi]
        for t in range(steps):
            sa = jnp.sum(st * nkk_ref[t, 0], axis=0, keepdims=True)
            st = st * jnp.exp(lw_ref[t, 0]) + sa * bb_ref[t, 0] + v_ref[t, 0, vi] * k_ref[t, 0]
            y_ref[t, 0, vi] = jnp.sum(st * r_ref[t, 0], axis=0, keepdims=True)
        s_ref[0, vi] = st
        return carry

    lax.fori_loop(0, HEAD_DIM, body, 0)


def _wkv_sample(s0, r, lw, k, v, nkk, bb):
    steps, _, _, nb = r.shape
    vec = pl.BlockSpec((steps, 1, HEAD_DIM, nb), lambda h: (0, h, 0, 0))
    vcol = pl.BlockSpec((steps, 1, HEAD_DIM, 1, nb), lambda h: (0, h, 0, 0, 0))
    st = pl.BlockSpec((1, HEAD_DIM, HEAD_DIM, nb), lambda h: (h, 0, 0, 0))
    y, s = pl.pallas_call(
        _wkv_seq_kernel,
        grid=(HEADS,),
        in_specs=[st, vec, vec, vec, vcol, vec, vec],
        out_specs=[vcol, st],
        out_shape=[jax.ShapeDtypeStruct((steps, HEADS, HEAD_DIM, 1, nb), F32),
                   jax.ShapeDtypeStruct(s0.shape, F32)],
        compiler_params=_cparams("parallel"),
        name="wkv_seq",
    )(s0, r, lw, k, v, nkk, bb)
    return y, s


def _tr_kernel(x_ref, o_ref):
    o_ref[...] = x_ref[...].T


def _transpose_layer(x3, layer, tc):
    _, a, c = x3.shape
    return pl.pallas_call(
        _tr_kernel,
        grid=(c // tc,),
        in_specs=[pl.BlockSpec((None, a, tc), lambda j: (layer, 0, j))],
        out_specs=pl.BlockSpec((tc, a), lambda j: (j, 0)),
        out_shape=jax.ShapeDtypeStruct((c, a), x3.dtype),
        compiler_params=_cparams("parallel"),
        name="transpose_in",
    )(x3)


def _transpose_2d(x, tc):
    c, a = x.shape
    return pl.pallas_call(
        _tr_kernel,
        grid=(c // tc,),
        in_specs=[pl.BlockSpec((tc, a), lambda j: (j, 0))],
        out_specs=pl.BlockSpec((a, tc), lambda j: (0, j)),
        out_shape=jax.ShapeDtypeStruct((a, c), x.dtype),
        compiler_params=_cparams("parallel"),
        name="transpose_out",
    )(x)


def _post_kernel(npt, yp_ref, ys_ref, bon_ref, g_ref, lg_ref, lb_ref, seg_ref, o_ref):
    i = pl.program_id(0)
    y = jnp.where(i < npt, yp_ref[...], ys_ref[...])
    seg = seg_ref[...]
    mu = _seg_sum(y, seg) * (1.0 / HEAD_DIM)
    d = y - mu
    var = _seg_sum(d * d, seg) * (1.0 / HEAD_DIM)
    yn = d * lax.rsqrt(var + GN_EPS) * lg_ref[...] + lb_ref[...]
    o_ref[...] = (yn + bon_ref[...]) * g_ref[...]


def _rwkv_post(npt, y_p, y_s, bonus, g, lnx_g, lnx_b):
    n = bonus.shape[0]
    R = ROW_TILE
    row = pl.BlockSpec((R, RW), lambda i: (i, 0))
    vec = pl.BlockSpec((1, RW), lambda i: (0, 0))
    return pl.pallas_call(
        functools.partial(_post_kernel, npt),
        grid=(n // R,),
        in_specs=[pl.BlockSpec((R, RW), lambda i: (jnp.minimum(i, npt - 1), 0)),
                  pl.BlockSpec((R, RW), lambda i: (0, 0)), row, row, vec, vec,
                  pl.BlockSpec((SEG_LANES, SEG_LANES), lambda i: (0, 0))],
        out_specs=row,
        out_shape=jax.ShapeDtypeStruct((n, RW), F32),
        compiler_params=_cparams("parallel"),
        name="rwkv_post",
    )(y_p, y_s, bonus, g, lnx_g, lnx_b, _seg_ones())


def _glu(p):
    return p[..., :CONV_CH] * _sigmoid(p[..., CONV_CH:])


def _conv_prompt_kernel(p_ref, halo_ref, w_ref, cb_ref, g_ref, b_ref, o_ref, tail_ref, ext_ref):
    j = pl.program_id(1)
    R = p_ref.shape[0]
    H = halo_ref.shape[0]
    glu = _glu(p_ref[...])
    halo = _glu(halo_ref[...])
    ext_ref[0:H, :] = jnp.where(j == 0, jnp.zeros_like(halo), halo)
    ext_ref[H:H + R, :] = glu
    acc = jnp.zeros((R, CONV_CH), F32)
    for t in range(CONV_WIDTH):
        acc = acc + ext_ref[pl.ds(H - CONV_BUF + t, R), :] * w_ref[t:t + 1, :]
    h = _layer_norm(acc + cb_ref[...], g_ref[...], b_ref[...])
    o_ref[...] = _silu(h)

    @pl.when(j == pl.num_programs(1) - 1)
    def _():
        tail_ref[0] = glu[R - H:R, :]


def _conv_prompt(B, T, p_glu, conv_w, conv_b, cn_g, cn_b):
    R = ROW_TILE
    H = 32
    tps = T // R
    vec = pl.BlockSpec((1, CONV_CH), lambda b, j: (0, 0))
    z, tail = pl.pallas_call(
        _conv_prompt_kernel,
        grid=(B, tps),
        in_specs=[pl.BlockSpec((R, 2 * CONV_CH), lambda b, j: (b * tps + j, 0)),
                  pl.BlockSpec((H, 2 * CONV_CH), lambda b, j: (jnp.maximum((b * tps + j) * (R // H) - 1, 0), 0)),
                  pl.BlockSpec((CONV_WIDTH, CONV_CH), lambda b, j: (0, 0)), vec, vec, vec],
        out_specs=[pl.BlockSpec((R, CONV_CH), lambda b, j: (b * tps + j, 0)),
                   pl.BlockSpec((1, H, CONV_CH), lambda b, j: (b, 0, 0))],
        out_shape=[jax.ShapeDtypeStruct((B * T, CONV_CH), F32),
                   jax.ShapeDtypeStruct((B, H, CONV_CH), F32)],
        scratch_shapes=[pltpu.VMEM((H + R, CONV_CH), F32)],
        compiler_params=_cparams("parallel", "arbitrary"),
        name="conv_prompt",
    )(p_glu, p_glu, conv_w, conv_b, cn_g, cn_b)
    return z, tail[:, H - CONV_BUF:, :]


def _conv_sample_kernel(st_ref, p_ref, w_ref, cb_ref, g_ref, b_ref, o_ref, glu_ref):
    steps = p_ref.shape[0]
    glu = [_glu(p_ref[t]) for t in range(steps)]
    for t in range(steps):
        glu_ref[t] = glu[t]
    for t in range(steps):
        acc = jnp.zeros(glu[0].shape, F32)
        for j in range(CONV_WIDTH):
            src = t + j
            x = st_ref[src] if src < CONV_BUF else glu[src - CONV_BUF]
            acc = acc + x * w_ref[j:j + 1, :]
        h = _layer_norm(acc + cb_ref[...], g_ref[...], b_ref[...])
        o_ref[t] = _silu(h)


def _conv_sample(state_t, p_glu, conv_w, conv_b, cn_g, cn_b):
    steps, nb, _ = p_glu.shape

    def full(a):
        return pl.BlockSpec(a.shape, lambda i: (0,) * a.ndim)

    ins = (state_t, p_glu, conv_w, conv_b, cn_g, cn_b)
    return pl.pallas_call(
        _conv_sample_kernel,
        grid=(1,),
        in_specs=[full(a) for a in ins],
        out_specs=[pl.BlockSpec((steps, nb, CONV_CH), lambda i: (0, 0, 0))] * 2,
        out_shape=[jax.ShapeDtypeStruct((steps, nb, CONV_CH), F32)] * 2,
        compiler_params=_cparams("arbitrary"),
        name="conv_sample",
    )(*ins)


def _attn_kernel(q_ref, k_ref, v_ref, o_ref):
    nb = q_ref.shape[0]
    scale = MEM_HEAD_DIM ** -0.5
    for bi in range(nb):
        for h in range(MEM_HEADS):
            sl = slice(h * MEM_HEAD_DIM, (h + 1) * MEM_HEAD_DIM)
            s = _bdot_nt(q_ref[bi, :, sl], k_ref[bi, :, sl]) * scale
            m = jnp.max(s, axis=-1, keepdims=True)
            e = jnp.exp(s - m)
            pr = e / jnp.sum(e, axis=-1, keepdims=True)
            o_ref[bi, :, sl] = _bdot(pr, v_ref[bi, :, sl])


def _attn_rows_kernel(q_ref, k_ref, v_ref, o_ref):
    nb = q_ref.shape[0]
    n_mem = k_ref.shape[1] // MEM_HEADS
    scale = MEM_HEAD_DIM ** -0.5
    for bi in range(nb):
        for h in range(MEM_HEADS):
            sl = slice(h * MEM_HEAD_DIM, (h + 1) * MEM_HEAD_DIM)
            rows = pl.ds(h, n_mem, stride=MEM_HEADS)
            s = _bdot_nt(q_ref[bi, :, sl], k_ref[bi, rows, :]) * scale
            m = jnp.max(s, axis=-1, keepdims=True)
            e = jnp.exp(s - m)
            pr = e / jnp.sum(e, axis=-1, keepdims=True)
            o_ref[bi, :, sl] = _bdot(pr, v_ref[bi, rows, :])


def _attention_cached(q, cache_k, cache_v, layer, bb):
    B, T, W = q.shape
    rows = cache_k.shape[2]
    kv_spec = pl.BlockSpec((None, bb, rows, MEM_HEAD_DIM), lambda b: (layer, b, 0, 0))
    return pl.pallas_call(
        _attn_rows_kernel,
        grid=(B // bb,),
        in_specs=[pl.BlockSpec((bb, T, W), lambda b: (b, 0, 0)), kv_spec, kv_spec],
        out_specs=pl.BlockSpec((bb, T, W), lambda b: (b, 0, 0)),
        out_shape=jax.ShapeDtypeStruct((B, T, W), F32),
        compiler_params=_cparams("parallel"),
        name="mem_attention_cached",
    )(q, cache_k, cache_v)


def _attention(q, mk, mv, kcol, vcol, bb, tq):
    B, T, W = q.shape
    M = mk.shape[1]
    return pl.pallas_call(
        _attn_kernel,
        grid=(B // bb, T // tq),
        in_specs=[pl.BlockSpec((bb, tq, W), lambda b, j: (b, j, 0)),
                  pl.BlockSpec((bb, M, W), lambda b, j: (b, 0, kcol)),
                  pl.BlockSpec((bb, M, W), lambda b, j: (b, 0, vcol))],
        out_specs=pl.BlockSpec((bb, tq, W), lambda b, j: (b, j, 0)),
        out_shape=jax.ShapeDtypeStruct((B, T, W), F32),
        compiler_params=_cparams("parallel", "parallel"),
        name=f"mem_attention_{tq}",
    )(q, mk, mv)


def _merge_kernel(npt, alpha, za_ref, zbp_ref, zbs_ref, zcp_ref, zcs_ref, gate_ref, x_ref,
                  woa_ref, wob_ref, woc_ref, wout_ref, g_ref, b_ref, o_ref):
    i = pl.program_id(0)
    D = x_ref.shape[1]
    is_p = i < npt
    zb = jnp.where(is_p, zbp_ref[...], zbs_ref[...])
    zc = jnp.where(is_p, zcp_ref[...], zcs_ref[...])
    merged = (_sigmoid(gate_ref[:, 0:D]) * _bdot(za_ref[...], woa_ref[...])
              + _sigmoid(gate_ref[:, D:2 * D]) * _bdot(zb, wob_ref[...])
              + _sigmoid(gate_ref[:, 2 * D:3 * D]) * _bdot(zc, woc_ref[...]))
    h = alpha * x_ref[...] + _bdot(merged, wout_ref[...])
    o_ref[...] = _layer_norm(h, g_ref[...], b_ref[...])


def _merge(n_p, alpha, za, zb_p, zb_s, zc_p, zc_s, gates, x, woa, wob, woc, wout, ln_g, ln_b):
    n, D = x.shape
    R = MERGE_TILE
    npt = n_p // R

    def row(w):
        return pl.BlockSpec((R, w), lambda i: (i, 0))

    def prow(w):
        return pl.BlockSpec((R, w), lambda i: (jnp.minimum(i, npt - 1), 0))

    def srow(w):
        return pl.BlockSpec((R, w), lambda i: (jnp.maximum(i - npt, 0), 0))

    def full(a):
        return pl.BlockSpec(a.shape, lambda i: (0,) * a.ndim)

    return pl.pallas_call(
        functools.partial(_merge_kernel, npt, alpha),
        grid=(n // R,),
        in_specs=[row(RW), prow(CONV_CH), srow(CONV_CH), prow(MEM_WIDTH), srow(MEM_WIDTH), row(3 * D), row(D),
                  full(woa), full(wob), full(woc), full(wout), full(ln_g), full(ln_b)],
        out_specs=row(D),
        out_shape=jax.ShapeDtypeStruct((n, D), F32),
        compiler_params=_cparams("parallel"),
        name="branch_merge",
    )(za, zb_p, zb_s, zc_p, zc_s, gates, x, woa, wob, woc, wout, ln_g, ln_b)


def _router_kernel(h_ref, wr_ref, br_ref, idx_o, gate_o, rank_o, cnt_o, carry_ref):
    i = pl.program_id(0)
    R = h_ref.shape[0]
    E = wr_ref.shape[1]

    @pl.when(i == 0)
    def _():
        carry_ref[...] = jnp.zeros_like(carry_ref)

    scores = _sigmoid(_bdot(h_ref[...], wr_ref[...]))
    work = scores + br_ref[...]
    lane = lax.broadcasted_iota(jnp.int32, (R, E), 1).astype(F32)
    sel_mask = jnp.zeros((R, E), F32)
    idx_cols, sel_cols = [], []
    for _ in range(TOP_K):
        m = jnp.max(work, axis=-1, keepdims=True)
        first = jnp.min(jnp.where(work == m, lane, float(E)), axis=-1, keepdims=True)
        onehot = lane == first
        sel_cols.append(jnp.sum(jnp.where(onehot, scores, 0.0), axis=-1, keepdims=True))
        idx_cols.append(first)
        sel_mask = jnp.where(onehot, 1.0, sel_mask)
        work = jnp.where(onehot, -jnp.inf, work)
    total = sel_cols[0]
    for c in sel_cols[1:]:
        total = total + c
    ri = lax.broadcasted_iota(jnp.int32, (R, R), 0)
    ci = lax.broadcasted_iota(jnp.int32, (R, R), 1)
    before = jnp.dot((ci < ri).astype(BF16), sel_mask.astype(BF16), preferred_element_type=F32) + carry_ref[0:1, :]
    idx_out = jnp.zeros((R, E), F32)
    gate_out = jnp.zeros((R, E), F32)
    rank_out = jnp.zeros((R, E), F32)
    for kk in range(TOP_K):
        onehot = lane == idx_cols[kk]
        rank = jnp.sum(jnp.where(onehot, before, 0.0), axis=-1, keepdims=True)
        col = lane == float(kk)
        idx_out = jnp.where(col, idx_cols[kk], idx_out)
        gate_out = jnp.where(col, sel_cols[kk] / total * ROUTED_SCALE, gate_out)
        rank_out = jnp.where(col, rank, rank_out)
    idx_o[...] = idx_out
    gate_o[...] = gate_out
    rank_o[...] = rank_out
    carry_ref[0:1, :] = carry_ref[0:1, :] + jnp.sum(sel_mask, axis=0, keepdims=True)
    cnt_o[...] = carry_ref[...]


def _router(h, w_router, b_router):
    n, D = h.shape
    E = w_router.shape[1]
    assert E == LANES
    R = 256
    row = pl.BlockSpec((R, E), lambda i: (i, 0))
    return pl.pallas_call(
        _router_kernel,
        grid=(n // R,),
        in_specs=[pl.BlockSpec((R, D), lambda i: (i, 0)),
                  pl.BlockSpec((D, E), lambda i: (0, 0)),
                  pl.BlockSpec((1, E), lambda i: (0, 0))],
        out_specs=[row, row, row, pl.BlockSpec((8, E), lambda i: (0, 0))],
        out_shape=[jax.ShapeDtypeStruct((n, E), F32), jax.ShapeDtypeStruct((n, E), F32),
                   jax.ShapeDtypeStruct((n, E), F32), jax.ShapeDtypeStruct((8, E), F32)],
        scratch_shapes=[pltpu.VMEM((8, E), F32)],
        compiler_params=_cparams("arbitrary"),
        name="moe_router",
    )(h, w_router, b_router)


def _dest_kernel(idx_ref, rank_ref, start_ref, o_ref):
    R, E = idx_ref.shape
    lane = lax.broadcasted_iota(jnp.int32, (R, E), 1).astype(F32)
    idx = idx_ref[...]
    out = jnp.zeros((R, E), F32)
    start = start_ref[...]
    for kk in range(TOP_K):
        e_k = jnp.sum(jnp.where(lane == float(kk), idx, 0.0), axis=-1, keepdims=True)
        s_k = jnp.sum(jnp.where(lane == e_k, start, 0.0), axis=-1, keepdims=True)
        out = jnp.where(lane == float(kk), s_k, out)
    o_ref[...] = (out + rank_ref[...]).astype(jnp.int32)


def _dest(idx, rank, start):
    n, E = idx.shape
    R = 256
    row = pl.BlockSpec((R, E), lambda i: (i, 0))
    return pl.pallas_call(
        _dest_kernel,
        grid=(n // R,),
        in_specs=[row, row, pl.BlockSpec((1, E), lambda i: (0, 0))],
        out_specs=row,
        out_shape=jax.ShapeDtypeStruct((n, E), jnp.int32),
        compiler_params=_cparams("parallel"),
        name="moe_dest",
    )(idx, rank, start)


def _zero_tail_kernel(eb_ref, o_ref):
    o_ref[...] = jnp.zeros_like(o_ref)


def _zero_tails(end_blocks, n_slots, width):
    BM = MOE_BLOCK
    grid_spec = pltpu.PrefetchScalarGridSpec(
        num_scalar_prefetch=1,
        grid=(end_blocks.shape[0],),
        in_specs=[],
        out_specs=pl.BlockSpec((BM, width), lambda e, eb: (jnp.maximum(eb[e] - 1, 0), 0)),
    )
    return pl.pallas_call(
        _zero_tail_kernel,
        grid_spec=grid_spec,
        out_shape=jax.ShapeDtypeStruct((n_slots, width), jnp.int32),
        compiler_params=_cparams("arbitrary"),
        name="moe_zero_tails",
    )(end_blocks)


def _dispatch_kernel(h_ref, dest_hbm, xg_in, xg_hbm, pk_ref, dest_smem, sem_idx, sem):
    del xg_in
    i = pl.program_id(0)
    R, D = h_ref.shape
    n_idx = R * TOP_K
    cp = pltpu.make_async_copy(dest_hbm.at[pl.ds(i * n_idx, n_idx)], dest_smem, sem_idx)
    cp.start()
    pk_ref[...] = _pack_pair(h_ref[:, 0:D // 2], h_ref[:, D // 2:D])
    cp.wait()

    def row_copy(t, kk):
        d = dest_smem[t * TOP_K + kk]
        return pltpu.make_async_copy(pk_ref.at[pl.ds(t, 1), :], xg_hbm.at[pl.ds(d, 1), :], sem)

    def issue(t, c):
        for kk in range(TOP_K):
            row_copy(t, kk).start(priority=kk % DMA_QUEUES)
        return c

    lax.fori_loop(0, R, issue, 0)

    def drain(t, c):
        for kk in range(TOP_K):
            row_copy(t, kk).wait()
        return c

    lax.fori_loop(0, R, drain, 0)


def _dispatch(h, dest_flat, xg0):
    n, D = h.shape
    R = DISPATCH_TILE
    return pl.pallas_call(
        _dispatch_kernel,
        grid=(n // R,),
        in_specs=[pl.BlockSpec((R, D), lambda i: (i, 0)),
                  pl.BlockSpec(memory_space=pl.ANY),
                  pl.BlockSpec(memory_space=pl.ANY)],
        out_specs=pl.BlockSpec(memory_space=pl.ANY),
        out_shape=jax.ShapeDtypeStruct(xg0.shape, xg0.dtype),
        scratch_shapes=[pltpu.VMEM((R, D // 2), jnp.int32), pltpu.SMEM((R * TOP_K,), jnp.int32),
                        pltpu.SemaphoreType.DMA, pltpu.SemaphoreType.DMA],
        input_output_aliases={2: 0},
        compiler_params=_cparams("arbitrary"),
        name="moe_dispatch",
    )(h, dest_flat, xg0)


def _expert_kernel(be_ref, nu_ref, x_ref, w1_ref, w3_ref, w2_ref, o_ref, w13_s, w2_s):
    i = pl.program_id(0)
    F = w1_ref.shape[1]

    @pl.when(i < nu_ref[0])
    def _():
        prev = be_ref[jnp.maximum(i - 1, 0)]

        @pl.when((i == 0) | (be_ref[i] != prev))
        def _():
            w13_s[:, 0:F] = w1_ref[...].astype(BF16)
            w13_s[:, F:2 * F] = w3_ref[...].astype(BF16)
            w2_s[...] = w2_ref[...].astype(BF16)

        x_lo, x_hi = _unpack_pair(x_ref[...])
        half = x_ref.shape[1]
        h13 = (jnp.dot(x_lo.astype(BF16), w13_s[0:half, :], preferred_element_type=F32)
               + jnp.dot(x_hi.astype(BF16), w13_s[half:2 * half, :], preferred_element_type=F32))
        hh = _silu(h13[:, 0:F]) * h13[:, F:2 * F]
        y = jnp.dot(hh.astype(BF16), w2_s[...], preferred_element_type=F32)
        o_ref[...] = _pack_pair(y[:, 0:half], y[:, half:2 * half])


def _experts(layer, block_e, n_used, xg, w_e1, w_e3, w_e2):
    n_slots, half = xg.shape
    D = 2 * half
    F = w_e1.shape[3]
    BM = MOE_BLOCK
    nb = n_slots // BM

    def blk(i, be, nu):
        return (jnp.minimum(i, nu[0] - 1), 0)

    grid_spec = pltpu.PrefetchScalarGridSpec(
        num_scalar_prefetch=2,
        grid=(nb,),
        in_specs=[pl.BlockSpec((BM, half), blk),
                  pl.BlockSpec((None, None, D, F), lambda i, be, nu: (layer, be[i], 0, 0)),
                  pl.BlockSpec((None, None, D, F), lambda i, be, nu: (layer, be[i], 0, 0)),
                  pl.BlockSpec((None, None, F, D), lambda i, be, nu: (layer, be[i], 0, 0))],
        out_specs=pl.BlockSpec((BM, half), blk),
        scratch_shapes=[pltpu.VMEM((D, 2 * F), BF16), pltpu.VMEM((F, D), BF16)],
    )
    return pl.pallas_call(
        _expert_kernel,
        grid_spec=grid_spec,
        out_shape=jax.ShapeDtypeStruct((n_slots, half), jnp.int32),
        compiler_params=_cparams("arbitrary"),
        name="moe_experts",
    )(block_e, n_used, xg, w_e1, w_e3, w_e2)


def _sc_gather(table, idx):
    n_idx = idx.shape[0]
    width = table.shape[1]
    workers = SC_CORES * SC_SUBCORES
    per_w = n_idx // workers
    assert per_w * workers == n_idx and per_w % SC_WINDOW == 0
    mesh = plsc.VectorSubcoreMesh(core_axis_name="c", subcore_axis_name="s",
                                  num_cores=SC_CORES, num_subcores=SC_SUBCORES)

    @functools.partial(
        pl.kernel, mesh=mesh,
        out_type=jax.ShapeDtypeStruct((n_idx, width), table.dtype),
        scratch_types=[pltpu.VMEM((SC_WINDOW,), jnp.int32),
                       pltpu.VMEM((SC_WINDOW, width), table.dtype),
                       pltpu.SemaphoreType.DMA],
        name="moe_sc_gather",
    )
    def gather(table_hbm, idx_hbm, out_hbm, idx_v, rows_v, sem):
        wid = lax.axis_index("s") * SC_CORES + lax.axis_index("c")

        @pl.loop(0, per_w // SC_WINDOW)
        def _(j):
            base = pl.multiple_of(wid * per_w + j * SC_WINDOW, SC_WINDOW)
            pltpu.sync_copy(idx_hbm.at[pl.ds(base, SC_WINDOW)], idx_v)
            pltpu.async_copy(table_hbm.at[idx_v], rows_v, sem).wait()
            pltpu.sync_copy(rows_v, out_hbm.at[pl.ds(base, SC_WINDOW)])

    return gather(table, idx)


def _combine_kernel(alpha, h_ref, gate_ref, yt_ref, ws13_ref, ws2_ref, g_ref, b_ref, o_ref):
    F = ws2_ref.shape[0]
    half = h_ref.shape[1] // 2
    h = h_ref[...]
    h13 = _bdot(h, ws13_ref[...])
    shared = _bdot(_silu(h13[:, 0:F]) * h13[:, F:2 * F], ws2_ref[...])
    r_lo = jnp.zeros((h.shape[0], half), F32)
    r_hi = jnp.zeros((h.shape[0], half), F32)
    for kk in range(TOP_K):
        y_lo, y_hi = _unpack_pair(yt_ref[kk])
        gk = gate_ref[:, kk:kk + 1]
        r_lo = r_lo + y_lo * gk
        r_hi = r_hi + y_hi * gk
    routed = jnp.concatenate([r_lo, r_hi], axis=-1)
    o_ref[...] = _layer_norm(alpha * h + (shared + routed), g_ref[...], b_ref[...])


def _combine(alpha, h, gate, yt, ws13, ws2, ln_g, ln_b):
    n, D = h.shape
    R = COMBINE_TILE

    def full(a):
        return pl.BlockSpec(a.shape, lambda i: (0,) * a.ndim)

    return pl.pallas_call(
        functools.partial(_combine_kernel, alpha),
        grid=(n // R,),
        in_specs=[pl.BlockSpec((R, D), lambda i: (i, 0)),
                  pl.BlockSpec((R, LANES), lambda i: (i, 0)),
                  pl.BlockSpec((TOP_K, R, D // 2), lambda i: (0, i, 0)),
                  full(ws13), full(ws2), full(ln_g), full(ln_b)],
        out_specs=pl.BlockSpec((R, D), lambda i: (i, 0)),
        out_shape=jax.ShapeDtypeStruct((n, D), F32),
        compiler_params=_cparams("parallel"),
        name="moe_combine",
    )(h, gate, yt, ws13, ws2, ln_g, ln_b)


def _moe(layer, alpha, h, w_router, b_router, w_e1, w_e3, w_e2, ws13, ws2, ln_g, ln_b):
    n, D = h.shape
    E = w_router.shape[1]
    BM = MOE_BLOCK
    idx, gate, rank, cnt = _router(h, w_router, b_router)
    counts = cnt[0].astype(jnp.int32)
    padded = (counts + BM - 1) // BM * BM
    end = jnp.cumsum(padded)
    start = (end - padded).astype(jnp.int32)
    n_blocks = (n * TOP_K + BM - 1) // BM + E
    block_e = jnp.minimum(jnp.searchsorted(end, jnp.arange(n_blocks, dtype=jnp.int32) * BM, side='right'),
                          E - 1).astype(jnp.int32)
    n_used = (end[-1] // BM).astype(jnp.int32).reshape(1)
    dest = _dest(idx, rank, start.astype(F32).reshape(1, E))
    dest_flat = dest[:, :TOP_K].reshape(-1)
    xg = _dispatch(h, dest_flat, _zero_tails((end // BM).astype(jnp.int32), n_blocks * BM, D // 2))
    yg = _experts(layer, block_e, n_used, xg, w_e1, w_e3, w_e2)
    dest_km = jnp.swapaxes(dest[:, :TOP_K], 0, 1).reshape(-1)
    yt = _sc_gather(yg, dest_km).reshape(TOP_K, n, D // 2)
    return _combine(alpha, h, gate, yt, ws13, ws2, ln_g, ln_b)


def kernel(x_prompt, x_sample, state_wkv, state_shift, state_conv, cache_mem_k, cache_mem_v, mem_prompt, w_in, w_in_vmix, mu_shift, mu_vmix, w0, w_up, a0, a_up, v0, v_up, g_up, k_k, k_a, r_k, lnx_g, lnx_b, w_o_a, conv_w, conv_b, cn_g, cn_b, w_o_b, w_mem_kv, w_o_c, w_out, ln1_g, ln1_b, ln2_g, ln2_b, w_router, b_router, w_e1, w_e3, w_e2, w_s1, w_s3, w_s2):
    B, T, D = x_prompt.shape
    SB, ST, _ = x_sample.shape
    L = w_in.shape[0]
    M = mem_prompt.shape[1]
    R = ROW_TILE
    n_p, n_s = B * T, SB * ST
    n = n_p + n_s
    assert n_s == R and SB == LANES and T % R == 0 and T % CHUNK == 0 and D % LANES == 0
    npt, tps = n_p // R, T // R
    alpha = (2.0 * L) ** 0.25
    off_glu = RWKV_COLS
    off_q = off_glu + 2 * CONV_CH
    off_gate = off_q + MEM_WIDTH
    in_cols = off_gate + 3 * D

    x = jnp.concatenate([x_prompt.reshape(n_p, D), jnp.swapaxes(x_sample, 0, 1).reshape(n_s, D)], axis=0)
    mem_rows = mem_prompt.reshape(B * M, D)
    row2 = lambda a: a.reshape(1, -1)
    v_first = None
    outs = {k: [] for k in ('wkv_p', 'shift_p', 'conv_p', 'mk', 'mv', 'wkv_s', 'shift_s', 'conv_s')}

    for l in range(L):
        first = l == 0
        w_in_l = w_in[l].astype(BF16)
        w_rwkv = w_in_l[:, :off_glu]
        p_rwkv = _matmul(x, w_rwkv, R, RWKV_COLS // 2)
        p_glu = _matmul(x, w_in_l[:, off_glu:off_q], R, 2 * CONV_CH)
        p_q = _matmul(x, w_in_l[:, off_q:off_gate], R, MEM_WIDTH)
        p_gate = _matmul(x, w_in_l[:, off_gate:in_cols], R, D)
        x_prev_s = state_shift[l]
        prev_rwkv = _matmul(x_prev_s, w_rwkv, SB, RWKV_COLS // 2)

        wup_pad = jnp.concatenate([w_up[l], jnp.zeros((LORA_A, RW), F32)], axis=0).astype(BF16)
        aup_pad = jnp.concatenate([jnp.zeros((LORA_W, RW), F32), a_up[l]], axis=0).astype(BF16)
        lora = (row2(w0[l]), wup_pad, row2(a0[l]), aup_pad, g_up[l].astype(BF16), row2(k_k[l]), row2(k_a[l]),
                row2(r_k[l]), _seg_ones())
        if first:
            pre = _rwkv_pre(True, npt, tps, p_rwkv, prev_rwkv, row2(mu_shift[l]), lora)
        else:
            wv_pad = jnp.pad(w_in_vmix[l - 1], ((0, 0), (0, LANES - LORA_V))).astype(BF16)
            p_vd = _matmul(x, wv_pad, R, LANES)
            prev_vd = _matmul(x_prev_s, wv_pad, SB, LANES)
            mu_v = jnp.pad(mu_vmix[l - 1], (0, LANES - LORA_V)).reshape(1, LANES)
            vup_pad = jnp.pad(v_up[l - 1], ((0, LANES - LORA_V), (0, 0))).astype(BF16)
            pre = _rwkv_pre(False, npt, tps, p_rwkv, prev_rwkv, row2(mu_shift[l]), lora, p_vd, prev_vd, mu_v,
                            v_first, row2(v0[l - 1]), vup_pad)
        r_, lw_, k_, v_, nkk_, bb_, g_, bonus_ = pre
        if first:
            v_first = v_

        y_p, s_p = _wkv_prompt(B, T, r_, lw_, k_, v_, nkk_, bb_)

        def lanes_b(a):
            return jnp.transpose(a[n_p:].reshape(ST, SB, RW), (0, 2, 1)).reshape(ST, HEADS, HEAD_DIM, SB)

        state_sz = HEADS * HEAD_DIM * HEAD_DIM
        s0 = _transpose_layer(state_wkv.astype(F32).reshape(L, SB, state_sz), l, 2048)
        s0 = s0.reshape(HEADS, HEAD_DIM, HEAD_DIM, SB)
        y_s, s_s = _wkv_sample(s0, lanes_b(r_), lanes_b(lw_), lanes_b(k_),
                               lanes_b(v_).reshape(ST, HEADS, HEAD_DIM, 1, SB), lanes_b(nkk_), lanes_b(bb_))
        y_s = jnp.transpose(y_s.reshape(ST, RW, SB), (0, 2, 1)).reshape(n_s, RW)
        s_s = _transpose_2d(s_s.reshape(state_sz, SB), 2048).reshape(SB, HEADS, HEAD_DIM, HEAD_DIM)
        z_a = _rwkv_post(npt, y_p, y_s, bonus_, g_, row2(lnx_g[l]), row2(lnx_b[l]))

        cvec = (conv_w[l], row2(conv_b[l]), row2(cn_g[l]), row2(cn_b[l]))
        zb_p, conv_p = _conv_prompt(B, T, p_glu, *cvec)
        st_t = jnp.swapaxes(state_conv[l], 0, 1)
        zb_s, glu_s = _conv_sample(st_t, p_glu[n_p:].reshape(ST, SB, 2 * CONV_CH), *cvec)
        zb_s = zb_s.reshape(n_s, CONV_CH)
        conv_s = jnp.concatenate([state_conv[l][:, ST:], jnp.swapaxes(glu_s, 0, 1)], axis=1)

        kv_p = _matmul(mem_rows, w_mem_kv[l].astype(BF16), M, 2 * MEM_WIDTH)
        kv3 = kv_p.reshape(B, M, 2 * MEM_WIDTH)
        zc_p = _attention(p_q[:n_p].reshape(B, T, MEM_WIDTH), kv3, kv3, 0, 1, 1, R).reshape(n_p, MEM_WIDTH)
        q_s = jnp.swapaxes(p_q[n_p:].reshape(ST, SB, MEM_WIDTH), 0, 1)
        q_s = jnp.pad(q_s, ((0, 0), (0, 8 - ST), (0, 0)))
        zc_s = _attention_cached(q_s, cache_mem_k.reshape(L, SB, M * MEM_HEADS, MEM_HEAD_DIM),
                                 cache_mem_v.reshape(L, SB, M * MEM_HEADS, MEM_HEAD_DIM), l, 8)[:, :ST]
        zc_s = jnp.swapaxes(zc_s, 0, 1).reshape(n_s, MEM_WIDTH)

        h = _merge(n_p, alpha, z_a, zb_p, zb_s, zc_p, zc_s, p_gate, x,
                   w_o_a[l].astype(BF16), w_o_b[l].astype(BF16), w_o_c[l].astype(BF16), w_out[l].astype(BF16),
                   row2(ln1_g[l]), row2(ln1_b[l]))

        ws13 = jnp.concatenate([w_s1[l], w_s3[l]], axis=1).astype(BF16)
        x_new = _moe(l, alpha, h, w_router[l].astype(BF16), row2(b_router[l]), w_e1, w_e3, w_e2,
                     ws13, w_s2[l].astype(BF16), row2(ln2_g[l]), row2(ln2_b[l]))

        outs['wkv_p'].append(s_p.astype(state_wkv.dtype))
        outs['shift_p'].append(x[:n_p].reshape(B, T, D)[:, -1].astype(state_shift.dtype))
        outs['conv_p'].append(conv_p.astype(state_conv.dtype))
        outs['mk'].append(kv3[..., :MEM_WIDTH].reshape(B, M, MEM_HEADS, MEM_HEAD_DIM).astype(cache_mem_k.dtype))
        outs['mv'].append(kv3[..., MEM_WIDTH:].reshape(B, M, MEM_HEADS, MEM_HEAD_DIM).astype(cache_mem_v.dtype))
        outs['wkv_s'].append(s_s.astype(state_wkv.dtype))
        outs['shift_s'].append(x[n_p + (ST - 1) * SB:].astype(state_shift.dtype))
        outs['conv_s'].append(conv_s.astype(state_conv.dtype))
        x = x_new

    y_p = x[:n_p].reshape(B, T, D)
    y_s = jnp.swapaxes(x[n_p:].reshape(ST, SB, D), 0, 1)
    return (y_p, y_s, jnp.stack(outs['wkv_p']), jnp.stack(outs['shift_p']), jnp.stack(outs['conv_p']),
            jnp.stack(outs['mk']), jnp.stack(outs['mv']), jnp.stack(outs['wkv_s']), jnp.stack(outs['shift_s']),
            jnp.stack(outs['conv_s']))
```

```python
import functools
import math

import jax
import jax.numpy as jnp
from jax import lax
from jax.experimental import pallas as pl
from jax.experimental.pallas import tpu as pltpu
from jax.experimental.pallas import tpu_sc as plsc

F32 = jnp.float32
BF16 = jnp.bfloat16

HEADS = 8
HEAD_DIM = 64
RW = HEADS * HEAD_DIM
LORA_W = 64
LORA_A = 64
LORA_V = 32
LORA_G = 128
RWKV_COLS = 3 * RW + LORA_W + LORA_A + LORA_G
GN_EPS = HEAD_DIM * 1e-5
CONV_WIDTH = 31
CONV_BUF = CONV_WIDTH - 1
CONV_CH = 512
MEM_HEADS = 4
MEM_HEAD_DIM = 128
MEM_WIDTH = MEM_HEADS * MEM_HEAD_DIM
TOP_K = 8
ROUTED_SCALE = 2.5
LN_EPS = 1e-5
EXP_M05 = math.exp(-0.5)

LANES = 128
ROW_TILE = 512
MERGE_TILE = 256
CHUNK = 64
NEUMANN_BLOCK = 16
MOE_BLOCK = 512
DMA_QUEUES = 2
SC_CORES = 2
SC_SUBCORES = 16
SC_WINDOW = 128
SC_DISPATCH_WINDOW = 48
SC_PAD_WINDOW = 64
DISPATCH_TILE = 256
COMBINE_TILE = 128
VMEM_LIMIT = 48 * 1024 * 1024


def _cparams(*sem, vmem=VMEM_LIMIT):
    return pltpu.CompilerParams(dimension_semantics=tuple(sem), vmem_limit_bytes=vmem)


def _bdot(a, b):
    return jnp.dot(a.astype(BF16), b.astype(BF16), preferred_element_type=F32)


def _bdot_nt(a, b):
    return lax.dot_general(a.astype(BF16), b.astype(BF16), (((1,), (1,)), ((), ())),
                           preferred_element_type=F32)


def _sigmoid(x):
    return 1.0 / (1.0 + jnp.exp(-x))


def _silu(x):
    return x * _sigmoid(x)


def _layer_norm(x, g, b):
    mu = jnp.mean(x, axis=-1, keepdims=True)
    d = x - mu
    var = jnp.mean(d * d, axis=-1, keepdims=True)
    return d * lax.rsqrt(var + LN_EPS) * g + b


SEG_LANES = 256


def _seg_ones():
    i = jnp.arange(SEG_LANES)
    return ((i[:, None] // HEAD_DIM) == (i[None, :] // HEAD_DIM)).astype(BF16)


def _seg_sum(x, seg):
    hi = x.astype(BF16)
    lo = (x - hi.astype(F32)).astype(BF16)
    parts = []
    for j in range(x.shape[-1] // SEG_LANES):
        sl = slice(j * SEG_LANES, (j + 1) * SEG_LANES)
        parts.append(jnp.dot(hi[:, sl], seg, preferred_element_type=F32)
                     + jnp.dot(lo[:, sl], seg, preferred_element_type=F32))
    return jnp.concatenate(parts, axis=-1)


def _pack_pair(lo, hi):
    lo_b = lax.bitcast_convert_type(lo.astype(BF16).astype(F32), jnp.int32)
    hi_b = lax.bitcast_convert_type(hi.astype(BF16).astype(F32), jnp.int32)
    return lax.shift_right_logical(lo_b, jnp.full(lo_b.shape, 16, jnp.int32)) | (hi_b & jnp.int32(-65536))


def _unpack_pair(p):
    lo = lax.bitcast_convert_type(lax.shift_left(p, jnp.full(p.shape, 16, jnp.int32)), F32)
    hi = lax.bitcast_convert_type(p & jnp.int32(-65536), F32)
    return lo, hi


def _mm_kernel(x_ref, w_ref, o_ref):
    o_ref[...] = _bdot(x_ref[...], w_ref[...]).astype(o_ref.dtype)


def _matmul(x, w, tm, tn):
    m, k = x.shape
    n_cols = w.shape[1]
    assert m % tm == 0 and n_cols % tn == 0
    return pl.pallas_call(
        _mm_kernel,
        grid=(m // tm, n_cols // tn),
        in_specs=[pl.BlockSpec((tm, k), lambda i, j: (i, 0)),
                  pl.BlockSpec((k, tn), lambda i, j: (0, j))],
        out_specs=pl.BlockSpec((tm, tn), lambda i, j: (i, j)),
        out_shape=jax.ShapeDtypeStruct((m, n_cols), F32),
        compiler_params=_cparams("parallel", "parallel"),
        name=f"matmul_{n_cols}",
    )(x, w)


def _pre_kernel(first, npt, tps, *refs):
    if first:
        (p_ref, halo_ref, prev_ref, mu_ref,
         w0_ref, wup_ref, a0_ref, aup_ref, gup_ref, kk_ref, ka_ref, rk_ref, seg_ref,
         r_o, lw_o, k_o, v_o, nkk_o, bb_o, g_o, bon_o, sh_ref) = refs
    else:
        (p_ref, halo_ref, prev_ref, mu_ref, pv_ref, halov_ref, prevv_ref, muv_ref, vf_ref, v0_ref, vup_ref,
         w0_ref, wup_ref, a0_ref, aup_ref, gup_ref, kk_ref, ka_ref, rk_ref, seg_ref,
         r_o, lw_o, k_o, v_o, nkk_o, bb_o, g_o, bon_o, sh_ref, shv_ref) = refs
    i = pl.program_id(0)
    rows_n = p_ref.shape[0]

    def build_shift(src_ref, halo_r, prev_r, dst_ref):
        @pl.when(i < npt)
        def _():
            x = src_ref[...]
            rolled = pltpu.roll(x, 1, axis=0)
            last = halo_r[7:8, :]
            row0 = jnp.where((i % tps) == 0, jnp.zeros_like(last), last)
            rows = lax.broadcasted_iota(jnp.int32, x.shape, 0)
            dst_ref[...] = jnp.where(rows == 0, row0, rolled)

        @pl.when(i >= npt)
        def _():
            nb = prev_r.shape[0]
            dst_ref[0:nb, :] = prev_r[...]
            dst_ref[nb:rows_n, :] = src_ref[0:rows_n - nb, :]

    build_shift(p_ref, halo_ref, prev_ref, sh_ref)
    p = p_ref[...]
    p = p + (sh_ref[...] - p) * mu_ref[...]
    r = p[:, 0:RW]
    k = p[:, RW:2 * RW]
    v = p[:, 2 * RW:3 * RW]
    u = p[:, 3 * RW:3 * RW + LORA_W + LORA_A]
    gd = p[:, 3 * RW + LORA_W + LORA_A:RWKV_COLS]

    z = w0_ref[...] + _bdot(jnp.tanh(u), wup_ref[...])
    lw = -EXP_M05 * _sigmoid(z)
    a = _sigmoid(a0_ref[...] + _bdot(u, aup_ref[...]))
    g = _bdot(_sigmoid(gd), gup_ref[...])
    if not first:
        build_shift(pv_ref, halov_ref, prevv_ref, shv_ref)
        pv = pv_ref[...]
        pv = pv + (shv_ref[...] - pv) * muv_ref[...]
        mix = _sigmoid(v0_ref[...] + _bdot(pv, vup_ref[...]))
        v = v + (vf_ref[...] - v) * mix
    kk = k * kk_ref[...]
    seg = seg_ref[...]
    nrm = jnp.sqrt(_seg_sum(kk * kk, seg))
    kkn = kk / jnp.maximum(nrm, 1e-12)
    k2 = k * (1.0 + (a - 1.0) * ka_ref[...])
    bonus = _seg_sum(r * k2 * rk_ref[...], seg) * v
    r_o[...] = r
    lw_o[...] = lw
    k_o[...] = k2
    v_o[...] = v
    nkk_o[...] = -kkn
    bb_o[...] = kkn * a
    g_o[...] = g
    bon_o[...] = bonus


def _rwkv_pre(first, npt, tps, p_rwkv, prev_rwkv, mu, lora, p_vd=None, prev_vd=None, mu_v=None, v_first=None,
              v0=None, vup=None):
    n = p_rwkv.shape[0]
    R = ROW_TILE
    nt = n // R

    def row_spec(w):
        return pl.BlockSpec((R, w), lambda i: (i, 0))

    def halo_spec(w):
        return pl.BlockSpec((8, w), lambda i: (jnp.maximum(i * (R // 8) - 1, 0), 0))

    def full(a):
        return pl.BlockSpec(a.shape, lambda i: (0,) * a.ndim)

    ins = [p_rwkv, p_rwkv, prev_rwkv, mu]
    specs = [row_spec(RWKV_COLS), halo_spec(RWKV_COLS), full(prev_rwkv), full(mu)]
    scratch = [pltpu.VMEM((R, RWKV_COLS), F32)]
    if not first:
        ins += [p_vd, p_vd, prev_vd, mu_v, v_first, v0, vup]
        specs += [row_spec(LANES), halo_spec(LANES), full(prev_vd), full(mu_v), row_spec(RW), full(v0), full(vup)]
        scratch.append(pltpu.VMEM((R, LANES), F32))
    ins += list(lora)
    specs += [full(a) for a in lora]
    outs = pl.pallas_call(
        functools.partial(_pre_kernel, first, npt, tps),
        grid=(nt,),
        in_specs=specs,
        out_specs=[row_spec(RW)] * 8,
        out_shape=[jax.ShapeDtypeStruct((n, RW), F32)] * 8,
        scratch_shapes=scratch,
        compiler_params=_cparams("parallel"),
        name="rwkv_pre",
    )(*ins)
    return outs


def _wkv_chunk_kernel(nbat, *refs):
    in_refs = refs[:6 * nbat]
    y_ref, s_out_ref, s_ref = refs[6 * nbat:]
    c = pl.program_id(1)
    C = in_refs[0].shape[0]
    G = 4 * HEAD_DIM
    assert 4 * C == G

    @pl.when(c == 0)
    def _():
        s_ref[...] = jnp.zeros_like(s_ref)

    ri = lax.broadcasted_iota(jnp.int32, (G, G), 0)
    ci = lax.broadcasted_iota(jnp.int32, (G, G), 1)
    same_head = (ri // C) == (ci // HEAD_DIM)
    tril_s = same_head & ((ci % C) < (ri % C))
    tril_i = same_head & ((ci % C) <= (ri % C))
    diag_blk = (ri // NEUMANN_BLOCK) == (ci // NEUMANN_BLOCK)
    eye = (ri == ci).astype(F32)

    def stack(x):
        return jnp.where(same_head, jnp.concatenate([x, x, x, x], axis=0), 0.0).astype(BF16)

    def tile4(x):
        return jnp.concatenate([x, x, x, x], axis=0).astype(BF16)

    def collapse(x4):
        return x4[0:C] + x4[C:2 * C] + x4[2 * C:3 * C] + x4[3 * C:4 * C]

    for b in range(nbat):
        r_ref, lw_ref, k_ref, v_ref, nkk_ref, bb_ref = in_refs[6 * b:6 * b + 6]
        lw = lw_ref[...]
        rows = lax.broadcasted_iota(jnp.int32, lw.shape, 0)
        cs = lw
        s = 1
        while s < C:
            cs = cs + jnp.where(rows >= s, pltpu.roll(cs, s, axis=0), 0.0)
            s *= 2
        cs_last = cs[C - 1:C, :]
        e_neg = jnp.exp(-cs)
        e_tail = jnp.exp(cs_last - cs)
        kk_ = k_ref[...]
        bb_ = bb_ref[...]
        at = nkk_ref[...] * jnp.exp(cs - lw)
        rt = r_ref[...] * jnp.exp(cs)
        bt = bb_ * e_neg
        kt = kk_ * e_neg
        bw = bb_ * e_tail
        kw = kk_ * e_tail
        vv = v_ref[...]
        w_end = jnp.exp(cs_last)

        for g in range(RW // G):
            sl = slice(g * G, (g + 1) * G)
            at4, rt4, v4 = stack(at[:, sl]), stack(rt[:, sl]), stack(vv[:, sl])
            bt4, kt4 = tile4(bt[:, sl]), tile4(kt[:, sl])
            a_ab = jnp.where(tril_s, _bdot_nt(at4, bt4), 0.0)
            a_ak = jnp.where(tril_s, _bdot_nt(at4, kt4), 0.0).astype(BF16)
            a_rb = jnp.where(tril_i, _bdot_nt(rt4, bt4), 0.0).astype(BF16)
            a_rk = jnp.where(tril_i, _bdot_nt(rt4, kt4), 0.0).astype(BF16)
            a_d = jnp.where(diag_blk, a_ab, 0.0)
            a_o = a_ab - a_d
            a_db = a_d.astype(BF16)
            x2 = _bdot(a_db, a_db).astype(BF16)
            x4 = _bdot(x2, x2).astype(BF16)
            x8 = _bdot(x4, x4)
            t_d = eye + a_d
            t_d = t_d + _bdot(t_d, x2)
            t_d = t_d + _bdot(t_d, x4)
            t_d = t_d + _bdot(t_d, x8)
            t_db = t_d.astype(BF16)
            nn = _bdot(t_db, a_o).astype(BF16)
            n2 = _bdot(nn, nn)
            m1 = t_d + _bdot(nn, t_db)
            t_m = m1 + _bdot(n2, m1)
            s4 = s_ref[b, g]
            s4b = s4.astype(BF16)
            u4 = _bdot(t_m, _bdot_nt(at4, s4b) + _bdot(a_ak, v4))
            y4 = _bdot_nt(rt4, s4b) + _bdot(jnp.concatenate([a_rb, a_rk], axis=1),
                                            jnp.concatenate([u4.astype(BF16), v4], axis=0))
            y_ref[b, :, sl] = collapse(y4)
            uv_t = jnp.concatenate([collapse(u4), vv[:, sl]], axis=0).T
            upd = _bdot(uv_t, jnp.concatenate([bw[:, sl], kw[:, sl]], axis=0))
            s_ref[b, g] = s4 * w_end[:, sl] + jnp.where((ri // HEAD_DIM) == (ci // HEAD_DIM), upd, 0.0)

    @pl.when(c == pl.num_programs(1) - 1)
    def _():
        s_out_ref[...] = s_ref[...]


def _wkv_prompt(B, T, r, lw, k, v, nkk, bb):
    C = CHUNK
    nc = T // C
    G = 4 * HEAD_DIM
    nbat = 2 if B % 2 == 0 else 1
    specs = []
    for j in range(nbat):
        specs += [pl.BlockSpec((C, RW), lambda i, c, j=j: ((nbat * i + j) * nc + c, 0))] * 6
    y, s = pl.pallas_call(
        functools.partial(_wkv_chunk_kernel, nbat),
        grid=(B // nbat, nc),
        in_specs=specs,
        out_specs=[pl.BlockSpec((nbat, C, RW), lambda i, c: (i, c, 0)),
                   pl.BlockSpec((nbat, RW // G, G, G), lambda i, c: (i, 0, 0, 0))],
        out_shape=[jax.ShapeDtypeStruct((B, T, RW), F32),
                   jax.ShapeDtypeStruct((B, RW // G, G, G), F32)],
        scratch_shapes=[pltpu.VMEM((nbat, RW // G, G, G), F32)],
        compiler_params=_cparams("parallel", "arbitrary"),
        name="wkv_chunk",
    )(*([r, lw, k, v, nkk, bb] * nbat))
    y = y.reshape(B * T, RW)
    s = s.reshape(B, RW // G, 4, HEAD_DIM, 4, HEAD_DIM)
    idx = jnp.arange(4)
    s = s[:, :, idx, :, idx, :]
    s = jnp.transpose(s, (1, 2, 0, 3, 4)).reshape(B, HEADS, HEAD_DIM, HEAD_DIM)
    return y, s


def _wkv_seq_kernel(s0_ref, r_ref, lw_ref, k_ref, v_ref, nkk_ref, bb_ref, y_ref, s_ref):
    steps = r_ref.shape[0]

    def body(vi, carry):
        st = s0_ref[0, vi]
        for t in range(steps):
            sa = jnp.sum(st * nkk_ref[t, 0], axis=0, keepdims=True)
            st = st * jnp.exp(lw_ref[t, 0]) + sa * bb_ref[t, 0] + v_ref[t, 0, vi] * k_ref[t, 0]
            y_ref[t, 0, vi] = jnp.sum(st * r_ref[t, 0], axis=0, keepdims=True)
        s_ref[0, vi] = st
        return carry

    lax.fori_loop(0, HEAD_DIM, body, 0)


def _wkv_sample(s0, r, lw, k, v, nkk, bb):
    steps, _, _, nb = r.shape
    vec = pl.BlockSpec((steps, 1, HEAD_DIM, nb), lambda h: (0, h, 0, 0))
    vcol = pl.BlockSpec((steps, 1, HEAD_DIM, 1, nb), lambda h: (0, h, 0, 0, 0))
    st = pl.BlockSpec((1, HEAD_DIM, HEAD_DIM, nb), lambda h: (h, 0, 0, 0))
    y, s = pl.pallas_call(
        _wkv_seq_kernel,
        grid=(HEADS,),
        in_specs=[st, vec, vec, vec, vcol, vec, vec],
        out_specs=[vcol, st],
        out_shape=[jax.ShapeDtypeStruct((steps, HEADS, HEAD_DIM, 1, nb), F32),
                   jax.ShapeDtypeStruct(s0.shape, F32)],
        compiler_params=_cparams("parallel"),
        name="wkv_seq",
    )(s0, r, lw, k, v, nkk, bb)
    return y, s


def _tr_kernel(x_ref, o_ref):
    o_ref[...] = x_ref[...].T


def _transpose_layer(x3, layer, tc):
    _, a, c = x3.shape
    return pl.pallas_call(
        _tr_kernel,
        grid=(c // tc,),
        in_specs=[pl.BlockSpec((None, a, tc), lambda j: (layer, 0, j))],
        out_specs=pl.BlockSpec((tc, a), lambda j: (j, 0)),
        out_shape=jax.ShapeDtypeStruct((c, a), x3.dtype),
        compiler_params=_cparams("parallel"),
        name="transpose_in",
    )(x3)


def _transpose_2d(x, tc):
    c, a = x.shape
    return pl.pallas_call(
        _tr_kernel,
        grid=(c // tc,),
        in_specs=[pl.BlockSpec((tc, a), lambda j: (j, 0))],
        out_specs=pl.BlockSpec((a, tc), lambda j: (0, j)),
        out_shape=jax.ShapeDtypeStruct((a, c), x.dtype),
        compiler_params=_cparams("parallel"),
        name="transpose_out",
    )(x)


def _post_kernel(npt, yp_ref, ys_ref, bon_ref, g_ref, lg_ref, lb_ref, seg_ref, o_ref):
    i = pl.program_id(0)
    y = jnp.where(i < npt, yp_ref[...], ys_ref[...])
    seg = seg_ref[...]
    mu = _seg_sum(y, seg) * (1.0 / HEAD_DIM)
    d = y - mu
    var = _seg_sum(d * d, seg) * (1.0 / HEAD_DIM)
    yn = d * lax.rsqrt(var + GN_EPS) * lg_ref[...] + lb_ref[...]
    o_ref[...] = (yn + bon_ref[...]) * g_ref[...]


def _rwkv_post(npt, y_p, y_s, bonus, g, lnx_g, lnx_b):
    n = bonus.shape[0]
    R = ROW_TILE
    row = pl.BlockSpec((R, RW), lambda i: (i, 0))
    vec = pl.BlockSpec((1, RW), lambda i: (0, 0))
    return pl.pallas_call(
        functools.partial(_post_kernel, npt),
        grid=(n // R,),
        in_specs=[pl.BlockSpec((R, RW), lambda i: (jnp.minimum(i, npt - 1), 0)),
                  pl.BlockSpec((R, RW), lambda i: (0, 0)), row, row, vec, vec,
                  pl.BlockSpec((SEG_LANES, SEG_LANES), lambda i: (0, 0))],
        out_specs=row,
        out_shape=jax.ShapeDtypeStruct((n, RW), F32),
        compiler_params=_cparams("parallel"),
        name="rwkv_post",
    )(y_p, y_s, bonus, g, lnx_g, lnx_b, _seg_ones())


def _glu(p):
    return p[..., :CONV_CH] * _sigmoid(p[..., CONV_CH:])


def _conv_prompt_kernel(p_ref, halo_ref, w_ref, cb_ref, g_ref, b_ref, o_ref, tail_ref, ext_ref):
    j = pl.program_id(1)
    R = p_ref.shape[0]
    H = halo_ref.shape[0]
    glu = _glu(p_ref[...])
    halo = _glu(halo_ref[...])
    ext_ref[0:H, :] = jnp.where(j == 0, jnp.zeros_like(halo), halo)
    ext_ref[H:H + R, :] = glu
    acc = jnp.zeros((R, CONV_CH), F32)
    for t in range(CONV_WIDTH):
        acc = acc + ext_ref[pl.ds(H - CONV_BUF + t, R), :] * w_ref[t:t + 1, :]
    h = _layer_norm(acc + cb_ref[...], g_ref[...], b_ref[...])
    o_ref[...] = _silu(h)

    @pl.when(j == pl.num_programs(1) - 1)
    def _():
        tail_ref[0] = glu[R - H:R, :]


def _conv_prompt(B, T, p_glu, conv_w, conv_b, cn_g, cn_b):
    R = ROW_TILE
    H = 32
    tps = T // R
    vec = pl.BlockSpec((1, CONV_CH), lambda b, j: (0, 0))
    z, tail = pl.pallas_call(
        _conv_prompt_kernel,
        grid=(B, tps),
        in_specs=[pl.BlockSpec((R, 2 * CONV_CH), lambda b, j: (b * tps + j, 0)),
                  pl.BlockSpec((H, 2 * CONV_CH), lambda b, j: (jnp.maximum((b * tps + j) * (R // H) - 1, 0), 0)),
                  pl.BlockSpec((CONV_WIDTH, CONV_CH), lambda b, j: (0, 0)), vec, vec, vec],
        out_specs=[pl.BlockSpec((R, CONV_CH), lambda b, j: (b * tps + j, 0)),
                   pl.BlockSpec((1, H, CONV_CH), lambda b, j: (b, 0, 0))],
        out_shape=[jax.ShapeDtypeStruct((B * T, CONV_CH), F32),
                   jax.ShapeDtypeStruct((B, H, CONV_CH), F32)],
        scratch_shapes=[pltpu.VMEM((H + R, CONV_CH), F32)],
        compiler_params=_cparams("parallel", "arbitrary"),
        name="conv_prompt",
    )(p_glu, p_glu, conv_w, conv_b, cn_g, cn_b)
    return z, tail[:, H - CONV_BUF:, :]


def _conv_sample_kernel(st_ref, p_ref, w_ref, cb_ref, g_ref, b_ref, o_ref, glu_ref):
    steps = p_ref.shape[0]
    glu = [_glu(p_ref[t]) for t in range(steps)]
    for t in range(steps):
        glu_ref[t] = glu[t]
    for t in range(steps):
        acc = jnp.zeros(glu[0].shape, F32)
        for j in range(CONV_WIDTH):
            src = t + j
            x = st_ref[src] if src < CONV_BUF else glu[src - CONV_BUF]
            acc = acc + x * w_ref[j:j + 1, :]
        h = _layer_norm(acc + cb_ref[...], g_ref[...], b_ref[...])
        o_ref[t] = _silu(h)


def _conv_sample(state_t, p_glu, conv_w, conv_b, cn_g, cn_b):
    steps, nb, _ = p_glu.shape

    def full(a):
        return pl.BlockSpec(a.shape, lambda i: (0,) * a.ndim)

    ins = (state_t, p_glu, conv_w, conv_b, cn_g, cn_b)
    return pl.pallas_call(
        _conv_sample_kernel,
        grid=(1,),
        in_specs=[full(a) for a in ins],
        out_specs=[pl.BlockSpec((steps, nb, CONV_CH), lambda i: (0, 0, 0))] * 2,
        out_shape=[jax.ShapeDtypeStruct((steps, nb, CONV_CH), F32)] * 2,
        compiler_params=_cparams("arbitrary"),
        name="conv_sample",
    )(*ins)


def _attn_kernel(q_ref, k_ref, v_ref, o_ref):
    nb = q_ref.shape[0]
    scale = MEM_HEAD_DIM ** -0.5
    for bi in range(nb):
        for h in range(MEM_HEADS):
            sl = slice(h * MEM_HEAD_DIM, (h + 1) * MEM_HEAD_DIM)
            s = _bdot_nt(q_ref[bi, :, sl], k_ref[bi, :, sl]) * scale
            m = jnp.max(s, axis=-1, keepdims=True)
            e = jnp.exp(s - m)
            pr = e / jnp.sum(e, axis=-1, keepdims=True)
            o_ref[bi, :, sl] = _bdot(pr, v_ref[bi, :, sl])


def _attn_rows_kernel(q_ref, k_ref, v_ref, o_ref):
    nb = q_ref.shape[0]
    n_mem = k_ref.shape[1] // MEM_HEADS
    scale = MEM_HEAD_DIM ** -0.5
    for bi in range(nb):
        for h in range(MEM_HEADS):
            sl = slice(h * MEM_HEAD_DIM, (h + 1) * MEM_HEAD_DIM)
            rows = pl.ds(h, n_mem, stride=MEM_HEADS)
            s = _bdot_nt(q_ref[bi, :, sl], k_ref[bi, rows, :]) * scale
            m = jnp.max(s, axis=-1, keepdims=True)
            e = jnp.exp(s - m)
            pr = e / jnp.sum(e, axis=-1, keepdims=True)
            o_ref[bi, :, sl] = _bdot(pr, v_ref[bi, rows, :])


def _attention_cached(q, cache_k, cache_v, layer, bb):
    B, T, W = q.shape
    rows = cache_k.shape[2]
    kv_spec = pl.BlockSpec((None, bb, rows, MEM_HEAD_DIM), lambda b: (layer, b, 0, 0))
    return pl.pallas_call(
        _attn_rows_kernel,
        grid=(B // bb,),
        in_specs=[pl.BlockSpec((bb, T, W), lambda b: (b, 0, 0)), kv_spec, kv_spec],
        out_specs=pl.BlockSpec((bb, T, W), lambda b: (b, 0, 0)),
        out_shape=jax.ShapeDtypeStruct((B, T, W), F32),
        compiler_params=_cparams("parallel"),
        name="mem_attention_cached",
    )(q, cache_k, cache_v)


def _attention(q, mk, mv, kcol, vcol, bb, tq):
    B, T, W = q.shape
    M = mk.shape[1]
    return pl.pallas_call(
        _attn_kernel,
        grid=(B // bb, T // tq),
        in_specs=[pl.BlockSpec((bb, tq, W), lambda b, j: (b, j, 0)),
                  pl.BlockSpec((bb, M, W), lambda b, j: (b, 0, kcol)),
                  pl.BlockSpec((bb, M, W), lambda b, j: (b, 0, vcol))],
        out_specs=pl.BlockSpec((bb, tq, W), lambda b, j: (b, j, 0)),
        out_shape=jax.ShapeDtypeStruct((B, T, W), F32),
        compiler_params=_cparams("parallel", "parallel"),
        name=f"mem_attention_{tq}",
    )(q, mk, mv)


def _merge_kernel(npt, alpha, za_ref, zbp_ref, zbs_ref, zcp_ref, zcs_ref, gate_ref, x_ref,
                  woa_ref, wob_ref, woc_ref, wout_ref, g_ref, b_ref, o_ref, pk_ref):
    i = pl.program_id(0)
    D = x_ref.shape[1]
    is_p = i < npt
    zb = jnp.where(is_p, zbp_ref[...], zbs_ref[...])
    zc = jnp.where(is_p, zcp_ref[...], zcs_ref[...])
    merged = (_sigmoid(gate_ref[:, 0:D]) * _bdot(za_ref[...], woa_ref[...])
              + _sigmoid(gate_ref[:, D:2 * D]) * _bdot(zb, wob_ref[...])
              + _sigmoid(gate_ref[:, 2 * D:3 * D]) * _bdot(zc, woc_ref[...]))
    h = _layer_norm(alpha * x_ref[...] + _bdot(merged, wout_ref[...]), g_ref[...], b_ref[...])
    o_ref[...] = h
    pk_ref[...] = _pack_pair(h[:, 0:D // 2], h[:, D // 2:D])


def _merge(n_p, alpha, za, zb_p, zb_s, zc_p, zc_s, gates, x, woa, wob, woc, wout, ln_g, ln_b):
    n, D = x.shape
    R = MERGE_TILE
    npt = n_p // R

    def row(w):
        return pl.BlockSpec((R, w), lambda i: (i, 0))

    def prow(w):
        return pl.BlockSpec((R, w), lambda i: (jnp.minimum(i, npt - 1), 0))

    def srow(w):
        return pl.BlockSpec((R, w), lambda i: (jnp.maximum(i - npt, 0), 0))

    def full(a):
        return pl.BlockSpec(a.shape, lambda i: (0,) * a.ndim)

    return pl.pallas_call(
        functools.partial(_merge_kernel, npt, alpha),
        grid=(n // R,),
        in_specs=[row(RW), prow(CONV_CH), srow(CONV_CH), prow(MEM_WIDTH), srow(MEM_WIDTH), row(3 * D), row(D),
                  full(woa), full(wob), full(woc), full(wout), full(ln_g), full(ln_b)],
        out_specs=[row(D), row(D // 2)],
        out_shape=[jax.ShapeDtypeStruct((n, D), F32), jax.ShapeDtypeStruct((n, D // 2), jnp.int32)],
        compiler_params=_cparams("parallel"),
        name="branch_merge",
    )(za, zb_p, zb_s, zc_p, zc_s, gates, x, woa, wob, woc, wout, ln_g, ln_b)


def _router_kernel(h_ref, wr_ref, br_ref, idx_o, gate_o, rank_o, cnt_o, carry_ref):
    i = pl.program_id(0)
    R = h_ref.shape[0]
    E = wr_ref.shape[1]

    @pl.when(i == 0)
    def _():
        carry_ref[...] = jnp.zeros_like(carry_ref)

    scores = _sigmoid(_bdot(h_ref[...], wr_ref[...]))
    work = scores + br_ref[...]
    lane = lax.broadcasted_iota(jnp.int32, (R, E), 1).astype(F32)
    sel_mask = jnp.zeros((R, E), F32)
    idx_cols, sel_cols = [], []
    for _ in range(TOP_K):
        m = jnp.max(work, axis=-1, keepdims=True)
        first = jnp.min(jnp.where(work == m, lane, float(E)), axis=-1, keepdims=True)
        onehot = lane == first
        sel_cols.append(jnp.sum(jnp.where(onehot, scores, 0.0), axis=-1, keepdims=True))
        idx_cols.append(first)
        sel_mask = jnp.where(onehot, 1.0, sel_mask)
        work = jnp.where(onehot, -jnp.inf, work)
    total = sel_cols[0]
    for c in sel_cols[1:]:
        total = total + c
    ri = lax.broadcasted_iota(jnp.int32, (R, R), 0)
    ci = lax.broadcasted_iota(jnp.int32, (R, R), 1)
    before = jnp.dot((ci < ri).astype(BF16), sel_mask.astype(BF16), preferred_element_type=F32) + carry_ref[0:1, :]
    idx_out = jnp.zeros((R, E), F32)
    gate_out = jnp.zeros((R, E), F32)
    rank_out = jnp.zeros((R, E), F32)
    for kk in range(TOP_K):
        onehot = lane == idx_cols[kk]
        rank = jnp.sum(jnp.where(onehot, before, 0.0), axis=-1, keepdims=True)
        col = lane == float(kk)
        idx_out = jnp.where(col, idx_cols[kk], idx_out)
        gate_out = jnp.where(col, sel_cols[kk] / total * ROUTED_SCALE, gate_out)
        rank_out = jnp.where(col, rank, rank_out)
    idx_o[...] = idx_out
    gate_o[...] = gate_out
    rank_o[...] = rank_out
    carry_ref[0:1, :] = carry_ref[0:1, :] + jnp.sum(sel_mask, axis=0, keepdims=True)
    cnt_o[...] = carry_ref[...]


def _router(h, w_router, b_router):
    n, D = h.shape
    E = w_router.shape[1]
    assert E == LANES
    R = 256
    row = pl.BlockSpec((R, E), lambda i: (i, 0))
    return pl.pallas_call(
        _router_kernel,
        grid=(n // R,),
        in_specs=[pl.BlockSpec((R, D), lambda i: (i, 0)),
                  pl.BlockSpec((D, E), lambda i: (0, 0)),
                  pl.BlockSpec((1, E), lambda i: (0, 0))],
        out_specs=[row, row, row, pl.BlockSpec((8, E), lambda i: (0, 0))],
        out_shape=[jax.ShapeDtypeStruct((n, E), F32), jax.ShapeDtypeStruct((n, E), F32),
                   jax.ShapeDtypeStruct((n, E), F32), jax.ShapeDtypeStruct((8, E), F32)],
        scratch_shapes=[pltpu.VMEM((8, E), F32)],
        compiler_params=_cparams("arbitrary"),
        name="moe_router",
    )(h, w_router, b_router)


def _dest_kernel(idx_ref, rank_ref, start_ref, o_ref):
    R, E = idx_ref.shape
    lane = lax.broadcasted_iota(jnp.int32, (R, E), 1).astype(F32)
    idx = idx_ref[...]
    out = jnp.zeros((R, E), F32)
    start = start_ref[...]
    for kk in range(TOP_K):
        e_k = jnp.sum(jnp.where(lane == float(kk), idx, 0.0), axis=-1, keepdims=True)
        s_k = jnp.sum(jnp.where(lane == e_k, start, 0.0), axis=-1, keepdims=True)
        out = jnp.where(lane == float(kk), s_k, out)
    o_ref[...] = (out + rank_ref[...]).astype(jnp.int32)


def _dest(idx, rank, start):
    n, E = idx.shape
    R = 256
    row = pl.BlockSpec((R, E), lambda i: (i, 0))
    return pl.pallas_call(
        _dest_kernel,
        grid=(n // R,),
        in_specs=[row, row, pl.BlockSpec((1, E), lambda i: (0, 0))],
        out_specs=row,
        out_shape=jax.ShapeDtypeStruct((n, E), jnp.int32),
        compiler_params=_cparams("parallel"),
        name="moe_dest",
    )(idx, rank, start)


def _sc_dispatch(h_pk, dest, pad_slots, n_rows):
    n, width = h_pk.shape
    workers = SC_CORES * SC_SUBCORES
    per_w = n // workers
    win = SC_DISPATCH_WINDOW
    assert per_w * workers == n and per_w % win == 0
    n_chunk = per_w // win
    dest_w = jnp.transpose(dest.reshape(workers, n_chunk, win, TOP_K), (0, 1, 3, 2))
    pad_per_w = pad_slots.size // workers
    pwin = SC_PAD_WINDOW
    n_pchunk = pad_per_w // (TOP_K * pwin)
    assert n_pchunk * TOP_K * pwin * workers == pad_slots.size
    pad_w = pad_slots.reshape(workers, n_pchunk, TOP_K, pwin)
    zero_rows = jnp.zeros((pwin, width), h_pk.dtype)
    mesh = plsc.VectorSubcoreMesh(core_axis_name="c", subcore_axis_name="s",
                                  num_cores=SC_CORES, num_subcores=SC_SUBCORES)

    @functools.partial(
        pl.kernel, mesh=mesh,
        out_type=jax.ShapeDtypeStruct((n_rows, width), h_pk.dtype),
        scratch_types=[pltpu.VMEM((TOP_K, win), jnp.int32),
                       pltpu.VMEM((win, width), h_pk.dtype),
                       pltpu.VMEM((TOP_K, pwin), jnp.int32),
                       pltpu.VMEM((pwin, width), h_pk.dtype),
                       pltpu.SemaphoreType.DMA],
        name="moe_sc_dispatch",
    )
    def dispatch(h_hbm, dest_hbm, pad_hbm, zero_hbm, xg_hbm, idx_v, rows_v, pidx_v, zeros_v, sem):
        wid = lax.axis_index("s") * SC_CORES + lax.axis_index("c")
        pltpu.sync_copy(zero_hbm, zeros_v)

        @pl.loop(0, n_chunk)
        def _(j):
            base = pl.multiple_of(wid * per_w + j * win, 8)
            pltpu.sync_copy(h_hbm.at[pl.ds(base, win)], rows_v)
            pltpu.sync_copy(dest_hbm.at[wid, j], idx_v)
            copies = [pltpu.async_copy(rows_v, xg_hbm.at[idx_v.at[kk]], sem) for kk in range(TOP_K)]
            for cp in copies:
                cp.wait()

        @pl.loop(0, n_pchunk)
        def _(j):
            pltpu.sync_copy(pad_hbm.at[wid, j], pidx_v)
            copies = [pltpu.async_copy(zeros_v, xg_hbm.at[pidx_v.at[kk]], sem) for kk in range(TOP_K)]
            for cp in copies:
                cp.wait()

    return dispatch(h_pk, dest_w, pad_w, zero_rows)


def _expert_kernel(be_ref, nu_ref, x_ref, w1_ref, w3_ref, w2_ref, o_ref, w13_s, w2_s):
    i = pl.program_id(0)
    F = w1_ref.shape[1]

    @pl.when(i < nu_ref[0])
    def _():
        prev = be_ref[jnp.maximum(i - 1, 0)]

        @pl.when((i == 0) | (be_ref[i] != prev))
        def _():
            w13_s[:, 0:F] = w1_ref[...].astype(BF16)
            w13_s[:, F:2 * F] = w3_ref[...].astype(BF16)
            w2_s[...] = w2_ref[...].astype(BF16)

        x_lo, x_hi = _unpack_pair(x_ref[...])
        half = x_ref.shape[1]
        h13 = (jnp.dot(x_lo.astype(BF16), w13_s[0:half, :], preferred_element_type=F32)
               + jnp.dot(x_hi.astype(BF16), w13_s[half:2 * half, :], preferred_element_type=F32))
        hh = _silu(h13[:, 0:F]) * h13[:, F:2 * F]
        y = jnp.dot(hh.astype(BF16), w2_s[...], preferred_element_type=F32)
        o_ref[...] = _pack_pair(y[:, 0:half], y[:, half:2 * half])


def _experts(layer, block_e, n_used, xg, w_e1, w_e3, w_e2):
    n_slots, half = xg.shape
    D = 2 * half
    F = w_e1.shape[3]
    BM = MOE_BLOCK
    nb = block_e.shape[0]
    n_slots = nb * BM

    def blk(i, be, nu):
        return (jnp.minimum(i, nu[0] - 1), 0)

    grid_spec = pltpu.PrefetchScalarGridSpec(
        num_scalar_prefetch=2,
        grid=(nb,),
        in_specs=[pl.BlockSpec((BM, half), blk),
                  pl.BlockSpec((None, None, D, F), lambda i, be, nu: (layer, be[i], 0, 0)),
                  pl.BlockSpec((None, None, D, F), lambda i, be, nu: (layer, be[i], 0, 0)),
                  pl.BlockSpec((None, None, F, D), lambda i, be, nu: (layer, be[i], 0, 0))],
        out_specs=pl.BlockSpec((BM, half), blk),
        scratch_shapes=[pltpu.VMEM((D, 2 * F), BF16), pltpu.VMEM((F, D), BF16)],
    )
    return pl.pallas_call(
        _expert_kernel,
        grid_spec=grid_spec,
        out_shape=jax.ShapeDtypeStruct((n_slots, half), jnp.int32),
        compiler_params=_cparams("arbitrary"),
        name="moe_experts",
    )(block_e, n_used, xg, w_e1, w_e3, w_e2)


def _sc_gather(table, idx):
    n_idx = idx.shape[0]
    width = table.shape[1]
    workers = SC_CORES * SC_SUBCORES
    per_w = n_idx // workers
    assert per_w * workers == n_idx and per_w % SC_WINDOW == 0
    mesh = plsc.VectorSubcoreMesh(core_axis_name="c", subcore_axis_name="s",
                                  num_cores=SC_CORES, num_subcores=SC_SUBCORES)

    @functools.partial(
        pl.kernel, mesh=mesh,
        out_type=jax.ShapeDtypeStruct((n_idx, width), table.dtype),
        scratch_types=[pltpu.VMEM((SC_WINDOW,), jnp.int32),
                       pltpu.VMEM((SC_WINDOW, width), table.dtype),
                       pltpu.SemaphoreType.DMA],
        name="moe_sc_gather",
    )
    def gather(table_hbm, idx_hbm, out_hbm, idx_v, rows_v, sem):
        wid = lax.axis_index("s") * SC_CORES + lax.axis_index("c")

        @pl.loop(0, per_w // SC_WINDOW)
        def _(j):
            base = pl.multiple_of(wid * per_w + j * SC_WINDOW, SC_WINDOW)
            pltpu.sync_copy(idx_hbm.at[pl.ds(base, SC_WINDOW)], idx_v)
            pltpu.async_copy(table_hbm.at[idx_v], rows_v, sem).wait()
            pltpu.sync_copy(rows_v, out_hbm.at[pl.ds(base, SC_WINDOW)])

    return gather(table, idx)


def _combine_kernel(alpha, h_ref, gate_ref, yt_ref, ws13_ref, ws2_ref, g_ref, b_ref, o_ref):
    F = ws2_ref.shape[0]
    half = h_ref.shape[1] // 2
    h = h_ref[...]
    h13 = _bdot(h, ws13_ref[...])
    shared = _bdot(_silu(h13[:, 0:F]) * h13[:, F:2 * F], ws2_ref[...])
    r_lo = jnp.zeros((h.shape[0], half), F32)
    r_hi = jnp.zeros((h.shape[0], half), F32)
    for kk in range(TOP_K):
        y_lo, y_hi = _unpack_pair(yt_ref[kk])
        gk = gate_ref[:, kk:kk + 1]
        r_lo = r_lo + y_lo * gk
        r_hi = r_hi + y_hi * gk
    routed = jnp.concatenate([r_lo, r_hi], axis=-1)
    o_ref[...] = _layer_norm(alpha * h + (shared + routed), g_ref[...], b_ref[...])


def _combine(alpha, h, gate, yt, ws13, ws2, ln_g, ln_b):
    n, D = h.shape
    R = COMBINE_TILE

    def full(a):
        return pl.BlockSpec(a.shape, lambda i: (0,) * a.ndim)

    return pl.pallas_call(
        functools.partial(_combine_kernel, alpha),
        grid=(n // R,),
        in_specs=[pl.BlockSpec((R, D), lambda i: (i, 0)),
                  pl.BlockSpec((R, LANES), lambda i: (i, 0)),
                  pl.BlockSpec((TOP_K, R, D // 2), lambda i: (0, i, 0)),
                  full(ws13), full(ws2), full(ln_g), full(ln_b)],
        out_specs=pl.BlockSpec((R, D), lambda i: (i, 0)),
        out_shape=jax.ShapeDtypeStruct((n, D), F32),
        compiler_params=_cparams("parallel"),
        name="moe_combine",
    )(h, gate, yt, ws13, ws2, ln_g, ln_b)


def _moe(layer, alpha, h, h_pk, w_router, b_router, w_e1, w_e3, w_e2, ws13, ws2, ln_g, ln_b):
    n, D = h.shape
    E = w_router.shape[1]
    BM = MOE_BLOCK
    idx, gate, rank, cnt = _router(h, w_router, b_router)
    counts = cnt[0].astype(jnp.int32)
    padded = (counts + BM - 1) // BM * BM
    end = jnp.cumsum(padded)
    start = (end - padded).astype(jnp.int32)
    n_blocks = (n * TOP_K + BM - 1) // BM + E
    block_e = jnp.minimum(jnp.searchsorted(end, jnp.arange(n_blocks, dtype=jnp.int32) * BM, side='right'),
                          E - 1).astype(jnp.int32)
    n_used = (end[-1] // BM).astype(jnp.int32).reshape(1)
    dest = _dest(idx, rank, start.astype(F32).reshape(1, E))
    n_slots = n_blocks * BM
    e_i = jnp.arange(E, dtype=jnp.int32)[:, None]
    j_i = jnp.arange(BM, dtype=jnp.int32)[None, :]
    pad = (start + counts)[:, None] + j_i
    pad_slots = jnp.where(pad < end.astype(jnp.int32)[:, None], pad, n_slots + e_i * BM + j_i)
    xg = _sc_dispatch(h_pk, dest[:, :TOP_K], pad_slots, n_slots + E * BM)
    yg = _experts(layer, block_e, n_used, xg, w_e1, w_e3, w_e2)
    dest_km = jnp.swapaxes(dest[:, :TOP_K], 0, 1).reshape(-1)
    yt = _sc_gather(yg, dest_km).reshape(TOP_K, n, D // 2)
    return _combine(alpha, h, gate, yt, ws13, ws2, ln_g, ln_b)


def kernel(x_prompt, x_sample, state_wkv, state_shift, state_conv, cache_mem_k, cache_mem_v, mem_prompt, w_in, w_in_vmix, mu_shift, mu_vmix, w0, w_up, a0, a_up, v0, v_up, g_up, k_k, k_a, r_k, lnx_g, lnx_b, w_o_a, conv_w, conv_b, cn_g, cn_b, w_o_b, w_mem_kv, w_o_c, w_out, ln1_g, ln1_b, ln2_g, ln2_b, w_router, b_router, w_e1, w_e3, w_e2, w_s1, w_s3, w_s2):
    B, T, D = x_prompt.shape
    SB, ST, _ = x_sample.shape
    L = w_in.shape[0]
    M = mem_prompt.shape[1]
    R = ROW_TILE
    n_p, n_s = B * T, SB * ST
    n = n_p + n_s
    assert n_s == R and SB == LANES and T % R == 0 and T % CHUNK == 0 and D % LANES == 0
    npt, tps = n_p // R, T // R
    alpha = (2.0 * L) ** 0.25
    off_glu = RWKV_COLS
    off_q = off_glu + 2 * CONV_CH
    off_gate = off_q + MEM_WIDTH
    in_cols = off_gate + 3 * D

    x = jnp.concatenate([x_prompt.reshape(n_p, D), jnp.swapaxes(x_sample, 0, 1).reshape(n_s, D)], axis=0)
    mem_rows = mem_prompt.reshape(B * M, D)
    row2 = lambda a: a.reshape(1, -1)
    v_first = None
    outs = {k: [] for k in ('wkv_p', 'shift_p', 'conv_p', 'mk', 'mv', 'wkv_s', 'shift_s', 'conv_s')}

    for l in range(L):
        first = l == 0
        w_in_l = w_in[l].astype(BF16)
        w_rwkv = w_in_l[:, :off_glu]
        p_rwkv = _matmul(x, w_rwkv, R, RWKV_COLS // 2)
        p_glu = _matmul(x, w_in_l[:, off_glu:off_q], R, 2 * CONV_CH)
        p_q = _matmul(x, w_in_l[:, off_q:off_gate], R, MEM_WIDTH)
        p_gate = _matmul(x, w_in_l[:, off_gate:in_cols], R, D)
        x_prev_s = state_shift[l]
        prev_rwkv = _matmul(x_prev_s, w_rwkv, SB, RWKV_COLS // 2)

        wup_pad = jnp.concatenate([w_up[l], jnp.zeros((LORA_A, RW), F32)], axis=0).astype(BF16)
        aup_pad = jnp.concatenate([jnp.zeros((LORA_W, RW), F32), a_up[l]], axis=0).astype(BF16)
        lora = (row2(w0[l]), wup_pad, row2(a0[l]), aup_pad, g_up[l].astype(BF16), row2(k_k[l]), row2(k_a[l]),
                row2(r_k[l]), _seg_ones())
        if first:
            pre = _rwkv_pre(True, npt, tps, p_rwkv, prev_rwkv, row2(mu_shift[l]), lora)
        else:
            wv_pad = jnp.pad(w_in_vmix[l - 1], ((0, 0), (0, LANES - LORA_V))).astype(BF16)
            p_vd = _matmul(x, wv_pad, R, LANES)
            prev_vd = _matmul(x_prev_s, wv_pad, SB, LANES)
            mu_v = jnp.pad(mu_vmix[l - 1], (0, LANES - LORA_V)).reshape(1, LANES)
            vup_pad = jnp.pad(v_up[l - 1], ((0, LANES - LORA_V), (0, 0))).astype(BF16)
            pre = _rwkv_pre(False, npt, tps, p_rwkv, prev_rwkv, row2(mu_shift[l]), lora, p_vd, prev_vd, mu_v,
                            v_first, row2(v0[l - 1]), vup_pad)
        r_, lw_, k_, v_, nkk_, bb_, g_, bonus_ = pre
        if first:
            v_first = v_

        y_p, s_p = _wkv_prompt(B, T, r_, lw_, k_, v_, nkk_, bb_)

        def lanes_b(a):
            return jnp.transpose(a[n_p:].reshape(ST, SB, RW), (0, 2, 1)).reshape(ST, HEADS, HEAD_DIM, SB)

        state_sz = HEADS * HEAD_DIM * HEAD_DIM
        s0 = _transpose_layer(state_wkv.astype(F32).reshape(L, SB, state_sz), l, 2048)
        s0 = s0.reshape(HEADS, HEAD_DIM, HEAD_DIM, SB)
        y_s, s_s = _wkv_sample(s0, lanes_b(r_), lanes_b(lw_), lanes_b(k_),
                               lanes_b(v_).reshape(ST, HEADS, HEAD_DIM, 1, SB), lanes_b(nkk_), lanes_b(bb_))
        y_s = jnp.transpose(y_s.reshape(ST, RW, SB), (0, 2, 1)).reshape(n_s, RW)
        s_s = _transpose_2d(s_s.reshape(state_sz, SB), 2048).reshape(SB, HEADS, HEAD_DIM, HEAD_DIM)
        z_a = _rwkv_post(npt, y_p, y_s, bonus_, g_, row2(lnx_g[l]), row2(lnx_b[l]))

        cvec = (conv_w[l], row2(conv_b[l]), row2(cn_g[l]), row2(cn_b[l]))
        zb_p, conv_p = _conv_prompt(B, T, p_glu, *cvec)
        st_t = jnp.swapaxes(state_conv[l], 0, 1)
        zb_s, glu_s = _conv_sample(st_t, p_glu[n_p:].reshape(ST, SB, 2 * CONV_CH), *cvec)
        zb_s = zb_s.reshape(n_s, CONV_CH)
        conv_s = jnp.concatenate([state_conv[l][:, ST:], jnp.swapaxes(glu_s, 0, 1)], axis=1)

        kv_p = _matmul(mem_rows, w_mem_kv[l].astype(BF16), M, 2 * MEM_WIDTH)
        kv3 = kv_p.reshape(B, M, 2 * MEM_WIDTH)
        zc_p = _attention(p_q[:n_p].reshape(B, T, MEM_WIDTH), kv3, kv3, 0, 1, 1, R).reshape(n_p, MEM_WIDTH)
        q_s = jnp.swapaxes(p_q[n_p:].reshape(ST, SB, MEM_WIDTH), 0, 1)
        q_s = jnp.pad(q_s, ((0, 0), (0, 8 - ST), (0, 0)))
        zc_s = _attention_cached(q_s, cache_mem_k.reshape(L, SB, M * MEM_HEADS, MEM_HEAD_DIM),
                                 cache_mem_v.reshape(L, SB, M * MEM_HEADS, MEM_HEAD_DIM), l, 8)[:, :ST]
        zc_s = jnp.swapaxes(zc_s, 0, 1).reshape(n_s, MEM_WIDTH)

        h, h_pk = _merge(n_p, alpha, z_a, zb_p, zb_s, zc_p, zc_s, p_gate, x,
                         w_o_a[l].astype(BF16), w_o_b[l].astype(BF16), w_o_c[l].astype(BF16),
                         w_out[l].astype(BF16), row2(ln1_g[l]), row2(ln1_b[l]))

        ws13 = jnp.concatenate([w_s1[l], w_s3[l]], axis=1).astype(BF16)
        x_new = _moe(l, alpha, h, h_pk, w_router[l].astype(BF16), row2(b_router[l]), w_e1, w_e3, w_e2,
                     ws13, w_s2[l].astype(BF16), row2(ln2_g[l]), row2(ln2_b[l]))

        outs['wkv_p'].append(s_p.astype(state_wkv.dtype))
        outs['shift_p'].append(x[:n_p].reshape(B, T, D)[:, -1].astype(state_shift.dtype))
        outs['conv_p'].append(conv_p.astype(state_conv.dtype))
        outs['mk'].append(kv3[..., :MEM_WIDTH].reshape(B, M, MEM_HEADS, MEM_HEAD_DIM).astype(cache_mem_k.dtype))
        outs['mv'].append(kv3[..., MEM_WIDTH:].reshape(B, M, MEM_HEADS, MEM_HEAD_DIM).astype(cache_mem_v.dtype))
        outs['wkv_s'].append(s_s.astype(state_wkv.dtype))
        outs['shift_s'].append(x[n_p + (ST - 1) * SB:].astype(state_shift.dtype))
        outs['conv_s'].append(conv_s.astype(state_conv.dtype))
        x = x_new

    y_p = x[:n_p].reshape(B, T, D)
    y_s = jnp.swapaxes(x[n_p:].reshape(ST, SB, D), 0, 1)
    return (y_p, y_s, jnp.stack(outs['wkv_p']), jnp.stack(outs['shift_p']), jnp.stack(outs['conv_p']),
            jnp.stack(outs['mk']), jnp.stack(outs['mv']), jnp.stack(outs['wkv_s']), jnp.stack(outs['shift_s']),
            jnp.stack(outs['conv_s']))
```

```python
import functools
import math

import jax
import jax.numpy as jnp
from jax import lax
from jax.experimental import pallas as pl
from jax.experimental.pallas import tpu as pltpu
from jax.experimental.pallas import tpu_sc as plsc

F32 = jnp.float32
BF16 = jnp.bfloat16

HEADS = 8
HEAD_DIM = 64
RW = HEADS * HEAD_DIM
LORA_W = 64
LORA_A = 64
LORA_V = 32
LORA_G = 128
RWKV_COLS = 3 * RW + LORA_W + LORA_A + LORA_G
GN_EPS = HEAD_DIM * 1e-5
CONV_WIDTH = 31
CONV_BUF = CONV_WIDTH - 1
CONV_CH = 512
MEM_HEADS = 4
MEM_HEAD_DIM = 128
MEM_WIDTH = MEM_HEADS * MEM_HEAD_DIM
TOP_K = 8
ROUTED_SCALE = 2.5
LN_EPS = 1e-5
EXP_M05 = math.exp(-0.5)

LANES = 128
ROW_TILE = 512
MERGE_TILE = 256
CHUNK = 64
NEUMANN_BLOCK = 16
MOE_BLOCK = 512
WKV_BATCH_PER_STEP = 4
SC_CORES = 2
SC_SUBCORES = 16
SC_WINDOW = 128
SC_DISPATCH_WINDOW = 48
SC_PAD_WINDOW = 64
COMBINE_TILE = 128
VMEM_LIMIT = 48 * 1024 * 1024


def _cparams(*sem, vmem=VMEM_LIMIT):
    return pltpu.CompilerParams(dimension_semantics=tuple(sem), vmem_limit_bytes=vmem)


def _bdot(a, b):
    return jnp.dot(a.astype(BF16), b.astype(BF16), preferred_element_type=F32)


def _bdot_nt(a, b):
    return lax.dot_general(a.astype(BF16), b.astype(BF16), (((1,), (1,)), ((), ())),
                           preferred_element_type=F32)


def _sigmoid(x):
    return 1.0 / (1.0 + jnp.exp(-x))


def _silu(x):
    return x * _sigmoid(x)


def _layer_norm(x, g, b):
    mu = jnp.mean(x, axis=-1, keepdims=True)
    d = x - mu
    var = jnp.mean(d * d, axis=-1, keepdims=True)
    return d * lax.rsqrt(var + LN_EPS) * g + b


SEG_LANES = 256


def _seg_ones():
    i = jnp.arange(SEG_LANES)
    return ((i[:, None] // HEAD_DIM) == (i[None, :] // HEAD_DIM)).astype(BF16)


def _seg_sum(x, seg):
    hi = x.astype(BF16)
    lo = (x - hi.astype(F32)).astype(BF16)
    parts = []
    for j in range(x.shape[-1] // SEG_LANES):
        sl = slice(j * SEG_LANES, (j + 1) * SEG_LANES)
        parts.append(jnp.dot(hi[:, sl], seg, preferred_element_type=F32)
                     + jnp.dot(lo[:, sl], seg, preferred_element_type=F32))
    return jnp.concatenate(parts, axis=-1)


def _pack_pair(lo, hi):
    lo_b = lax.bitcast_convert_type(lo.astype(BF16).astype(F32), jnp.int32)
    hi_b = lax.bitcast_convert_type(hi.astype(BF16).astype(F32), jnp.int32)
    return lax.shift_right_logical(lo_b, jnp.full(lo_b.shape, 16, jnp.int32)) | (hi_b & jnp.int32(-65536))


def _unpack_pair(p):
    lo = lax.bitcast_convert_type(lax.shift_left(p, jnp.full(p.shape, 16, jnp.int32)), F32)
    hi = lax.bitcast_convert_type(p & jnp.int32(-65536), F32)
    return lo, hi


def _mm_kernel(x_ref, w_ref, o_ref):
    o_ref[...] = _bdot(x_ref[...], w_ref[...]).astype(o_ref.dtype)


def _matmul(x, w, tm, tn):
    m, k = x.shape
    n_cols = w.shape[1]
    assert m % tm == 0 and n_cols % tn == 0
    return pl.pallas_call(
        _mm_kernel,
        grid=(m // tm, n_cols // tn),
        in_specs=[pl.BlockSpec((tm, k), lambda i, j: (i, 0)),
                  pl.BlockSpec((k, tn), lambda i, j: (0, j))],
        out_specs=pl.BlockSpec((tm, tn), lambda i, j: (i, j)),
        out_shape=jax.ShapeDtypeStruct((m, n_cols), F32),
        compiler_params=_cparams("parallel", "parallel"),
        name=f"matmul_{n_cols}",
    )(x, w)


def _pre_kernel(first, npt, tps, *refs):
    if first:
        (p_ref, halo_ref, prev_ref, mu_ref,
         w0_ref, wup_ref, a0_ref, aup_ref, gup_ref, kk_ref, ka_ref, rk_ref, seg_ref,
         r_o, lw_o, k_o, v_o, nkk_o, bb_o, g_o, bon_o, sh_ref) = refs
    else:
        (p_ref, halo_ref, prev_ref, mu_ref, pv_ref, halov_ref, prevv_ref, muv_ref, vf_ref, v0_ref, vup_ref,
         w0_ref, wup_ref, a0_ref, aup_ref, gup_ref, kk_ref, ka_ref, rk_ref, seg_ref,
         r_o, lw_o, k_o, v_o, nkk_o, bb_o, g_o, bon_o, sh_ref, shv_ref) = refs
    i = pl.program_id(0)
    rows_n = p_ref.shape[0]

    def build_shift(src_ref, halo_r, prev_r, dst_ref):
        @pl.when(i < npt)
        def _():
            x = src_ref[...]
            rolled = pltpu.roll(x, 1, axis=0)
            last = halo_r[7:8, :]
            row0 = jnp.where((i % tps) == 0, jnp.zeros_like(last), last)
            rows = lax.broadcasted_iota(jnp.int32, x.shape, 0)
            dst_ref[...] = jnp.where(rows == 0, row0, rolled)

        @pl.when(i >= npt)
        def _():
            nb = prev_r.shape[0]
            dst_ref[0:nb, :] = prev_r[...]
            dst_ref[nb:rows_n, :] = src_ref[0:rows_n - nb, :]

    build_shift(p_ref, halo_ref, prev_ref, sh_ref)
    p = p_ref[...]
    p = p + (sh_ref[...] - p) * mu_ref[...]
    r = p[:, 0:RW]
    k = p[:, RW:2 * RW]
    v = p[:, 2 * RW:3 * RW]
    u = p[:, 3 * RW:3 * RW + LORA_W + LORA_A]
    gd = p[:, 3 * RW + LORA_W + LORA_A:RWKV_COLS]

    z = w0_ref[...] + _bdot(jnp.tanh(u), wup_ref[...])
    lw = -EXP_M05 * _sigmoid(z)
    a = _sigmoid(a0_ref[...] + _bdot(u, aup_ref[...]))
    g = _bdot(_sigmoid(gd), gup_ref[...])
    if not first:
        build_shift(pv_ref, halov_ref, prevv_ref, shv_ref)
        pv = pv_ref[...]
        pv = pv + (shv_ref[...] - pv) * muv_ref[...]
        mix = _sigmoid(v0_ref[...] + _bdot(pv, vup_ref[...]))
        v = v + (vf_ref[...] - v) * mix
    kk = k * kk_ref[...]
    seg = seg_ref[...]
    nrm = jnp.sqrt(_seg_sum(kk * kk, seg))
    kkn = kk / jnp.maximum(nrm, 1e-12)
    k2 = k * (1.0 + (a - 1.0) * ka_ref[...])
    bonus = _seg_sum(r * k2 * rk_ref[...], seg) * v
    r_o[...] = r
    lw_o[...] = lw
    k_o[...] = k2
    v_o[...] = v
    nkk_o[...] = -kkn
    bb_o[...] = kkn * a
    g_o[...] = g
    bon_o[...] = bonus


def _rwkv_pre(first, npt, tps, p_rwkv, prev_rwkv, mu, lora, p_vd=None, prev_vd=None, mu_v=None, v_first=None,
              v0=None, vup=None):
    n = p_rwkv.shape[0]
    R = ROW_TILE
    nt = n // R

    def row_spec(w):
        return pl.BlockSpec((R, w), lambda i: (i, 0))

    def halo_spec(w):
        return pl.BlockSpec((8, w), lambda i: (jnp.maximum(i * (R // 8) - 1, 0), 0))

    def full(a):
        return pl.BlockSpec(a.shape, lambda i: (0,) * a.ndim)

    ins = [p_rwkv, p_rwkv, prev_rwkv, mu]
    specs = [row_spec(RWKV_COLS), halo_spec(RWKV_COLS), full(prev_rwkv), full(mu)]
    scratch = [pltpu.VMEM((R, RWKV_COLS), F32)]
    if not first:
        ins += [p_vd, p_vd, prev_vd, mu_v, v_first, v0, vup]
        specs += [row_spec(LANES), halo_spec(LANES), full(prev_vd), full(mu_v), row_spec(RW), full(v0), full(vup)]
        scratch.append(pltpu.VMEM((R, LANES), F32))
    ins += list(lora)
    specs += [full(a) for a in lora]
    outs = pl.pallas_call(
        functools.partial(_pre_kernel, first, npt, tps),
        grid=(nt,),
        in_specs=specs,
        out_specs=[row_spec(RW)] * 8,
        out_shape=[jax.ShapeDtypeStruct((n, RW), F32)] * 8,
        scratch_shapes=scratch,
        compiler_params=_cparams("parallel"),
        name="rwkv_pre",
    )(*ins)
    return outs


def _wkv_chunk_kernel(nbat, *refs):
    in_refs = refs[:6 * nbat]
    y_ref, s_out_ref, s_ref = refs[6 * nbat:]
    c = pl.program_id(1)
    C = in_refs[0].shape[0]
    G = 4 * HEAD_DIM
    assert 4 * C == G

    @pl.when(c == 0)
    def _():
        s_ref[...] = jnp.zeros_like(s_ref)

    ri = lax.broadcasted_iota(jnp.int32, (G, G), 0)
    ci = lax.broadcasted_iota(jnp.int32, (G, G), 1)
    same_head = (ri // C) == (ci // HEAD_DIM)
    tril_s = same_head & ((ci % C) < (ri % C))
    tril_i = same_head & ((ci % C) <= (ri % C))
    diag_blk = (ri // NEUMANN_BLOCK) == (ci // NEUMANN_BLOCK)
    eye = (ri == ci).astype(F32)

    def stack(x):
        return jnp.where(same_head, jnp.concatenate([x, x, x, x], axis=0), 0.0).astype(BF16)

    def tile4(x):
        return jnp.concatenate([x, x, x, x], axis=0).astype(BF16)

    def collapse(x4):
        return x4[0:C] + x4[C:2 * C] + x4[2 * C:3 * C] + x4[3 * C:4 * C]

    for b in range(nbat):
        r_ref, lw_ref, k_ref, v_ref, nkk_ref, bb_ref = in_refs[6 * b:6 * b + 6]
        lw = lw_ref[...]
        rows = lax.broadcasted_iota(jnp.int32, lw.shape, 0)
        cs = lw
        s = 1
        while s < C:
            cs = cs + jnp.where(rows >= s, pltpu.roll(cs, s, axis=0), 0.0)
            s *= 2
        cs_last = cs[C - 1:C, :]
        e_neg = jnp.exp(-cs)
        e_tail = jnp.exp(cs_last - cs)
        kk_ = k_ref[...]
        bb_ = bb_ref[...]
        at = nkk_ref[...] * jnp.exp(cs - lw)
        rt = r_ref[...] * jnp.exp(cs)
        bt = bb_ * e_neg
        kt = kk_ * e_neg
        bw = bb_ * e_tail
        kw = kk_ * e_tail
        vv = v_ref[...]
        w_end = jnp.exp(cs_last)

        for g in range(RW // G):
            sl = slice(g * G, (g + 1) * G)
            at4, rt4, v4 = stack(at[:, sl]), stack(rt[:, sl]), stack(vv[:, sl])
            bt4, kt4 = tile4(bt[:, sl]), tile4(kt[:, sl])
            a_ab = jnp.where(tril_s, _bdot_nt(at4, bt4), 0.0)
            a_ak = jnp.where(tril_s, _bdot_nt(at4, kt4), 0.0).astype(BF16)
            a_rb = jnp.where(tril_i, _bdot_nt(rt4, bt4), 0.0).astype(BF16)
            a_rk = jnp.where(tril_i, _bdot_nt(rt4, kt4), 0.0).astype(BF16)
            a_d = jnp.where(diag_blk, a_ab, 0.0)
            a_o = a_ab - a_d
            a_db = a_d.astype(BF16)
            x2 = _bdot(a_db, a_db).astype(BF16)
            x4 = _bdot(x2, x2).astype(BF16)
            x8 = _bdot(x4, x4)
            t_d = eye + a_d
            t_d = t_d + _bdot(t_d, x2)
            t_d = t_d + _bdot(t_d, x4)
            t_d = t_d + _bdot(t_d, x8)
            t_db = t_d.astype(BF16)
            nn = _bdot(t_db, a_o).astype(BF16)
            n2 = _bdot(nn, nn)
            m1 = t_d + _bdot(nn, t_db)
            t_m = m1 + _bdot(n2, m1)
            s4 = s_ref[b, g]
            s4b = s4.astype(BF16)
            u4 = _bdot(t_m, _bdot_nt(at4, s4b) + _bdot(a_ak, v4))
            y4 = _bdot_nt(rt4, s4b) + _bdot(jnp.concatenate([a_rb, a_rk], axis=1),
                                            jnp.concatenate([u4.astype(BF16), v4], axis=0))
            y_ref[b, :, sl] = collapse(y4)
            uv_t = jnp.concatenate([collapse(u4), vv[:, sl]], axis=0).T
            upd = _bdot(uv_t, jnp.concatenate([bw[:, sl], kw[:, sl]], axis=0))
            s_ref[b, g] = s4 * w_end[:, sl] + jnp.where((ri // HEAD_DIM) == (ci // HEAD_DIM), upd, 0.0)

    @pl.when(c == pl.num_programs(1) - 1)
    def _():
        s_out_ref[...] = s_ref[...]


def _wkv_prompt(B, T, r, lw, k, v, nkk, bb):
    C = CHUNK
    nc = T // C
    G = 4 * HEAD_DIM
    nbat = math.gcd(B, WKV_BATCH_PER_STEP)
    specs = []
    for j in range(nbat):
        specs += [pl.BlockSpec((C, RW), lambda i, c, j=j: ((nbat * i + j) * nc + c, 0))] * 6
    y, s = pl.pallas_call(
        functools.partial(_wkv_chunk_kernel, nbat),
        grid=(B // nbat, nc),
        in_specs=specs,
        out_specs=[pl.BlockSpec((nbat, C, RW), lambda i, c: (i, c, 0)),
                   pl.BlockSpec((nbat, RW // G, G, G), lambda i, c: (i, 0, 0, 0))],
        out_shape=[jax.ShapeDtypeStruct((B, T, RW), F32),
                   jax.ShapeDtypeStruct((B, RW // G, G, G), F32)],
        scratch_shapes=[pltpu.VMEM((nbat, RW // G, G, G), F32)],
        compiler_params=_cparams("parallel", "arbitrary"),
        name="wkv_chunk",
    )(*([r, lw, k, v, nkk, bb] * nbat))
    y = y.reshape(B * T, RW)
    s = s.reshape(B, RW // G, 4, HEAD_DIM, 4, HEAD_DIM)
    idx = jnp.arange(4)
    s = s[:, :, idx, :, idx, :]
    s = jnp.transpose(s, (1, 2, 0, 3, 4)).reshape(B, HEADS, HEAD_DIM, HEAD_DIM)
    return y, s


def _wkv_seq_kernel(s0_ref, r_ref, lw_ref, k_ref, v_ref, nkk_ref, bb_ref, y_ref, s_ref):
    steps = r_ref.shape[0]

    def body(vi, carry):
        st = s0_ref[0, vi]
        for t in range(steps):
            sa = jnp.sum(st * nkk_ref[t, 0], axis=0, keepdims=True)
            st = st * jnp.exp(lw_ref[t, 0]) + sa * bb_ref[t, 0] + v_ref[t, 0, vi] * k_ref[t, 0]
            y_ref[t, 0, vi] = jnp.sum(st * r_ref[t, 0], axis=0, keepdims=True)
        s_ref[0, vi] = st
        return carry

    lax.fori_loop(0, HEAD_DIM, body, 0)


def _wkv_sample(s0, r, lw, k, v, nkk, bb):
    steps, _, _, nb = r.shape
    vec = pl.BlockSpec((steps, 1, HEAD_DIM, nb), lambda h: (0, h, 0, 0))
    vcol = pl.BlockSpec((steps, 1, HEAD_DIM, 1, nb), lambda h: (0, h, 0, 0, 0))
    st = pl.BlockSpec((1, HEAD_DIM, HEAD_DIM, nb), lambda h: (h, 0, 0, 0))
    y, s = pl.pallas_call(
        _wkv_seq_kernel,
        grid=(HEADS,),
        in_specs=[st, vec, vec, vec, vcol, vec, vec],
        out_specs=[vcol, st],
        out_shape=[jax.ShapeDtypeStruct((steps, HEADS, HEAD_DIM, 1, nb), F32),
                   jax.ShapeDtypeStruct(s0.shape, F32)],
        compiler_params=_cparams("parallel"),
        name="wkv_seq",
    )(s0, r, lw, k, v, nkk, bb)
    return y, s


def _tr_kernel(x_ref, o_ref):
    o_ref[...] = x_ref[...].T


def _transpose_layer(x3, layer, tc):
    _, a, c = x3.shape
    return pl.pallas_call(
        _tr_kernel,
        grid=(c // tc,),
        in_specs=[pl.BlockSpec((None, a, tc), lambda j: (layer, 0, j))],
        out_specs=pl.BlockSpec((tc, a), lambda j: (j, 0)),
        out_shape=jax.ShapeDtypeStruct((c, a), x3.dtype),
        compiler_params=_cparams("parallel"),
        name="transpose_in",
    )(x3)


def _transpose_2d(x, tc):
    c, a = x.shape
    return pl.pallas_call(
        _tr_kernel,
        grid=(c // tc,),
        in_specs=[pl.BlockSpec((tc, a), lambda j: (j, 0))],
        out_specs=pl.BlockSpec((a, tc), lambda j: (0, j)),
        out_shape=jax.ShapeDtypeStruct((a, c), x.dtype),
        compiler_params=_cparams("parallel"),
        name="transpose_out",
    )(x)


def _post_kernel(npt, yp_ref, ys_ref, bon_ref, g_ref, lg_ref, lb_ref, seg_ref, o_ref):
    i = pl.program_id(0)
    y = jnp.where(i < npt, yp_ref[...], ys_ref[...])
    seg = seg_ref[...]
    mu = _seg_sum(y, seg) * (1.0 / HEAD_DIM)
    d = y - mu
    var = _seg_sum(d * d, seg) * (1.0 / HEAD_DIM)
    yn = d * lax.rsqrt(var + GN_EPS) * lg_ref[...] + lb_ref[...]
    o_ref[...] = (yn + bon_ref[...]) * g_ref[...]


def _rwkv_post(npt, y_p, y_s, bonus, g, lnx_g, lnx_b):
    n = bonus.shape[0]
    R = ROW_TILE
    row = pl.BlockSpec((R, RW), lambda i: (i, 0))
    vec = pl.BlockSpec((1, RW), lambda i: (0, 0))
    return pl.pallas_call(
        functools.partial(_post_kernel, npt),
        grid=(n // R,),
        in_specs=[pl.BlockSpec((R, RW), lambda i: (jnp.minimum(i, npt - 1), 0)),
                  pl.BlockSpec((R, RW), lambda i: (0, 0)), row, row, vec, vec,
                  pl.BlockSpec((SEG_LANES, SEG_LANES), lambda i: (0, 0))],
        out_specs=row,
        out_shape=jax.ShapeDtypeStruct((n, RW), F32),
        compiler_params=_cparams("parallel"),
        name="rwkv_post",
    )(y_p, y_s, bonus, g, lnx_g, lnx_b, _seg_ones())


def _glu(p):
    return p[..., :CONV_CH] * _sigmoid(p[..., CONV_CH:])


def _conv_prompt_kernel(p_ref, halo_ref, w_ref, cb_ref, g_ref, b_ref, o_ref, tail_ref, ext_ref, sh_ref):
    j = pl.program_id(1)
    R = p_ref.shape[0]
    H = halo_ref.shape[0]
    glu = _glu(p_ref[...])
    halo = _glu(halo_ref[...])
    ext_ref[0:H, :] = jnp.where(j == 0, jnp.zeros_like(halo), halo)
    ext_ref[H:H + R, :] = glu
    span = sh_ref.shape[1]
    for ph in range(1, 8):
        sh_ref[ph] = ext_ref[ph:ph + span, :]
    acc = jnp.zeros((R, CONV_CH), F32)
    for t in range(CONV_WIDTH):
        off = H - CONV_BUF + t
        q8 = (off // 8) * 8
        tap = ext_ref[q8:q8 + R, :] if off % 8 == 0 else sh_ref[off % 8, q8:q8 + R, :]
        acc = acc + tap * w_ref[t:t + 1, :]
    h = _layer_norm(acc + cb_ref[...], g_ref[...], b_ref[...])
    o_ref[...] = _silu(h)

    @pl.when(j == pl.num_programs(1) - 1)
    def _():
        tail_ref[0] = glu[R - H:R, :]


def _conv_prompt(B, T, p_glu, conv_w, conv_b, cn_g, cn_b):
    R = ROW_TILE
    H = 32
    tps = T // R
    vec = pl.BlockSpec((1, CONV_CH), lambda b, j: (0, 0))
    z, tail = pl.pallas_call(
        _conv_prompt_kernel,
        grid=(B, tps),
        in_specs=[pl.BlockSpec((R, 2 * CONV_CH), lambda b, j: (b * tps + j, 0)),
                  pl.BlockSpec((H, 2 * CONV_CH), lambda b, j: (jnp.maximum((b * tps + j) * (R // H) - 1, 0), 0)),
                  pl.BlockSpec((CONV_WIDTH, CONV_CH), lambda b, j: (0, 0)), vec, vec, vec],
        out_specs=[pl.BlockSpec((R, CONV_CH), lambda b, j: (b * tps + j, 0)),
                   pl.BlockSpec((1, H, CONV_CH), lambda b, j: (b, 0, 0))],
        out_shape=[jax.ShapeDtypeStruct((B * T, CONV_CH), F32),
                   jax.ShapeDtypeStruct((B, H, CONV_CH), F32)],
        scratch_shapes=[pltpu.VMEM((H + R, CONV_CH), F32), pltpu.VMEM((8, H + R - 8, CONV_CH), F32)],
        compiler_params=_cparams("parallel", "arbitrary"),
        name="conv_prompt",
    )(p_glu, p_glu, conv_w, conv_b, cn_g, cn_b)
    return z, tail[:, H - CONV_BUF:, :]


def _conv_sample_kernel(st_ref, p_ref, w_ref, cb_ref, g_ref, b_ref, o_ref, glu_ref):
    steps = p_ref.shape[0]
    glu = [_glu(p_ref[t]) for t in range(steps)]
    for t in range(steps):
        glu_ref[t] = glu[t]
    for t in range(steps):
        acc = jnp.zeros(glu[0].shape, F32)
        for j in range(CONV_WIDTH):
            src = t + j
            x = st_ref[src] if src < CONV_BUF else glu[src - CONV_BUF]
            acc = acc + x * w_ref[j:j + 1, :]
        h = _layer_norm(acc + cb_ref[...], g_ref[...], b_ref[...])
        o_ref[t] = _silu(h)


def _conv_sample(state_t, p_glu, conv_w, conv_b, cn_g, cn_b):
    steps, nb, _ = p_glu.shape

    def full(a):
        return pl.BlockSpec(a.shape, lambda i: (0,) * a.ndim)

    ins = (state_t, p_glu, conv_w, conv_b, cn_g, cn_b)
    return pl.pallas_call(
        _conv_sample_kernel,
        grid=(1,),
        in_specs=[full(a) for a in ins],
        out_specs=[pl.BlockSpec((steps, nb, CONV_CH), lambda i: (0, 0, 0))] * 2,
        out_shape=[jax.ShapeDtypeStruct((steps, nb, CONV_CH), F32)] * 2,
        compiler_params=_cparams("arbitrary"),
        name="conv_sample",
    )(*ins)


def _attn_kernel(q_ref, k_ref, v_ref, o_ref):
    nb = q_ref.shape[0]
    scale = MEM_HEAD_DIM ** -0.5
    for bi in range(nb):
        for h in range(MEM_HEADS):
            sl = slice(h * MEM_HEAD_DIM, (h + 1) * MEM_HEAD_DIM)
            s = _bdot_nt(q_ref[bi, :, sl], k_ref[bi, :, sl]) * scale
            m = jnp.max(s, axis=-1, keepdims=True)
            e = jnp.exp(s - m)
            pr = e / jnp.sum(e, axis=-1, keepdims=True)
            o_ref[bi, :, sl] = _bdot(pr, v_ref[bi, :, sl])


def _attn_rows_kernel(q_ref, k_ref, v_ref, o_ref):
    nb, tq, width = q_ref.shape
    n_mem = k_ref.shape[1] // MEM_HEADS
    scale = MEM_HEAD_DIM ** -0.5
    ri = lax.broadcasted_iota(jnp.int32, (MEM_HEADS * tq, width), 0)
    ci = lax.broadcasted_iota(jnp.int32, (MEM_HEADS * tq, width), 1)
    own = (ri // tq) == (ci // MEM_HEAD_DIM)
    for bi in range(nb):
        k_all = jnp.concatenate([k_ref[bi, pl.ds(h, n_mem, stride=MEM_HEADS), :].astype(BF16)
                                 for h in range(MEM_HEADS)], axis=1)
        v_all = jnp.concatenate([v_ref[bi, pl.ds(h, n_mem, stride=MEM_HEADS), :].astype(BF16)
                                 for h in range(MEM_HEADS)], axis=1)
        q = q_ref[bi]
        q4 = jnp.where(own, jnp.concatenate([q] * MEM_HEADS, axis=0), 0.0)
        s = _bdot_nt(q4, k_all) * scale
        m = jnp.max(s, axis=-1, keepdims=True)
        e = jnp.exp(s - m)
        pr = e / jnp.sum(e, axis=-1, keepdims=True)
        pv = jnp.where(own, _bdot(pr, v_all), 0.0)
        o = pv[0:tq]
        for h in range(1, MEM_HEADS):
            o = o + pv[h * tq:(h + 1) * tq]
        o_ref[bi] = o


def _attention_cached(q, cache_k, cache_v, layer, bb):
    B, T, W = q.shape
    rows = cache_k.shape[2]
    kv_spec = pl.BlockSpec((None, bb, rows, MEM_HEAD_DIM), lambda b: (layer, b, 0, 0))
    return pl.pallas_call(
        _attn_rows_kernel,
        grid=(B // bb,),
        in_specs=[pl.BlockSpec((bb, T, W), lambda b: (b, 0, 0)), kv_spec, kv_spec],
        out_specs=pl.BlockSpec((bb, T, W), lambda b: (b, 0, 0)),
        out_shape=jax.ShapeDtypeStruct((B, T, W), F32),
        compiler_params=_cparams("parallel"),
        name="mem_attention_cached",
    )(q, cache_k, cache_v)


def _attention(q, mk, mv, kcol, vcol, bb, tq):
    B, T, W = q.shape
    M = mk.shape[1]
    return pl.pallas_call(
        _attn_kernel,
        grid=(B // bb, T // tq),
        in_specs=[pl.BlockSpec((bb, tq, W), lambda b, j: (b, j, 0)),
                  pl.BlockSpec((bb, M, W), lambda b, j: (b, 0, kcol)),
                  pl.BlockSpec((bb, M, W), lambda b, j: (b, 0, vcol))],
        out_specs=pl.BlockSpec((bb, tq, W), lambda b, j: (b, j, 0)),
        out_shape=jax.ShapeDtypeStruct((B, T, W), F32),
        compiler_params=_cparams("parallel", "parallel"),
        name=f"mem_attention_{tq}",
    )(q, mk, mv)


def _merge_kernel(npt, alpha, za_ref, zbp_ref, zbs_ref, zcp_ref, zcs_ref, gate_ref, x_ref,
                  woa_ref, wob_ref, woc_ref, wout_ref, g_ref, b_ref, o_ref, pk_ref):
    i = pl.program_id(0)
    D = x_ref.shape[1]
    is_p = i < npt
    zb = jnp.where(is_p, zbp_ref[...], zbs_ref[...])
    zc = jnp.where(is_p, zcp_ref[...], zcs_ref[...])
    merged = (_sigmoid(gate_ref[:, 0:D]) * _bdot(za_ref[...], woa_ref[...])
              + _sigmoid(gate_ref[:, D:2 * D]) * _bdot(zb, wob_ref[...])
              + _sigmoid(gate_ref[:, 2 * D:3 * D]) * _bdot(zc, woc_ref[...]))
    h = _layer_norm(alpha * x_ref[...] + _bdot(merged, wout_ref[...]), g_ref[...], b_ref[...])
    o_ref[...] = h
    pk_ref[...] = _pack_pair(h[:, 0:D // 2], h[:, D // 2:D])


def _merge(n_p, alpha, za, zb_p, zb_s, zc_p, zc_s, gates, x, woa, wob, woc, wout, ln_g, ln_b):
    n, D = x.shape
    R = MERGE_TILE
    npt = n_p // R

    def row(w):
        return pl.BlockSpec((R, w), lambda i: (i, 0))

    def prow(w):
        return pl.BlockSpec((R, w), lambda i: (jnp.minimum(i, npt - 1), 0))

    def srow(w):
        return pl.BlockSpec((R, w), lambda i: (jnp.maximum(i - npt, 0), 0))

    def full(a):
        return pl.BlockSpec(a.shape, lambda i: (0,) * a.ndim)

    return pl.pallas_call(
        functools.partial(_merge_kernel, npt, alpha),
        grid=(n // R,),
        in_specs=[row(RW), prow(CONV_CH), srow(CONV_CH), prow(MEM_WIDTH), srow(MEM_WIDTH), row(3 * D), row(D),
                  full(woa), full(wob), full(woc), full(wout), full(ln_g), full(ln_b)],
        out_specs=[row(D), row(D // 2)],
        out_shape=[jax.ShapeDtypeStruct((n, D), F32), jax.ShapeDtypeStruct((n, D // 2), jnp.int32)],
        compiler_params=_cparams("parallel"),
        name="branch_merge",
    )(za, zb_p, zb_s, zc_p, zc_s, gates, x, woa, wob, woc, wout, ln_g, ln_b)


def _router_kernel(h_ref, wr_ref, br_ref, idx_o, gate_o, rank_o, cnt_o, carry_ref):
    i = pl.program_id(0)
    R = h_ref.shape[0]
    E = wr_ref.shape[1]

    @pl.when(i == 0)
    def _():
        carry_ref[...] = jnp.zeros_like(carry_ref)

    scores = _sigmoid(_bdot(h_ref[...], wr_ref[...]))
    work = scores + br_ref[...]
    lane = lax.broadcasted_iota(jnp.int32, (R, E), 1).astype(F32)
    sel_mask = jnp.zeros((R, E), F32)
    idx_cols, sel_cols = [], []
    for _ in range(TOP_K):
        m = jnp.max(work, axis=-1, keepdims=True)
        first = jnp.min(jnp.where(work == m, lane, float(E)), axis=-1, keepdims=True)
        onehot = lane == first
        sel_cols.append(jnp.sum(jnp.where(onehot, scores, 0.0), axis=-1, keepdims=True))
        idx_cols.append(first)
        sel_mask = jnp.where(onehot, 1.0, sel_mask)
        work = jnp.where(onehot, -jnp.inf, work)
    total = sel_cols[0]
    for c in sel_cols[1:]:
        total = total + c
    ri = lax.broadcasted_iota(jnp.int32, (R, R), 0)
    ci = lax.broadcasted_iota(jnp.int32, (R, R), 1)
    before = jnp.dot((ci < ri).astype(BF16), sel_mask.astype(BF16), preferred_element_type=F32) + carry_ref[0:1, :]
    idx_out = jnp.zeros((R, E), F32)
    gate_out = jnp.zeros((R, E), F32)
    rank_out = jnp.zeros((R, E), F32)
    for kk in range(TOP_K):
        onehot = lane == idx_cols[kk]
        rank = jnp.sum(jnp.where(onehot, before, 0.0), axis=-1, keepdims=True)
        col = lane == float(kk)
        idx_out = jnp.where(col, idx_cols[kk], idx_out)
        gate_out = jnp.where(col, sel_cols[kk] / total * ROUTED_SCALE, gate_out)
        rank_out = jnp.where(col, rank, rank_out)
    idx_o[...] = idx_out
    gate_o[...] = gate_out
    rank_o[...] = rank_out
    carry_ref[0:1, :] = carry_ref[0:1, :] + jnp.sum(sel_mask, axis=0, keepdims=True)
    cnt_o[...] = carry_ref[...]


def _router(h, w_router, b_router):
    n, D = h.shape
    E = w_router.shape[1]
    assert E == LANES
    R = 256
    row = pl.BlockSpec((R, E), lambda i: (i, 0))
    return pl.pallas_call(
        _router_kernel,
        grid=(n // R,),
        in_specs=[pl.BlockSpec((R, D), lambda i: (i, 0)),
                  pl.BlockSpec((D, E), lambda i: (0, 0)),
                  pl.BlockSpec((1, E), lambda i: (0, 0))],
        out_specs=[row, row, row, pl.BlockSpec((8, E), lambda i: (0, 0))],
        out_shape=[jax.ShapeDtypeStruct((n, E), F32), jax.ShapeDtypeStruct((n, E), F32),
                   jax.ShapeDtypeStruct((n, E), F32), jax.ShapeDtypeStruct((8, E), F32)],
        scratch_shapes=[pltpu.VMEM((8, E), F32)],
        compiler_params=_cparams("arbitrary"),
        name="moe_router",
    )(h, w_router, b_router)


def _dest_kernel(idx_ref, rank_ref, start_ref, o_ref):
    R, E = idx_ref.shape
    lane = lax.broadcasted_iota(jnp.int32, (R, E), 1).astype(F32)
    idx = idx_ref[...]
    out = jnp.zeros((R, E), F32)
    start = start_ref[...]
    for kk in range(TOP_K):
        e_k = jnp.sum(jnp.where(lane == float(kk), idx, 0.0), axis=-1, keepdims=True)
        s_k = jnp.sum(jnp.where(lane == e_k, start, 0.0), axis=-1, keepdims=True)
        out = jnp.where(lane == float(kk), s_k, out)
    o_ref[...] = (out + rank_ref[...]).astype(jnp.int32)


def _dest(idx, rank, start):
    n, E = idx.shape
    R = 256
    row = pl.BlockSpec((R, E), lambda i: (i, 0))
    return pl.pallas_call(
        _dest_kernel,
        grid=(n // R,),
        in_specs=[row, row, pl.BlockSpec((1, E), lambda i: (0, 0))],
        out_specs=row,
        out_shape=jax.ShapeDtypeStruct((n, E), jnp.int32),
        compiler_params=_cparams("parallel"),
        name="moe_dest",
    )(idx, rank, start)


def _sc_dispatch(h_pk, dest, pad_slots, n_rows):
    n, width = h_pk.shape
    workers = SC_CORES * SC_SUBCORES
    per_w = n // workers
    win = SC_DISPATCH_WINDOW
    assert per_w * workers == n and per_w % win == 0
    n_chunk = per_w // win
    dest_w = jnp.transpose(dest.reshape(workers, n_chunk, win, TOP_K), (0, 1, 3, 2))
    pad_per_w = pad_slots.size // workers
    pwin = SC_PAD_WINDOW
    n_pchunk = pad_per_w // (TOP_K * pwin)
    assert n_pchunk * TOP_K * pwin * workers == pad_slots.size
    pad_w = pad_slots.reshape(workers, n_pchunk, TOP_K, pwin)
    zero_rows = jnp.zeros((pwin, width), h_pk.dtype)
    mesh = plsc.VectorSubcoreMesh(core_axis_name="c", subcore_axis_name="s",
                                  num_cores=SC_CORES, num_subcores=SC_SUBCORES)

    @functools.partial(
        pl.kernel, mesh=mesh,
        out_type=jax.ShapeDtypeStruct((n_rows, width), h_pk.dtype),
        scratch_types=[pltpu.VMEM((TOP_K, win), jnp.int32),
                       pltpu.VMEM((win, width), h_pk.dtype),
                       pltpu.VMEM((TOP_K, pwin), jnp.int32),
                       pltpu.VMEM((pwin, width), h_pk.dtype),
                       pltpu.SemaphoreType.DMA],
        name="moe_sc_dispatch",
    )
    def dispatch(h_hbm, dest_hbm, pad_hbm, zero_hbm, xg_hbm, idx_v, rows_v, pidx_v, zeros_v, sem):
        wid = lax.axis_index("s") * SC_CORES + lax.axis_index("c")
        pltpu.sync_copy(zero_hbm, zeros_v)

        @pl.loop(0, n_chunk)
        def _(j):
            base = pl.multiple_of(wid * per_w + j * win, 8)
            pltpu.sync_copy(h_hbm.at[pl.ds(base, win)], rows_v)
            pltpu.sync_copy(dest_hbm.at[wid, j], idx_v)
            copies = [pltpu.async_copy(rows_v, xg_hbm.at[idx_v.at[kk]], sem) for kk in range(TOP_K)]
            for cp in copies:
                cp.wait()

        @pl.loop(0, n_pchunk)
        def _(j):
            pltpu.sync_copy(pad_hbm.at[wid, j], pidx_v)
            copies = [pltpu.async_copy(zeros_v, xg_hbm.at[pidx_v.at[kk]], sem) for kk in range(TOP_K)]
            for cp in copies:
                cp.wait()

    return dispatch(h_pk, dest_w, pad_w, zero_rows)


def _expert_kernel(be_ref, nu_ref, x_ref, w1_ref, w3_ref, w2_ref, o_ref, w13_s, w2_s):
    i = pl.program_id(0)
    F = w1_ref.shape[1]

    @pl.when(i < nu_ref[0])
    def _():
        prev = be_ref[jnp.maximum(i - 1, 0)]

        @pl.when((i == 0) | (be_ref[i] != prev))
        def _():
            w13_s[:, 0:F] = w1_ref[...].astype(BF16)
            w13_s[:, F:2 * F] = w3_ref[...].astype(BF16)
            w2_s[...] = w2_ref[...].astype(BF16)

        x_lo, x_hi = _unpack_pair(x_ref[...])
        half = x_ref.shape[1]
        h13 = (jnp.dot(x_lo.astype(BF16), w13_s[0:half, :], preferred_element_type=F32)
               + jnp.dot(x_hi.astype(BF16), w13_s[half:2 * half, :], preferred_element_type=F32))
        hh = _silu(h13[:, 0:F]) * h13[:, F:2 * F]
        y = jnp.dot(hh.astype(BF16), w2_s[...], preferred_element_type=F32)
        o_ref[...] = _pack_pair(y[:, 0:half], y[:, half:2 * half])


def _experts(layer, block_e, n_used, xg, w_e1, w_e3, w_e2):
    n_slots, half = xg.shape
    D = 2 * half
    F = w_e1.shape[3]
    BM = MOE_BLOCK
    nb = block_e.shape[0]
    n_slots = nb * BM

    def blk(i, be, nu):
        return (jnp.minimum(i, nu[0] - 1), 0)

    grid_spec = pltpu.PrefetchScalarGridSpec(
        num_scalar_prefetch=2,
        grid=(nb,),
        in_specs=[pl.BlockSpec((BM, half), blk),
                  pl.BlockSpec((None, None, D, F), lambda i, be, nu: (layer, be[i], 0, 0)),
                  pl.BlockSpec((None, None, D, F), lambda i, be, nu: (layer, be[i], 0, 0)),
                  pl.BlockSpec((None, None, F, D), lambda i, be, nu: (layer, be[i], 0, 0))],
        out_specs=pl.BlockSpec((BM, half), blk),
        scratch_shapes=[pltpu.VMEM((D, 2 * F), BF16), pltpu.VMEM((F, D), BF16)],
    )
    return pl.pallas_call(
        _expert_kernel,
        grid_spec=grid_spec,
        out_shape=jax.ShapeDtypeStruct((n_slots, half), jnp.int32),
        compiler_params=_cparams("arbitrary"),
        name="moe_experts",
    )(block_e, n_used, xg, w_e1, w_e3, w_e2)


def _sc_gather(table, idx):
    n_idx = idx.shape[0]
    width = table.shape[1]
    workers = SC_CORES * SC_SUBCORES
    per_w = n_idx // workers
    assert per_w * workers == n_idx and per_w % SC_WINDOW == 0
    mesh = plsc.VectorSubcoreMesh(core_axis_name="c", subcore_axis_name="s",
                                  num_cores=SC_CORES, num_subcores=SC_SUBCORES)

    @functools.partial(
        pl.kernel, mesh=mesh,
        out_type=jax.ShapeDtypeStruct((n_idx, width), table.dtype),
        scratch_types=[pltpu.VMEM((SC_WINDOW,), jnp.int32),
                       pltpu.VMEM((SC_WINDOW, width), table.dtype),
                       pltpu.SemaphoreType.DMA],
        name="moe_sc_gather",
    )
    def gather(table_hbm, idx_hbm, out_hbm, idx_v, rows_v, sem):
        wid = lax.axis_index("s") * SC_CORES + lax.axis_index("c")

        @pl.loop(0, per_w // SC_WINDOW)
        def _(j):
            base = pl.multiple_of(wid * per_w + j * SC_WINDOW, SC_WINDOW)
            pltpu.sync_copy(idx_hbm.at[pl.ds(base, SC_WINDOW)], idx_v)
            pltpu.async_copy(table_hbm.at[idx_v], rows_v, sem).wait()
            pltpu.sync_copy(rows_v, out_hbm.at[pl.ds(base, SC_WINDOW)])

    return gather(table, idx)


def _combine_kernel(alpha, h_ref, gate_ref, yt_ref, ws13_ref, ws2_ref, g_ref, b_ref, o_ref):
    F = ws2_ref.shape[0]
    half = h_ref.shape[1] // 2
    h = h_ref[...]
    h13 = _bdot(h, ws13_ref[...])
    shared = _bdot(_silu(h13[:, 0:F]) * h13[:, F:2 * F], ws2_ref[...])
    r_lo = jnp.zeros((h.shape[0], half), F32)
    r_hi = jnp.zeros((h.shape[0], half), F32)
    for kk in range(TOP_K):
        y_lo, y_hi = _unpack_pair(yt_ref[kk])
        gk = gate_ref[:, kk:kk + 1]
        r_lo = r_lo + y_lo * gk
        r_hi = r_hi + y_hi * gk
    routed = jnp.concatenate([r_lo, r_hi], axis=-1)
    o_ref[...] = _layer_norm(alpha * h + (shared + routed), g_ref[...], b_ref[...])


def _combine(alpha, h, gate, yt, ws13, ws2, ln_g, ln_b):
    n, D = h.shape
    R = COMBINE_TILE

    def full(a):
        return pl.BlockSpec(a.shape, lambda i: (0,) * a.ndim)

    return pl.pallas_call(
        functools.partial(_combine_kernel, alpha),
        grid=(n // R,),
        in_specs=[pl.BlockSpec((R, D), lambda i: (i, 0)),
                  pl.BlockSpec((R, LANES), lambda i: (i, 0)),
                  pl.BlockSpec((TOP_K, R, D // 2), lambda i: (0, i, 0)),
                  full(ws13), full(ws2), full(ln_g), full(ln_b)],
        out_specs=pl.BlockSpec((R, D), lambda i: (i, 0)),
        out_shape=jax.ShapeDtypeStruct((n, D), F32),
        compiler_params=_cparams("parallel"),
        name="moe_combine",
    )(h, gate, yt, ws13, ws2, ln_g, ln_b)


def _moe(layer, alpha, h, h_pk, w_router, b_router, w_e1, w_e3, w_e2, ws13, ws2, ln_g, ln_b):
    n, D = h.shape
    E = w_router.shape[1]
    BM = MOE_BLOCK
    idx, gate, rank, cnt = _router(h, w_router, b_router)
    counts = cnt[0].astype(jnp.int32)
    padded = (counts + BM - 1) // BM * BM
    end = jnp.cumsum(padded)
    start = (end - padded).astype(jnp.int32)
    n_blocks = (n * TOP_K + BM - 1) // BM + E
    block_e = jnp.minimum(jnp.searchsorted(end, jnp.arange(n_blocks, dtype=jnp.int32) * BM, side='right'),
                          E - 1).astype(jnp.int32)
    n_used = (end[-1] // BM).astype(jnp.int32).reshape(1)
    dest = _dest(idx, rank, start.astype(F32).reshape(1, E))
    n_slots = n_blocks * BM
    e_i = jnp.arange(E, dtype=jnp.int32)[:, None]
    j_i = jnp.arange(BM, dtype=jnp.int32)[None, :]
    pad = (start + counts)[:, None] + j_i
    pad_slots = jnp.where(pad < end.astype(jnp.int32)[:, None], pad, n_slots + e_i * BM + j_i)
    xg = _sc_dispatch(h_pk, dest[:, :TOP_K], pad_slots, n_slots + E * BM)
    yg = _experts(layer, block_e, n_used, xg, w_e1, w_e3, w_e2)
    dest_km = jnp.swapaxes(dest[:, :TOP_K], 0, 1).reshape(-1)
    yt = _sc_gather(yg, dest_km).reshape(TOP_K, n, D // 2)
    return _combine(alpha, h, gate, yt, ws13, ws2, ln_g, ln_b)


def kernel(x_prompt, x_sample, state_wkv, state_shift, state_conv, cache_mem_k, cache_mem_v, mem_prompt, w_in, w_in_vmix, mu_shift, mu_vmix, w0, w_up, a0, a_up, v0, v_up, g_up, k_k, k_a, r_k, lnx_g, lnx_b, w_o_a, conv_w, conv_b, cn_g, cn_b, w_o_b, w_mem_kv, w_o_c, w_out, ln1_g, ln1_b, ln2_g, ln2_b, w_router, b_router, w_e1, w_e3, w_e2, w_s1, w_s3, w_s2):
    B, T, D = x_prompt.shape
    SB, ST, _ = x_sample.shape
    L = w_in.shape[0]
    M = mem_prompt.shape[1]
    R = ROW_TILE
    n_p, n_s = B * T, SB * ST
    n = n_p + n_s
    assert n_s == R and SB == LANES and T % R == 0 and T % CHUNK == 0 and D % LANES == 0
    npt, tps = n_p // R, T // R
    alpha = (2.0 * L) ** 0.25
    off_glu = RWKV_COLS
    off_q = off_glu + 2 * CONV_CH
    off_gate = off_q + MEM_WIDTH
    in_cols = off_gate + 3 * D

    x = jnp.concatenate([x_prompt.reshape(n_p, D), jnp.swapaxes(x_sample, 0, 1).reshape(n_s, D)], axis=0)
    mem_rows = mem_prompt.reshape(B * M, D)
    row2 = lambda a: a.reshape(1, -1)
    v_first = None
    outs = {k: [] for k in ('wkv_p', 'shift_p', 'conv_p', 'mk', 'mv', 'wkv_s', 'shift_s', 'conv_s')}

    for l in range(L):
        first = l == 0
        w_in_l = w_in[l].astype(BF16)
        w_rwkv = w_in_l[:, :off_glu]
        p_rwkv = _matmul(x, w_rwkv, R, RWKV_COLS // 2)
        p_glu = _matmul(x, w_in_l[:, off_glu:off_q], R, 2 * CONV_CH)
        p_q = _matmul(x, w_in_l[:, off_q:off_gate], R, MEM_WIDTH)
        p_gate = _matmul(x, w_in_l[:, off_gate:in_cols], R, D)
        x_prev_s = state_shift[l]
        prev_rwkv = _matmul(x_prev_s, w_rwkv, SB, RWKV_COLS // 2)

        wup_pad = jnp.concatenate([w_up[l], jnp.zeros((LORA_A, RW), F32)], axis=0).astype(BF16)
        aup_pad = jnp.concatenate([jnp.zeros((LORA_W, RW), F32), a_up[l]], axis=0).astype(BF16)
        lora = (row2(w0[l]), wup_pad, row2(a0[l]), aup_pad, g_up[l].astype(BF16), row2(k_k[l]), row2(k_a[l]),
                row2(r_k[l]), _seg_ones())
        if first:
            pre = _rwkv_pre(True, npt, tps, p_rwkv, prev_rwkv, row2(mu_shift[l]), lora)
        else:
            wv_pad = jnp.pad(w_in_vmix[l - 1], ((0, 0), (0, LANES - LORA_V))).astype(BF16)
            p_vd = _matmul(x, wv_pad, R, LANES)
            prev_vd = _matmul(x_prev_s, wv_pad, SB, LANES)
            mu_v = jnp.pad(mu_vmix[l - 1], (0, LANES - LORA_V)).reshape(1, LANES)
            vup_pad = jnp.pad(v_up[l - 1], ((0, LANES - LORA_V), (0, 0))).astype(BF16)
            pre = _rwkv_pre(False, npt, tps, p_rwkv, prev_rwkv, row2(mu_shift[l]), lora, p_vd, prev_vd, mu_v,
                            v_first, row2(v0[l - 1]), vup_pad)
        r_, lw_, k_, v_, nkk_, bb_, g_, bonus_ = pre
        if first:
            v_first = v_

        y_p, s_p = _wkv_prompt(B, T, r_, lw_, k_, v_, nkk_, bb_)

        def lanes_b(a):
            return jnp.transpose(a[n_p:].reshape(ST, SB, RW), (0, 2, 1)).reshape(ST, HEADS, HEAD_DIM, SB)

        state_sz = HEADS * HEAD_DIM * HEAD_DIM
        s0 = _transpose_layer(state_wkv.astype(F32).reshape(L, SB, state_sz), l, 2048)
        s0 = s0.reshape(HEADS, HEAD_DIM, HEAD_DIM, SB)
        y_s, s_s = _wkv_sample(s0, lanes_b(r_), lanes_b(lw_), lanes_b(k_),
                               lanes_b(v_).reshape(ST, HEADS, HEAD_DIM, 1, SB), lanes_b(nkk_), lanes_b(bb_))
        y_s = jnp.transpose(y_s.reshape(ST, RW, SB), (0, 2, 1)).reshape(n_s, RW)
        s_s = _transpose_2d(s_s.reshape(state_sz, SB), 2048).reshape(SB, HEADS, HEAD_DIM, HEAD_DIM)
        z_a = _rwkv_post(npt, y_p, y_s, bonus_, g_, row2(lnx_g[l]), row2(lnx_b[l]))

        cvec = (conv_w[l], row2(conv_b[l]), row2(cn_g[l]), row2(cn_b[l]))
        zb_p, conv_p = _conv_prompt(B, T, p_glu, *cvec)
        st_t = jnp.swapaxes(state_conv[l], 0, 1)
        zb_s, glu_s = _conv_sample(st_t, p_glu[n_p:].reshape(ST, SB, 2 * CONV_CH), *cvec)
        zb_s = zb_s.reshape(n_s, CONV_CH)
        conv_s = jnp.concatenate([state_conv[l][:, ST:], jnp.swapaxes(glu_s, 0, 1)], axis=1)

        kv_p = _matmul(mem_rows, w_mem_kv[l].astype(BF16), M, 2 * MEM_WIDTH)
        kv3 = kv_p.reshape(B, M, 2 * MEM_WIDTH)
        zc_p = _attention(p_q[:n_p].reshape(B, T, MEM_WIDTH), kv3, kv3, 0, 1, 1, R).reshape(n_p, MEM_WIDTH)
        q_s = jnp.swapaxes(p_q[n_p:].reshape(ST, SB, MEM_WIDTH), 0, 1)
        q_s = jnp.pad(q_s, ((0, 0), (0, 8 - ST), (0, 0)))
        zc_s = _attention_cached(q_s, cache_mem_k.reshape(L, SB, M * MEM_HEADS, MEM_HEAD_DIM),
                                 cache_mem_v.reshape(L, SB, M * MEM_HEADS, MEM_HEAD_DIM), l, 8)[:, :ST]
        zc_s = jnp.swapaxes(zc_s, 0, 1).reshape(n_s, MEM_WIDTH)

        h, h_pk = _merge(n_p, alpha, z_a, zb_p, zb_s, zc_p, zc_s, p_gate, x,
                         w_o_a[l].astype(BF16), w_o_b[l].astype(BF16), w_o_c[l].astype(BF16),
                         w_out[l].astype(BF16), row2(ln1_g[l]), row2(ln1_b[l]))

        ws13 = jnp.concatenate([w_s1[l], w_s3[l]], axis=1).astype(BF16)
        x_new = _moe(l, alpha, h, h_pk, w_router[l].astype(BF16), row2(b_router[l]), w_e1, w_e3, w_e2,
                     ws13, w_s2[l].astype(BF16), row2(ln2_g[l]), row2(ln2_b[l]))

        outs['wkv_p'].append(s_p.astype(state_wkv.dtype))
        outs['shift_p'].append(x[:n_p].reshape(B, T, D)[:, -1].astype(state_shift.dtype))
        outs['conv_p'].append(conv_p.astype(state_conv.dtype))
        outs['mk'].append(kv3[..., :MEM_WIDTH].reshape(B, M, MEM_HEADS, MEM_HEAD_DIM).astype(cache_mem_k.dtype))
        outs['mv'].append(kv3[..., MEM_WIDTH:].reshape(B, M, MEM_HEADS, MEM_HEAD_DIM).astype(cache_mem_v.dtype))
        outs['wkv_s'].append(s_s.astype(state_wkv.dtype))
        outs['shift_s'].append(x[n_p + (ST - 1) * SB:].astype(state_shift.dtype))
        outs['conv_s'].append(conv_s.astype(state_conv.dtype))
        x = x_new

    y_p = x[:n_p].reshape(B, T, D)
    y_s = jnp.swapaxes(x[n_p:].reshape(ST, SB, D), 0, 1)
    return (y_p, y_s, jnp.stack(outs['wkv_p']), jnp.stack(outs['shift_p']), jnp.stack(outs['conv_p']),
            jnp.stack(outs['mk']), jnp.stack(outs['mv']), jnp.stack(outs['wkv_s']), jnp.stack(outs['shift_s']),
            jnp.stack(outs['conv_s']))
```

```python
import functools
import math

import jax
import jax.numpy as jnp
from jax import lax
from jax.experimental import pallas as pl
from jax.experimental.pallas import tpu as pltpu
from jax.experimental.pallas import tpu_sc as plsc

F32 = jnp.float32
BF16 = jnp.bfloat16

HEADS = 8
HEAD_DIM = 64
RW = HEADS * HEAD_DIM
LORA_W = 64
LORA_A = 64
LORA_V = 32
LORA_G = 128
RWKV_COLS = 3 * RW + LORA_W + LORA_A + LORA_G
GN_EPS = HEAD_DIM * 1e-5
CONV_WIDTH = 31
CONV_BUF = CONV_WIDTH - 1
CONV_CH = 512
MEM_HEADS = 4
MEM_HEAD_DIM = 128
MEM_WIDTH = MEM_HEADS * MEM_HEAD_DIM
TOP_K = 8
ROUTED_SCALE = 2.5
LN_EPS = 1e-5
EXP_M05 = math.exp(-0.5)

LANES = 128
ROW_TILE = 512
MERGE_TILE = 256
CHUNK = 64
NEUMANN_BLOCK = 16
MOE_BLOCK = 512
WKV_BATCH_PER_STEP = 4
SC_CORES = 2
SC_SUBCORES = 16
SC_WINDOW = 128
SC_DISPATCH_WINDOW = 48
SC_PAD_WINDOW = 64
COMBINE_TILE = 128
VMEM_LIMIT = 48 * 1024 * 1024


def _cparams(*sem, vmem=VMEM_LIMIT):
    return pltpu.CompilerParams(dimension_semantics=tuple(sem), vmem_limit_bytes=vmem)


def _bdot(a, b):
    return jnp.dot(a.astype(BF16), b.astype(BF16), preferred_element_type=F32)


def _bdot_nt(a, b):
    return lax.dot_general(a.astype(BF16), b.astype(BF16), (((1,), (1,)), ((), ())),
                           preferred_element_type=F32)


def _sigmoid(x):
    return 1.0 / (1.0 + jnp.exp(-x))


def _silu(x):
    return x * _sigmoid(x)


def _layer_norm(x, g, b):
    mu = jnp.mean(x, axis=-1, keepdims=True)
    d = x - mu
    var = jnp.mean(d * d, axis=-1, keepdims=True)
    return d * lax.rsqrt(var + LN_EPS) * g + b


SEG_LANES = 256


def _seg_ones():
    i = jnp.arange(SEG_LANES)
    return ((i[:, None] // HEAD_DIM) == (i[None, :] // HEAD_DIM)).astype(BF16)


def _seg_sum(x, seg):
    hi = x.astype(BF16)
    lo = (x - hi.astype(F32)).astype(BF16)
    parts = []
    for j in range(x.shape[-1] // SEG_LANES):
        sl = slice(j * SEG_LANES, (j + 1) * SEG_LANES)
        parts.append(jnp.dot(hi[:, sl], seg, preferred_element_type=F32)
                     + jnp.dot(lo[:, sl], seg, preferred_element_type=F32))
    return jnp.concatenate(parts, axis=-1)


def _pack_pair(lo, hi):
    lo_b = lax.bitcast_convert_type(lo.astype(BF16).astype(F32), jnp.int32)
    hi_b = lax.bitcast_convert_type(hi.astype(BF16).astype(F32), jnp.int32)
    return lax.shift_right_logical(lo_b, jnp.full(lo_b.shape, 16, jnp.int32)) | (hi_b & jnp.int32(-65536))


def _unpack_pair(p):
    lo = lax.bitcast_convert_type(lax.shift_left(p, jnp.full(p.shape, 16, jnp.int32)), F32)
    hi = lax.bitcast_convert_type(p & jnp.int32(-65536), F32)
    return lo, hi


def _mm_kernel(x_ref, w_ref, o_ref):
    o_ref[...] = _bdot(x_ref[...], w_ref[...]).astype(o_ref.dtype)


def _matmul(x, w, tm, tn, out_dtype=F32):
    m, k = x.shape
    n_cols = w.shape[1]
    assert m % tm == 0 and n_cols % tn == 0
    return pl.pallas_call(
        _mm_kernel,
        grid=(m // tm, n_cols // tn),
        in_specs=[pl.BlockSpec((tm, k), lambda i, j: (i, 0)),
                  pl.BlockSpec((k, tn), lambda i, j: (0, j))],
        out_specs=pl.BlockSpec((tm, tn), lambda i, j: (i, j)),
        out_shape=jax.ShapeDtypeStruct((m, n_cols), out_dtype),
        compiler_params=_cparams("parallel", "parallel"),
        name=f"matmul_{n_cols}",
    )(x, w)


def _pre_kernel(first, npt, tps, *refs):
    if first:
        (p_ref, halo_ref, prev_ref, mu_ref,
         w0_ref, wup_ref, a0_ref, aup_ref, gup_ref, kk_ref, ka_ref, rk_ref, seg_ref,
         r_o, lw_o, k_o, v_o, nkk_o, bb_o, g_o, bon_o, sh_ref) = refs
    else:
        (p_ref, halo_ref, prev_ref, mu_ref, pv_ref, halov_ref, prevv_ref, muv_ref, vf_ref, v0_ref, vup_ref,
         w0_ref, wup_ref, a0_ref, aup_ref, gup_ref, kk_ref, ka_ref, rk_ref, seg_ref,
         r_o, lw_o, k_o, v_o, nkk_o, bb_o, g_o, bon_o, sh_ref, shv_ref) = refs
    i = pl.program_id(0)
    rows_n = p_ref.shape[0]

    def build_shift(src_ref, halo_r, prev_r, dst_ref):
        @pl.when(i < npt)
        def _():
            x = src_ref[...]
            rolled = pltpu.roll(x, 1, axis=0)
            last = halo_r[7:8, :]
            row0 = jnp.where((i % tps) == 0, jnp.zeros_like(last), last)
            rows = lax.broadcasted_iota(jnp.int32, x.shape, 0)
            dst_ref[...] = jnp.where(rows == 0, row0, rolled)

        @pl.when(i >= npt)
        def _():
            nb = prev_r.shape[0]
            dst_ref[0:nb, :] = prev_r[...]
            dst_ref[nb:rows_n, :] = src_ref[0:rows_n - nb, :]

    build_shift(p_ref, halo_ref, prev_ref, sh_ref)
    p = p_ref[...]
    p = p + (sh_ref[...] - p) * mu_ref[...]
    r = p[:, 0:RW]
    k = p[:, RW:2 * RW]
    v = p[:, 2 * RW:3 * RW]
    u = p[:, 3 * RW:3 * RW + LORA_W + LORA_A]
    gd = p[:, 3 * RW + LORA_W + LORA_A:RWKV_COLS]

    z = w0_ref[...] + _bdot(jnp.tanh(u), wup_ref[...])
    lw = -EXP_M05 * _sigmoid(z)
    a = _sigmoid(a0_ref[...] + _bdot(u, aup_ref[...]))
    g = _bdot(_sigmoid(gd), gup_ref[...])
    if not first:
        build_shift(pv_ref, halov_ref, prevv_ref, shv_ref)
        pv = pv_ref[...]
        pv = pv + (shv_ref[...] - pv) * muv_ref[...]
        mix = _sigmoid(v0_ref[...] + _bdot(pv, vup_ref[...]))
        v = v + (vf_ref[...] - v) * mix
    kk = k * kk_ref[...]
    seg = seg_ref[...]
    nrm = jnp.sqrt(_seg_sum(kk * kk, seg))
    kkn = kk / jnp.maximum(nrm, 1e-12)
    k2 = k * (1.0 + (a - 1.0) * ka_ref[...])
    bonus = _seg_sum(r * k2 * rk_ref[...], seg) * v
    r_o[...] = r
    lw_o[...] = lw
    k_o[...] = k2
    v_o[...] = v
    nkk_o[...] = -kkn
    bb_o[...] = kkn * a
    g_o[...] = g
    bon_o[...] = bonus


def _rwkv_pre(first, npt, tps, p_rwkv, prev_rwkv, mu, lora, p_vd=None, prev_vd=None, mu_v=None, v_first=None,
              v0=None, vup=None):
    n = p_rwkv.shape[0]
    R = ROW_TILE
    nt = n // R

    def row_spec(w):
        return pl.BlockSpec((R, w), lambda i: (i, 0))

    def halo_spec(w):
        return pl.BlockSpec((8, w), lambda i: (jnp.maximum(i * (R // 8) - 1, 0), 0))

    def full(a):
        return pl.BlockSpec(a.shape, lambda i: (0,) * a.ndim)

    ins = [p_rwkv, p_rwkv, prev_rwkv, mu]
    specs = [row_spec(RWKV_COLS), halo_spec(RWKV_COLS), full(prev_rwkv), full(mu)]
    scratch = [pltpu.VMEM((R, RWKV_COLS), F32)]
    if not first:
        ins += [p_vd, p_vd, prev_vd, mu_v, v_first, v0, vup]
        specs += [row_spec(LANES), halo_spec(LANES), full(prev_vd), full(mu_v), row_spec(RW), full(v0), full(vup)]
        scratch.append(pltpu.VMEM((R, LANES), F32))
    ins += list(lora)
    specs += [full(a) for a in lora]
    outs = pl.pallas_call(
        functools.partial(_pre_kernel, first, npt, tps),
        grid=(nt,),
        in_specs=specs,
        out_specs=[row_spec(RW)] * 8,
        out_shape=[jax.ShapeDtypeStruct((n, RW), F32)] * 8,
        scratch_shapes=scratch,
        compiler_params=_cparams("parallel"),
        name="rwkv_pre",
    )(*ins)
    return outs


def _wkv_chunk_kernel(nbat, *refs):
    in_refs = refs[:6 * nbat]
    y_ref, s_out_ref, s_ref = refs[6 * nbat:]
    c = pl.program_id(1)
    C = in_refs[0].shape[0]
    G = 4 * HEAD_DIM
    assert 4 * C == G

    @pl.when(c == 0)
    def _():
        s_ref[...] = jnp.zeros_like(s_ref)

    ri = lax.broadcasted_iota(jnp.int32, (G, G), 0)
    ci = lax.broadcasted_iota(jnp.int32, (G, G), 1)
    same_head = (ri // C) == (ci // HEAD_DIM)
    tril_s = same_head & ((ci % C) < (ri % C))
    tril_i = same_head & ((ci % C) <= (ri % C))
    diag_blk = (ri // NEUMANN_BLOCK) == (ci // NEUMANN_BLOCK)
    eye = (ri == ci).astype(F32)

    def stack(x):
        return jnp.where(same_head, jnp.concatenate([x, x, x, x], axis=0), 0.0).astype(BF16)

    def tile4(x):
        return jnp.concatenate([x, x, x, x], axis=0).astype(BF16)

    def collapse(x4):
        return x4[0:C] + x4[C:2 * C] + x4[2 * C:3 * C] + x4[3 * C:4 * C]

    for b in range(nbat):
        r_ref, lw_ref, k_ref, v_ref, nkk_ref, bb_ref = in_refs[6 * b:6 * b + 6]
        lw = lw_ref[...]
        rows = lax.broadcasted_iota(jnp.int32, lw.shape, 0)
        cs = lw
        s = 1
        while s < C:
            cs = cs + jnp.where(rows >= s, pltpu.roll(cs, s, axis=0), 0.0)
            s *= 2
        cs_last = cs[C - 1:C, :]
        e_neg = jnp.exp(-cs)
        e_tail = jnp.exp(cs_last - cs)
        kk_ = k_ref[...]
        bb_ = bb_ref[...]
        at = nkk_ref[...] * jnp.exp(cs - lw)
        rt = r_ref[...] * jnp.exp(cs)
        bt = bb_ * e_neg
        kt = kk_ * e_neg
        bw = bb_ * e_tail
        kw = kk_ * e_tail
        vv = v_ref[...]
        w_end = jnp.exp(cs_last)

        for g in range(RW // G):
            sl = slice(g * G, (g + 1) * G)
            at4, rt4, v4 = stack(at[:, sl]), stack(rt[:, sl]), stack(vv[:, sl])
            bt4, kt4 = tile4(bt[:, sl]), tile4(kt[:, sl])
            a_ab = jnp.where(tril_s, _bdot_nt(at4, bt4), 0.0)
            a_ak = jnp.where(tril_s, _bdot_nt(at4, kt4), 0.0).astype(BF16)
            a_rb = jnp.where(tril_i, _bdot_nt(rt4, bt4), 0.0).astype(BF16)
            a_rk = jnp.where(tril_i, _bdot_nt(rt4, kt4), 0.0).astype(BF16)
            a_d = jnp.where(diag_blk, a_ab, 0.0)
            a_o = a_ab - a_d
            a_db = a_d.astype(BF16)
            x2 = _bdot(a_db, a_db).astype(BF16)
            x4 = _bdot(x2, x2).astype(BF16)
            x8 = _bdot(x4, x4)
            t_d = eye + a_d
            t_d = t_d + _bdot(t_d, x2)
            t_d = t_d + _bdot(t_d, x4)
            t_d = t_d + _bdot(t_d, x8)
            t_db = t_d.astype(BF16)
            nn = _bdot(t_db, a_o).astype(BF16)
            n2 = _bdot(nn, nn)
            m1 = t_d + _bdot(nn, t_db)
            t_m = m1 + _bdot(n2, m1)
            s4 = s_ref[b, g]
            s4b = s4.astype(BF16)
            u4 = _bdot(t_m, _bdot_nt(at4, s4b) + _bdot(a_ak, v4))
            y4 = _bdot_nt(rt4, s4b) + _bdot(jnp.concatenate([a_rb, a_rk], axis=1),
                                            jnp.concatenate([u4.astype(BF16), v4], axis=0))
            y_ref[b, :, sl] = collapse(y4)
            uv_t = jnp.concatenate([collapse(u4), vv[:, sl]], axis=0).T
            upd = _bdot(uv_t, jnp.concatenate([bw[:, sl], kw[:, sl]], axis=0))
            s_ref[b, g] = s4 * w_end[:, sl] + jnp.where((ri // HEAD_DIM) == (ci // HEAD_DIM), upd, 0.0)

    @pl.when(c == pl.num_programs(1) - 1)
    def _():
        for b in range(nbat):
            for hd in range(HEADS):
                g, j = divmod(hd, 4)
                blk = slice(j * HEAD_DIM, (j + 1) * HEAD_DIM)
                s_out_ref[b, hd] = s_ref[b, g, blk, blk]


def _wkv_prompt(B, T, r, lw, k, v, nkk, bb):
    C = CHUNK
    nc = T // C
    G = 4 * HEAD_DIM
    nbat = math.gcd(B, WKV_BATCH_PER_STEP)
    specs = []
    for j in range(nbat):
        specs += [pl.BlockSpec((C, RW), lambda i, c, j=j: ((nbat * i + j) * nc + c, 0))] * 6
    y, s = pl.pallas_call(
        functools.partial(_wkv_chunk_kernel, nbat),
        grid=(B // nbat, nc),
        in_specs=specs,
        out_specs=[pl.BlockSpec((nbat, C, RW), lambda i, c: (i, c, 0)),
                   pl.BlockSpec((nbat, HEADS, HEAD_DIM, HEAD_DIM), lambda i, c: (i, 0, 0, 0))],
        out_shape=[jax.ShapeDtypeStruct((B, T, RW), F32),
                   jax.ShapeDtypeStruct((B, HEADS, HEAD_DIM, HEAD_DIM), F32)],
        scratch_shapes=[pltpu.VMEM((nbat, RW // G, G, G), F32)],
        compiler_params=_cparams("parallel", "arbitrary"),
        name="wkv_chunk",
    )(*([r, lw, k, v, nkk, bb] * nbat))
    return y.reshape(B * T, RW), s


def _wkv_seq_kernel(s0_ref, r_ref, lw_ref, k_ref, v_ref, nkk_ref, bb_ref, y_ref, s_ref):
    steps = r_ref.shape[0]

    def body(vi, carry):
        st = s0_ref[0, vi]
        for t in range(steps):
            sa = jnp.sum(st * nkk_ref[t, 0], axis=0, keepdims=True)
            st = st * jnp.exp(lw_ref[t, 0]) + sa * bb_ref[t, 0] + v_ref[t, 0, vi] * k_ref[t, 0]
            y_ref[t, 0, vi] = jnp.sum(st * r_ref[t, 0], axis=0, keepdims=True)
        s_ref[0, vi] = st
        return carry

    lax.fori_loop(0, HEAD_DIM, body, 0)


def _wkv_sample(s0, r, lw, k, v, nkk, bb):
    steps, _, _, nb = r.shape
    vec = pl.BlockSpec((steps, 1, HEAD_DIM, nb), lambda h: (0, h, 0, 0))
    vcol = pl.BlockSpec((steps, 1, HEAD_DIM, 1, nb), lambda h: (0, h, 0, 0, 0))
    st = pl.BlockSpec((1, HEAD_DIM, HEAD_DIM, nb), lambda h: (h, 0, 0, 0))
    y, s = pl.pallas_call(
        _wkv_seq_kernel,
        grid=(HEADS,),
        in_specs=[st, vec, vec, vec, vcol, vec, vec],
        out_specs=[vcol, st],
        out_shape=[jax.ShapeDtypeStruct((steps, HEADS, HEAD_DIM, 1, nb), F32),
                   jax.ShapeDtypeStruct(s0.shape, F32)],
        compiler_params=_cparams("parallel"),
        name="wkv_seq",
    )(s0, r, lw, k, v, nkk, bb)
    return y, s


def _tr_kernel(x_ref, o_ref):
    o_ref[...] = x_ref[...].T


def _transpose_layer(x3, layer, tc):
    _, a, c = x3.shape
    return pl.pallas_call(
        _tr_kernel,
        grid=(c // tc,),
        in_specs=[pl.BlockSpec((None, a, tc), lambda j: (layer, 0, j))],
        out_specs=pl.BlockSpec((tc, a), lambda j: (j, 0)),
        out_shape=jax.ShapeDtypeStruct((c, a), x3.dtype),
        compiler_params=_cparams("parallel"),
        name="transpose_in",
    )(x3)


def _transpose_2d(x, tc):
    c, a = x.shape
    return pl.pallas_call(
        _tr_kernel,
        grid=(c // tc,),
        in_specs=[pl.BlockSpec((tc, a), lambda j: (j, 0))],
        out_specs=pl.BlockSpec((a, tc), lambda j: (0, j)),
        out_shape=jax.ShapeDtypeStruct((a, c), x.dtype),
        compiler_params=_cparams("parallel"),
        name="transpose_out",
    )(x)


def _post_kernel(npt, yp_ref, ys_ref, bon_ref, g_ref, lg_ref, lb_ref, seg_ref, o_ref):
    i = pl.program_id(0)
    y = jnp.where(i < npt, yp_ref[...], ys_ref[...])
    seg = seg_ref[...]
    mu = _seg_sum(y, seg) * (1.0 / HEAD_DIM)
    d = y - mu
    var = _seg_sum(d * d, seg) * (1.0 / HEAD_DIM)
    yn = d * lax.rsqrt(var + GN_EPS) * lg_ref[...] + lb_ref[...]
    o_ref[...] = (yn + bon_ref[...]) * g_ref[...]


def _rwkv_post(npt, y_p, y_s, bonus, g, lnx_g, lnx_b):
    n = bonus.shape[0]
    R = ROW_TILE
    row = pl.BlockSpec((R, RW), lambda i: (i, 0))
    vec = pl.BlockSpec((1, RW), lambda i: (0, 0))
    return pl.pallas_call(
        functools.partial(_post_kernel, npt),
        grid=(n // R,),
        in_specs=[pl.BlockSpec((R, RW), lambda i: (jnp.minimum(i, npt - 1), 0)),
                  pl.BlockSpec((R, RW), lambda i: (0, 0)), row, row, vec, vec,
                  pl.BlockSpec((SEG_LANES, SEG_LANES), lambda i: (0, 0))],
        out_specs=row,
        out_shape=jax.ShapeDtypeStruct((n, RW), F32),
        compiler_params=_cparams("parallel"),
        name="rwkv_post",
    )(y_p, y_s, bonus, g, lnx_g, lnx_b, _seg_ones())


def _glu(p):
    p = p.astype(F32)
    return p[..., :CONV_CH] * _sigmoid(p[..., CONV_CH:])


def _conv_prompt_kernel(p_ref, halo_ref, w_ref, cb_ref, g_ref, b_ref, o_ref, tail_ref, ext_ref, sh_ref):
    j = pl.program_id(1)
    R = p_ref.shape[0]
    H = halo_ref.shape[0]
    glu = _glu(p_ref[...])
    halo = _glu(halo_ref[...])
    ext_ref[0:H, :] = jnp.where(j == 0, jnp.zeros_like(halo), halo)
    ext_ref[H:H + R, :] = glu
    span = sh_ref.shape[1]
    for ph in range(1, 8):
        sh_ref[ph] = ext_ref[ph:ph + span, :]
    acc = jnp.zeros((R, CONV_CH), F32)
    for t in range(CONV_WIDTH):
        off = H - CONV_BUF + t
        q8 = (off // 8) * 8
        tap = ext_ref[q8:q8 + R, :] if off % 8 == 0 else sh_ref[off % 8, q8:q8 + R, :]
        acc = acc + tap * w_ref[t:t + 1, :]
    h = _layer_norm(acc + cb_ref[...], g_ref[...], b_ref[...])
    o_ref[...] = _silu(h)

    @pl.when(j == pl.num_programs(1) - 1)
    def _():
        tail_ref[0] = glu[R - H:R, :]


def _conv_prompt(B, T, p_glu, conv_w, conv_b, cn_g, cn_b):
    R = ROW_TILE
    H = 32
    tps = T // R
    vec = pl.BlockSpec((1, CONV_CH), lambda b, j: (0, 0))
    z, tail = pl.pallas_call(
        _conv_prompt_kernel,
        grid=(B, tps),
        in_specs=[pl.BlockSpec((R, 2 * CONV_CH), lambda b, j: (b * tps + j, 0)),
                  pl.BlockSpec((H, 2 * CONV_CH), lambda b, j: (jnp.maximum((b * tps + j) * (R // H) - 1, 0), 0)),
                  pl.BlockSpec((CONV_WIDTH, CONV_CH), lambda b, j: (0, 0)), vec, vec, vec],
        out_specs=[pl.BlockSpec((R, CONV_CH), lambda b, j: (b * tps + j, 0)),
                   pl.BlockSpec((1, H, CONV_CH), lambda b, j: (b, 0, 0))],
        out_shape=[jax.ShapeDtypeStruct((B * T, CONV_CH), F32),
                   jax.ShapeDtypeStruct((B, H, CONV_CH), F32)],
        scratch_shapes=[pltpu.VMEM((H + R, CONV_CH), F32), pltpu.VMEM((8, H + R - 8, CONV_CH), F32)],
        compiler_params=_cparams("parallel", "arbitrary"),
        name="conv_prompt",
    )(p_glu, p_glu, conv_w, conv_b, cn_g, cn_b)
    return z, tail[:, H - CONV_BUF:, :]


def _conv_sample_kernel(st_ref, p_ref, w_ref, cb_ref, g_ref, b_ref, o_ref, glu_ref):
    steps = p_ref.shape[0]
    glu = [_glu(p_ref[t]) for t in range(steps)]
    for t in range(steps):
        glu_ref[t] = glu[t]
    for t in range(steps):
        acc = jnp.zeros(glu[0].shape, F32)
        for j in range(CONV_WIDTH):
            src = t + j
            x = st_ref[src] if src < CONV_BUF else glu[src - CONV_BUF]
            acc = acc + x * w_ref[j:j + 1, :]
        h = _layer_norm(acc + cb_ref[...], g_ref[...], b_ref[...])
        o_ref[t] = _silu(h)


def _conv_sample(state_t, p_glu, conv_w, conv_b, cn_g, cn_b):
    steps, nb, _ = p_glu.shape

    def full(a):
        return pl.BlockSpec(a.shape, lambda i: (0,) * a.ndim)

    ins = (state_t, p_glu, conv_w, conv_b, cn_g, cn_b)
    return pl.pallas_call(
        _conv_sample_kernel,
        grid=(1,),
        in_specs=[full(a) for a in ins],
        out_specs=[pl.BlockSpec((steps, nb, CONV_CH), lambda i: (0, 0, 0))] * 2,
        out_shape=[jax.ShapeDtypeStruct((steps, nb, CONV_CH), F32)] * 2,
        compiler_params=_cparams("arbitrary"),
        name="conv_sample",
    )(*ins)


def _attn_kernel(q_ref, k_ref, v_ref, o_ref):
    nb = q_ref.shape[0]
    scale = MEM_HEAD_DIM ** -0.5
    for bi in range(nb):
        for h in range(MEM_HEADS):
            sl = slice(h * MEM_HEAD_DIM, (h + 1) * MEM_HEAD_DIM)
            s = _bdot_nt(q_ref[bi, :, sl], k_ref[bi, :, sl]) * scale
            m = jnp.max(s, axis=-1, keepdims=True)
            e = jnp.exp(s - m)
            pr = e / jnp.sum(e, axis=-1, keepdims=True)
            o_ref[bi, :, sl] = _bdot(pr, v_ref[bi, :, sl])


def _attn_rows_kernel(q_ref, k_ref, v_ref, o_ref):
    nb, tq, width = q_ref.shape
    n_mem = k_ref.shape[1] // MEM_HEADS
    scale = MEM_HEAD_DIM ** -0.5
    ri = lax.broadcasted_iota(jnp.int32, (MEM_HEADS * tq, width), 0)
    ci = lax.broadcasted_iota(jnp.int32, (MEM_HEADS * tq, width), 1)
    own = (ri // tq) == (ci // MEM_HEAD_DIM)
    for bi in range(nb):
        k_all = jnp.concatenate([k_ref[bi, pl.ds(h, n_mem, stride=MEM_HEADS), :].astype(BF16)
                                 for h in range(MEM_HEADS)], axis=1)
        v_all = jnp.concatenate([v_ref[bi, pl.ds(h, n_mem, stride=MEM_HEADS), :].astype(BF16)
                                 for h in range(MEM_HEADS)], axis=1)
        q = q_ref[bi]
        q4 = jnp.where(own, jnp.concatenate([q] * MEM_HEADS, axis=0), 0.0)
        s = _bdot_nt(q4, k_all) * scale
        m = jnp.max(s, axis=-1, keepdims=True)
        e = jnp.exp(s - m)
        pr = e / jnp.sum(e, axis=-1, keepdims=True)
        pv = jnp.where(own, _bdot(pr, v_all), 0.0)
        o = pv[0:tq]
        for h in range(1, MEM_HEADS):
            o = o + pv[h * tq:(h + 1) * tq]
        o_ref[bi] = o


def _attention_cached(q, cache_k, cache_v, layer, bb):
    B, T, W = q.shape
    rows = cache_k.shape[2]
    kv_spec = pl.BlockSpec((None, bb, rows, MEM_HEAD_DIM), lambda b: (layer, b, 0, 0))
    return pl.pallas_call(
        _attn_rows_kernel,
        grid=(B // bb,),
        in_specs=[pl.BlockSpec((bb, T, W), lambda b: (b, 0, 0)), kv_spec, kv_spec],
        out_specs=pl.BlockSpec((bb, T, W), lambda b: (b, 0, 0)),
        out_shape=jax.ShapeDtypeStruct((B, T, W), F32),
        compiler_params=_cparams("parallel"),
        name="mem_attention_cached",
    )(q, cache_k, cache_v)


def _attention(q, mk, mv, kcol, vcol, bb, tq):
    B, T, W = q.shape
    M = mk.shape[1]
    return pl.pallas_call(
        _attn_kernel,
        grid=(B // bb, T // tq),
        in_specs=[pl.BlockSpec((bb, tq, W), lambda b, j: (b, j, 0)),
                  pl.BlockSpec((bb, M, W), lambda b, j: (b, 0, kcol)),
                  pl.BlockSpec((bb, M, W), lambda b, j: (b, 0, vcol))],
        out_specs=pl.BlockSpec((bb, tq, W), lambda b, j: (b, j, 0)),
        out_shape=jax.ShapeDtypeStruct((B, T, W), F32),
        compiler_params=_cparams("parallel", "parallel"),
        name=f"mem_attention_{tq}",
    )(q, mk, mv)


def _merge_kernel(npt, alpha, za_ref, zbp_ref, zbs_ref, zcp_ref, zcs_ref, gate_ref, x_ref,
                  woa_ref, wob_ref, woc_ref, wout_ref, g_ref, b_ref, o_ref, pk_ref):
    i = pl.program_id(0)
    D = x_ref.shape[1]
    is_p = i < npt
    zb = jnp.where(is_p, zbp_ref[...], zbs_ref[...])
    zc = jnp.where(is_p, zcp_ref[...], zcs_ref[...])
    merged = (_sigmoid(gate_ref[:, 0:D].astype(F32)) * _bdot(za_ref[...], woa_ref[...])
              + _sigmoid(gate_ref[:, D:2 * D].astype(F32)) * _bdot(zb, wob_ref[...])
              + _sigmoid(gate_ref[:, 2 * D:3 * D].astype(F32)) * _bdot(zc, woc_ref[...]))
    h = _layer_norm(alpha * x_ref[...] + _bdot(merged, wout_ref[...]), g_ref[...], b_ref[...])
    o_ref[...] = h
    pk_ref[...] = _pack_pair(h[:, 0:D // 2], h[:, D // 2:D])


def _merge(n_p, alpha, za, zb_p, zb_s, zc_p, zc_s, gates, x, woa, wob, woc, wout, ln_g, ln_b):
    n, D = x.shape
    R = MERGE_TILE
    npt = n_p // R

    def row(w):
        return pl.BlockSpec((R, w), lambda i: (i, 0))

    def prow(w):
        return pl.BlockSpec((R, w), lambda i: (jnp.minimum(i, npt - 1), 0))

    def srow(w):
        return pl.BlockSpec((R, w), lambda i: (jnp.maximum(i - npt, 0), 0))

    def full(a):
        return pl.BlockSpec(a.shape, lambda i: (0,) * a.ndim)

    return pl.pallas_call(
        functools.partial(_merge_kernel, npt, alpha),
        grid=(n // R,),
        in_specs=[row(RW), prow(CONV_CH), srow(CONV_CH), prow(MEM_WIDTH), srow(MEM_WIDTH), row(3 * D), row(D),
                  full(woa), full(wob), full(woc), full(wout), full(ln_g), full(ln_b)],
        out_specs=[row(D), row(D // 2)],
        out_shape=[jax.ShapeDtypeStruct((n, D), F32), jax.ShapeDtypeStruct((n, D // 2), jnp.int32)],
        compiler_params=_cparams("parallel"),
        name="branch_merge",
    )(za, zb_p, zb_s, zc_p, zc_s, gates, x, woa, wob, woc, wout, ln_g, ln_b)


def _router_kernel(h_ref, wr_ref, br_ref, idx_o, gate_o, rank_o, cnt_o, carry_ref):
    i = pl.program_id(0)
    R = h_ref.shape[0]
    E = wr_ref.shape[1]

    @pl.when(i == 0)
    def _():
        carry_ref[...] = jnp.zeros_like(carry_ref)

    scores = _sigmoid(_bdot(h_ref[...], wr_ref[...]))
    work = scores + br_ref[...]
    lane = lax.broadcasted_iota(jnp.int32, (R, E), 1).astype(F32)
    sel_mask = jnp.zeros((R, E), F32)
    idx_cols, sel_cols = [], []
    for _ in range(TOP_K):
        m = jnp.max(work, axis=-1, keepdims=True)
        first = jnp.min(jnp.where(work == m, lane, float(E)), axis=-1, keepdims=True)
        onehot = lane == first
        sel_cols.append(jnp.sum(jnp.where(onehot, scores, 0.0), axis=-1, keepdims=True))
        idx_cols.append(first)
        sel_mask = jnp.where(onehot, 1.0, sel_mask)
        work = jnp.where(onehot, -jnp.inf, work)
    total = sel_cols[0]
    for c in sel_cols[1:]:
        total = total + c
    ri = lax.broadcasted_iota(jnp.int32, (R, R), 0)
    ci = lax.broadcasted_iota(jnp.int32, (R, R), 1)
    before = jnp.dot((ci < ri).astype(BF16), sel_mask.astype(BF16), preferred_element_type=F32) + carry_ref[0:1, :]
    idx_out = jnp.zeros((R, E), F32)
    gate_out = jnp.zeros((R, E), F32)
    rank_out = jnp.zeros((R, E), F32)
    for kk in range(TOP_K):
        onehot = lane == idx_cols[kk]
        rank = jnp.sum(jnp.where(onehot, before, 0.0), axis=-1, keepdims=True)
        col = lane == float(kk)
        idx_out = jnp.where(col, idx_cols[kk], idx_out)
        gate_out = jnp.where(col, sel_cols[kk] / total * ROUTED_SCALE, gate_out)
        rank_out = jnp.where(col, rank, rank_out)
    idx_o[...] = idx_out
    gate_o[...] = gate_out
    rank_o[...] = rank_out
    carry_ref[0:1, :] = carry_ref[0:1, :] + jnp.sum(sel_mask, axis=0, keepdims=True)
    cnt_o[...] = carry_ref[...]


def _router(h, w_router, b_router):
    n, D = h.shape
    E = w_router.shape[1]
    assert E == LANES
    R = 256
    row = pl.BlockSpec((R, E), lambda i: (i, 0))
    return pl.pallas_call(
        _router_kernel,
        grid=(n // R,),
        in_specs=[pl.BlockSpec((R, D), lambda i: (i, 0)),
                  pl.BlockSpec((D, E), lambda i: (0, 0)),
                  pl.BlockSpec((1, E), lambda i: (0, 0))],
        out_specs=[row, row, row, pl.BlockSpec((8, E), lambda i: (0, 0))],
        out_shape=[jax.ShapeDtypeStruct((n, E), F32), jax.ShapeDtypeStruct((n, E), F32),
                   jax.ShapeDtypeStruct((n, E), F32), jax.ShapeDtypeStruct((8, E), F32)],
        scratch_shapes=[pltpu.VMEM((8, E), F32)],
        compiler_params=_cparams("arbitrary"),
        name="moe_router",
    )(h, w_router, b_router)


def _dest_kernel(idx_ref, rank_ref, start_ref, o_ref):
    R, E = idx_ref.shape
    lane = lax.broadcasted_iota(jnp.int32, (R, E), 1).astype(F32)
    idx = idx_ref[...]
    out = jnp.zeros((R, E), F32)
    start = start_ref[...]
    for kk in range(TOP_K):
        e_k = jnp.sum(jnp.where(lane == float(kk), idx, 0.0), axis=-1, keepdims=True)
        s_k = jnp.sum(jnp.where(lane == e_k, start, 0.0), axis=-1, keepdims=True)
        out = jnp.where(lane == float(kk), s_k, out)
    o_ref[...] = (out + rank_ref[...]).astype(jnp.int32)


def _dest(idx, rank, start):
    n, E = idx.shape
    R = 256
    row = pl.BlockSpec((R, E), lambda i: (i, 0))
    return pl.pallas_call(
        _dest_kernel,
        grid=(n // R,),
        in_specs=[row, row, pl.BlockSpec((1, E), lambda i: (0, 0))],
        out_specs=row,
        out_shape=jax.ShapeDtypeStruct((n, E), jnp.int32),
        compiler_params=_cparams("parallel"),
        name="moe_dest",
    )(idx, rank, start)


def _sc_dispatch(h_pk, dest, pad_slots, n_rows):
    n, width = h_pk.shape
    workers = SC_CORES * SC_SUBCORES
    per_w = n // workers
    win = SC_DISPATCH_WINDOW
    assert per_w * workers == n and per_w % win == 0
    n_chunk = per_w // win
    dest_w = jnp.transpose(dest.reshape(workers, n_chunk, win, TOP_K), (0, 1, 3, 2))
    pad_per_w = pad_slots.size // workers
    pwin = SC_PAD_WINDOW
    n_pchunk = pad_per_w // (TOP_K * pwin)
    assert n_pchunk * TOP_K * pwin * workers == pad_slots.size
    pad_w = pad_slots.reshape(workers, n_pchunk, TOP_K, pwin)
    zero_rows = jnp.zeros((pwin, width), h_pk.dtype)
    mesh = plsc.VectorSubcoreMesh(core_axis_name="c", subcore_axis_name="s",
                                  num_cores=SC_CORES, num_subcores=SC_SUBCORES)

    @functools.partial(
        pl.kernel, mesh=mesh,
        out_type=jax.ShapeDtypeStruct((n_rows, width), h_pk.dtype),
        scratch_types=[pltpu.VMEM((TOP_K, win), jnp.int32),
                       pltpu.VMEM((win, width), h_pk.dtype),
                       pltpu.VMEM((TOP_K, pwin), jnp.int32),
                       pltpu.VMEM((pwin, width), h_pk.dtype),
                       pltpu.SemaphoreType.DMA],
        name="moe_sc_dispatch",
    )
    def dispatch(h_hbm, dest_hbm, pad_hbm, zero_hbm, xg_hbm, idx_v, rows_v, pidx_v, zeros_v, sem):
        wid = lax.axis_index("s") * SC_CORES + lax.axis_index("c")
        pltpu.sync_copy(zero_hbm, zeros_v)

        @pl.loop(0, n_chunk)
        def _(j):
            base = pl.multiple_of(wid * per_w + j * win, 8)
            pltpu.sync_copy(h_hbm.at[pl.ds(base, win)], rows_v)
            pltpu.sync_copy(dest_hbm.at[wid, j], idx_v)
            copies = [pltpu.async_copy(rows_v, xg_hbm.at[idx_v.at[kk]], sem) for kk in range(TOP_K)]
            for cp in copies:
                cp.wait()

        @pl.loop(0, n_pchunk)
        def _(j):
            pltpu.sync_copy(pad_hbm.at[wid, j], pidx_v)
            copies = [pltpu.async_copy(zeros_v, xg_hbm.at[pidx_v.at[kk]], sem) for kk in range(TOP_K)]
            for cp in copies:
                cp.wait()

    return dispatch(h_pk, dest_w, pad_w, zero_rows)


def _expert_kernel(be_ref, nu_ref, x_ref, w1_ref, w3_ref, w2_ref, o_ref, w13_s, w2_s):
    i = pl.program_id(0)
    F = w1_ref.shape[1]

    @pl.when(i < nu_ref[0])
    def _():
        prev = be_ref[jnp.maximum(i - 1, 0)]

        @pl.when((i == 0) | (be_ref[i] != prev))
        def _():
            w13_s[:, 0:F] = w1_ref[...].astype(BF16)
            w13_s[:, F:2 * F] = w3_ref[...].astype(BF16)
            w2_s[...] = w2_ref[...].astype(BF16)

        x_lo, x_hi = _unpack_pair(x_ref[...])
        half = x_ref.shape[1]
        h13 = (jnp.dot(x_lo.astype(BF16), w13_s[0:half, :], preferred_element_type=F32)
               + jnp.dot(x_hi.astype(BF16), w13_s[half:2 * half, :], preferred_element_type=F32))
        hh = _silu(h13[:, 0:F]) * h13[:, F:2 * F]
        y = jnp.dot(hh.astype(BF16), w2_s[...], preferred_element_type=F32)
        o_ref[...] = _pack_pair(y[:, 0:half], y[:, half:2 * half])


def _experts(layer, block_e, n_used, xg, w_e1, w_e3, w_e2):
    n_slots, half = xg.shape
    D = 2 * half
    F = w_e1.shape[3]
    BM = MOE_BLOCK
    nb = block_e.shape[0]
    n_slots = nb * BM

    def blk(i, be, nu):
        return (jnp.minimum(i, nu[0] - 1), 0)

    grid_spec = pltpu.PrefetchScalarGridSpec(
        num_scalar_prefetch=2,
        grid=(nb,),
        in_specs=[pl.BlockSpec((BM, half), blk),
                  pl.BlockSpec((None, None, D, F), lambda i, be, nu: (layer, be[i], 0, 0)),
                  pl.BlockSpec((None, None, D, F), lambda i, be, nu: (layer, be[i], 0, 0)),
                  pl.BlockSpec((None, None, F, D), lambda i, be, nu: (layer, be[i], 0, 0))],
        out_specs=pl.BlockSpec((BM, half), blk),
        scratch_shapes=[pltpu.VMEM((D, 2 * F), BF16), pltpu.VMEM((F, D), BF16)],
    )
    return pl.pallas_call(
        _expert_kernel,
        grid_spec=grid_spec,
        out_shape=jax.ShapeDtypeStruct((n_slots, half), jnp.int32),
        compiler_params=_cparams("arbitrary"),
        name="moe_experts",
    )(block_e, n_used, xg, w_e1, w_e3, w_e2)


def _sc_gather(table, idx):
    n_idx = idx.shape[0]
    width = table.shape[1]
    workers = SC_CORES * SC_SUBCORES
    per_w = n_idx // workers
    assert per_w * workers == n_idx and per_w % SC_WINDOW == 0
    mesh = plsc.VectorSubcoreMesh(core_axis_name="c", subcore_axis_name="s",
                                  num_cores=SC_CORES, num_subcores=SC_SUBCORES)

    @functools.partial(
        pl.kernel, mesh=mesh,
        out_type=jax.ShapeDtypeStruct((n_idx, width), table.dtype),
        scratch_types=[pltpu.VMEM((SC_WINDOW,), jnp.int32),
                       pltpu.VMEM((SC_WINDOW, width), table.dtype),
                       pltpu.SemaphoreType.DMA],
        name="moe_sc_gather",
    )
    def gather(table_hbm, idx_hbm, out_hbm, idx_v, rows_v, sem):
        wid = lax.axis_index("s") * SC_CORES + lax.axis_index("c")

        @pl.loop(0, per_w // SC_WINDOW)
        def _(j):
            base = pl.multiple_of(wid * per_w + j * SC_WINDOW, SC_WINDOW)
            pltpu.sync_copy(idx_hbm.at[pl.ds(base, SC_WINDOW)], idx_v)
            pltpu.async_copy(table_hbm.at[idx_v], rows_v, sem).wait()
            pltpu.sync_copy(rows_v, out_hbm.at[pl.ds(base, SC_WINDOW)])

    return gather(table, idx)


def _combine_kernel(alpha, h_ref, gate_ref, yt_ref, ws13_ref, ws2_ref, g_ref, b_ref, o_ref):
    F = ws2_ref.shape[0]
    half = h_ref.shape[1] // 2
    h = h_ref[...]
    h13 = _bdot(h, ws13_ref[...])
    shared = _bdot(_silu(h13[:, 0:F]) * h13[:, F:2 * F], ws2_ref[...])
    r_lo = jnp.zeros((h.shape[0], half), F32)
    r_hi = jnp.zeros((h.shape[0], half), F32)
    for kk in range(TOP_K):
        y_lo, y_hi = _unpack_pair(yt_ref[kk])
        gk = gate_ref[:, kk:kk + 1]
        r_lo = r_lo + y_lo * gk
        r_hi = r_hi + y_hi * gk
    routed = jnp.concatenate([r_lo, r_hi], axis=-1)
    o_ref[...] = _layer_norm(alpha * h + (shared + routed), g_ref[...], b_ref[...])


def _combine(alpha, h, gate, yt, ws13, ws2, ln_g, ln_b):
    n, D = h.shape
    R = COMBINE_TILE

    def full(a):
        return pl.BlockSpec(a.shape, lambda i: (0,) * a.ndim)

    return pl.pallas_call(
        functools.partial(_combine_kernel, alpha),
        grid=(n // R,),
        in_specs=[pl.BlockSpec((R, D), lambda i: (i, 0)),
                  pl.BlockSpec((R, LANES), lambda i: (i, 0)),
                  pl.BlockSpec((TOP_K, R, D // 2), lambda i: (0, i, 0)),
                  full(ws13), full(ws2), full(ln_g), full(ln_b)],
        out_specs=pl.BlockSpec((R, D), lambda i: (i, 0)),
        out_shape=jax.ShapeDtypeStruct((n, D), F32),
        compiler_params=_cparams("parallel"),
        name="moe_combine",
    )(h, gate, yt, ws13, ws2, ln_g, ln_b)


def _moe(layer, alpha, h, h_pk, w_router, b_router, w_e1, w_e3, w_e2, ws13, ws2, ln_g, ln_b):
    n, D = h.shape
    E = w_router.shape[1]
    BM = MOE_BLOCK
    idx, gate, rank, cnt = _router(h, w_router, b_router)
    counts = cnt[0].astype(jnp.int32)
    padded = (counts + BM - 1) // BM * BM
    end = jnp.cumsum(padded)
    start = (end - padded).astype(jnp.int32)
    n_blocks = (n * TOP_K + BM - 1) // BM + E
    first_slot = jnp.arange(n_blocks, dtype=jnp.int32)[:, None] * BM
    block_e = jnp.minimum(jnp.sum((end[None, :] <= first_slot).astype(jnp.int32), axis=1), E - 1)
    n_used = (end[-1] // BM).astype(jnp.int32).reshape(1)
    dest = _dest(idx, rank, start.astype(F32).reshape(1, E))
    n_slots = n_blocks * BM
    e_i = jnp.arange(E, dtype=jnp.int32)[:, None]
    j_i = jnp.arange(BM, dtype=jnp.int32)[None, :]
    pad = (start + counts)[:, None] + j_i
    pad_slots = jnp.where(pad < end.astype(jnp.int32)[:, None], pad, n_slots + e_i * BM + j_i)
    xg = _sc_dispatch(h_pk, dest[:, :TOP_K], pad_slots, n_slots + E * BM)
    yg = _experts(layer, block_e, n_used, xg, w_e1, w_e3, w_e2)
    dest_km = jnp.swapaxes(dest[:, :TOP_K], 0, 1).reshape(-1)
    yt = _sc_gather(yg, dest_km).reshape(TOP_K, n, D // 2)
    return _combine(alpha, h, gate, yt, ws13, ws2, ln_g, ln_b)


def kernel(x_prompt, x_sample, state_wkv, state_shift, state_conv, cache_mem_k, cache_mem_v, mem_prompt, w_in, w_in_vmix, mu_shift, mu_vmix, w0, w_up, a0, a_up, v0, v_up, g_up, k_k, k_a, r_k, lnx_g, lnx_b, w_o_a, conv_w, conv_b, cn_g, cn_b, w_o_b, w_mem_kv, w_o_c, w_out, ln1_g, ln1_b, ln2_g, ln2_b, w_router, b_router, w_e1, w_e3, w_e2, w_s1, w_s3, w_s2):
    B, T, D = x_prompt.shape
    SB, ST, _ = x_sample.shape
    L = w_in.shape[0]
    M = mem_prompt.shape[1]
    R = ROW_TILE
    n_p, n_s = B * T, SB * ST
    n = n_p + n_s
    assert n_s == R and SB == LANES and T % R == 0 and T % CHUNK == 0 and D % LANES == 0
    npt, tps = n_p // R, T // R
    alpha = (2.0 * L) ** 0.25
    off_glu = RWKV_COLS
    off_q = off_glu + 2 * CONV_CH
    off_gate = off_q + MEM_WIDTH
    in_cols = off_gate + 3 * D

    x = jnp.concatenate([x_prompt.reshape(n_p, D), jnp.swapaxes(x_sample, 0, 1).reshape(n_s, D)], axis=0)
    mem_rows = mem_prompt.reshape(B * M, D)
    row2 = lambda a: a.reshape(1, -1)
    v_first = None
    outs = {k: [] for k in ('wkv_p', 'shift_p', 'conv_p', 'mk', 'mv', 'wkv_s', 'shift_s', 'conv_s')}

    for l in range(L):
        first = l == 0
        w_in_l = w_in[l].astype(BF16)
        w_rwkv = w_in_l[:, :off_glu]
        p_rwkv = _matmul(x, w_rwkv, R, RWKV_COLS // 2)
        p_glu = _matmul(x, w_in_l[:, off_glu:off_q], R, 2 * CONV_CH, BF16)
        p_q = _matmul(x, w_in_l[:, off_q:off_gate], R, MEM_WIDTH)
        p_gate = _matmul(x, w_in_l[:, off_gate:in_cols], R, D, BF16)
        x_prev_s = state_shift[l]
        prev_rwkv = _matmul(x_prev_s, w_rwkv, SB, RWKV_COLS // 2)

        wup_pad = jnp.concatenate([w_up[l], jnp.zeros((LORA_A, RW), F32)], axis=0).astype(BF16)
        aup_pad = jnp.concatenate([jnp.zeros((LORA_W, RW), F32), a_up[l]], axis=0).astype(BF16)
        lora = (row2(w0[l]), wup_pad, row2(a0[l]), aup_pad, g_up[l].astype(BF16), row2(k_k[l]), row2(k_a[l]),
                row2(r_k[l]), _seg_ones())
        if first:
            pre = _rwkv_pre(True, npt, tps, p_rwkv, prev_rwkv, row2(mu_shift[l]), lora)
        else:
            wv_pad = jnp.pad(w_in_vmix[l - 1], ((0, 0), (0, LANES - LORA_V))).astype(BF16)
            p_vd = _matmul(x, wv_pad, R, LANES)
            prev_vd = _matmul(x_prev_s, wv_pad, SB, LANES)
            mu_v = jnp.pad(mu_vmix[l - 1], (0, LANES - LORA_V)).reshape(1, LANES)
            vup_pad = jnp.pad(v_up[l - 1], ((0, LANES - LORA_V), (0, 0))).astype(BF16)
            pre = _rwkv_pre(False, npt, tps, p_rwkv, prev_rwkv, row2(mu_shift[l]), lora, p_vd, prev_vd, mu_v,
                            v_first, row2(v0[l - 1]), vup_pad)
        r_, lw_, k_, v_, nkk_, bb_, g_, bonus_ = pre
        if first:
            v_first = v_

        y_p, s_p = _wkv_prompt(B, T, r_, lw_, k_, v_, nkk_, bb_)

        def lanes_b(a):
            return jnp.transpose(a[n_p:].reshape(ST, SB, RW), (0, 2, 1)).reshape(ST, HEADS, HEAD_DIM, SB)

        state_sz = HEADS * HEAD_DIM * HEAD_DIM
        s0 = _transpose_layer(state_wkv.astype(F32).reshape(L, SB, state_sz), l, 2048)
        s0 = s0.reshape(HEADS, HEAD_DIM, HEAD_DIM, SB)
        y_s, s_s = _wkv_sample(s0, lanes_b(r_), lanes_b(lw_), lanes_b(k_),
                               lanes_b(v_).reshape(ST, HEADS, HEAD_DIM, 1, SB), lanes_b(nkk_), lanes_b(bb_))
        y_s = jnp.transpose(y_s.reshape(ST, RW, SB), (0, 2, 1)).reshape(n_s, RW)
        s_s = _transpose_2d(s_s.reshape(state_sz, SB), 2048).reshape(SB, HEADS, HEAD_DIM, HEAD_DIM)
        z_a = _rwkv_post(npt, y_p, y_s, bonus_, g_, row2(lnx_g[l]), row2(lnx_b[l]))

        cvec = (conv_w[l], row2(conv_b[l]), row2(cn_g[l]), row2(cn_b[l]))
        zb_p, conv_p = _conv_prompt(B, T, p_glu, *cvec)
        st_t = jnp.swapaxes(state_conv[l], 0, 1)
        zb_s, glu_s = _conv_sample(st_t, p_glu[n_p:].reshape(ST, SB, 2 * CONV_CH), *cvec)
        zb_s = zb_s.reshape(n_s, CONV_CH)
        conv_s = jnp.concatenate([state_conv[l][:, ST:], jnp.swapaxes(glu_s, 0, 1)], axis=1)

        kv_p = _matmul(mem_rows, w_mem_kv[l].astype(BF16), M, 2 * MEM_WIDTH)
        kv3 = kv_p.reshape(B, M, 2 * MEM_WIDTH)
        zc_p = _attention(p_q[:n_p].reshape(B, T, MEM_WIDTH), kv3, kv3, 0, 1, 1, R).reshape(n_p, MEM_WIDTH)
        q_s = jnp.swapaxes(p_q[n_p:].reshape(ST, SB, MEM_WIDTH), 0, 1)
        q_s = jnp.pad(q_s, ((0, 0), (0, 8 - ST), (0, 0)))
        zc_s = _attention_cached(q_s, cache_mem_k.reshape(L, SB, M * MEM_HEADS, MEM_HEAD_DIM),
                                 cache_mem_v.reshape(L, SB, M * MEM_HEADS, MEM_HEAD_DIM), l, 8)[:, :ST]
        zc_s = jnp.swapaxes(zc_s, 0, 1).reshape(n_s, MEM_WIDTH)

        h, h_pk = _merge(n_p, alpha, z_a, zb_p, zb_s, zc_p, zc_s, p_gate, x,
                         w_o_a[l].astype(BF16), w_o_b[l].astype(BF16), w_o_c[l].astype(BF16),
                         w_out[l].astype(BF16), row2(ln1_g[l]), row2(ln1_b[l]))

        ws13 = jnp.concatenate([w_s1[l], w_s3[l]], axis=1).astype(BF16)
        x_new = _moe(l, alpha, h, h_pk, w_router[l].astype(BF16), row2(b_router[l]), w_e1, w_e3, w_e2,
                     ws13, w_s2[l].astype(BF16), row2(ln2_g[l]), row2(ln2_b[l]))

        outs['wkv_p'].append(s_p.astype(state_wkv.dtype))
        outs['shift_p'].append(x[T - 1:n_p:T].astype(state_shift.dtype))
        outs['conv_p'].append(conv_p.astype(state_conv.dtype))
        outs['mk'].append(kv3[..., :MEM_WIDTH].reshape(B, M, MEM_HEADS, MEM_HEAD_DIM).astype(cache_mem_k.dtype))
        outs['mv'].append(kv3[..., MEM_WIDTH:].reshape(B, M, MEM_HEADS, MEM_HEAD_DIM).astype(cache_mem_v.dtype))
        outs['wkv_s'].append(s_s.astype(state_wkv.dtype))
        outs['shift_s'].append(x[n_p + (ST - 1) * SB:].astype(state_shift.dtype))
        outs['conv_s'].append(conv_s.astype(state_conv.dtype))
        x = x_new

    y_p = x[:n_p].reshape(B, T, D)
    y_s = jnp.swapaxes(x[n_p:].reshape(ST, SB, D), 0, 1)
    return (y_p, y_s, jnp.stack(outs['wkv_p']), jnp.stack(outs['shift_p']), jnp.stack(outs['conv_p']),
            jnp.stack(outs['mk']), jnp.stack(outs['mv']), jnp.stack(outs['wkv_s']), jnp.stack(outs['shift_s']),
            jnp.stack(outs['conv_s']))
```

```python
import functools
import math

import jax
import jax.numpy as jnp
from jax import lax
from jax.experimental import pallas as pl
from jax.experimental.pallas import tpu as pltpu
from jax.experimental.pallas import tpu_sc as plsc

F32 = jnp.float32
BF16 = jnp.bfloat16

HEADS = 8
HEAD_DIM = 64
RW = HEADS * HEAD_DIM
LORA_W = 64
LORA_A = 64
LORA_V = 32
LORA_G = 128
RWKV_COLS = 3 * RW + LORA_W + LORA_A + LORA_G
GN_EPS = HEAD_DIM * 1e-5
CONV_WIDTH = 31
CONV_BUF = CONV_WIDTH - 1
CONV_CH = 512
MEM_HEADS = 4
MEM_HEAD_DIM = 128
MEM_WIDTH = MEM_HEADS * MEM_HEAD_DIM
TOP_K = 8
ROUTED_SCALE = 2.5
LN_EPS = 1e-5
EXP_M05 = math.exp(-0.5)

LANES = 128
ROW_TILE = 512
MERGE_TILE = 256
CHUNK = 64
NEUMANN_BLOCK = 16
MOE_BLOCK = 512
WKV_BATCH_PER_STEP = 4
ROUTER_SUB = 64
PROJ_TILE = 256
SC_CORES = 2
SC_SUBCORES = 16
SC_WINDOW = 128
SC_DISPATCH_WINDOW = 48
SC_PAD_WINDOW = 64
COMBINE_TILE = 128
VMEM_LIMIT = 48 * 1024 * 1024


def _cparams(*sem, vmem=VMEM_LIMIT):
    return pltpu.CompilerParams(dimension_semantics=tuple(sem), vmem_limit_bytes=vmem)


def _bdot(a, b):
    return jnp.dot(a.astype(BF16), b.astype(BF16), preferred_element_type=F32)


def _bdot_nt(a, b):
    return lax.dot_general(a.astype(BF16), b.astype(BF16), (((1,), (1,)), ((), ())),
                           preferred_element_type=F32)


def _sigmoid(x):
    return 1.0 / (1.0 + jnp.exp(-x))


def _silu(x):
    return x * _sigmoid(x)


def _layer_norm(x, g, b):
    mu = jnp.mean(x, axis=-1, keepdims=True)
    d = x - mu
    var = jnp.mean(d * d, axis=-1, keepdims=True)
    return d * lax.rsqrt(var + LN_EPS) * g + b


SEG_LANES = 256


def _seg_ones():
    i = jnp.arange(SEG_LANES)
    return ((i[:, None] // HEAD_DIM) == (i[None, :] // HEAD_DIM)).astype(BF16)


def _seg_sum(x, seg):
    hi = x.astype(BF16)
    lo = (x - hi.astype(F32)).astype(BF16)
    parts = []
    for j in range(x.shape[-1] // SEG_LANES):
        sl = slice(j * SEG_LANES, (j + 1) * SEG_LANES)
        parts.append(jnp.dot(hi[:, sl], seg, preferred_element_type=F32)
                     + jnp.dot(lo[:, sl], seg, preferred_element_type=F32))
    return jnp.concatenate(parts, axis=-1)


def _pack_pair(lo, hi):
    lo_b = lax.bitcast_convert_type(lo.astype(BF16).astype(F32), jnp.int32)
    hi_b = lax.bitcast_convert_type(hi.astype(BF16).astype(F32), jnp.int32)
    return lax.shift_right_logical(lo_b, jnp.full(lo_b.shape, 16, jnp.int32)) | (hi_b & jnp.int32(-65536))


def _unpack_pair(p):
    lo = lax.bitcast_convert_type(lax.shift_left(p, jnp.full(p.shape, 16, jnp.int32)), F32)
    hi = lax.bitcast_convert_type(p & jnp.int32(-65536), F32)
    return lo, hi


def _mm_kernel(x_ref, w_ref, o_ref):
    o_ref[...] = _bdot(x_ref[...], w_ref[...]).astype(o_ref.dtype)


def _matmul(x, w, tm, tn, out_dtype=F32):
    m, k = x.shape
    n_cols = w.shape[1]
    assert m % tm == 0 and n_cols % tn == 0
    return pl.pallas_call(
        _mm_kernel,
        grid=(m // tm, n_cols // tn),
        in_specs=[pl.BlockSpec((tm, k), lambda i, j: (i, 0)),
                  pl.BlockSpec((k, tn), lambda i, j: (0, j))],
        out_specs=pl.BlockSpec((tm, tn), lambda i, j: (i, j)),
        out_shape=jax.ShapeDtypeStruct((m, n_cols), out_dtype),
        compiler_params=_cparams("parallel", "parallel"),
        name=f"matmul_{n_cols}",
    )(x, w)


def _proj_kernel(n_out, x_ref, *refs):
    xb = x_ref[...].astype(BF16)
    for w_ref, o_ref in zip(refs[:n_out], refs[n_out:]):
        o_ref[...] = jnp.dot(xb, w_ref[...], preferred_element_type=F32).astype(o_ref.dtype)


def _project(x, weights, out_dtypes):
    m, k = x.shape
    tm = PROJ_TILE
    assert m % tm == 0
    return pl.pallas_call(
        functools.partial(_proj_kernel, len(weights)),
        grid=(m // tm,),
        in_specs=[pl.BlockSpec((tm, k), lambda i: (i, 0))]
        + [pl.BlockSpec(w.shape, lambda i: (0, 0), pipeline_mode=pl.Buffered(1)) for w in weights],
        out_specs=[pl.BlockSpec((tm, w.shape[1]), lambda i: (i, 0)) for w in weights],
        out_shape=[jax.ShapeDtypeStruct((m, w.shape[1]), dt) for w, dt in zip(weights, out_dtypes)],
        compiler_params=_cparams("parallel"),
        name="in_proj",
    )(x, *weights)


def _pre_kernel(first, npt, tps, *refs):
    if first:
        (p_ref, halo_ref, prev_ref, mu_ref,
         w0_ref, wup_ref, a0_ref, aup_ref, gup_ref, kk_ref, ka_ref, rk_ref, seg_ref,
         r_o, lw_o, k_o, v_o, nkk_o, bb_o, g_o, bon_o, sh_ref) = refs
    else:
        (p_ref, halo_ref, prev_ref, mu_ref, pv_ref, halov_ref, prevv_ref, muv_ref, vf_ref, v0_ref, vup_ref,
         w0_ref, wup_ref, a0_ref, aup_ref, gup_ref, kk_ref, ka_ref, rk_ref, seg_ref,
         r_o, lw_o, k_o, v_o, nkk_o, bb_o, g_o, bon_o, sh_ref, shv_ref) = refs
    i = pl.program_id(0)
    rows_n = p_ref.shape[0]

    def build_shift(src_ref, halo_r, prev_r, dst_ref):
        @pl.when(i < npt)
        def _():
            x = src_ref[...]
            rolled = pltpu.roll(x, 1, axis=0)
            last = halo_r[7:8, :]
            row0 = jnp.where((i % tps) == 0, jnp.zeros_like(last), last)
            rows = lax.broadcasted_iota(jnp.int32, x.shape, 0)
            dst_ref[...] = jnp.where(rows == 0, row0, rolled)

        @pl.when(i >= npt)
        def _():
            nb = prev_r.shape[0]
            dst_ref[0:nb, :] = prev_r[...]
            dst_ref[nb:rows_n, :] = src_ref[0:rows_n - nb, :]

    build_shift(p_ref, halo_ref, prev_ref, sh_ref)
    p = p_ref[...]
    p = p + (sh_ref[...] - p) * mu_ref[...]
    r = p[:, 0:RW]
    k = p[:, RW:2 * RW]
    v = p[:, 2 * RW:3 * RW]
    u = p[:, 3 * RW:3 * RW + LORA_W + LORA_A]
    gd = p[:, 3 * RW + LORA_W + LORA_A:RWKV_COLS]

    z = w0_ref[...] + _bdot(jnp.tanh(u), wup_ref[...])
    lw = -EXP_M05 * _sigmoid(z)
    a = _sigmoid(a0_ref[...] + _bdot(u, aup_ref[...]))
    g = _bdot(_sigmoid(gd), gup_ref[...])
    if not first:
        build_shift(pv_ref, halov_ref, prevv_ref, shv_ref)
        pv = pv_ref[...]
        pv = pv + (shv_ref[...] - pv) * muv_ref[...]
        mix = _sigmoid(v0_ref[...] + _bdot(pv, vup_ref[...]))
        v = v + (vf_ref[...] - v) * mix
    kk = k * kk_ref[...]
    seg = seg_ref[...]
    nrm = jnp.sqrt(_seg_sum(kk * kk, seg))
    kkn = kk / jnp.maximum(nrm, 1e-12)
    k2 = k * (1.0 + (a - 1.0) * ka_ref[...])
    bonus = _seg_sum(r * k2 * rk_ref[...], seg) * v
    r_o[...] = r
    lw_o[...] = lw
    k_o[...] = k2
    v_o[...] = v
    nkk_o[...] = -kkn
    bb_o[...] = kkn * a
    g_o[...] = g
    bon_o[...] = bonus


def _rwkv_pre(first, npt, tps, p_rwkv, prev_rwkv, mu, lora, p_vd=None, prev_vd=None, mu_v=None, v_first=None,
              v0=None, vup=None):
    n = p_rwkv.shape[0]
    R = ROW_TILE
    nt = n // R

    def row_spec(w):
        return pl.BlockSpec((R, w), lambda i: (i, 0))

    def halo_spec(w):
        return pl.BlockSpec((8, w), lambda i: (jnp.maximum(i * (R // 8) - 1, 0), 0))

    def full(a):
        return pl.BlockSpec(a.shape, lambda i: (0,) * a.ndim)

    ins = [p_rwkv, p_rwkv, prev_rwkv, mu]
    specs = [row_spec(RWKV_COLS), halo_spec(RWKV_COLS), full(prev_rwkv), full(mu)]
    scratch = [pltpu.VMEM((R, RWKV_COLS), F32)]
    if not first:
        ins += [p_vd, p_vd, prev_vd, mu_v, v_first, v0, vup]
        specs += [row_spec(LANES), halo_spec(LANES), full(prev_vd), full(mu_v), row_spec(RW), full(v0), full(vup)]
        scratch.append(pltpu.VMEM((R, LANES), F32))
    ins += list(lora)
    specs += [full(a) for a in lora]
    outs = pl.pallas_call(
        functools.partial(_pre_kernel, first, npt, tps),
        grid=(nt,),
        in_specs=specs,
        out_specs=[row_spec(RW)] * 8,
        out_shape=[jax.ShapeDtypeStruct((n, RW), F32)] * 8,
        scratch_shapes=scratch,
        compiler_params=_cparams("parallel"),
        name="rwkv_pre",
    )(*ins)
    return outs


def _wkv_chunk_kernel(nbat, *refs):
    in_refs = refs[:6 * nbat]
    y_ref, s_out_ref, s_ref = refs[6 * nbat:]
    c = pl.program_id(1)
    C = in_refs[0].shape[0]
    G = 4 * HEAD_DIM
    assert 4 * C == G

    @pl.when(c == 0)
    def _():
        s_ref[...] = jnp.zeros_like(s_ref)

    ri = lax.broadcasted_iota(jnp.int32, (G, G), 0)
    ci = lax.broadcasted_iota(jnp.int32, (G, G), 1)
    same_head = (ri // C) == (ci // HEAD_DIM)
    tril_s = same_head & ((ci % C) < (ri % C))
    tril_i = same_head & ((ci % C) <= (ri % C))
    diag_blk = (ri // NEUMANN_BLOCK) == (ci // NEUMANN_BLOCK)
    eye = (ri == ci).astype(F32)

    def stack(x):
        return jnp.where(same_head, jnp.concatenate([x, x, x, x], axis=0), 0.0).astype(BF16)

    def tile4(x):
        return jnp.concatenate([x, x, x, x], axis=0).astype(BF16)

    def collapse(x4):
        return x4[0:C] + x4[C:2 * C] + x4[2 * C:3 * C] + x4[3 * C:4 * C]

    for b in range(nbat):
        r_ref, lw_ref, k_ref, v_ref, nkk_ref, bb_ref = in_refs[6 * b:6 * b + 6]
        lw = lw_ref[...]
        rows = lax.broadcasted_iota(jnp.int32, lw.shape, 0)
        cs = lw
        s = 1
        while s < C:
            cs = cs + jnp.where(rows >= s, pltpu.roll(cs, s, axis=0), 0.0)
            s *= 2
        cs_last = cs[C - 1:C, :]
        e_neg = jnp.exp(-cs)
        e_tail = jnp.exp(cs_last - cs)
        kk_ = k_ref[...]
        bb_ = bb_ref[...]
        at = nkk_ref[...] * jnp.exp(cs - lw)
        rt = r_ref[...] * jnp.exp(cs)
        bt = bb_ * e_neg
        kt = kk_ * e_neg
        bw = bb_ * e_tail
        kw = kk_ * e_tail
        vv = v_ref[...]
        w_end = jnp.exp(cs_last)

        for g in range(RW // G):
            sl = slice(g * G, (g + 1) * G)
            at4, rt4, v4 = stack(at[:, sl]), stack(rt[:, sl]), stack(vv[:, sl])
            bt4, kt4 = tile4(bt[:, sl]), tile4(kt[:, sl])
            a_ab = jnp.where(tril_s, _bdot_nt(at4, bt4), 0.0)
            a_ak = jnp.where(tril_s, _bdot_nt(at4, kt4), 0.0).astype(BF16)
            a_rb = jnp.where(tril_i, _bdot_nt(rt4, bt4), 0.0).astype(BF16)
            a_rk = jnp.where(tril_i, _bdot_nt(rt4, kt4), 0.0).astype(BF16)
            a_d = jnp.where(diag_blk, a_ab, 0.0)
            a_o = a_ab - a_d
            a_db = a_d.astype(BF16)
            x2 = _bdot(a_db, a_db).astype(BF16)
            x4 = _bdot(x2, x2).astype(BF16)
            x8 = _bdot(x4, x4)
            t_d = eye + a_d
            t_d = t_d + _bdot(t_d, x2)
            t_d = t_d + _bdot(t_d, x4)
            t_d = t_d + _bdot(t_d, x8)
            t_db = t_d.astype(BF16)
            nn = _bdot(t_db, a_o).astype(BF16)
            n2 = _bdot(nn, nn)
            m1 = t_d + _bdot(nn, t_db)
            t_m = m1 + _bdot(n2, m1)
            s4 = s_ref[b, g]
            s4b = s4.astype(BF16)
            u4 = _bdot(t_m, _bdot_nt(at4, s4b) + _bdot(a_ak, v4))
            y4 = _bdot_nt(rt4, s4b) + _bdot(jnp.concatenate([a_rb, a_rk], axis=1),
                                            jnp.concatenate([u4.astype(BF16), v4], axis=0))
            y_ref[b, :, sl] = collapse(y4)
            uv_t = jnp.concatenate([collapse(u4), vv[:, sl]], axis=0).T
            upd = _bdot(uv_t, jnp.concatenate([bw[:, sl], kw[:, sl]], axis=0))
            s_ref[b, g] = s4 * w_end[:, sl] + jnp.where((ri // HEAD_DIM) == (ci // HEAD_DIM), upd, 0.0)

    @pl.when(c == pl.num_programs(1) - 1)
    def _():
        for b in range(nbat):
            for hd in range(HEADS):
                g, j = divmod(hd, 4)
                blk = slice(j * HEAD_DIM, (j + 1) * HEAD_DIM)
                s_out_ref[b, hd] = s_ref[b, g, blk, blk]


def _wkv_prompt(B, T, r, lw, k, v, nkk, bb):
    C = CHUNK
    nc = T // C
    G = 4 * HEAD_DIM
    nbat = math.gcd(B, WKV_BATCH_PER_STEP)
    specs = []
    for j in range(nbat):
        specs += [pl.BlockSpec((C, RW), lambda i, c, j=j: ((nbat * i + j) * nc + c, 0))] * 6
    y, s = pl.pallas_call(
        functools.partial(_wkv_chunk_kernel, nbat),
        grid=(B // nbat, nc),
        in_specs=specs,
        out_specs=[pl.BlockSpec((nbat, C, RW), lambda i, c: (i, c, 0)),
                   pl.BlockSpec((nbat, HEADS, HEAD_DIM, HEAD_DIM), lambda i, c: (i, 0, 0, 0))],
        out_shape=[jax.ShapeDtypeStruct((B, T, RW), F32),
                   jax.ShapeDtypeStruct((B, HEADS, HEAD_DIM, HEAD_DIM), F32)],
        scratch_shapes=[pltpu.VMEM((nbat, RW // G, G, G), F32)],
        compiler_params=_cparams("parallel", "arbitrary"),
        name="wkv_chunk",
    )(*([r, lw, k, v, nkk, bb] * nbat))
    return y.reshape(B * T, RW), s


def _wkv_seq_kernel(s0_ref, r_ref, lw_ref, k_ref, v_ref, nkk_ref, bb_ref, y_ref, s_ref):
    steps = r_ref.shape[0]

    def body(vi, carry):
        st = s0_ref[0, vi]
        for t in range(steps):
            sa = jnp.sum(st * nkk_ref[t, 0], axis=0, keepdims=True)
            st = st * jnp.exp(lw_ref[t, 0]) + sa * bb_ref[t, 0] + v_ref[t, 0, vi] * k_ref[t, 0]
            y_ref[t, 0, vi] = jnp.sum(st * r_ref[t, 0], axis=0, keepdims=True)
        s_ref[0, vi] = st
        return carry

    lax.fori_loop(0, HEAD_DIM, body, 0)


def _wkv_sample(s0, r, lw, k, v, nkk, bb):
    steps, _, _, nb = r.shape
    vec = pl.BlockSpec((steps, 1, HEAD_DIM, nb), lambda h: (0, h, 0, 0))
    vcol = pl.BlockSpec((steps, 1, HEAD_DIM, 1, nb), lambda h: (0, h, 0, 0, 0))
    st = pl.BlockSpec((1, HEAD_DIM, HEAD_DIM, nb), lambda h: (h, 0, 0, 0))
    y, s = pl.pallas_call(
        _wkv_seq_kernel,
        grid=(HEADS,),
        in_specs=[st, vec, vec, vec, vcol, vec, vec],
        out_specs=[vcol, st],
        out_shape=[jax.ShapeDtypeStruct((steps, HEADS, HEAD_DIM, 1, nb), F32),
                   jax.ShapeDtypeStruct(s0.shape, F32)],
        compiler_params=_cparams("parallel"),
        name="wkv_seq",
    )(s0, r, lw, k, v, nkk, bb)
    return y, s


def _tr_kernel(x_ref, o_ref):
    o_ref[...] = x_ref[...].T


def _transpose_layer(x3, layer, tc):
    _, a, c = x3.shape
    return pl.pallas_call(
        _tr_kernel,
        grid=(c // tc,),
        in_specs=[pl.BlockSpec((None, a, tc), lambda j: (layer, 0, j))],
        out_specs=pl.BlockSpec((tc, a), lambda j: (j, 0)),
        out_shape=jax.ShapeDtypeStruct((c, a), x3.dtype),
        compiler_params=_cparams("parallel"),
        name="transpose_in",
    )(x3)


def _transpose_2d(x, tc):
    c, a = x.shape
    return pl.pallas_call(
        _tr_kernel,
        grid=(c // tc,),
        in_specs=[pl.BlockSpec((tc, a), lambda j: (j, 0))],
        out_specs=pl.BlockSpec((a, tc), lambda j: (0, j)),
        out_shape=jax.ShapeDtypeStruct((a, c), x.dtype),
        compiler_params=_cparams("parallel"),
        name="transpose_out",
    )(x)


def _post_kernel(npt, yp_ref, ys_ref, bon_ref, g_ref, lg_ref, lb_ref, seg_ref, o_ref):
    i = pl.program_id(0)
    y = jnp.where(i < npt, yp_ref[...], ys_ref[...])
    seg = seg_ref[...]
    mu = _seg_sum(y, seg) * (1.0 / HEAD_DIM)
    d = y - mu
    var = _seg_sum(d * d, seg) * (1.0 / HEAD_DIM)
    yn = d * lax.rsqrt(var + GN_EPS) * lg_ref[...] + lb_ref[...]
    o_ref[...] = (yn + bon_ref[...]) * g_ref[...]


def _rwkv_post(npt, y_p, y_s, bonus, g, lnx_g, lnx_b):
    n = bonus.shape[0]
    R = ROW_TILE
    row = pl.BlockSpec((R, RW), lambda i: (i, 0))
    vec = pl.BlockSpec((1, RW), lambda i: (0, 0))
    return pl.pallas_call(
        functools.partial(_post_kernel, npt),
        grid=(n // R,),
        in_specs=[pl.BlockSpec((R, RW), lambda i: (jnp.minimum(i, npt - 1), 0)),
                  pl.BlockSpec((R, RW), lambda i: (0, 0)), row, row, vec, vec,
                  pl.BlockSpec((SEG_LANES, SEG_LANES), lambda i: (0, 0))],
        out_specs=row,
        out_shape=jax.ShapeDtypeStruct((n, RW), F32),
        compiler_params=_cparams("parallel"),
        name="rwkv_post",
    )(y_p, y_s, bonus, g, lnx_g, lnx_b, _seg_ones())


def _glu(p):
    p = p.astype(F32)
    return p[..., :CONV_CH] * _sigmoid(p[..., CONV_CH:])


def _conv_prompt_kernel(p_ref, halo_ref, w_ref, cb_ref, g_ref, b_ref, o_ref, tail_ref, ext_ref, sh_ref):
    j = pl.program_id(1)
    R = p_ref.shape[0]
    H = halo_ref.shape[0]
    glu = _glu(p_ref[...])
    halo = _glu(halo_ref[...])
    ext_ref[0:H, :] = jnp.where(j == 0, jnp.zeros_like(halo), halo)
    ext_ref[H:H + R, :] = glu
    span = sh_ref.shape[1]
    for ph in range(1, 8):
        sh_ref[ph] = ext_ref[ph:ph + span, :]
    acc = jnp.zeros((R, CONV_CH), F32)
    for t in range(CONV_WIDTH):
        off = H - CONV_BUF + t
        q8 = (off // 8) * 8
        tap = ext_ref[q8:q8 + R, :] if off % 8 == 0 else sh_ref[off % 8, q8:q8 + R, :]
        acc = acc + tap * w_ref[t:t + 1, :]
    h = _layer_norm(acc + cb_ref[...], g_ref[...], b_ref[...])
    o_ref[...] = _silu(h)

    @pl.when(j == pl.num_programs(1) - 1)
    def _():
        tail_ref[0] = glu[R - H:R, :]


def _conv_prompt(B, T, p_glu, conv_w, conv_b, cn_g, cn_b):
    R = ROW_TILE
    H = 32
    tps = T // R
    vec = pl.BlockSpec((1, CONV_CH), lambda b, j: (0, 0))
    z, tail = pl.pallas_call(
        _conv_prompt_kernel,
        grid=(B, tps),
        in_specs=[pl.BlockSpec((R, 2 * CONV_CH), lambda b, j: (b * tps + j, 0)),
                  pl.BlockSpec((H, 2 * CONV_CH), lambda b, j: (jnp.maximum((b * tps + j) * (R // H) - 1, 0), 0)),
                  pl.BlockSpec((CONV_WIDTH, CONV_CH), lambda b, j: (0, 0)), vec, vec, vec],
        out_specs=[pl.BlockSpec((R, CONV_CH), lambda b, j: (b * tps + j, 0)),
                   pl.BlockSpec((1, H, CONV_CH), lambda b, j: (b, 0, 0))],
        out_shape=[jax.ShapeDtypeStruct((B * T, CONV_CH), F32),
                   jax.ShapeDtypeStruct((B, H, CONV_CH), F32)],
        scratch_shapes=[pltpu.VMEM((H + R, CONV_CH), F32), pltpu.VMEM((8, H + R - 8, CONV_CH), F32)],
        compiler_params=_cparams("parallel", "arbitrary"),
        name="conv_prompt",
    )(p_glu, p_glu, conv_w, conv_b, cn_g, cn_b)
    return z, tail[:, H - CONV_BUF:, :]


def _conv_sample_kernel(st_ref, p_ref, w_ref, cb_ref, g_ref, b_ref, o_ref, glu_ref):
    steps = p_ref.shape[0]
    glu = [_glu(p_ref[t]) for t in range(steps)]
    for t in range(steps):
        glu_ref[t] = glu[t]
    for t in range(steps):
        acc = jnp.zeros(glu[0].shape, F32)
        for j in range(CONV_WIDTH):
            src = t + j
            x = st_ref[src] if src < CONV_BUF else glu[src - CONV_BUF]
            acc = acc + x * w_ref[j:j + 1, :]
        h = _layer_norm(acc + cb_ref[...], g_ref[...], b_ref[...])
        o_ref[t] = _silu(h)


def _conv_sample(state_t, p_glu, conv_w, conv_b, cn_g, cn_b):
    steps, nb, _ = p_glu.shape

    def full(a):
        return pl.BlockSpec(a.shape, lambda i: (0,) * a.ndim)

    ins = (state_t, p_glu, conv_w, conv_b, cn_g, cn_b)
    return pl.pallas_call(
        _conv_sample_kernel,
        grid=(1,),
        in_specs=[full(a) for a in ins],
        out_specs=[pl.BlockSpec((steps, nb, CONV_CH), lambda i: (0, 0, 0))] * 2,
        out_shape=[jax.ShapeDtypeStruct((steps, nb, CONV_CH), F32)] * 2,
        compiler_params=_cparams("arbitrary"),
        name="conv_sample",
    )(*ins)


def _attn_kernel(q_ref, k_ref, v_ref, o_ref):
    scale = MEM_HEAD_DIM ** -0.5
    for h in range(MEM_HEADS):
        sl = slice(h * MEM_HEAD_DIM, (h + 1) * MEM_HEAD_DIM)
        s = _bdot_nt(q_ref[:, sl], k_ref[:, sl]) * scale
        m = jnp.max(s, axis=-1, keepdims=True)
        e = jnp.exp(s - m)
        pr = e / jnp.sum(e, axis=-1, keepdims=True)
        o_ref[:, sl] = _bdot(pr, v_ref[:, sl])


def _attn_rows_kernel(q_ref, k_ref, v_ref, o_ref):
    nb, tq, width = q_ref.shape
    n_mem = k_ref.shape[1] // MEM_HEADS
    scale = MEM_HEAD_DIM ** -0.5
    ri = lax.broadcasted_iota(jnp.int32, (MEM_HEADS * tq, width), 0)
    ci = lax.broadcasted_iota(jnp.int32, (MEM_HEADS * tq, width), 1)
    own = (ri // tq) == (ci // MEM_HEAD_DIM)
    for bi in range(nb):
        k_all = jnp.concatenate([k_ref[bi, pl.ds(h, n_mem, stride=MEM_HEADS), :].astype(BF16)
                                 for h in range(MEM_HEADS)], axis=1)
        v_all = jnp.concatenate([v_ref[bi, pl.ds(h, n_mem, stride=MEM_HEADS), :].astype(BF16)
                                 for h in range(MEM_HEADS)], axis=1)
        q = q_ref[bi]
        q4 = jnp.where(own, jnp.concatenate([q] * MEM_HEADS, axis=0), 0.0)
        s = _bdot_nt(q4, k_all) * scale
        m = jnp.max(s, axis=-1, keepdims=True)
        e = jnp.exp(s - m)
        pr = e / jnp.sum(e, axis=-1, keepdims=True)
        pv = jnp.where(own, _bdot(pr, v_all), 0.0)
        o = pv[0:tq]
        for h in range(1, MEM_HEADS):
            o = o + pv[h * tq:(h + 1) * tq]
        o_ref[bi] = o


def _attention_cached(q, cache_k, cache_v, layer, bb):
    B, T, W = q.shape
    rows = cache_k.shape[2]
    kv_spec = pl.BlockSpec((None, bb, rows, MEM_HEAD_DIM), lambda b: (layer, b, 0, 0))
    return pl.pallas_call(
        _attn_rows_kernel,
        grid=(B // bb,),
        in_specs=[pl.BlockSpec((bb, T, W), lambda b: (b, 0, 0)), kv_spec, kv_spec],
        out_specs=pl.BlockSpec((bb, T, W), lambda b: (b, 0, 0)),
        out_shape=jax.ShapeDtypeStruct((B, T, W), F32),
        compiler_params=_cparams("parallel"),
        name="mem_attention_cached",
    )(q, cache_k, cache_v)


def _attention_prompt(q_rows, kv, B, T, tq):
    W = q_rows.shape[1]
    M = kv.shape[1]
    tps = T // tq
    return pl.pallas_call(
        _attn_kernel,
        grid=(B, tps),
        in_specs=[pl.BlockSpec((tq, W), lambda b, j: (b * tps + j, 0)),
                  pl.BlockSpec((None, M, W), lambda b, j: (b, 0, 0)),
                  pl.BlockSpec((None, M, W), lambda b, j: (b, 0, 1))],
        out_specs=pl.BlockSpec((tq, W), lambda b, j: (b * tps + j, 0)),
        out_shape=jax.ShapeDtypeStruct((B * T, W), F32),
        compiler_params=_cparams("parallel", "parallel"),
        name="mem_attention_prompt",
    )(q_rows, kv, kv)


def _merge_kernel(npt, alpha, za_ref, zbp_ref, zbs_ref, zcp_ref, zcs_ref, gate_ref, x_ref,
                  woa_ref, wob_ref, woc_ref, wout_ref, g_ref, b_ref, o_ref, pk_ref):
    i = pl.program_id(0)
    D = x_ref.shape[1]
    is_p = i < npt
    zb = jnp.where(is_p, zbp_ref[...], zbs_ref[...])
    zc = jnp.where(is_p, zcp_ref[...], zcs_ref[...])
    merged = (_sigmoid(gate_ref[:, 0:D].astype(F32)) * _bdot(za_ref[...], woa_ref[...])
              + _sigmoid(gate_ref[:, D:2 * D].astype(F32)) * _bdot(zb, wob_ref[...])
              + _sigmoid(gate_ref[:, 2 * D:3 * D].astype(F32)) * _bdot(zc, woc_ref[...]))
    h = _layer_norm(alpha * x_ref[...] + _bdot(merged, wout_ref[...]), g_ref[...], b_ref[...])
    o_ref[...] = h
    pk_ref[...] = _pack_pair(h[:, 0:D // 2], h[:, D // 2:D])


def _merge(n_p, alpha, za, zb_p, zb_s, zc_p, zc_s, gates, x, woa, wob, woc, wout, ln_g, ln_b):
    n, D = x.shape
    R = MERGE_TILE
    npt = n_p // R

    def row(w):
        return pl.BlockSpec((R, w), lambda i: (i, 0))

    def prow(w):
        return pl.BlockSpec((R, w), lambda i: (jnp.minimum(i, npt - 1), 0))

    def srow(w):
        return pl.BlockSpec((R, w), lambda i: (jnp.maximum(i - npt, 0), 0))

    def full(a):
        return pl.BlockSpec(a.shape, lambda i: (0,) * a.ndim)

    return pl.pallas_call(
        functools.partial(_merge_kernel, npt, alpha),
        grid=(n // R,),
        in_specs=[row(RW), prow(CONV_CH), srow(CONV_CH), prow(MEM_WIDTH), srow(MEM_WIDTH), row(3 * D), row(D),
                  full(woa), full(wob), full(woc), full(wout), full(ln_g), full(ln_b)],
        out_specs=[row(D), row(D // 2)],
        out_shape=[jax.ShapeDtypeStruct((n, D), F32), jax.ShapeDtypeStruct((n, D // 2), jnp.int32)],
        compiler_params=_cparams("parallel"),
        name="branch_merge",
    )(za, zb_p, zb_s, zc_p, zc_s, gates, x, woa, wob, woc, wout, ln_g, ln_b)


def _router_kernel(h_ref, wr_ref, br_ref, idx_o, gate_o, rank_o, cnt_o, carry_ref):
    i = pl.program_id(0)
    R = h_ref.shape[0]
    E = wr_ref.shape[1]

    @pl.when(i == 0)
    def _():
        carry_ref[...] = jnp.zeros_like(carry_ref)

    scores_all = _sigmoid(_bdot(h_ref[...], wr_ref[...]))
    S = ROUTER_SUB
    lane = lax.broadcasted_iota(jnp.int32, (S, E), 1).astype(F32)

    def top_k(scores):
        work = scores + br_ref[...]
        sel_mask = jnp.zeros((S, E), F32)
        idx_cols, sel_cols = [], []
        for _ in range(TOP_K):
            m = jnp.max(work, axis=-1, keepdims=True)
            first = jnp.min(jnp.where(work == m, lane, float(E)), axis=-1, keepdims=True)
            onehot = lane == first
            sel_cols.append(jnp.sum(jnp.where(onehot, scores, 0.0), axis=-1, keepdims=True))
            idx_cols.append(first)
            sel_mask = jnp.where(onehot, 1.0, sel_mask)
            work = jnp.where(onehot, -jnp.inf, work)
        return idx_cols, sel_cols, sel_mask

    picks = [top_k(scores_all[s * S:(s + 1) * S]) for s in range(R // S)]
    sel_mask = jnp.concatenate([p[2] for p in picks], axis=0)
    ri = lax.broadcasted_iota(jnp.int32, (R, R), 0)
    ci = lax.broadcasted_iota(jnp.int32, (R, R), 1)
    before = jnp.dot((ci < ri).astype(BF16), sel_mask.astype(BF16), preferred_element_type=F32) + carry_ref[0:1, :]
    for s, (idx_cols, sel_cols, _) in enumerate(picks):
        rows = slice(s * S, (s + 1) * S)
        total = sel_cols[0]
        for c in sel_cols[1:]:
            total = total + c
        idx_out = jnp.zeros((S, E), F32)
        gate_out = jnp.zeros((S, E), F32)
        rank_out = jnp.zeros((S, E), F32)
        for kk in range(TOP_K):
            onehot = lane == idx_cols[kk]
            rank = jnp.sum(jnp.where(onehot, before[rows], 0.0), axis=-1, keepdims=True)
            col = lane == float(kk)
            idx_out = jnp.where(col, idx_cols[kk], idx_out)
            gate_out = jnp.where(col, sel_cols[kk] / total * ROUTED_SCALE, gate_out)
            rank_out = jnp.where(col, rank, rank_out)
        idx_o[rows, :] = idx_out
        gate_o[rows, :] = gate_out
        rank_o[rows, :] = rank_out
    carry_ref[0:1, :] = carry_ref[0:1, :] + jnp.sum(sel_mask, axis=0, keepdims=True)
    cnt_o[...] = carry_ref[...]


def _router(h, w_router, b_router):
    n, D = h.shape
    E = w_router.shape[1]
    assert E == LANES
    R = 256
    row = pl.BlockSpec((R, E), lambda i: (i, 0))
    return pl.pallas_call(
        _router_kernel,
        grid=(n // R,),
        in_specs=[pl.BlockSpec((R, D), lambda i: (i, 0)),
                  pl.BlockSpec((D, E), lambda i: (0, 0)),
                  pl.BlockSpec((1, E), lambda i: (0, 0))],
        out_specs=[row, row, row, pl.BlockSpec((8, E), lambda i: (0, 0))],
        out_shape=[jax.ShapeDtypeStruct((n, E), F32), jax.ShapeDtypeStruct((n, E), F32),
                   jax.ShapeDtypeStruct((n, E), F32), jax.ShapeDtypeStruct((8, E), F32)],
        scratch_shapes=[pltpu.VMEM((8, E), F32)],
        compiler_params=_cparams("arbitrary"),
        name="moe_router",
    )(h, w_router, b_router)


def _dest_kernel(idx_ref, rank_ref, start_ref, o_ref):
    R, E = idx_ref.shape
    lane = lax.broadcasted_iota(jnp.int32, (R, E), 1).astype(F32)
    idx = idx_ref[...]
    out = jnp.zeros((R, E), F32)
    start = start_ref[...]
    for kk in range(TOP_K):
        e_k = jnp.sum(jnp.where(lane == float(kk), idx, 0.0), axis=-1, keepdims=True)
        s_k = jnp.sum(jnp.where(lane == e_k, start, 0.0), axis=-1, keepdims=True)
        out = jnp.where(lane == float(kk), s_k, out)
    o_ref[...] = (out + rank_ref[...]).astype(jnp.int32)


def _dest(idx, rank, start):
    n, E = idx.shape
    R = 256
    row = pl.BlockSpec((R, E), lambda i: (i, 0))
    return pl.pallas_call(
        _dest_kernel,
        grid=(n // R,),
        in_specs=[row, row, pl.BlockSpec((1, E), lambda i: (0, 0))],
        out_specs=row,
        out_shape=jax.ShapeDtypeStruct((n, E), jnp.int32),
        compiler_params=_cparams("parallel"),
        name="moe_dest",
    )(idx, rank, start)


def _sc_dispatch(h_pk, dest, pad_slots, n_rows):
    n, width = h_pk.shape
    workers = SC_CORES * SC_SUBCORES
    per_w = n // workers
    win = SC_DISPATCH_WINDOW
    assert per_w * workers == n and per_w % win == 0
    n_chunk = per_w // win
    dest_w = jnp.transpose(dest.reshape(workers, n_chunk, win, TOP_K), (0, 1, 3, 2))
    pad_per_w = pad_slots.size // workers
    pwin = SC_PAD_WINDOW
    n_pchunk = pad_per_w // (TOP_K * pwin)
    assert n_pchunk * TOP_K * pwin * workers == pad_slots.size
    pad_w = pad_slots.reshape(workers, n_pchunk, TOP_K, pwin)
    zero_rows = jnp.zeros((pwin, width), h_pk.dtype)
    mesh = plsc.VectorSubcoreMesh(core_axis_name="c", subcore_axis_name="s",
                                  num_cores=SC_CORES, num_subcores=SC_SUBCORES)

    @functools.partial(
        pl.kernel, mesh=mesh,
        out_type=jax.ShapeDtypeStruct((n_rows, width), h_pk.dtype),
        scratch_types=[pltpu.VMEM((TOP_K, win), jnp.int32),
                       pltpu.VMEM((win, width), h_pk.dtype),
                       pltpu.VMEM((TOP_K, pwin), jnp.int32),
                       pltpu.VMEM((pwin, width), h_pk.dtype),
                       pltpu.SemaphoreType.DMA],
        name="moe_sc_dispatch",
    )
    def dispatch(h_hbm, dest_hbm, pad_hbm, zero_hbm, xg_hbm, idx_v, rows_v, pidx_v, zeros_v, sem):
        wid = lax.axis_index("s") * SC_CORES + lax.axis_index("c")
        pltpu.sync_copy(zero_hbm, zeros_v)

        @pl.loop(0, n_chunk)
        def _(j):
            base = pl.multiple_of(wid * per_w + j * win, 8)
            pltpu.sync_copy(h_hbm.at[pl.ds(base, win)], rows_v)
            pltpu.sync_copy(dest_hbm.at[wid, j], idx_v)
            copies = [pltpu.async_copy(rows_v, xg_hbm.at[idx_v.at[kk]], sem) for kk in range(TOP_K)]
            for cp in copies:
                cp.wait()

        @pl.loop(0, n_pchunk)
        def _(j):
            pltpu.sync_copy(pad_hbm.at[wid, j], pidx_v)
            copies = [pltpu.async_copy(zeros_v, xg_hbm.at[pidx_v.at[kk]], sem) for kk in range(TOP_K)]
            for cp in copies:
                cp.wait()

    return dispatch(h_pk, dest_w, pad_w, zero_rows)


def _expert_kernel(be_ref, nu_ref, x_ref, w1_ref, w3_ref, w2_ref, o_ref, w13_s, w2_s):
    i = pl.program_id(0)
    F = w1_ref.shape[1]

    @pl.when(i < nu_ref[0])
    def _():
        prev = be_ref[jnp.maximum(i - 1, 0)]

        @pl.when((i == 0) | (be_ref[i] != prev))
        def _():
            w13_s[:, 0:F] = w1_ref[...].astype(BF16)
            w13_s[:, F:2 * F] = w3_ref[...].astype(BF16)
            w2_s[...] = w2_ref[...].astype(BF16)

        x_lo, x_hi = _unpack_pair(x_ref[...])
        half = x_ref.shape[1]
        h13 = (jnp.dot(x_lo.astype(BF16), w13_s[0:half, :], preferred_element_type=F32)
               + jnp.dot(x_hi.astype(BF16), w13_s[half:2 * half, :], preferred_element_type=F32))
        hh = _silu(h13[:, 0:F]) * h13[:, F:2 * F]
        y = jnp.dot(hh.astype(BF16), w2_s[...], preferred_element_type=F32)
        o_ref[...] = _pack_pair(y[:, 0:half], y[:, half:2 * half])


def _experts(layer, block_e, n_used, xg, w_e1, w_e3, w_e2):
    n_slots, half = xg.shape
    D = 2 * half
    F = w_e1.shape[3]
    BM = MOE_BLOCK
    nb = block_e.shape[0]
    n_slots = nb * BM

    def blk(i, be, nu):
        return (jnp.minimum(i, nu[0] - 1), 0)

    grid_spec = pltpu.PrefetchScalarGridSpec(
        num_scalar_prefetch=2,
        grid=(nb,),
        in_specs=[pl.BlockSpec((BM, half), blk),
                  pl.BlockSpec((None, None, D, F), lambda i, be, nu: (layer, be[i], 0, 0)),
                  pl.BlockSpec((None, None, D, F), lambda i, be, nu: (layer, be[i], 0, 0)),
                  pl.BlockSpec((None, None, F, D), lambda i, be, nu: (layer, be[i], 0, 0))],
        out_specs=pl.BlockSpec((BM, half), blk),
        scratch_shapes=[pltpu.VMEM((D, 2 * F), BF16), pltpu.VMEM((F, D), BF16)],
    )
    return pl.pallas_call(
        _expert_kernel,
        grid_spec=grid_spec,
        out_shape=jax.ShapeDtypeStruct((n_slots, half), jnp.int32),
        compiler_params=_cparams("arbitrary"),
        name="moe_experts",
    )(block_e, n_used, xg, w_e1, w_e3, w_e2)


def _sc_gather(table, idx):
    n_idx = idx.shape[0]
    width = table.shape[1]
    workers = SC_CORES * SC_SUBCORES
    per_w = n_idx // workers
    assert per_w * workers == n_idx and per_w % SC_WINDOW == 0
    mesh = plsc.VectorSubcoreMesh(core_axis_name="c", subcore_axis_name="s",
                                  num_cores=SC_CORES, num_subcores=SC_SUBCORES)

    @functools.partial(
        pl.kernel, mesh=mesh,
        out_type=jax.ShapeDtypeStruct((n_idx, width), table.dtype),
        scratch_types=[pltpu.VMEM((SC_WINDOW,), jnp.int32),
                       pltpu.VMEM((SC_WINDOW, width), table.dtype),
                       pltpu.SemaphoreType.DMA],
        name="moe_sc_gather",
    )
    def gather(table_hbm, idx_hbm, out_hbm, idx_v, rows_v, sem):
        wid = lax.axis_index("s") * SC_CORES + lax.axis_index("c")

        @pl.loop(0, per_w // SC_WINDOW)
        def _(j):
            base = pl.multiple_of(wid * per_w + j * SC_WINDOW, SC_WINDOW)
            pltpu.sync_copy(idx_hbm.at[pl.ds(base, SC_WINDOW)], idx_v)
            pltpu.async_copy(table_hbm.at[idx_v], rows_v, sem).wait()
            pltpu.sync_copy(rows_v, out_hbm.at[pl.ds(base, SC_WINDOW)])

    return gather(table, idx)


def _combine_kernel(alpha, h_ref, gate_ref, yt_ref, ws13_ref, ws2_ref, g_ref, b_ref, o_ref):
    F = ws2_ref.shape[0]
    half = h_ref.shape[1] // 2
    h = h_ref[...]
    h13 = _bdot(h, ws13_ref[...])
    shared = _bdot(_silu(h13[:, 0:F]) * h13[:, F:2 * F], ws2_ref[...])
    r_lo = jnp.zeros((h.shape[0], half), F32)
    r_hi = jnp.zeros((h.shape[0], half), F32)
    for kk in range(TOP_K):
        y_lo, y_hi = _unpack_pair(yt_ref[kk])
        gk = gate_ref[:, kk:kk + 1]
        r_lo = r_lo + y_lo * gk
        r_hi = r_hi + y_hi * gk
    routed = jnp.concatenate([r_lo, r_hi], axis=-1)
    o_ref[...] = _layer_norm(alpha * h + (shared + routed), g_ref[...], b_ref[...])


def _combine(alpha, h, gate, yt, ws13, ws2, ln_g, ln_b):
    n, D = h.shape
    R = COMBINE_TILE

    def full(a):
        return pl.BlockSpec(a.shape, lambda i: (0,) * a.ndim)

    return pl.pallas_call(
        functools.partial(_combine_kernel, alpha),
        grid=(n // R,),
        in_specs=[pl.BlockSpec((R, D), lambda i: (i, 0)),
                  pl.BlockSpec((R, LANES), lambda i: (i, 0)),
                  pl.BlockSpec((TOP_K, R, D // 2), lambda i: (0, i, 0)),
                  full(ws13), full(ws2), full(ln_g), full(ln_b)],
        out_specs=pl.BlockSpec((R, D), lambda i: (i, 0)),
        out_shape=jax.ShapeDtypeStruct((n, D), F32),
        compiler_params=_cparams("parallel"),
        name="moe_combine",
    )(h, gate, yt, ws13, ws2, ln_g, ln_b)


def _moe(layer, alpha, h, h_pk, w_router, b_router, w_e1, w_e3, w_e2, ws13, ws2, ln_g, ln_b):
    n, D = h.shape
    E = w_router.shape[1]
    BM = MOE_BLOCK
    idx, gate, rank, cnt = _router(h, w_router, b_router)
    counts = cnt[0].astype(jnp.int32)
    padded = (counts + BM - 1) // BM * BM
    end = jnp.cumsum(padded)
    start = (end - padded).astype(jnp.int32)
    n_blocks = (n * TOP_K + BM - 1) // BM + E
    first_slot = jnp.arange(n_blocks, dtype=jnp.int32)[:, None] * BM
    block_e = jnp.minimum(jnp.sum((end[None, :] <= first_slot).astype(jnp.int32), axis=1), E - 1)
    n_used = (end[-1] // BM).astype(jnp.int32).reshape(1)
    dest = _dest(idx, rank, start.astype(F32).reshape(1, E))
    n_slots = n_blocks * BM
    e_i = jnp.arange(E, dtype=jnp.int32)[:, None]
    j_i = jnp.arange(BM, dtype=jnp.int32)[None, :]
    pad = (start + counts)[:, None] + j_i
    pad_slots = jnp.where(pad < end.astype(jnp.int32)[:, None], pad, n_slots + e_i * BM + j_i)
    xg = _sc_dispatch(h_pk, dest[:, :TOP_K], pad_slots, n_slots + E * BM)
    yg = _experts(layer, block_e, n_used, xg, w_e1, w_e3, w_e2)
    dest_km = jnp.swapaxes(dest[:, :TOP_K], 0, 1).reshape(-1)
    yt = _sc_gather(yg, dest_km).reshape(TOP_K, n, D // 2)
    return _combine(alpha, h, gate, yt, ws13, ws2, ln_g, ln_b)


def kernel(x_prompt, x_sample, state_wkv, state_shift, state_conv, cache_mem_k, cache_mem_v, mem_prompt, w_in, w_in_vmix, mu_shift, mu_vmix, w0, w_up, a0, a_up, v0, v_up, g_up, k_k, k_a, r_k, lnx_g, lnx_b, w_o_a, conv_w, conv_b, cn_g, cn_b, w_o_b, w_mem_kv, w_o_c, w_out, ln1_g, ln1_b, ln2_g, ln2_b, w_router, b_router, w_e1, w_e3, w_e2, w_s1, w_s3, w_s2):
    B, T, D = x_prompt.shape
    SB, ST, _ = x_sample.shape
    L = w_in.shape[0]
    M = mem_prompt.shape[1]
    R = ROW_TILE
    n_p, n_s = B * T, SB * ST
    n = n_p + n_s
    assert n_s == R and SB == LANES and T % R == 0 and T % CHUNK == 0 and D % LANES == 0
    npt, tps = n_p // R, T // R
    alpha = (2.0 * L) ** 0.25
    off_glu = RWKV_COLS
    off_q = off_glu + 2 * CONV_CH
    off_gate = off_q + MEM_WIDTH
    in_cols = off_gate + 3 * D

    x = jnp.concatenate([x_prompt.reshape(n_p, D), jnp.swapaxes(x_sample, 0, 1).reshape(n_s, D)], axis=0)
    mem_rows = mem_prompt.reshape(B * M, D)
    row2 = lambda a: a.reshape(1, -1)
    v_first = None
    outs = {k: [] for k in ('wkv_p', 'shift_p', 'conv_p', 'mk', 'mv', 'wkv_s', 'shift_s', 'conv_s')}

    for l in range(L):
        first = l == 0
        w_in_l = w_in[l].astype(BF16)
        w_rwkv = w_in_l[:, :off_glu]
        groups = [w_rwkv, w_in_l[:, off_glu:off_q], w_in_l[:, off_q:off_gate], w_in_l[:, off_gate:in_cols]]
        group_dtypes = [F32, BF16, F32, BF16]
        if not first:
            wv_pad = jnp.pad(w_in_vmix[l - 1], ((0, 0), (0, LANES - LORA_V))).astype(BF16)
            groups.append(wv_pad)
            group_dtypes.append(F32)
        proj = _project(x, groups, group_dtypes)
        p_rwkv, p_glu, p_q, p_gate = proj[:4]
        x_prev_s = state_shift[l]
        prev_rwkv = _matmul(x_prev_s, w_rwkv, SB, RWKV_COLS // 2)

        wup_pad = jnp.concatenate([w_up[l], jnp.zeros((LORA_A, RW), F32)], axis=0).astype(BF16)
        aup_pad = jnp.concatenate([jnp.zeros((LORA_W, RW), F32), a_up[l]], axis=0).astype(BF16)
        lora = (row2(w0[l]), wup_pad, row2(a0[l]), aup_pad, g_up[l].astype(BF16), row2(k_k[l]), row2(k_a[l]),
                row2(r_k[l]), _seg_ones())
        if first:
            pre = _rwkv_pre(True, npt, tps, p_rwkv, prev_rwkv, row2(mu_shift[l]), lora)
        else:
            p_vd = proj[4]
            prev_vd = _matmul(x_prev_s, wv_pad, SB, LANES)
            mu_v = jnp.pad(mu_vmix[l - 1], (0, LANES - LORA_V)).reshape(1, LANES)
            vup_pad = jnp.pad(v_up[l - 1], ((0, LANES - LORA_V), (0, 0))).astype(BF16)
            pre = _rwkv_pre(False, npt, tps, p_rwkv, prev_rwkv, row2(mu_shift[l]), lora, p_vd, prev_vd, mu_v,
                            v_first, row2(v0[l - 1]), vup_pad)
        r_, lw_, k_, v_, nkk_, bb_, g_, bonus_ = pre
        if first:
            v_first = v_

        y_p, s_p = _wkv_prompt(B, T, r_, lw_, k_, v_, nkk_, bb_)

        def lanes_b(a):
            return jnp.transpose(a[n_p:].reshape(ST, SB, RW), (0, 2, 1)).reshape(ST, HEADS, HEAD_DIM, SB)

        state_sz = HEADS * HEAD_DIM * HEAD_DIM
        s0 = _transpose_layer(state_wkv.astype(F32).reshape(L, SB, state_sz), l, 2048)
        s0 = s0.reshape(HEADS, HEAD_DIM, HEAD_DIM, SB)
        y_s, s_s = _wkv_sample(s0, lanes_b(r_), lanes_b(lw_), lanes_b(k_),
                               lanes_b(v_).reshape(ST, HEADS, HEAD_DIM, 1, SB), lanes_b(nkk_), lanes_b(bb_))
        y_s = jnp.transpose(y_s.reshape(ST, RW, SB), (0, 2, 1)).reshape(n_s, RW)
        s_s = _transpose_2d(s_s.reshape(state_sz, SB), 2048).reshape(SB, HEADS, HEAD_DIM, HEAD_DIM)
        z_a = _rwkv_post(npt, y_p, y_s, bonus_, g_, row2(lnx_g[l]), row2(lnx_b[l]))

        cvec = (conv_w[l], row2(conv_b[l]), row2(cn_g[l]), row2(cn_b[l]))
        zb_p, conv_p = _conv_prompt(B, T, p_glu, *cvec)
        st_t = jnp.swapaxes(state_conv[l], 0, 1)
        zb_s, glu_s = _conv_sample(st_t, p_glu[n_p:].reshape(ST, SB, 2 * CONV_CH), *cvec)
        zb_s = zb_s.reshape(n_s, CONV_CH)
        conv_s = jnp.concatenate([state_conv[l][:, ST:], jnp.swapaxes(glu_s, 0, 1)], axis=1)

        kv_p = _matmul(mem_rows, w_mem_kv[l].astype(BF16), M, 2 * MEM_WIDTH)
        kv3 = kv_p.reshape(B, M, 2 * MEM_WIDTH)
        zc_p = _attention_prompt(p_q, kv3, B, T, R)
        q_s = jnp.swapaxes(p_q[n_p:].reshape(ST, SB, MEM_WIDTH), 0, 1)
        q_s = jnp.pad(q_s, ((0, 0), (0, 8 - ST), (0, 0)))
        zc_s = _attention_cached(q_s, cache_mem_k.reshape(L, SB, M * MEM_HEADS, MEM_HEAD_DIM),
                                 cache_mem_v.reshape(L, SB, M * MEM_HEADS, MEM_HEAD_DIM), l, 8)[:, :ST]
        zc_s = jnp.swapaxes(zc_s, 0, 1).reshape(n_s, MEM_WIDTH)

        h, h_pk = _merge(n_p, alpha, z_a, zb_p, zb_s, zc_p, zc_s, p_gate, x,
                         w_o_a[l].astype(BF16), w_o_b[l].astype(BF16), w_o_c[l].astype(BF16),
                         w_out[l].astype(BF16), row2(ln1_g[l]), row2(ln1_b[l]))

        ws13 = jnp.concatenate([w_s1[l], w_s3[l]], axis=1).astype(BF16)
        x_new = _moe(l, alpha, h, h_pk, w_router[l].astype(BF16), row2(b_router[l]), w_e1, w_e3, w_e2,
                     ws13, w_s2[l].astype(BF16), row2(ln2_g[l]), row2(ln2_b[l]))

        outs['wkv_p'].append(s_p.astype(state_wkv.dtype))
        outs['shift_p'].append(x[T - 1:n_p:T].astype(state_shift.dtype))
        outs['conv_p'].append(conv_p.astype(state_conv.dtype))
        outs['mk'].append(kv3[..., :MEM_WIDTH].reshape(B, M, MEM_HEADS, MEM_HEAD_DIM).astype(cache_mem_k.dtype))
        outs['mv'].append(kv3[..., MEM_WIDTH:].reshape(B, M, MEM_HEADS, MEM_HEAD_DIM).astype(cache_mem_v.dtype))
        outs['wkv_s'].append(s_s.astype(state_wkv.dtype))
        outs['shift_s'].append(x[n_p + (ST - 1) * SB:].astype(state_shift.dtype))
        outs['conv_s'].append(conv_s.astype(state_conv.dtype))
        x = x_new

    y_p = x[:n_p].reshape(B, T, D)
    y_s = jnp.swapaxes(x[n_p:].reshape(ST, SB, D), 0, 1)
    return (y_p, y_s, jnp.stack(outs['wkv_p']), jnp.stack(outs['shift_p']), jnp.stack(outs['conv_p']),
            jnp.stack(outs['mk']), jnp.stack(outs['mv']), jnp.stack(outs['wkv_s']), jnp.stack(outs['shift_s']),
            jnp.stack(outs['conv_s']))
```

```python
import functools
import math

import jax
import jax.numpy as jnp
from jax import lax
from jax.experimental import pallas as pl
from jax.experimental.pallas import tpu as pltpu
from jax.experimental.pallas import tpu_sc as plsc

F32 = jnp.float32
BF16 = jnp.bfloat16

HEADS = 8
HEAD_DIM = 64
RW = HEADS * HEAD_DIM
LORA_W = 64
LORA_A = 64
LORA_V = 32
LORA_G = 128
RWKV_COLS = 3 * RW + LORA_W + LORA_A + LORA_G
GN_EPS = HEAD_DIM * 1e-5
CONV_WIDTH = 31
CONV_BUF = CONV_WIDTH - 1
CONV_CH = 512
MEM_HEADS = 4
MEM_HEAD_DIM = 128
MEM_WIDTH = MEM_HEADS * MEM_HEAD_DIM
TOP_K = 8
ROUTED_SCALE = 2.5
LN_EPS = 1e-5
EXP_M05 = math.exp(-0.5)

LANES = 128
ROW_TILE = 512
MERGE_TILE = 256
CHUNK = 64
NEUMANN_BLOCK = 16
MOE_BLOCK = 512
WKV_BATCH_PER_STEP = 4
ROUTER_SUB = 64
PROJ_TILE = 256
SC_CORES = 2
SC_SUBCORES = 16
SC_WINDOW = 128
SC_DISPATCH_WINDOW = 48
SC_PAD_WINDOW = 64
COMBINE_TILE = 128
VMEM_LIMIT = 48 * 1024 * 1024


def _cparams(*sem, vmem=VMEM_LIMIT):
    return pltpu.CompilerParams(dimension_semantics=tuple(sem), vmem_limit_bytes=vmem)


def _bdot(a, b):
    return jnp.dot(a.astype(BF16), b.astype(BF16), preferred_element_type=F32)


def _bdot_nt(a, b):
    return lax.dot_general(a.astype(BF16), b.astype(BF16), (((1,), (1,)), ((), ())),
                           preferred_element_type=F32)


def _sigmoid(x):
    return 1.0 / (1.0 + jnp.exp(-x))


def _silu(x):
    return x * _sigmoid(x)


def _layer_norm(x, g, b):
    mu = jnp.mean(x, axis=-1, keepdims=True)
    d = x - mu
    var = jnp.mean(d * d, axis=-1, keepdims=True)
    return d * lax.rsqrt(var + LN_EPS) * g + b


SEG_LANES = 256


def _seg_ones():
    i = jnp.arange(SEG_LANES)
    return ((i[:, None] // HEAD_DIM) == (i[None, :] // HEAD_DIM)).astype(BF16)


def _seg_sum(x, seg):
    hi = x.astype(BF16)
    lo = (x - hi.astype(F32)).astype(BF16)
    parts = []
    for j in range(x.shape[-1] // SEG_LANES):
        sl = slice(j * SEG_LANES, (j + 1) * SEG_LANES)
        parts.append(jnp.dot(hi[:, sl], seg, preferred_element_type=F32)
                     + jnp.dot(lo[:, sl], seg, preferred_element_type=F32))
    return jnp.concatenate(parts, axis=-1)


def _pack_pair(lo, hi):
    lo_b = lax.bitcast_convert_type(lo.astype(BF16).astype(F32), jnp.int32)
    hi_b = lax.bitcast_convert_type(hi.astype(BF16).astype(F32), jnp.int32)
    return lax.shift_right_logical(lo_b, jnp.full(lo_b.shape, 16, jnp.int32)) | (hi_b & jnp.int32(-65536))


def _unpack_pair(p):
    lo = lax.bitcast_convert_type(lax.shift_left(p, jnp.full(p.shape, 16, jnp.int32)), F32)
    hi = lax.bitcast_convert_type(p & jnp.int32(-65536), F32)
    return lo, hi


def _mm_kernel(x_ref, w_ref, o_ref):
    o_ref[...] = _bdot(x_ref[...], w_ref[...]).astype(o_ref.dtype)


def _matmul(x, w, tm, tn, out_dtype=F32):
    m, k = x.shape
    n_cols = w.shape[1]
    assert m % tm == 0 and n_cols % tn == 0
    return pl.pallas_call(
        _mm_kernel,
        grid=(m // tm, n_cols // tn),
        in_specs=[pl.BlockSpec((tm, k), lambda i, j: (i, 0)),
                  pl.BlockSpec((k, tn), lambda i, j: (0, j))],
        out_specs=pl.BlockSpec((tm, tn), lambda i, j: (i, j)),
        out_shape=jax.ShapeDtypeStruct((m, n_cols), out_dtype),
        compiler_params=_cparams("parallel", "parallel"),
        name=f"matmul_{n_cols}",
    )(x, w)


def _proj_kernel(n_out, x_ref, *refs):
    xb = x_ref[...].astype(BF16)
    for w_ref, o_ref in zip(refs[:n_out], refs[n_out:]):
        o_ref[...] = jnp.dot(xb, w_ref[...], preferred_element_type=F32).astype(o_ref.dtype)


def _project(x, weights, out_dtypes):
    m, k = x.shape
    tm = PROJ_TILE
    assert m % tm == 0
    return pl.pallas_call(
        functools.partial(_proj_kernel, len(weights)),
        grid=(m // tm,),
        in_specs=[pl.BlockSpec((tm, k), lambda i: (i, 0))]
        + [pl.BlockSpec(w.shape, lambda i: (0, 0), pipeline_mode=pl.Buffered(1)) for w in weights],
        out_specs=[pl.BlockSpec((tm, w.shape[1]), lambda i: (i, 0)) for w in weights],
        out_shape=[jax.ShapeDtypeStruct((m, w.shape[1]), dt) for w, dt in zip(weights, out_dtypes)],
        compiler_params=_cparams("parallel"),
        name="in_proj",
    )(x, *weights)


def _pre_kernel(first, npt, tps, *refs):
    if first:
        (p_ref, halo_ref, prev_ref, mu_ref,
         w0_ref, wup_ref, a0_ref, aup_ref, gup_ref, kk_ref, ka_ref, rk_ref, seg_ref,
         r_o, lw_o, k_o, v_o, nkk_o, bb_o, g_o, bon_o, sh_ref) = refs
    else:
        (p_ref, halo_ref, prev_ref, mu_ref, pv_ref, halov_ref, prevv_ref, muv_ref, vf_ref, v0_ref, vup_ref,
         w0_ref, wup_ref, a0_ref, aup_ref, gup_ref, kk_ref, ka_ref, rk_ref, seg_ref,
         r_o, lw_o, k_o, v_o, nkk_o, bb_o, g_o, bon_o, sh_ref, shv_ref) = refs
    i = pl.program_id(0)
    rows_n = p_ref.shape[0]

    def build_shift(src_ref, halo_r, prev_r, dst_ref):
        @pl.when(i < npt)
        def _():
            x = src_ref[...]
            rolled = pltpu.roll(x, 1, axis=0)
            last = halo_r[7:8, :]
            row0 = jnp.where((i % tps) == 0, jnp.zeros_like(last), last)
            rows = lax.broadcasted_iota(jnp.int32, x.shape, 0)
            dst_ref[...] = jnp.where(rows == 0, row0, rolled)

        @pl.when(i >= npt)
        def _():
            nb = prev_r.shape[0]
            dst_ref[0:nb, :] = prev_r[...]
            dst_ref[nb:rows_n, :] = src_ref[0:rows_n - nb, :]

    build_shift(p_ref, halo_ref, prev_ref, sh_ref)
    p = p_ref[...]
    p = p + (sh_ref[...] - p) * mu_ref[...]
    r = p[:, 0:RW]
    k = p[:, RW:2 * RW]
    v = p[:, 2 * RW:3 * RW]
    u = p[:, 3 * RW:3 * RW + LORA_W + LORA_A]
    gd = p[:, 3 * RW + LORA_W + LORA_A:RWKV_COLS]

    z = w0_ref[...] + _bdot(jnp.tanh(u), wup_ref[...])
    lw = -EXP_M05 * _sigmoid(z)
    a = _sigmoid(a0_ref[...] + _bdot(u, aup_ref[...]))
    g = _bdot(_sigmoid(gd), gup_ref[...])
    if not first:
        build_shift(pv_ref, halov_ref, prevv_ref, shv_ref)
        pv = pv_ref[...]
        pv = pv + (shv_ref[...] - pv) * muv_ref[...]
        mix = _sigmoid(v0_ref[...] + _bdot(pv, vup_ref[...]))
        v = v + (vf_ref[...] - v) * mix
    kk = k * kk_ref[...]
    seg = seg_ref[...]
    nrm = jnp.sqrt(_seg_sum(kk * kk, seg))
    kkn = kk / jnp.maximum(nrm, 1e-12)
    k2 = k * (1.0 + (a - 1.0) * ka_ref[...])
    bonus = _seg_sum(r * k2 * rk_ref[...], seg) * v
    r_o[...] = r
    lw_o[...] = lw
    k_o[...] = k2
    v_o[...] = v
    nkk_o[...] = -kkn
    bb_o[...] = kkn * a
    g_o[...] = g
    bon_o[...] = bonus


def _rwkv_pre(first, npt, tps, p_rwkv, prev_rwkv, mu, lora, p_vd=None, prev_vd=None, mu_v=None, v_first=None,
              v0=None, vup=None):
    n = p_rwkv.shape[0]
    R = ROW_TILE
    nt = n // R

    def row_spec(w):
        return pl.BlockSpec((R, w), lambda i: (i, 0))

    def halo_spec(w):
        return pl.BlockSpec((8, w), lambda i: (jnp.maximum(i * (R // 8) - 1, 0), 0))

    def full(a):
        return pl.BlockSpec(a.shape, lambda i: (0,) * a.ndim)

    ins = [p_rwkv, p_rwkv, prev_rwkv, mu]
    specs = [row_spec(RWKV_COLS), halo_spec(RWKV_COLS), full(prev_rwkv), full(mu)]
    scratch = [pltpu.VMEM((R, RWKV_COLS), F32)]
    if not first:
        ins += [p_vd, p_vd, prev_vd, mu_v, v_first, v0, vup]
        specs += [row_spec(LANES), halo_spec(LANES), full(prev_vd), full(mu_v), row_spec(RW), full(v0), full(vup)]
        scratch.append(pltpu.VMEM((R, LANES), F32))
    ins += list(lora)
    specs += [full(a) for a in lora]
    outs = pl.pallas_call(
        functools.partial(_pre_kernel, first, npt, tps),
        grid=(nt,),
        in_specs=specs,
        out_specs=[row_spec(RW)] * 8,
        out_shape=[jax.ShapeDtypeStruct((n, RW), F32)] * 8,
        scratch_shapes=scratch,
        compiler_params=_cparams("parallel"),
        name="rwkv_pre",
    )(*ins)
    return outs


def _wkv_chunk_kernel(nbat, *refs):
    in_refs = refs[:6 * nbat]
    y_ref, s_out_ref, s_ref = refs[6 * nbat:]
    c = pl.program_id(1)
    C = in_refs[0].shape[0]
    G = 4 * HEAD_DIM
    assert 4 * C == G

    @pl.when(c == 0)
    def _():
        s_ref[...] = jnp.zeros_like(s_ref)

    ri = lax.broadcasted_iota(jnp.int32, (G, G), 0)
    ci = lax.broadcasted_iota(jnp.int32, (G, G), 1)
    same_head = (ri // C) == (ci // HEAD_DIM)
    tril_s = same_head & ((ci % C) < (ri % C))
    tril_i = same_head & ((ci % C) <= (ri % C))
    diag_blk = (ri // NEUMANN_BLOCK) == (ci // NEUMANN_BLOCK)
    eye = (ri == ci).astype(F32)

    def stack(x):
        return jnp.where(same_head, jnp.concatenate([x, x, x, x], axis=0), 0.0).astype(BF16)

    def tile4(x):
        return jnp.concatenate([x, x, x, x], axis=0).astype(BF16)

    def collapse(x4):
        return x4[0:C] + x4[C:2 * C] + x4[2 * C:3 * C] + x4[3 * C:4 * C]

    for b in range(nbat):
        r_ref, lw_ref, k_ref, v_ref, nkk_ref, bb_ref = in_refs[6 * b:6 * b + 6]
        lw = lw_ref[...]
        rows = lax.broadcasted_iota(jnp.int32, lw.shape, 0)
        cs = lw
        s = 1
        while s < C:
            cs = cs + jnp.where(rows >= s, pltpu.roll(cs, s, axis=0), 0.0)
            s *= 2
        cs_last = cs[C - 1:C, :]
        e_neg = jnp.exp(-cs)
        e_tail = jnp.exp(cs_last - cs)
        kk_ = k_ref[...]
        bb_ = bb_ref[...]
        at = nkk_ref[...] * jnp.exp(cs - lw)
        rt = r_ref[...] * jnp.exp(cs)
        bt = bb_ * e_neg
        kt = kk_ * e_neg
        bw = bb_ * e_tail
        kw = kk_ * e_tail
        vv = v_ref[...]
        w_end = jnp.exp(cs_last)

        for g in range(RW // G):
            sl = slice(g * G, (g + 1) * G)
            at4, rt4, v4 = stack(at[:, sl]), stack(rt[:, sl]), stack(vv[:, sl])
            bt4, kt4 = tile4(bt[:, sl]), tile4(kt[:, sl])
            a_ab = jnp.where(tril_s, _bdot_nt(at4, bt4), 0.0)
            a_ak = jnp.where(tril_s, _bdot_nt(at4, kt4), 0.0).astype(BF16)
            a_rb = jnp.where(tril_i, _bdot_nt(rt4, bt4), 0.0).astype(BF16)
            a_rk = jnp.where(tril_i, _bdot_nt(rt4, kt4), 0.0).astype(BF16)
            a_d = jnp.where(diag_blk, a_ab, 0.0)
            a_o = a_ab - a_d
            a_db = a_d.astype(BF16)
            x2 = _bdot(a_db, a_db).astype(BF16)
            x4 = _bdot(x2, x2).astype(BF16)
            x8 = _bdot(x4, x4)
            t_d = eye + a_d
            t_d = t_d + _bdot(t_d, x2)
            t_d = t_d + _bdot(t_d, x4)
            t_d = t_d + _bdot(t_d, x8)
            t_db = t_d.astype(BF16)
            nn = _bdot(t_db, a_o).astype(BF16)
            n2 = _bdot(nn, nn)
            m1 = t_d + _bdot(nn, t_db)
            t_m = m1 + _bdot(n2, m1)
            s4 = s_ref[b, g]
            s4b = s4.astype(BF16)
            u4 = _bdot(t_m, _bdot_nt(at4, s4b) + _bdot(a_ak, v4))
            y4 = _bdot_nt(rt4, s4b) + _bdot(jnp.concatenate([a_rb, a_rk], axis=1),
                                            jnp.concatenate([u4.astype(BF16), v4], axis=0))
            y_ref[b, :, sl] = collapse(y4)
            uv_t = jnp.concatenate([collapse(u4), vv[:, sl]], axis=0).T
            upd = _bdot(uv_t, jnp.concatenate([bw[:, sl], kw[:, sl]], axis=0))
            s_ref[b, g] = s4 * w_end[:, sl] + jnp.where((ri // HEAD_DIM) == (ci // HEAD_DIM), upd, 0.0)

    @pl.when(c == pl.num_programs(1) - 1)
    def _():
        for b in range(nbat):
            for hd in range(HEADS):
                g, j = divmod(hd, 4)
                blk = slice(j * HEAD_DIM, (j + 1) * HEAD_DIM)
                s_out_ref[b, hd] = s_ref[b, g, blk, blk]


def _wkv_prompt(B, T, r, lw, k, v, nkk, bb):
    C = CHUNK
    nc = T // C
    G = 4 * HEAD_DIM
    nbat = math.gcd(B, WKV_BATCH_PER_STEP)
    specs = []
    for j in range(nbat):
        specs += [pl.BlockSpec((C, RW), lambda i, c, j=j: ((nbat * i + j) * nc + c, 0))] * 6
    y, s = pl.pallas_call(
        functools.partial(_wkv_chunk_kernel, nbat),
        grid=(B // nbat, nc),
        in_specs=specs,
        out_specs=[pl.BlockSpec((nbat, C, RW), lambda i, c: (i, c, 0)),
                   pl.BlockSpec((nbat, HEADS, HEAD_DIM, HEAD_DIM), lambda i, c: (i, 0, 0, 0))],
        out_shape=[jax.ShapeDtypeStruct((B, T, RW), F32),
                   jax.ShapeDtypeStruct((B, HEADS, HEAD_DIM, HEAD_DIM), F32)],
        scratch_shapes=[pltpu.VMEM((nbat, RW // G, G, G), F32)],
        compiler_params=_cparams("parallel", "arbitrary"),
        name="wkv_chunk",
    )(*([r, lw, k, v, nkk, bb] * nbat))
    return y.reshape(B * T, RW), s


def _wkv_seq_kernel(s0_ref, r_ref, lw_ref, k_ref, v_ref, nkk_ref, bb_ref, y_ref, s_ref):
    steps = r_ref.shape[0]

    def body(vi, carry):
        st = s0_ref[0, vi]
        for t in range(steps):
            sa = jnp.sum(st * nkk_ref[t, 0], axis=0, keepdims=True)
            st = st * jnp.exp(lw_ref[t, 0]) + sa * bb_ref[t, 0] + v_ref[t, 0, vi] * k_ref[t, 0]
            y_ref[t, 0, vi] = jnp.sum(st * r_ref[t, 0], axis=0, keepdims=True)
        s_ref[0, vi] = st
        return carry

    lax.fori_loop(0, HEAD_DIM, body, 0)


def _wkv_sample(s0, r, lw, k, v, nkk, bb):
    steps, _, _, nb = r.shape
    vec = pl.BlockSpec((steps, 1, HEAD_DIM, nb), lambda h: (0, h, 0, 0))
    vcol = pl.BlockSpec((steps, 1, HEAD_DIM, 1, nb), lambda h: (0, h, 0, 0, 0))
    st = pl.BlockSpec((1, HEAD_DIM, HEAD_DIM, nb), lambda h: (h, 0, 0, 0))
    y, s = pl.pallas_call(
        _wkv_seq_kernel,
        grid=(HEADS,),
        in_specs=[st, vec, vec, vec, vcol, vec, vec],
        out_specs=[vcol, st],
        out_shape=[jax.ShapeDtypeStruct((steps, HEADS, HEAD_DIM, 1, nb), F32),
                   jax.ShapeDtypeStruct(s0.shape, F32)],
        compiler_params=_cparams("parallel"),
        name="wkv_seq",
    )(s0, r, lw, k, v, nkk, bb)
    return y, s


def _tr_kernel(x_ref, o_ref):
    o_ref[...] = x_ref[...].T


def _transpose_layer(x3, layer, tc):
    _, a, c = x3.shape
    return pl.pallas_call(
        _tr_kernel,
        grid=(c // tc,),
        in_specs=[pl.BlockSpec((None, a, tc), lambda j: (layer, 0, j))],
        out_specs=pl.BlockSpec((tc, a), lambda j: (j, 0)),
        out_shape=jax.ShapeDtypeStruct((c, a), x3.dtype),
        compiler_params=_cparams("parallel"),
        name="transpose_in",
    )(x3)


def _transpose_2d(x, tc):
    c, a = x.shape
    return pl.pallas_call(
        _tr_kernel,
        grid=(c // tc,),
        in_specs=[pl.BlockSpec((tc, a), lambda j: (j, 0))],
        out_specs=pl.BlockSpec((a, tc), lambda j: (0, j)),
        out_shape=jax.ShapeDtypeStruct((a, c), x.dtype),
        compiler_params=_cparams("parallel"),
        name="transpose_out",
    )(x)


def _glu(p):
    p = p.astype(F32)
    return p[..., :CONV_CH] * _sigmoid(p[..., CONV_CH:])


def _conv_prompt_kernel(p_ref, halo_ref, w_ref, cb_ref, g_ref, b_ref, o_ref, tail_ref, ext_ref, sh_ref):
    j = pl.program_id(1)
    R = p_ref.shape[0]
    H = halo_ref.shape[0]
    glu = _glu(p_ref[...])
    halo = _glu(halo_ref[...])
    ext_ref[0:H, :] = jnp.where(j == 0, jnp.zeros_like(halo), halo)
    ext_ref[H:H + R, :] = glu
    span = sh_ref.shape[1]
    for ph in range(1, 8):
        sh_ref[ph] = ext_ref[ph:ph + span, :]
    acc = jnp.zeros((R, CONV_CH), F32)
    for t in range(CONV_WIDTH):
        off = H - CONV_BUF + t
        q8 = (off // 8) * 8
        tap = ext_ref[q8:q8 + R, :] if off % 8 == 0 else sh_ref[off % 8, q8:q8 + R, :]
        acc = acc + tap * w_ref[t:t + 1, :]
    h = _layer_norm(acc + cb_ref[...], g_ref[...], b_ref[...])
    o_ref[...] = _silu(h)

    @pl.when(j == pl.num_programs(1) - 1)
    def _():
        tail_ref[0] = glu[R - H:R, :]


def _conv_prompt(B, T, p_glu, conv_w, conv_b, cn_g, cn_b):
    R = ROW_TILE
    H = 32
    tps = T // R
    vec = pl.BlockSpec((1, CONV_CH), lambda b, j: (0, 0))
    z, tail = pl.pallas_call(
        _conv_prompt_kernel,
        grid=(B, tps),
        in_specs=[pl.BlockSpec((R, 2 * CONV_CH), lambda b, j: (b * tps + j, 0)),
                  pl.BlockSpec((H, 2 * CONV_CH), lambda b, j: (jnp.maximum((b * tps + j) * (R // H) - 1, 0), 0)),
                  pl.BlockSpec((CONV_WIDTH, CONV_CH), lambda b, j: (0, 0)), vec, vec, vec],
        out_specs=[pl.BlockSpec((R, CONV_CH), lambda b, j: (b * tps + j, 0)),
                   pl.BlockSpec((1, H, CONV_CH), lambda b, j: (b, 0, 0))],
        out_shape=[jax.ShapeDtypeStruct((B * T, CONV_CH), F32),
                   jax.ShapeDtypeStruct((B, H, CONV_CH), F32)],
        scratch_shapes=[pltpu.VMEM((H + R, CONV_CH), F32), pltpu.VMEM((8, H + R - 8, CONV_CH), F32)],
        compiler_params=_cparams("parallel", "arbitrary"),
        name="conv_prompt",
    )(p_glu, p_glu, conv_w, conv_b, cn_g, cn_b)
    return z, tail[:, H - CONV_BUF:, :]


def _conv_sample_kernel(st_ref, p_ref, w_ref, cb_ref, g_ref, b_ref, o_ref, glu_ref):
    steps = p_ref.shape[0]
    glu = [_glu(p_ref[t]) for t in range(steps)]
    for t in range(steps):
        glu_ref[t] = glu[t]
    for t in range(steps):
        acc = jnp.zeros(glu[0].shape, F32)
        for j in range(CONV_WIDTH):
            src = t + j
            x = st_ref[src] if src < CONV_BUF else glu[src - CONV_BUF]
            acc = acc + x * w_ref[j:j + 1, :]
        h = _layer_norm(acc + cb_ref[...], g_ref[...], b_ref[...])
        o_ref[t] = _silu(h)


def _conv_sample(state_t, p_glu, conv_w, conv_b, cn_g, cn_b):
    steps, nb, _ = p_glu.shape

    def full(a):
        return pl.BlockSpec(a.shape, lambda i: (0,) * a.ndim)

    ins = (state_t, p_glu, conv_w, conv_b, cn_g, cn_b)
    return pl.pallas_call(
        _conv_sample_kernel,
        grid=(1,),
        in_specs=[full(a) for a in ins],
        out_specs=[pl.BlockSpec((steps, nb, CONV_CH), lambda i: (0, 0, 0))] * 2,
        out_shape=[jax.ShapeDtypeStruct((steps, nb, CONV_CH), F32)] * 2,
        compiler_params=_cparams("arbitrary"),
        name="conv_sample",
    )(*ins)


def _attn_kernel(q_ref, k_ref, v_ref, o_ref):
    scale = MEM_HEAD_DIM ** -0.5
    for h in range(MEM_HEADS):
        sl = slice(h * MEM_HEAD_DIM, (h + 1) * MEM_HEAD_DIM)
        s = _bdot_nt(q_ref[:, sl], k_ref[:, sl]) * scale
        m = jnp.max(s, axis=-1, keepdims=True)
        e = jnp.exp(s - m)
        pr = e / jnp.sum(e, axis=-1, keepdims=True)
        o_ref[:, sl] = _bdot(pr, v_ref[:, sl])


def _attn_rows_kernel(q_ref, k_ref, v_ref, o_ref):
    nb, tq, width = q_ref.shape
    n_mem = k_ref.shape[1] // MEM_HEADS
    scale = MEM_HEAD_DIM ** -0.5
    ri = lax.broadcasted_iota(jnp.int32, (MEM_HEADS * tq, width), 0)
    ci = lax.broadcasted_iota(jnp.int32, (MEM_HEADS * tq, width), 1)
    own = (ri // tq) == (ci // MEM_HEAD_DIM)
    for bi in range(nb):
        k_all = jnp.concatenate([k_ref[bi, pl.ds(h, n_mem, stride=MEM_HEADS), :].astype(BF16)
                                 for h in range(MEM_HEADS)], axis=1)
        v_all = jnp.concatenate([v_ref[bi, pl.ds(h, n_mem, stride=MEM_HEADS), :].astype(BF16)
                                 for h in range(MEM_HEADS)], axis=1)
        q = q_ref[bi]
        q4 = jnp.where(own, jnp.concatenate([q] * MEM_HEADS, axis=0), 0.0)
        s = _bdot_nt(q4, k_all) * scale
        m = jnp.max(s, axis=-1, keepdims=True)
        e = jnp.exp(s - m)
        pr = e / jnp.sum(e, axis=-1, keepdims=True)
        pv = jnp.where(own, _bdot(pr, v_all), 0.0)
        o = pv[0:tq]
        for h in range(1, MEM_HEADS):
            o = o + pv[h * tq:(h + 1) * tq]
        o_ref[bi] = o


def _attention_cached(q, cache_k, cache_v, layer, bb):
    B, T, W = q.shape
    rows = cache_k.shape[2]
    kv_spec = pl.BlockSpec((None, bb, rows, MEM_HEAD_DIM), lambda b: (layer, b, 0, 0))
    return pl.pallas_call(
        _attn_rows_kernel,
        grid=(B // bb,),
        in_specs=[pl.BlockSpec((bb, T, W), lambda b: (b, 0, 0)), kv_spec, kv_spec],
        out_specs=pl.BlockSpec((bb, T, W), lambda b: (b, 0, 0)),
        out_shape=jax.ShapeDtypeStruct((B, T, W), F32),
        compiler_params=_cparams("parallel"),
        name="mem_attention_cached",
    )(q, cache_k, cache_v)


def _attention_prompt(q_rows, kv, B, T, tq):
    W = q_rows.shape[1]
    M = kv.shape[1]
    tps = T // tq
    return pl.pallas_call(
        _attn_kernel,
        grid=(B, tps),
        in_specs=[pl.BlockSpec((tq, W), lambda b, j: (b * tps + j, 0)),
                  pl.BlockSpec((None, M, W), lambda b, j: (b, 0, 0)),
                  pl.BlockSpec((None, M, W), lambda b, j: (b, 0, 1))],
        out_specs=pl.BlockSpec((tq, W), lambda b, j: (b * tps + j, 0)),
        out_shape=jax.ShapeDtypeStruct((B * T, W), F32),
        compiler_params=_cparams("parallel", "parallel"),
        name="mem_attention_prompt",
    )(q_rows, kv, kv)


def _merge_kernel(npt, alpha, yp_ref, ys_ref, bon_ref, og_ref, lg_ref, lb_ref, seg_ref,
                  zbp_ref, zbs_ref, zcp_ref, zcs_ref, gate_ref, x_ref,
                  woa_ref, wob_ref, woc_ref, wout_ref, g_ref, b_ref, o_ref, pk_ref):
    i = pl.program_id(0)
    D = x_ref.shape[1]
    is_p = i < npt
    y = jnp.where(is_p, yp_ref[...], ys_ref[...])
    seg = seg_ref[...]
    mu = _seg_sum(y, seg) * (1.0 / HEAD_DIM)
    d = y - mu
    var = _seg_sum(d * d, seg) * (1.0 / HEAD_DIM)
    za = (d * lax.rsqrt(var + GN_EPS) * lg_ref[...] + lb_ref[...] + bon_ref[...]) * og_ref[...]
    zb = jnp.where(is_p, zbp_ref[...], zbs_ref[...])
    zc = jnp.where(is_p, zcp_ref[...], zcs_ref[...])
    merged = (_sigmoid(gate_ref[:, 0:D].astype(F32)) * _bdot(za, woa_ref[...])
              + _sigmoid(gate_ref[:, D:2 * D].astype(F32)) * _bdot(zb, wob_ref[...])
              + _sigmoid(gate_ref[:, 2 * D:3 * D].astype(F32)) * _bdot(zc, woc_ref[...]))
    h = _layer_norm(alpha * x_ref[...] + _bdot(merged, wout_ref[...]), g_ref[...], b_ref[...])
    o_ref[...] = h
    pk_ref[...] = _pack_pair(h[:, 0:D // 2], h[:, D // 2:D])


def _merge(n_p, alpha, y_p, y_s, bonus, out_gate, lnx_g, lnx_b, zb_p, zb_s, zc_p, zc_s, gates, x,
           woa, wob, woc, wout, ln_g, ln_b):
    n, D = x.shape
    seg = _seg_ones()
    R = MERGE_TILE
    npt = n_p // R

    def row(w):
        return pl.BlockSpec((R, w), lambda i: (i, 0))

    def prow(w):
        return pl.BlockSpec((R, w), lambda i: (jnp.minimum(i, npt - 1), 0))

    def srow(w):
        return pl.BlockSpec((R, w), lambda i: (jnp.maximum(i - npt, 0), 0))

    def full(a):
        return pl.BlockSpec(a.shape, lambda i: (0,) * a.ndim)

    return pl.pallas_call(
        functools.partial(_merge_kernel, npt, alpha),
        grid=(n // R,),
        in_specs=[prow(RW), srow(RW), row(RW), row(RW), full(lnx_g), full(lnx_b), full(seg),
                  prow(CONV_CH), srow(CONV_CH), prow(MEM_WIDTH), srow(MEM_WIDTH), row(3 * D), row(D),
                  full(woa), full(wob), full(woc), full(wout), full(ln_g), full(ln_b)],
        out_specs=[row(D), row(D // 2)],
        out_shape=[jax.ShapeDtypeStruct((n, D), F32), jax.ShapeDtypeStruct((n, D // 2), jnp.int32)],
        compiler_params=_cparams("parallel"),
        name="branch_merge",
    )(y_p, y_s, bonus, out_gate, lnx_g, lnx_b, seg, zb_p, zb_s, zc_p, zc_s, gates, x,
      woa, wob, woc, wout, ln_g, ln_b)


def _router_kernel(h_ref, wr_ref, br_ref, idx_o, gate_o, rank_o, cnt_o, carry_ref):
    i = pl.program_id(0)
    R = h_ref.shape[0]
    E = wr_ref.shape[1]

    @pl.when(i == 0)
    def _():
        carry_ref[...] = jnp.zeros_like(carry_ref)

    scores_all = _sigmoid(_bdot(h_ref[...], wr_ref[...]))
    S = ROUTER_SUB
    lane = lax.broadcasted_iota(jnp.int32, (S, E), 1).astype(F32)

    def top_k(scores):
        work = scores + br_ref[...]
        sel_mask = jnp.zeros((S, E), F32)
        idx_cols, sel_cols = [], []
        for _ in range(TOP_K):
            m = jnp.max(work, axis=-1, keepdims=True)
            first = jnp.min(jnp.where(work == m, lane, float(E)), axis=-1, keepdims=True)
            onehot = lane == first
            sel_cols.append(jnp.sum(jnp.where(onehot, scores, 0.0), axis=-1, keepdims=True))
            idx_cols.append(first)
            sel_mask = jnp.where(onehot, 1.0, sel_mask)
            work = jnp.where(onehot, -jnp.inf, work)
        return idx_cols, sel_cols, sel_mask

    picks = [top_k(scores_all[s * S:(s + 1) * S]) for s in range(R // S)]
    sel_mask = jnp.concatenate([p[2] for p in picks], axis=0)
    ri = lax.broadcasted_iota(jnp.int32, (R, R), 0)
    ci = lax.broadcasted_iota(jnp.int32, (R, R), 1)
    before = jnp.dot((ci < ri).astype(BF16), sel_mask.astype(BF16), preferred_element_type=F32) + carry_ref[0:1, :]
    for s, (idx_cols, sel_cols, _) in enumerate(picks):
        rows = slice(s * S, (s + 1) * S)
        total = sel_cols[0]
        for c in sel_cols[1:]:
            total = total + c
        idx_out = jnp.zeros((S, E), F32)
        gate_out = jnp.zeros((S, E), F32)
        rank_out = jnp.zeros((S, E), F32)
        for kk in range(TOP_K):
            onehot = lane == idx_cols[kk]
            rank = jnp.sum(jnp.where(onehot, before[rows], 0.0), axis=-1, keepdims=True)
            col = lane == float(kk)
            idx_out = jnp.where(col, idx_cols[kk], idx_out)
            gate_out = jnp.where(col, sel_cols[kk] / total * ROUTED_SCALE, gate_out)
            rank_out = jnp.where(col, rank, rank_out)
        idx_o[rows, :] = idx_out
        gate_o[rows, :] = gate_out
        rank_o[rows, :] = rank_out
    carry_ref[0:1, :] = carry_ref[0:1, :] + jnp.sum(sel_mask, axis=0, keepdims=True)
    cnt_o[...] = carry_ref[...]


def _router(h, w_router, b_router):
    n, D = h.shape
    E = w_router.shape[1]
    assert E == LANES
    R = 256
    row = pl.BlockSpec((R, E), lambda i: (i, 0))
    return pl.pallas_call(
        _router_kernel,
        grid=(n // R,),
        in_specs=[pl.BlockSpec((R, D), lambda i: (i, 0)),
                  pl.BlockSpec((D, E), lambda i: (0, 0)),
                  pl.BlockSpec((1, E), lambda i: (0, 0))],
        out_specs=[row, row, row, pl.BlockSpec((8, E), lambda i: (0, 0))],
        out_shape=[jax.ShapeDtypeStruct((n, E), F32), jax.ShapeDtypeStruct((n, E), F32),
                   jax.ShapeDtypeStruct((n, E), F32), jax.ShapeDtypeStruct((8, E), F32)],
        scratch_shapes=[pltpu.VMEM((8, E), F32)],
        compiler_params=_cparams("arbitrary"),
        name="moe_router",
    )(h, w_router, b_router)


def _dest_kernel(idx_ref, rank_ref, start_ref, o_ref):
    R, E = idx_ref.shape
    lane = lax.broadcasted_iota(jnp.int32, (R, E), 1).astype(F32)
    idx = idx_ref[...]
    out = jnp.zeros((R, E), F32)
    start = start_ref[...]
    for kk in range(TOP_K):
        e_k = jnp.sum(jnp.where(lane == float(kk), idx, 0.0), axis=-1, keepdims=True)
        s_k = jnp.sum(jnp.where(lane == e_k, start, 0.0), axis=-1, keepdims=True)
        out = jnp.where(lane == float(kk), s_k, out)
    o_ref[...] = (out + rank_ref[...]).astype(jnp.int32)


def _dest(idx, rank, start):
    n, E = idx.shape
    R = 256
    row = pl.BlockSpec((R, E), lambda i: (i, 0))
    return pl.pallas_call(
        _dest_kernel,
        grid=(n // R,),
        in_specs=[row, row, pl.BlockSpec((1, E), lambda i: (0, 0))],
        out_specs=row,
        out_shape=jax.ShapeDtypeStruct((n, E), jnp.int32),
        compiler_params=_cparams("parallel"),
        name="moe_dest",
    )(idx, rank, start)


def _sc_dispatch(h_pk, dest, pad_slots, n_rows):
    n, width = h_pk.shape
    workers = SC_CORES * SC_SUBCORES
    per_w = n // workers
    win = SC_DISPATCH_WINDOW
    assert per_w * workers == n and per_w % win == 0
    n_chunk = per_w // win
    dest_w = jnp.transpose(dest.reshape(workers, n_chunk, win, TOP_K), (0, 1, 3, 2))
    pad_per_w = pad_slots.size // workers
    pwin = SC_PAD_WINDOW
    n_pchunk = pad_per_w // (TOP_K * pwin)
    assert n_pchunk * TOP_K * pwin * workers == pad_slots.size
    pad_w = pad_slots.reshape(workers, n_pchunk, TOP_K, pwin)
    zero_rows = jnp.zeros((pwin, width), h_pk.dtype)
    mesh = plsc.VectorSubcoreMesh(core_axis_name="c", subcore_axis_name="s",
                                  num_cores=SC_CORES, num_subcores=SC_SUBCORES)

    @functools.partial(
        pl.kernel, mesh=mesh,
        out_type=jax.ShapeDtypeStruct((n_rows, width), h_pk.dtype),
        scratch_types=[pltpu.VMEM((TOP_K, win), jnp.int32),
                       pltpu.VMEM((win, width), h_pk.dtype),
                       pltpu.VMEM((TOP_K, pwin), jnp.int32),
                       pltpu.VMEM((pwin, width), h_pk.dtype),
                       pltpu.SemaphoreType.DMA],
        name="moe_sc_dispatch",
    )
    def dispatch(h_hbm, dest_hbm, pad_hbm, zero_hbm, xg_hbm, idx_v, rows_v, pidx_v, zeros_v, sem):
        wid = lax.axis_index("s") * SC_CORES + lax.axis_index("c")
        pltpu.sync_copy(zero_hbm, zeros_v)

        @pl.loop(0, n_chunk)
        def _(j):
            base = pl.multiple_of(wid * per_w + j * win, 8)
            pltpu.sync_copy(h_hbm.at[pl.ds(base, win)], rows_v)
            pltpu.sync_copy(dest_hbm.at[wid, j], idx_v)
            copies = [pltpu.async_copy(rows_v, xg_hbm.at[idx_v.at[kk]], sem) for kk in range(TOP_K)]
            for cp in copies:
                cp.wait()

        @pl.loop(0, n_pchunk)
        def _(j):
            pltpu.sync_copy(pad_hbm.at[wid, j], pidx_v)
            copies = [pltpu.async_copy(zeros_v, xg_hbm.at[pidx_v.at[kk]], sem) for kk in range(TOP_K)]
            for cp in copies:
                cp.wait()

    return dispatch(h_pk, dest_w, pad_w, zero_rows)


def _expert_kernel(be_ref, nu_ref, x_ref, w1_ref, w3_ref, w2_ref, o_ref, w13_s, w2_s):
    i = pl.program_id(0)
    F = w1_ref.shape[1]

    @pl.when(i < nu_ref[0])
    def _():
        prev = be_ref[jnp.maximum(i - 1, 0)]

        @pl.when((i == 0) | (be_ref[i] != prev))
        def _():
            w13_s[:, 0:F] = w1_ref[...].astype(BF16)
            w13_s[:, F:2 * F] = w3_ref[...].astype(BF16)
            w2_s[...] = w2_ref[...].astype(BF16)

        x_lo, x_hi = _unpack_pair(x_ref[...])
        half = x_ref.shape[1]
        h13 = (jnp.dot(x_lo.astype(BF16), w13_s[0:half, :], preferred_element_type=F32)
               + jnp.dot(x_hi.astype(BF16), w13_s[half:2 * half, :], preferred_element_type=F32))
        hh = _silu(h13[:, 0:F]) * h13[:, F:2 * F]
        y = jnp.dot(hh.astype(BF16), w2_s[...], preferred_element_type=F32)
        o_ref[...] = _pack_pair(y[:, 0:half], y[:, half:2 * half])


def _experts(layer, block_e, n_used, xg, w_e1, w_e3, w_e2):
    n_slots, half = xg.shape
    D = 2 * half
    F = w_e1.shape[3]
    BM = MOE_BLOCK
    nb = block_e.shape[0]
    n_slots = nb * BM

    def blk(i, be, nu):
        return (jnp.minimum(i, nu[0] - 1), 0)

    grid_spec = pltpu.PrefetchScalarGridSpec(
        num_scalar_prefetch=2,
        grid=(nb,),
        in_specs=[pl.BlockSpec((BM, half), blk),
                  pl.BlockSpec((None, None, D, F), lambda i, be, nu: (layer, be[i], 0, 0)),
                  pl.BlockSpec((None, None, D, F), lambda i, be, nu: (layer, be[i], 0, 0)),
                  pl.BlockSpec((None, None, F, D), lambda i, be, nu: (layer, be[i], 0, 0))],
        out_specs=pl.BlockSpec((BM, half), blk),
        scratch_shapes=[pltpu.VMEM((D, 2 * F), BF16), pltpu.VMEM((F, D), BF16)],
    )
    return pl.pallas_call(
        _expert_kernel,
        grid_spec=grid_spec,
        out_shape=jax.ShapeDtypeStruct((n_slots, half), jnp.int32),
        compiler_params=_cparams("arbitrary"),
        name="moe_experts",
    )(block_e, n_used, xg, w_e1, w_e3, w_e2)


def _sc_gather(table, idx):
    n_idx = idx.shape[0]
    width = table.shape[1]
    workers = SC_CORES * SC_SUBCORES
    per_w = n_idx // workers
    assert per_w * workers == n_idx and per_w % SC_WINDOW == 0
    mesh = plsc.VectorSubcoreMesh(core_axis_name="c", subcore_axis_name="s",
                                  num_cores=SC_CORES, num_subcores=SC_SUBCORES)

    @functools.partial(
        pl.kernel, mesh=mesh,
        out_type=jax.ShapeDtypeStruct((n_idx, width), table.dtype),
        scratch_types=[pltpu.VMEM((SC_WINDOW,), jnp.int32),
                       pltpu.VMEM((SC_WINDOW, width), table.dtype),
                       pltpu.SemaphoreType.DMA],
        name="moe_sc_gather",
    )
    def gather(table_hbm, idx_hbm, out_hbm, idx_v, rows_v, sem):
        wid = lax.axis_index("s") * SC_CORES + lax.axis_index("c")

        @pl.loop(0, per_w // SC_WINDOW)
        def _(j):
            base = pl.multiple_of(wid * per_w + j * SC_WINDOW, SC_WINDOW)
            pltpu.sync_copy(idx_hbm.at[pl.ds(base, SC_WINDOW)], idx_v)
            pltpu.async_copy(table_hbm.at[idx_v], rows_v, sem).wait()
            pltpu.sync_copy(rows_v, out_hbm.at[pl.ds(base, SC_WINDOW)])

    return gather(table, idx)


def _combine_kernel(alpha, h_ref, gate_ref, yt_ref, ws13_ref, ws2_ref, g_ref, b_ref, o_ref):
    F = ws2_ref.shape[0]
    half = h_ref.shape[1] // 2
    h = h_ref[...]
    h13 = _bdot(h, ws13_ref[...])
    shared = _bdot(_silu(h13[:, 0:F]) * h13[:, F:2 * F], ws2_ref[...])
    r_lo = jnp.zeros((h.shape[0], half), F32)
    r_hi = jnp.zeros((h.shape[0], half), F32)
    for kk in range(TOP_K):
        y_lo, y_hi = _unpack_pair(yt_ref[kk])
        gk = gate_ref[:, kk:kk + 1]
        r_lo = r_lo + y_lo * gk
        r_hi = r_hi + y_hi * gk
    routed = jnp.concatenate([r_lo, r_hi], axis=-1)
    o_ref[...] = _layer_norm(alpha * h + (shared + routed), g_ref[...], b_ref[...])


def _combine(alpha, h, gate, yt, ws13, ws2, ln_g, ln_b):
    n, D = h.shape
    R = COMBINE_TILE

    def full(a):
        return pl.BlockSpec(a.shape, lambda i: (0,) * a.ndim)

    return pl.pallas_call(
        functools.partial(_combine_kernel, alpha),
        grid=(n // R,),
        in_specs=[pl.BlockSpec((R, D), lambda i: (i, 0)),
                  pl.BlockSpec((R, LANES), lambda i: (i, 0)),
                  pl.BlockSpec((TOP_K, R, D // 2), lambda i: (0, i, 0)),
                  full(ws13), full(ws2), full(ln_g), full(ln_b)],
        out_specs=pl.BlockSpec((R, D), lambda i: (i, 0)),
        out_shape=jax.ShapeDtypeStruct((n, D), F32),
        compiler_params=_cparams("parallel"),
        name="moe_combine",
    )(h, gate, yt, ws13, ws2, ln_g, ln_b)


def _moe(layer, alpha, h, h_pk, w_router, b_router, w_e1, w_e3, w_e2, ws13, ws2, ln_g, ln_b):
    n, D = h.shape
    E = w_router.shape[1]
    BM = MOE_BLOCK
    idx, gate, rank, cnt = _router(h, w_router, b_router)
    counts = cnt[0].astype(jnp.int32)
    padded = (counts + BM - 1) // BM * BM
    end = jnp.cumsum(padded)
    start = (end - padded).astype(jnp.int32)
    n_blocks = (n * TOP_K + BM - 1) // BM + E
    first_slot = jnp.arange(n_blocks, dtype=jnp.int32)[:, None] * BM
    block_e = jnp.minimum(jnp.sum((end[None, :] <= first_slot).astype(jnp.int32), axis=1), E - 1)
    n_used = (end[-1] // BM).astype(jnp.int32).reshape(1)
    dest = _dest(idx, rank, start.astype(F32).reshape(1, E))
    n_slots = n_blocks * BM
    e_i = jnp.arange(E, dtype=jnp.int32)[:, None]
    j_i = jnp.arange(BM, dtype=jnp.int32)[None, :]
    pad = (start + counts)[:, None] + j_i
    pad_slots = jnp.where(pad < end.astype(jnp.int32)[:, None], pad, n_slots + e_i * BM + j_i)
    xg = _sc_dispatch(h_pk, dest[:, :TOP_K], pad_slots, n_slots + E * BM)
    yg = _experts(layer, block_e, n_used, xg, w_e1, w_e3, w_e2)
    dest_km = jnp.swapaxes(dest[:, :TOP_K], 0, 1).reshape(-1)
    yt = _sc_gather(yg, dest_km).reshape(TOP_K, n, D // 2)
    return _combine(alpha, h, gate, yt, ws13, ws2, ln_g, ln_b)


def kernel(x_prompt, x_sample, state_wkv, state_shift, state_conv, cache_mem_k, cache_mem_v, mem_prompt, w_in, w_in_vmix, mu_shift, mu_vmix, w0, w_up, a0, a_up, v0, v_up, g_up, k_k, k_a, r_k, lnx_g, lnx_b, w_o_a, conv_w, conv_b, cn_g, cn_b, w_o_b, w_mem_kv, w_o_c, w_out, ln1_g, ln1_b, ln2_g, ln2_b, w_router, b_router, w_e1, w_e3, w_e2, w_s1, w_s3, w_s2):
    B, T, D = x_prompt.shape
    SB, ST, _ = x_sample.shape
    L = w_in.shape[0]
    M = mem_prompt.shape[1]
    R = ROW_TILE
    n_p, n_s = B * T, SB * ST
    n = n_p + n_s
    assert n_s == R and SB == LANES and T % R == 0 and T % CHUNK == 0 and D % LANES == 0
    npt, tps = n_p // R, T // R
    alpha = (2.0 * L) ** 0.25
    off_glu = RWKV_COLS
    off_q = off_glu + 2 * CONV_CH
    off_gate = off_q + MEM_WIDTH
    in_cols = off_gate + 3 * D

    x = jnp.concatenate([x_prompt.reshape(n_p, D), jnp.swapaxes(x_sample, 0, 1).reshape(n_s, D)], axis=0)
    mem_rows = mem_prompt.reshape(B * M, D)
    row2 = lambda a: a.reshape(1, -1)
    v_first = None
    outs = {k: [] for k in ('wkv_p', 'shift_p', 'conv_p', 'mk', 'mv', 'wkv_s', 'shift_s', 'conv_s')}

    for l in range(L):
        first = l == 0
        w_in_l = w_in[l].astype(BF16)
        w_rwkv = w_in_l[:, :off_glu]
        groups = [w_rwkv, w_in_l[:, off_glu:off_q], w_in_l[:, off_q:off_gate], w_in_l[:, off_gate:in_cols]]
        group_dtypes = [F32, BF16, F32, BF16]
        if not first:
            wv_pad = jnp.pad(w_in_vmix[l - 1], ((0, 0), (0, LANES - LORA_V))).astype(BF16)
            groups.append(wv_pad)
            group_dtypes.append(F32)
        proj = _project(x, groups, group_dtypes)
        p_rwkv, p_glu, p_q, p_gate = proj[:4]
        x_prev_s = state_shift[l]
        prev_rwkv = _matmul(x_prev_s, w_rwkv, SB, RWKV_COLS // 2)

        wup_pad = jnp.concatenate([w_up[l], jnp.zeros((LORA_A, RW), F32)], axis=0).astype(BF16)
        aup_pad = jnp.concatenate([jnp.zeros((LORA_W, RW), F32), a_up[l]], axis=0).astype(BF16)
        lora = (row2(w0[l]), wup_pad, row2(a0[l]), aup_pad, g_up[l].astype(BF16), row2(k_k[l]), row2(k_a[l]),
                row2(r_k[l]), _seg_ones())
        if first:
            pre = _rwkv_pre(True, npt, tps, p_rwkv, prev_rwkv, row2(mu_shift[l]), lora)
        else:
            p_vd = proj[4]
            prev_vd = _matmul(x_prev_s, wv_pad, SB, LANES)
            mu_v = jnp.pad(mu_vmix[l - 1], (0, LANES - LORA_V)).reshape(1, LANES)
            vup_pad = jnp.pad(v_up[l - 1], ((0, LANES - LORA_V), (0, 0))).astype(BF16)
            pre = _rwkv_pre(False, npt, tps, p_rwkv, prev_rwkv, row2(mu_shift[l]), lora, p_vd, prev_vd, mu_v,
                            v_first, row2(v0[l - 1]), vup_pad)
        r_, lw_, k_, v_, nkk_, bb_, g_, bonus_ = pre
        if first:
            v_first = v_

        y_p, s_p = _wkv_prompt(B, T, r_, lw_, k_, v_, nkk_, bb_)

        def lanes_b(a):
            return jnp.transpose(a[n_p:].reshape(ST, SB, RW), (0, 2, 1)).reshape(ST, HEADS, HEAD_DIM, SB)

        state_sz = HEADS * HEAD_DIM * HEAD_DIM
        s0 = _transpose_layer(state_wkv.astype(F32).reshape(L, SB, state_sz), l, 2048)
        s0 = s0.reshape(HEADS, HEAD_DIM, HEAD_DIM, SB)
        y_s, s_s = _wkv_sample(s0, lanes_b(r_), lanes_b(lw_), lanes_b(k_),
                               lanes_b(v_).reshape(ST, HEADS, HEAD_DIM, 1, SB), lanes_b(nkk_), lanes_b(bb_))
        y_s = jnp.transpose(y_s.reshape(ST, RW, SB), (0, 2, 1)).reshape(n_s, RW)
        s_s = _transpose_2d(s_s.reshape(state_sz, SB), 2048).reshape(SB, HEADS, HEAD_DIM, HEAD_DIM)

        cvec = (conv_w[l], row2(conv_b[l]), row2(cn_g[l]), row2(cn_b[l]))
        zb_p, conv_p = _conv_prompt(B, T, p_glu, *cvec)
        st_t = jnp.swapaxes(state_conv[l], 0, 1)
        zb_s, glu_s = _conv_sample(st_t, p_glu[n_p:].reshape(ST, SB, 2 * CONV_CH), *cvec)
        zb_s = zb_s.reshape(n_s, CONV_CH)
        conv_s = jnp.concatenate([state_conv[l][:, ST:], jnp.swapaxes(glu_s, 0, 1)], axis=1)

        kv_p = _matmul(mem_rows, w_mem_kv[l].astype(BF16), M, 2 * MEM_WIDTH)
        kv3 = kv_p.reshape(B, M, 2 * MEM_WIDTH)
        zc_p = _attention_prompt(p_q, kv3, B, T, R)
        q_s = jnp.swapaxes(p_q[n_p:].reshape(ST, SB, MEM_WIDTH), 0, 1)
        q_s = jnp.pad(q_s, ((0, 0), (0, 8 - ST), (0, 0)))
        zc_s = _attention_cached(q_s, cache_mem_k.reshape(L, SB, M * MEM_HEADS, MEM_HEAD_DIM),
                                 cache_mem_v.reshape(L, SB, M * MEM_HEADS, MEM_HEAD_DIM), l, 8)[:, :ST]
        zc_s = jnp.swapaxes(zc_s, 0, 1).reshape(n_s, MEM_WIDTH)

        h, h_pk = _merge(n_p, alpha, y_p, y_s, bonus_, g_, row2(lnx_g[l]), row2(lnx_b[l]),
                         zb_p, zb_s, zc_p, zc_s, p_gate, x,
                         w_o_a[l].astype(BF16), w_o_b[l].astype(BF16), w_o_c[l].astype(BF16),
                         w_out[l].astype(BF16), row2(ln1_g[l]), row2(ln1_b[l]))

        ws13 = jnp.concatenate([w_s1[l], w_s3[l]], axis=1).astype(BF16)
        x_new = _moe(l, alpha, h, h_pk, w_router[l].astype(BF16), row2(b_router[l]), w_e1, w_e3, w_e2,
                     ws13, w_s2[l].astype(BF16), row2(ln2_g[l]), row2(ln2_b[l]))

        outs['wkv_p'].append(s_p.astype(state_wkv.dtype))
        outs['shift_p'].append(x[T - 1:n_p:T].astype(state_shift.dtype))
        outs['conv_p'].append(conv_p.astype(state_conv.dtype))
        outs['mk'].append(kv3[..., :MEM_WIDTH].reshape(B, M, MEM_HEADS, MEM_HEAD_DIM).astype(cache_mem_k.dtype))
        outs['mv'].append(kv3[..., MEM_WIDTH:].reshape(B, M, MEM_HEADS, MEM_HEAD_DIM).astype(cache_mem_v.dtype))
        outs['wkv_s'].append(s_s.astype(state_wkv.dtype))
        outs['shift_s'].append(x[n_p + (ST - 1) * SB:].astype(state_shift.dtype))
        outs['conv_s'].append(conv_s.astype(state_conv.dtype))
        x = x_new

    y_p = x[:n_p].reshape(B, T, D)
    y_s = jnp.swapaxes(x[n_p:].reshape(ST, SB, D), 0, 1)
    return (y_p, y_s, jnp.stack(outs['wkv_p']), jnp.stack(outs['shift_p']), jnp.stack(outs['conv_p']),
            jnp.stack(outs['mk']), jnp.stack(outs['mv']), jnp.stack(outs['wkv_s']), jnp.stack(outs['shift_s']),
            jnp.stack(outs['conv_s']))
```

```python
import functools
import math

import jax
import jax.numpy as jnp
from jax import lax
from jax.experimental import pallas as pl
from jax.experimental.pallas import tpu as pltpu
from jax.experimental.pallas import tpu_sc as plsc

F32 = jnp.float32
BF16 = jnp.bfloat16

HEADS = 8
HEAD_DIM = 64
RW = HEADS * HEAD_DIM
LORA_W = 64
LORA_A = 64
LORA_V = 32
LORA_G = 128
RWKV_COLS = 3 * RW + LORA_W + LORA_A + LORA_G
GN_EPS = HEAD_DIM * 1e-5
CONV_WIDTH = 31
CONV_BUF = CONV_WIDTH - 1
CONV_CH = 512
MEM_HEADS = 4
MEM_HEAD_DIM = 128
MEM_WIDTH = MEM_HEADS * MEM_HEAD_DIM
TOP_K = 8
ROUTED_SCALE = 2.5
LN_EPS = 1e-5
EXP_M05 = math.exp(-0.5)

LANES = 128
ROW_TILE = 512
MERGE_TILE = 256
CHUNK = 64
NEUMANN_BLOCK = 16
MOE_BLOCK = 512
WKV_BATCH_PER_STEP = 4
ROUTER_SUB = 64
PROJ_TILE = 256
SC_CORES = 2
SC_SUBCORES = 16
SC_WINDOW = 128
SC_DISPATCH_WINDOW = 48
SC_PAD_WINDOW = 64
COMBINE_TILE = 256
VMEM_LIMIT = 48 * 1024 * 1024


def _cparams(*sem, vmem=VMEM_LIMIT):
    return pltpu.CompilerParams(dimension_semantics=tuple(sem), vmem_limit_bytes=vmem)


def _bdot(a, b):
    return jnp.dot(a.astype(BF16), b.astype(BF16), preferred_element_type=F32)


def _bdot_nt(a, b):
    return lax.dot_general(a.astype(BF16), b.astype(BF16), (((1,), (1,)), ((), ())),
                           preferred_element_type=F32)


def _sigmoid(x):
    return 1.0 / (1.0 + jnp.exp(-x))


def _silu(x):
    return x * _sigmoid(x)


def _layer_norm(x, g, b):
    mu = jnp.mean(x, axis=-1, keepdims=True)
    d = x - mu
    var = jnp.mean(d * d, axis=-1, keepdims=True)
    return d * lax.rsqrt(var + LN_EPS) * g + b


SEG_LANES = 256


def _seg_ones():
    i = jnp.arange(SEG_LANES)
    return ((i[:, None] // HEAD_DIM) == (i[None, :] // HEAD_DIM)).astype(BF16)


def _seg_sum(x, seg):
    hi = x.astype(BF16)
    lo = (x - hi.astype(F32)).astype(BF16)
    parts = []
    for j in range(x.shape[-1] // SEG_LANES):
        sl = slice(j * SEG_LANES, (j + 1) * SEG_LANES)
        parts.append(jnp.dot(hi[:, sl], seg, preferred_element_type=F32)
                     + jnp.dot(lo[:, sl], seg, preferred_element_type=F32))
    return jnp.concatenate(parts, axis=-1)


def _pack_pair(lo, hi):
    lo_b = lax.bitcast_convert_type(lo.astype(BF16).astype(F32), jnp.int32)
    hi_b = lax.bitcast_convert_type(hi.astype(BF16).astype(F32), jnp.int32)
    return lax.shift_right_logical(lo_b, jnp.full(lo_b.shape, 16, jnp.int32)) | (hi_b & jnp.int32(-65536))


def _unpack_pair(p):
    lo = lax.bitcast_convert_type(lax.shift_left(p, jnp.full(p.shape, 16, jnp.int32)), F32)
    hi = lax.bitcast_convert_type(p & jnp.int32(-65536), F32)
    return lo, hi


def _mm_kernel(x_ref, w_ref, o_ref):
    o_ref[...] = _bdot(x_ref[...], w_ref[...]).astype(o_ref.dtype)


def _matmul(x, w, tm, tn, out_dtype=F32):
    m, k = x.shape
    n_cols = w.shape[1]
    assert m % tm == 0 and n_cols % tn == 0
    return pl.pallas_call(
        _mm_kernel,
        grid=(m // tm, n_cols // tn),
        in_specs=[pl.BlockSpec((tm, k), lambda i, j: (i, 0)),
                  pl.BlockSpec((k, tn), lambda i, j: (0, j))],
        out_specs=pl.BlockSpec((tm, tn), lambda i, j: (i, j)),
        out_shape=jax.ShapeDtypeStruct((m, n_cols), out_dtype),
        compiler_params=_cparams("parallel", "parallel"),
        name=f"matmul_{n_cols}",
    )(x, w)


def _proj_kernel(n_out, x_ref, *refs):
    xb = x_ref[...].astype(BF16)
    for w_ref, o_ref in zip(refs[:n_out], refs[n_out:]):
        o_ref[...] = jnp.dot(xb, w_ref[...], preferred_element_type=F32).astype(o_ref.dtype)


def _project(x, weights, out_dtypes):
    m, k = x.shape
    tm = PROJ_TILE
    assert m % tm == 0
    return pl.pallas_call(
        functools.partial(_proj_kernel, len(weights)),
        grid=(m // tm,),
        in_specs=[pl.BlockSpec((tm, k), lambda i: (i, 0))]
        + [pl.BlockSpec(w.shape, lambda i: (0, 0), pipeline_mode=pl.Buffered(1)) for w in weights],
        out_specs=[pl.BlockSpec((tm, w.shape[1]), lambda i: (i, 0)) for w in weights],
        out_shape=[jax.ShapeDtypeStruct((m, w.shape[1]), dt) for w, dt in zip(weights, out_dtypes)],
        compiler_params=_cparams("parallel"),
        name="in_proj",
    )(x, *weights)


def _pre_kernel(first, npt, tps, *refs):
    if first:
        (p_ref, halo_ref, prev_ref, mu_ref,
         w0_ref, wup_ref, a0_ref, aup_ref, gup_ref, kk_ref, ka_ref, rk_ref, seg_ref,
         r_o, lw_o, k_o, v_o, nkk_o, bb_o, g_o, bon_o, sh_ref) = refs
    else:
        (p_ref, halo_ref, prev_ref, mu_ref, pv_ref, halov_ref, prevv_ref, muv_ref, vf_ref, v0_ref, vup_ref,
         w0_ref, wup_ref, a0_ref, aup_ref, gup_ref, kk_ref, ka_ref, rk_ref, seg_ref,
         r_o, lw_o, k_o, v_o, nkk_o, bb_o, g_o, bon_o, sh_ref, shv_ref) = refs
    i = pl.program_id(0)
    rows_n = p_ref.shape[0]

    def build_shift(src_ref, halo_r, prev_r, dst_ref):
        @pl.when(i < npt)
        def _():
            x = src_ref[...]
            rolled = pltpu.roll(x, 1, axis=0)
            last = halo_r[7:8, :]
            row0 = jnp.where((i % tps) == 0, jnp.zeros_like(last), last)
            rows = lax.broadcasted_iota(jnp.int32, x.shape, 0)
            dst_ref[...] = jnp.where(rows == 0, row0, rolled)

        @pl.when(i >= npt)
        def _():
            nb = prev_r.shape[0]
            dst_ref[0:nb, :] = prev_r[...]
            dst_ref[nb:rows_n, :] = src_ref[0:rows_n - nb, :]

    build_shift(p_ref, halo_ref, prev_ref, sh_ref)
    p = p_ref[...]
    p = p + (sh_ref[...] - p) * mu_ref[...]
    r = p[:, 0:RW]
    k = p[:, RW:2 * RW]
    v = p[:, 2 * RW:3 * RW]
    u = p[:, 3 * RW:3 * RW + LORA_W + LORA_A]
    gd = p[:, 3 * RW + LORA_W + LORA_A:RWKV_COLS]

    z = w0_ref[...] + _bdot(jnp.tanh(u), wup_ref[...])
    lw = -EXP_M05 * _sigmoid(z)
    a = _sigmoid(a0_ref[...] + _bdot(u, aup_ref[...]))
    g = _bdot(_sigmoid(gd), gup_ref[...])
    if not first:
        build_shift(pv_ref, halov_ref, prevv_ref, shv_ref)
        pv = pv_ref[...]
        pv = pv + (shv_ref[...] - pv) * muv_ref[...]
        mix = _sigmoid(v0_ref[...] + _bdot(pv, vup_ref[...]))
        v = v + (vf_ref[...] - v) * mix
    kk = k * kk_ref[...]
    seg = seg_ref[...]
    nrm = jnp.sqrt(_seg_sum(kk * kk, seg))
    kkn = kk / jnp.maximum(nrm, 1e-12)
    k2 = k * (1.0 + (a - 1.0) * ka_ref[...])
    bonus = _seg_sum(r * k2 * rk_ref[...], seg) * v
    r_o[...] = r
    lw_o[...] = lw
    k_o[...] = k2
    v_o[...] = v
    nkk_o[...] = -kkn
    bb_o[...] = kkn * a
    g_o[...] = g.astype(g_o.dtype)
    bon_o[...] = bonus.astype(bon_o.dtype)


def _rwkv_pre(first, npt, tps, p_rwkv, prev_rwkv, mu, lora, p_vd=None, prev_vd=None, mu_v=None, v_first=None,
              v0=None, vup=None):
    n = p_rwkv.shape[0]
    R = ROW_TILE
    nt = n // R

    def row_spec(w):
        return pl.BlockSpec((R, w), lambda i: (i, 0))

    def halo_spec(w):
        return pl.BlockSpec((8, w), lambda i: (jnp.maximum(i * (R // 8) - 1, 0), 0))

    def full(a):
        return pl.BlockSpec(a.shape, lambda i: (0,) * a.ndim)

    ins = [p_rwkv, p_rwkv, prev_rwkv, mu]
    specs = [row_spec(RWKV_COLS), halo_spec(RWKV_COLS), full(prev_rwkv), full(mu)]
    scratch = [pltpu.VMEM((R, RWKV_COLS), F32)]
    if not first:
        ins += [p_vd, p_vd, prev_vd, mu_v, v_first, v0, vup]
        specs += [row_spec(LANES), halo_spec(LANES), full(prev_vd), full(mu_v), row_spec(RW), full(v0), full(vup)]
        scratch.append(pltpu.VMEM((R, LANES), F32))
    ins += list(lora)
    specs += [full(a) for a in lora]
    outs = pl.pallas_call(
        functools.partial(_pre_kernel, first, npt, tps),
        grid=(nt,),
        in_specs=specs,
        out_specs=[row_spec(RW)] * 8,
        out_shape=[jax.ShapeDtypeStruct((n, RW), F32)] * 6 + [jax.ShapeDtypeStruct((n, RW), BF16)] * 2,
        scratch_shapes=scratch,
        compiler_params=_cparams("parallel"),
        name="rwkv_pre",
    )(*ins)
    return outs


def _wkv_chunk_kernel(nbat, *refs):
    in_refs = refs[:6 * nbat]
    y_ref, s_out_ref, s_ref = refs[6 * nbat:]
    c = pl.program_id(1)
    C = in_refs[0].shape[0]
    G = 4 * HEAD_DIM
    assert 4 * C == G

    @pl.when(c == 0)
    def _():
        s_ref[...] = jnp.zeros_like(s_ref)

    ri = lax.broadcasted_iota(jnp.int32, (G, G), 0)
    ci = lax.broadcasted_iota(jnp.int32, (G, G), 1)
    same_head = (ri // C) == (ci // HEAD_DIM)
    tril_s = same_head & ((ci % C) < (ri % C))
    tril_i = same_head & ((ci % C) <= (ri % C))
    diag_blk = (ri // NEUMANN_BLOCK) == (ci // NEUMANN_BLOCK)
    eye = (ri == ci).astype(F32)

    def stack(x):
        return jnp.where(same_head, jnp.concatenate([x, x, x, x], axis=0), 0.0).astype(BF16)

    def tile4(x):
        return jnp.concatenate([x, x, x, x], axis=0).astype(BF16)

    def collapse(x4):
        return x4[0:C] + x4[C:2 * C] + x4[2 * C:3 * C] + x4[3 * C:4 * C]

    for b in range(nbat):
        r_ref, lw_ref, k_ref, v_ref, nkk_ref, bb_ref = in_refs[6 * b:6 * b + 6]
        lw = lw_ref[...]
        rows = lax.broadcasted_iota(jnp.int32, lw.shape, 0)
        cs = lw
        s = 1
        while s < C:
            cs = cs + jnp.where(rows >= s, pltpu.roll(cs, s, axis=0), 0.0)
            s *= 2
        cs_last = cs[C - 1:C, :]
        e_neg = jnp.exp(-cs)
        e_tail = jnp.exp(cs_last - cs)
        kk_ = k_ref[...]
        bb_ = bb_ref[...]
        at = nkk_ref[...] * jnp.exp(cs - lw)
        rt = r_ref[...] * jnp.exp(cs)
        bt = bb_ * e_neg
        kt = kk_ * e_neg
        bw = bb_ * e_tail
        kw = kk_ * e_tail
        vv = v_ref[...]
        w_end = jnp.exp(cs_last)

        for g in range(RW // G):
            sl = slice(g * G, (g + 1) * G)
            at4, rt4, v4 = stack(at[:, sl]), stack(rt[:, sl]), stack(vv[:, sl])
            bt4, kt4 = tile4(bt[:, sl]), tile4(kt[:, sl])
            a_ab = jnp.where(tril_s, _bdot_nt(at4, bt4), 0.0)
            a_ak = jnp.where(tril_s, _bdot_nt(at4, kt4), 0.0).astype(BF16)
            a_rb = jnp.where(tril_i, _bdot_nt(rt4, bt4), 0.0).astype(BF16)
            a_rk = jnp.where(tril_i, _bdot_nt(rt4, kt4), 0.0).astype(BF16)
            a_d = jnp.where(diag_blk, a_ab, 0.0)
            a_o = a_ab - a_d
            a_db = a_d.astype(BF16)
            x2 = _bdot(a_db, a_db).astype(BF16)
            x4 = _bdot(x2, x2).astype(BF16)
            x8 = _bdot(x4, x4)
            t_d = eye + a_d
            t_d = t_d + _bdot(t_d, x2)
            t_d = t_d + _bdot(t_d, x4)
            t_d = t_d + _bdot(t_d, x8)
            t_db = t_d.astype(BF16)
            nn = _bdot(t_db, a_o).astype(BF16)
            n2 = _bdot(nn, nn)
            m1 = t_d + _bdot(nn, t_db)
            t_m = m1 + _bdot(n2, m1)
            s4 = s_ref[b, g]
            s4b = s4.astype(BF16)
            u4 = _bdot(t_m, _bdot_nt(at4, s4b) + _bdot(a_ak, v4))
            y4 = _bdot_nt(rt4, s4b) + _bdot(jnp.concatenate([a_rb, a_rk], axis=1),
                                            jnp.concatenate([u4.astype(BF16), v4], axis=0))
            y_ref[b, :, sl] = collapse(y4)
            uv_t = jnp.concatenate([collapse(u4), vv[:, sl]], axis=0).T
            upd = _bdot(uv_t, jnp.concatenate([bw[:, sl], kw[:, sl]], axis=0))
            s_ref[b, g] = s4 * w_end[:, sl] + jnp.where((ri // HEAD_DIM) == (ci // HEAD_DIM), upd, 0.0)

    @pl.when(c == pl.num_programs(1) - 1)
    def _():
        for b in range(nbat):
            for hd in range(HEADS):
                g, j = divmod(hd, 4)
                blk = slice(j * HEAD_DIM, (j + 1) * HEAD_DIM)
                s_out_ref[b, hd] = s_ref[b, g, blk, blk]


def _wkv_prompt(B, T, r, lw, k, v, nkk, bb):
    C = CHUNK
    nc = T // C
    G = 4 * HEAD_DIM
    nbat = math.gcd(B, WKV_BATCH_PER_STEP)
    specs = []
    for j in range(nbat):
        specs += [pl.BlockSpec((C, RW), lambda i, c, j=j: ((nbat * i + j) * nc + c, 0))] * 6
    y, s = pl.pallas_call(
        functools.partial(_wkv_chunk_kernel, nbat),
        grid=(B // nbat, nc),
        in_specs=specs,
        out_specs=[pl.BlockSpec((nbat, C, RW), lambda i, c: (i, c, 0)),
                   pl.BlockSpec((nbat, HEADS, HEAD_DIM, HEAD_DIM), lambda i, c: (i, 0, 0, 0))],
        out_shape=[jax.ShapeDtypeStruct((B, T, RW), F32),
                   jax.ShapeDtypeStruct((B, HEADS, HEAD_DIM, HEAD_DIM), F32)],
        scratch_shapes=[pltpu.VMEM((nbat, RW // G, G, G), F32)],
        compiler_params=_cparams("parallel", "arbitrary"),
        name="wkv_chunk",
    )(*([r, lw, k, v, nkk, bb] * nbat))
    return y.reshape(B * T, RW), s


def _wkv_seq_kernel(s0_ref, r_ref, lw_ref, k_ref, v_ref, nkk_ref, bb_ref, y_ref, s_ref):
    steps = r_ref.shape[0]

    def body(vi, carry):
        st = s0_ref[0, vi]
        for t in range(steps):
            sa = jnp.sum(st * nkk_ref[t, 0], axis=0, keepdims=True)
            st = st * jnp.exp(lw_ref[t, 0]) + sa * bb_ref[t, 0] + v_ref[t, 0, vi] * k_ref[t, 0]
            y_ref[t, 0, vi] = jnp.sum(st * r_ref[t, 0], axis=0, keepdims=True)
        s_ref[0, vi] = st
        return carry

    lax.fori_loop(0, HEAD_DIM, body, 0)


def _wkv_sample(s0, r, lw, k, v, nkk, bb):
    steps, _, _, nb = r.shape
    vec = pl.BlockSpec((steps, 1, HEAD_DIM, nb), lambda h: (0, h, 0, 0))
    vcol = pl.BlockSpec((steps, 1, HEAD_DIM, 1, nb), lambda h: (0, h, 0, 0, 0))
    st = pl.BlockSpec((1, HEAD_DIM, HEAD_DIM, nb), lambda h: (h, 0, 0, 0))
    y, s = pl.pallas_call(
        _wkv_seq_kernel,
        grid=(HEADS,),
        in_specs=[st, vec, vec, vec, vcol, vec, vec],
        out_specs=[vcol, st],
        out_shape=[jax.ShapeDtypeStruct((steps, HEADS, HEAD_DIM, 1, nb), F32),
                   jax.ShapeDtypeStruct(s0.shape, F32)],
        compiler_params=_cparams("parallel"),
        name="wkv_seq",
    )(s0, r, lw, k, v, nkk, bb)
    return y, s


def _tr_kernel(x_ref, o_ref):
    o_ref[...] = x_ref[...].T


def _transpose_layer(x3, layer, tc):
    _, a, c = x3.shape
    return pl.pallas_call(
        _tr_kernel,
        grid=(c // tc,),
        in_specs=[pl.BlockSpec((None, a, tc), lambda j: (layer, 0, j))],
        out_specs=pl.BlockSpec((tc, a), lambda j: (j, 0)),
        out_shape=jax.ShapeDtypeStruct((c, a), x3.dtype),
        compiler_params=_cparams("parallel"),
        name="transpose_in",
    )(x3)


def _transpose_2d(x, tc):
    c, a = x.shape
    return pl.pallas_call(
        _tr_kernel,
        grid=(c // tc,),
        in_specs=[pl.BlockSpec((tc, a), lambda j: (j, 0))],
        out_specs=pl.BlockSpec((a, tc), lambda j: (0, j)),
        out_shape=jax.ShapeDtypeStruct((a, c), x.dtype),
        compiler_params=_cparams("parallel"),
        name="transpose_out",
    )(x)


def _glu(p):
    p = p.astype(F32)
    return p[..., :CONV_CH] * _sigmoid(p[..., CONV_CH:])


def _conv_prompt_kernel(p_ref, halo_ref, w_ref, cb_ref, g_ref, b_ref, o_ref, tail_ref, ext_ref, sh_ref):
    j = pl.program_id(1)
    R = p_ref.shape[0]
    H = halo_ref.shape[0]
    glu = _glu(p_ref[...])
    halo = _glu(halo_ref[...])
    ext_ref[0:H, :] = jnp.where(j == 0, jnp.zeros_like(halo), halo)
    ext_ref[H:H + R, :] = glu
    span = sh_ref.shape[1]
    for ph in range(1, 8):
        sh_ref[ph] = ext_ref[ph:ph + span, :]
    acc = jnp.zeros((R, CONV_CH), F32)
    for t in range(CONV_WIDTH):
        off = H - CONV_BUF + t
        q8 = (off // 8) * 8
        tap = ext_ref[q8:q8 + R, :] if off % 8 == 0 else sh_ref[off % 8, q8:q8 + R, :]
        acc = acc + tap * w_ref[t:t + 1, :]
    h = _layer_norm(acc + cb_ref[...], g_ref[...], b_ref[...])
    o_ref[...] = _silu(h)

    @pl.when(j == pl.num_programs(1) - 1)
    def _():
        tail_ref[0] = glu[R - H:R, :]


def _conv_prompt(B, T, p_glu, conv_w, conv_b, cn_g, cn_b):
    R = ROW_TILE
    H = 32
    tps = T // R
    vec = pl.BlockSpec((1, CONV_CH), lambda b, j: (0, 0))
    z, tail = pl.pallas_call(
        _conv_prompt_kernel,
        grid=(B, tps),
        in_specs=[pl.BlockSpec((R, 2 * CONV_CH), lambda b, j: (b * tps + j, 0)),
                  pl.BlockSpec((H, 2 * CONV_CH), lambda b, j: (jnp.maximum((b * tps + j) * (R // H) - 1, 0), 0)),
                  pl.BlockSpec((CONV_WIDTH, CONV_CH), lambda b, j: (0, 0)), vec, vec, vec],
        out_specs=[pl.BlockSpec((R, CONV_CH), lambda b, j: (b * tps + j, 0)),
                   pl.BlockSpec((1, H, CONV_CH), lambda b, j: (b, 0, 0))],
        out_shape=[jax.ShapeDtypeStruct((B * T, CONV_CH), F32),
                   jax.ShapeDtypeStruct((B, H, CONV_CH), F32)],
        scratch_shapes=[pltpu.VMEM((H + R, CONV_CH), F32), pltpu.VMEM((8, H + R - 8, CONV_CH), F32)],
        compiler_params=_cparams("parallel", "arbitrary"),
        name="conv_prompt",
    )(p_glu, p_glu, conv_w, conv_b, cn_g, cn_b)
    return z, tail[:, H - CONV_BUF:, :]


def _conv_sample_kernel(st_ref, p_ref, w_ref, cb_ref, g_ref, b_ref, o_ref, glu_ref):
    steps = p_ref.shape[0]
    glu = [_glu(p_ref[t]) for t in range(steps)]
    for t in range(steps):
        glu_ref[t] = glu[t]
    for t in range(steps):
        acc = jnp.zeros(glu[0].shape, F32)
        for j in range(CONV_WIDTH):
            src = t + j
            x = st_ref[src] if src < CONV_BUF else glu[src - CONV_BUF]
            acc = acc + x * w_ref[j:j + 1, :]
        h = _layer_norm(acc + cb_ref[...], g_ref[...], b_ref[...])
        o_ref[t] = _silu(h)


def _conv_sample(state_t, p_glu, conv_w, conv_b, cn_g, cn_b):
    steps, nb, _ = p_glu.shape

    def full(a):
        return pl.BlockSpec(a.shape, lambda i: (0,) * a.ndim)

    ins = (state_t, p_glu, conv_w, conv_b, cn_g, cn_b)
    return pl.pallas_call(
        _conv_sample_kernel,
        grid=(1,),
        in_specs=[full(a) for a in ins],
        out_specs=[pl.BlockSpec((steps, nb, CONV_CH), lambda i: (0, 0, 0))] * 2,
        out_shape=[jax.ShapeDtypeStruct((steps, nb, CONV_CH), F32)] * 2,
        compiler_params=_cparams("arbitrary"),
        name="conv_sample",
    )(*ins)


def _attn_kernel(q_ref, k_ref, v_ref, o_ref):
    scale = MEM_HEAD_DIM ** -0.5
    for h in range(MEM_HEADS):
        sl = slice(h * MEM_HEAD_DIM, (h + 1) * MEM_HEAD_DIM)
        s = _bdot_nt(q_ref[:, sl], k_ref[:, sl]) * scale
        m = jnp.max(s, axis=-1, keepdims=True)
        e = jnp.exp(s - m)
        pr = e / jnp.sum(e, axis=-1, keepdims=True)
        o_ref[:, sl] = _bdot(pr, v_ref[:, sl])


def _attn_rows_kernel(q_ref, k_ref, v_ref, o_ref):
    nb, tq, width = q_ref.shape
    n_mem = k_ref.shape[1] // MEM_HEADS
    scale = MEM_HEAD_DIM ** -0.5
    ri = lax.broadcasted_iota(jnp.int32, (MEM_HEADS * tq, width), 0)
    ci = lax.broadcasted_iota(jnp.int32, (MEM_HEADS * tq, width), 1)
    own = (ri // tq) == (ci // MEM_HEAD_DIM)
    for bi in range(nb):
        k_all = jnp.concatenate([k_ref[bi, pl.ds(h, n_mem, stride=MEM_HEADS), :].astype(BF16)
                                 for h in range(MEM_HEADS)], axis=1)
        v_all = jnp.concatenate([v_ref[bi, pl.ds(h, n_mem, stride=MEM_HEADS), :].astype(BF16)
                                 for h in range(MEM_HEADS)], axis=1)
        q = q_ref[bi]
        q4 = jnp.where(own, jnp.concatenate([q] * MEM_HEADS, axis=0), 0.0)
        s = _bdot_nt(q4, k_all) * scale
        m = jnp.max(s, axis=-1, keepdims=True)
        e = jnp.exp(s - m)
        pr = e / jnp.sum(e, axis=-1, keepdims=True)
        pv = jnp.where(own, _bdot(pr, v_all), 0.0)
        o = pv[0:tq]
        for h in range(1, MEM_HEADS):
            o = o + pv[h * tq:(h + 1) * tq]
        o_ref[bi] = o


def _attention_cached(q, cache_k, cache_v, layer, bb):
    B, T, W = q.shape
    rows = cache_k.shape[2]
    kv_spec = pl.BlockSpec((None, bb, rows, MEM_HEAD_DIM), lambda b: (layer, b, 0, 0))
    return pl.pallas_call(
        _attn_rows_kernel,
        grid=(B // bb,),
        in_specs=[pl.BlockSpec((bb, T, W), lambda b: (b, 0, 0)), kv_spec, kv_spec],
        out_specs=pl.BlockSpec((bb, T, W), lambda b: (b, 0, 0)),
        out_shape=jax.ShapeDtypeStruct((B, T, W), F32),
        compiler_params=_cparams("parallel"),
        name="mem_attention_cached",
    )(q, cache_k, cache_v)


def _attention_prompt(q_rows, kv, B, T, tq):
    W = q_rows.shape[1]
    M = kv.shape[1]
    tps = T // tq
    return pl.pallas_call(
        _attn_kernel,
        grid=(B, tps),
        in_specs=[pl.BlockSpec((tq, W), lambda b, j: (b * tps + j, 0)),
                  pl.BlockSpec((None, M, W), lambda b, j: (b, 0, 0)),
                  pl.BlockSpec((None, M, W), lambda b, j: (b, 0, 1))],
        out_specs=pl.BlockSpec((tq, W), lambda b, j: (b * tps + j, 0)),
        out_shape=jax.ShapeDtypeStruct((B * T, W), F32),
        compiler_params=_cparams("parallel", "parallel"),
        name="mem_attention_prompt",
    )(q_rows, kv, kv)


def _merge_kernel(npt, alpha, yp_ref, ys_ref, bon_ref, og_ref, lg_ref, lb_ref, seg_ref,
                  zbp_ref, zbs_ref, zcp_ref, zcs_ref, gate_ref, x_ref,
                  woa_ref, wob_ref, woc_ref, wout_ref, g_ref, b_ref, o_ref, pk_ref):
    i = pl.program_id(0)
    D = x_ref.shape[1]
    is_p = i < npt
    y = jnp.where(is_p, yp_ref[...], ys_ref[...])
    seg = seg_ref[...]
    mu = _seg_sum(y, seg) * (1.0 / HEAD_DIM)
    d = y - mu
    var = _seg_sum(d * d, seg) * (1.0 / HEAD_DIM)
    za = ((d * lax.rsqrt(var + GN_EPS) * lg_ref[...] + lb_ref[...] + bon_ref[...].astype(F32))
          * og_ref[...].astype(F32))
    zb = jnp.where(is_p, zbp_ref[...], zbs_ref[...])
    zc = jnp.where(is_p, zcp_ref[...], zcs_ref[...])
    merged = (_sigmoid(gate_ref[:, 0:D].astype(F32)) * _bdot(za, woa_ref[...])
              + _sigmoid(gate_ref[:, D:2 * D].astype(F32)) * _bdot(zb, wob_ref[...])
              + _sigmoid(gate_ref[:, 2 * D:3 * D].astype(F32)) * _bdot(zc, woc_ref[...]))
    h = _layer_norm(alpha * x_ref[...] + _bdot(merged, wout_ref[...]), g_ref[...], b_ref[...])
    o_ref[...] = h
    pk_ref[...] = _pack_pair(h[:, 0:D // 2], h[:, D // 2:D])


def _merge(n_p, alpha, y_p, y_s, bonus, out_gate, lnx_g, lnx_b, zb_p, zb_s, zc_p, zc_s, gates, x,
           woa, wob, woc, wout, ln_g, ln_b):
    n, D = x.shape
    seg = _seg_ones()
    R = MERGE_TILE
    npt = n_p // R

    def row(w):
        return pl.BlockSpec((R, w), lambda i: (i, 0))

    def prow(w):
        return pl.BlockSpec((R, w), lambda i: (jnp.minimum(i, npt - 1), 0))

    def srow(w):
        return pl.BlockSpec((R, w), lambda i: (jnp.maximum(i - npt, 0), 0))

    def full(a):
        return pl.BlockSpec(a.shape, lambda i: (0,) * a.ndim)

    return pl.pallas_call(
        functools.partial(_merge_kernel, npt, alpha),
        grid=(n // R,),
        in_specs=[prow(RW), srow(RW), row(RW), row(RW), full(lnx_g), full(lnx_b), full(seg),
                  prow(CONV_CH), srow(CONV_CH), prow(MEM_WIDTH), srow(MEM_WIDTH), row(3 * D), row(D),
                  full(woa), full(wob), full(woc), full(wout), full(ln_g), full(ln_b)],
        out_specs=[row(D), row(D // 2)],
        out_shape=[jax.ShapeDtypeStruct((n, D), F32), jax.ShapeDtypeStruct((n, D // 2), jnp.int32)],
        compiler_params=_cparams("parallel"),
        name="branch_merge",
    )(y_p, y_s, bonus, out_gate, lnx_g, lnx_b, seg, zb_p, zb_s, zc_p, zc_s, gates, x,
      woa, wob, woc, wout, ln_g, ln_b)


def _router_kernel(h_ref, wr_ref, br_ref, idx_o, gate_o, rank_o, cnt_o, carry_ref):
    i = pl.program_id(0)
    R = h_ref.shape[0]
    E = wr_ref.shape[1]

    @pl.when(i == 0)
    def _():
        carry_ref[...] = jnp.zeros_like(carry_ref)

    scores_all = _sigmoid(_bdot(h_ref[...], wr_ref[...]))
    S = ROUTER_SUB
    lane = lax.broadcasted_iota(jnp.int32, (S, E), 1).astype(F32)

    def top_k(scores):
        work = scores + br_ref[...]
        sel_mask = jnp.zeros((S, E), F32)
        idx_cols, sel_cols = [], []
        for _ in range(TOP_K):
            m = jnp.max(work, axis=-1, keepdims=True)
            first = jnp.min(jnp.where(work == m, lane, float(E)), axis=-1, keepdims=True)
            onehot = lane == first
            sel_cols.append(jnp.sum(jnp.where(onehot, scores, 0.0), axis=-1, keepdims=True))
            idx_cols.append(first)
            sel_mask = jnp.where(onehot, 1.0, sel_mask)
            work = jnp.where(onehot, -jnp.inf, work)
        return idx_cols, sel_cols, sel_mask

    picks = [top_k(scores_all[s * S:(s + 1) * S]) for s in range(R // S)]
    sel_mask = jnp.concatenate([p[2] for p in picks], axis=0)
    ri = lax.broadcasted_iota(jnp.int32, (R, R), 0)
    ci = lax.broadcasted_iota(jnp.int32, (R, R), 1)
    before = jnp.dot((ci < ri).astype(BF16), sel_mask.astype(BF16), preferred_element_type=F32) + carry_ref[0:1, :]
    for s, (idx_cols, sel_cols, _) in enumerate(picks):
        rows = slice(s * S, (s + 1) * S)
        total = sel_cols[0]
        for c in sel_cols[1:]:
            total = total + c
        idx_out = jnp.zeros((S, E), F32)
        gate_out = jnp.zeros((S, E), F32)
        rank_out = jnp.zeros((S, E), F32)
        for kk in range(TOP_K):
            onehot = lane == idx_cols[kk]
            rank = jnp.sum(jnp.where(onehot, before[rows], 0.0), axis=-1, keepdims=True)
            col = lane == float(kk)
            idx_out = jnp.where(col, idx_cols[kk], idx_out)
            gate_out = jnp.where(col, sel_cols[kk] / total * ROUTED_SCALE, gate_out)
            rank_out = jnp.where(col, rank, rank_out)
        idx_o[rows, :] = idx_out
        gate_o[rows, :] = gate_out
        rank_o[rows, :] = rank_out
    carry_ref[0:1, :] = carry_ref[0:1, :] + jnp.sum(sel_mask, axis=0, keepdims=True)
    cnt_o[...] = carry_ref[...]


def _router(h, w_router, b_router):
    n, D = h.shape
    E = w_router.shape[1]
    assert E == LANES
    R = 256
    row = pl.BlockSpec((R, E), lambda i: (i, 0))
    return pl.pallas_call(
        _router_kernel,
        grid=(n // R,),
        in_specs=[pl.BlockSpec((R, D), lambda i: (i, 0)),
                  pl.BlockSpec((D, E), lambda i: (0, 0)),
                  pl.BlockSpec((1, E), lambda i: (0, 0))],
        out_specs=[row, row, row, pl.BlockSpec((8, E), lambda i: (0, 0))],
        out_shape=[jax.ShapeDtypeStruct((n, E), F32), jax.ShapeDtypeStruct((n, E), F32),
                   jax.ShapeDtypeStruct((n, E), F32), jax.ShapeDtypeStruct((8, E), F32)],
        scratch_shapes=[pltpu.VMEM((8, E), F32)],
        compiler_params=_cparams("arbitrary"),
        name="moe_router",
    )(h, w_router, b_router)


def _dest_kernel(idx_ref, rank_ref, start_ref, o_ref):
    R, E = idx_ref.shape
    lane = lax.broadcasted_iota(jnp.int32, (R, E), 1).astype(F32)
    idx = idx_ref[...]
    out = jnp.zeros((R, E), F32)
    start = start_ref[...]
    for kk in range(TOP_K):
        e_k = jnp.sum(jnp.where(lane == float(kk), idx, 0.0), axis=-1, keepdims=True)
        s_k = jnp.sum(jnp.where(lane == e_k, start, 0.0), axis=-1, keepdims=True)
        out = jnp.where(lane == float(kk), s_k, out)
    o_ref[...] = (out + rank_ref[...]).astype(jnp.int32)


def _dest(idx, rank, start):
    n, E = idx.shape
    R = 256
    row = pl.BlockSpec((R, E), lambda i: (i, 0))
    return pl.pallas_call(
        _dest_kernel,
        grid=(n // R,),
        in_specs=[row, row, pl.BlockSpec((1, E), lambda i: (0, 0))],
        out_specs=row,
        out_shape=jax.ShapeDtypeStruct((n, E), jnp.int32),
        compiler_params=_cparams("parallel"),
        name="moe_dest",
    )(idx, rank, start)


def _sc_dispatch(h_pk, dest, pad_slots, n_rows):
    n, width = h_pk.shape
    workers = SC_CORES * SC_SUBCORES
    per_w = n // workers
    win = SC_DISPATCH_WINDOW
    assert per_w * workers == n and per_w % win == 0
    n_chunk = per_w // win
    dest_w = jnp.transpose(dest.reshape(workers, n_chunk, win, TOP_K), (0, 1, 3, 2))
    pad_per_w = pad_slots.size // workers
    pwin = SC_PAD_WINDOW
    n_pchunk = pad_per_w // (TOP_K * pwin)
    assert n_pchunk * TOP_K * pwin * workers == pad_slots.size
    pad_w = pad_slots.reshape(workers, n_pchunk, TOP_K, pwin)
    zero_rows = jnp.zeros((pwin, width), h_pk.dtype)
    mesh = plsc.VectorSubcoreMesh(core_axis_name="c", subcore_axis_name="s",
                                  num_cores=SC_CORES, num_subcores=SC_SUBCORES)

    @functools.partial(
        pl.kernel, mesh=mesh,
        out_type=jax.ShapeDtypeStruct((n_rows, width), h_pk.dtype),
        scratch_types=[pltpu.VMEM((TOP_K, win), jnp.int32),
                       pltpu.VMEM((win, width), h_pk.dtype),
                       pltpu.VMEM((TOP_K, pwin), jnp.int32),
                       pltpu.VMEM((pwin, width), h_pk.dtype),
                       pltpu.SemaphoreType.DMA],
        name="moe_sc_dispatch",
    )
    def dispatch(h_hbm, dest_hbm, pad_hbm, zero_hbm, xg_hbm, idx_v, rows_v, pidx_v, zeros_v, sem):
        wid = lax.axis_index("s") * SC_CORES + lax.axis_index("c")
        pltpu.sync_copy(zero_hbm, zeros_v)

        @pl.loop(0, n_chunk)
        def _(j):
            base = pl.multiple_of(wid * per_w + j * win, 8)
            pltpu.sync_copy(h_hbm.at[pl.ds(base, win)], rows_v)
            pltpu.sync_copy(dest_hbm.at[wid, j], idx_v)
            copies = [pltpu.async_copy(rows_v, xg_hbm.at[idx_v.at[kk]], sem) for kk in range(TOP_K)]
            for cp in copies:
                cp.wait()

        @pl.loop(0, n_pchunk)
        def _(j):
            pltpu.sync_copy(pad_hbm.at[wid, j], pidx_v)
            copies = [pltpu.async_copy(zeros_v, xg_hbm.at[pidx_v.at[kk]], sem) for kk in range(TOP_K)]
            for cp in copies:
                cp.wait()

    return dispatch(h_pk, dest_w, pad_w, zero_rows)


def _expert_kernel(be_ref, nu_ref, x_ref, w1_ref, w3_ref, w2_ref, o_ref, w13_s, w2_s):
    i = pl.program_id(0)
    F = w1_ref.shape[1]

    @pl.when(i < nu_ref[0])
    def _():
        prev = be_ref[jnp.maximum(i - 1, 0)]

        @pl.when((i == 0) | (be_ref[i] != prev))
        def _():
            w13_s[:, 0:F] = w1_ref[...].astype(BF16)
            w13_s[:, F:2 * F] = w3_ref[...].astype(BF16)
            w2_s[...] = w2_ref[...].astype(BF16)

        x_lo, x_hi = _unpack_pair(x_ref[...])
        half = x_ref.shape[1]
        h13 = (jnp.dot(x_lo.astype(BF16), w13_s[0:half, :], preferred_element_type=F32)
               + jnp.dot(x_hi.astype(BF16), w13_s[half:2 * half, :], preferred_element_type=F32))
        hh = _silu(h13[:, 0:F]) * h13[:, F:2 * F]
        y = jnp.dot(hh.astype(BF16), w2_s[...], preferred_element_type=F32)
        o_ref[...] = _pack_pair(y[:, 0:half], y[:, half:2 * half])


def _experts(layer, block_e, n_used, xg, w_e1, w_e3, w_e2):
    n_slots, half = xg.shape
    D = 2 * half
    F = w_e1.shape[3]
    BM = MOE_BLOCK
    nb = block_e.shape[0]
    n_slots = nb * BM

    def blk(i, be, nu):
        return (jnp.minimum(i, nu[0] - 1), 0)

    grid_spec = pltpu.PrefetchScalarGridSpec(
        num_scalar_prefetch=2,
        grid=(nb,),
        in_specs=[pl.BlockSpec((BM, half), blk),
                  pl.BlockSpec((None, None, D, F), lambda i, be, nu: (layer, be[i], 0, 0)),
                  pl.BlockSpec((None, None, D, F), lambda i, be, nu: (layer, be[i], 0, 0)),
                  pl.BlockSpec((None, None, F, D), lambda i, be, nu: (layer, be[i], 0, 0))],
        out_specs=pl.BlockSpec((BM, half), blk),
        scratch_shapes=[pltpu.VMEM((D, 2 * F), BF16), pltpu.VMEM((F, D), BF16)],
    )
    return pl.pallas_call(
        _expert_kernel,
        grid_spec=grid_spec,
        out_shape=jax.ShapeDtypeStruct((n_slots, half), jnp.int32),
        compiler_params=_cparams("arbitrary"),
        name="moe_experts",
    )(block_e, n_used, xg, w_e1, w_e3, w_e2)


def _sc_gather(table, idx):
    n_idx = idx.shape[0]
    width = table.shape[1]
    workers = SC_CORES * SC_SUBCORES
    per_w = n_idx // workers
    assert per_w * workers == n_idx and per_w % SC_WINDOW == 0
    mesh = plsc.VectorSubcoreMesh(core_axis_name="c", subcore_axis_name="s",
                                  num_cores=SC_CORES, num_subcores=SC_SUBCORES)

    @functools.partial(
        pl.kernel, mesh=mesh,
        out_type=jax.ShapeDtypeStruct((n_idx, width), table.dtype),
        scratch_types=[pltpu.VMEM((SC_WINDOW,), jnp.int32),
                       pltpu.VMEM((SC_WINDOW, width), table.dtype),
                       pltpu.SemaphoreType.DMA],
        name="moe_sc_gather",
    )
    def gather(table_hbm, idx_hbm, out_hbm, idx_v, rows_v, sem):
        wid = lax.axis_index("s") * SC_CORES + lax.axis_index("c")

        @pl.loop(0, per_w // SC_WINDOW)
        def _(j):
            base = pl.multiple_of(wid * per_w + j * SC_WINDOW, SC_WINDOW)
            pltpu.sync_copy(idx_hbm.at[pl.ds(base, SC_WINDOW)], idx_v)
            pltpu.async_copy(table_hbm.at[idx_v], rows_v, sem).wait()
            pltpu.sync_copy(rows_v, out_hbm.at[pl.ds(base, SC_WINDOW)])

    return gather(table, idx)


def _combine_kernel(alpha, h_ref, gate_ref, yt_ref, ws13_ref, ws2_ref, g_ref, b_ref, o_ref):
    F = ws2_ref.shape[0]
    half = h_ref.shape[1] // 2
    h = h_ref[...]
    h13 = _bdot(h, ws13_ref[...])
    shared = _bdot(_silu(h13[:, 0:F]) * h13[:, F:2 * F], ws2_ref[...])
    r_lo = jnp.zeros((h.shape[0], half), F32)
    r_hi = jnp.zeros((h.shape[0], half), F32)
    for kk in range(TOP_K):
        y_lo, y_hi = _unpack_pair(yt_ref[kk])
        gk = gate_ref[:, kk:kk + 1]
        r_lo = r_lo + y_lo * gk
        r_hi = r_hi + y_hi * gk
    routed = jnp.concatenate([r_lo, r_hi], axis=-1)
    o_ref[...] = _layer_norm(alpha * h + (shared + routed), g_ref[...], b_ref[...])


def _combine(alpha, h, gate, yt, ws13, ws2, ln_g, ln_b):
    n, D = h.shape
    R = COMBINE_TILE

    def full(a):
        return pl.BlockSpec(a.shape, lambda i: (0,) * a.ndim)

    return pl.pallas_call(
        functools.partial(_combine_kernel, alpha),
        grid=(n // R,),
        in_specs=[pl.BlockSpec((R, D), lambda i: (i, 0)),
                  pl.BlockSpec((R, LANES), lambda i: (i, 0)),
                  pl.BlockSpec((TOP_K, R, D // 2), lambda i: (0, i, 0)),
                  full(ws13), full(ws2), full(ln_g), full(ln_b)],
        out_specs=pl.BlockSpec((R, D), lambda i: (i, 0)),
        out_shape=jax.ShapeDtypeStruct((n, D), F32),
        compiler_params=_cparams("parallel"),
        name="moe_combine",
    )(h, gate, yt, ws13, ws2, ln_g, ln_b)


def _moe(layer, alpha, h, h_pk, w_router, b_router, w_e1, w_e3, w_e2, ws13, ws2, ln_g, ln_b):
    n, D = h.shape
    E = w_router.shape[1]
    BM = MOE_BLOCK
    idx, gate, rank, cnt = _router(h, w_router, b_router)
    counts = cnt[0].astype(jnp.int32)
    padded = (counts + BM - 1) // BM * BM
    end = jnp.cumsum(padded)
    start = (end - padded).astype(jnp.int32)
    n_blocks = (n * TOP_K + BM - 1) // BM + E
    first_slot = jnp.arange(n_blocks, dtype=jnp.int32)[:, None] * BM
    block_e = jnp.minimum(jnp.sum((end[None, :] <= first_slot).astype(jnp.int32), axis=1), E - 1)
    n_used = (end[-1] // BM).astype(jnp.int32).reshape(1)
    dest = _dest(idx, rank, start.astype(F32).reshape(1, E))
    n_slots = n_blocks * BM
    e_i = jnp.arange(E, dtype=jnp.int32)[:, None]
    j_i = jnp.arange(BM, dtype=jnp.int32)[None, :]
    pad = (start + counts)[:, None] + j_i
    pad_slots = jnp.where(pad < end.astype(jnp.int32)[:, None], pad, n_slots + e_i * BM + j_i)
    xg = _sc_dispatch(h_pk, dest[:, :TOP_K], pad_slots, n_slots + E * BM)
    yg = _experts(layer, block_e, n_used, xg, w_e1, w_e3, w_e2)
    dest_km = jnp.swapaxes(dest[:, :TOP_K], 0, 1).reshape(-1)
    yt = _sc_gather(yg, dest_km).reshape(TOP_K, n, D // 2)
    return _combine(alpha, h, gate, yt, ws13, ws2, ln_g, ln_b)


def kernel(x_prompt, x_sample, state_wkv, state_shift, state_conv, cache_mem_k, cache_mem_v, mem_prompt, w_in, w_in_vmix, mu_shift, mu_vmix, w0, w_up, a0, a_up, v0, v_up, g_up, k_k, k_a, r_k, lnx_g, lnx_b, w_o_a, conv_w, conv_b, cn_g, cn_b, w_o_b, w_mem_kv, w_o_c, w_out, ln1_g, ln1_b, ln2_g, ln2_b, w_router, b_router, w_e1, w_e3, w_e2, w_s1, w_s3, w_s2):
    B, T, D = x_prompt.shape
    SB, ST, _ = x_sample.shape
    L = w_in.shape[0]
    M = mem_prompt.shape[1]
    R = ROW_TILE
    n_p, n_s = B * T, SB * ST
    n = n_p + n_s
    assert n_s == R and SB == LANES and T % R == 0 and T % CHUNK == 0 and D % LANES == 0
    npt, tps = n_p // R, T // R
    alpha = (2.0 * L) ** 0.25
    off_glu = RWKV_COLS
    off_q = off_glu + 2 * CONV_CH
    off_gate = off_q + MEM_WIDTH
    in_cols = off_gate + 3 * D

    x = jnp.concatenate([x_prompt.reshape(n_p, D), jnp.swapaxes(x_sample, 0, 1).reshape(n_s, D)], axis=0)
    mem_rows = mem_prompt.reshape(B * M, D)
    row2 = lambda a: a.reshape(1, -1)
    v_first = None
    outs = {k: [] for k in ('wkv_p', 'shift_p', 'conv_p', 'mk', 'mv', 'wkv_s', 'shift_s', 'conv_s')}

    for l in range(L):
        first = l == 0
        w_in_l = w_in[l].astype(BF16)
        w_rwkv = w_in_l[:, :off_glu]
        groups = [w_rwkv, w_in_l[:, off_glu:off_q], w_in_l[:, off_q:off_gate], w_in_l[:, off_gate:in_cols]]
        group_dtypes = [F32, BF16, F32, BF16]
        if not first:
            wv_pad = jnp.pad(w_in_vmix[l - 1], ((0, 0), (0, LANES - LORA_V))).astype(BF16)
            groups.append(wv_pad)
            group_dtypes.append(F32)
        proj = _project(x, groups, group_dtypes)
        p_rwkv, p_glu, p_q, p_gate = proj[:4]
        x_prev_s = state_shift[l]
        prev_rwkv = _matmul(x_prev_s, w_rwkv, SB, RWKV_COLS // 2)

        wup_pad = jnp.concatenate([w_up[l], jnp.zeros((LORA_A, RW), F32)], axis=0).astype(BF16)
        aup_pad = jnp.concatenate([jnp.zeros((LORA_W, RW), F32), a_up[l]], axis=0).astype(BF16)
        lora = (row2(w0[l]), wup_pad, row2(a0[l]), aup_pad, g_up[l].astype(BF16), row2(k_k[l]), row2(k_a[l]),
                row2(r_k[l]), _seg_ones())
        if first:
            pre = _rwkv_pre(True, npt, tps, p_rwkv, prev_rwkv, row2(mu_shift[l]), lora)
        else:
            p_vd = proj[4]
            prev_vd = _matmul(x_prev_s, wv_pad, SB, LANES)
            mu_v = jnp.pad(mu_vmix[l - 1], (0, LANES - LORA_V)).reshape(1, LANES)
            vup_pad = jnp.pad(v_up[l - 1], ((0, LANES - LORA_V), (0, 0))).astype(BF16)
            pre = _rwkv_pre(False, npt, tps, p_rwkv, prev_rwkv, row2(mu_shift[l]), lora, p_vd, prev_vd, mu_v,
                            v_first, row2(v0[l - 1]), vup_pad)
        r_, lw_, k_, v_, nkk_, bb_, g_, bonus_ = pre
        if first:
            v_first = v_

        y_p, s_p = _wkv_prompt(B, T, r_, lw_, k_, v_, nkk_, bb_)

        def lanes_b(a):
            return jnp.transpose(a[n_p:].reshape(ST, SB, RW), (0, 2, 1)).reshape(ST, HEADS, HEAD_DIM, SB)

        state_sz = HEADS * HEAD_DIM * HEAD_DIM
        s0 = _transpose_layer(state_wkv.astype(F32).reshape(L, SB, state_sz), l, 2048)
        s0 = s0.reshape(HEADS, HEAD_DIM, HEAD_DIM, SB)
        y_s, s_s = _wkv_sample(s0, lanes_b(r_), lanes_b(lw_), lanes_b(k_),
                               lanes_b(v_).reshape(ST, HEADS, HEAD_DIM, 1, SB), lanes_b(nkk_), lanes_b(bb_))
        y_s = jnp.transpose(y_s.reshape(ST, RW, SB), (0, 2, 1)).reshape(n_s, RW)
        s_s = _transpose_2d(s_s.reshape(state_sz, SB), 2048).reshape(SB, HEADS, HEAD_DIM, HEAD_DIM)

        cvec = (conv_w[l], row2(conv_b[l]), row2(cn_g[l]), row2(cn_b[l]))
        zb_p, conv_p = _conv_prompt(B, T, p_glu, *cvec)
        st_t = jnp.swapaxes(state_conv[l], 0, 1)
        zb_s, glu_s = _conv_sample(st_t, p_glu[n_p:].reshape(ST, SB, 2 * CONV_CH), *cvec)
        zb_s = zb_s.reshape(n_s, CONV_CH)
        conv_s = jnp.concatenate([state_conv[l][:, ST:], jnp.swapaxes(glu_s, 0, 1)], axis=1)

        kv_p = _matmul(mem_rows, w_mem_kv[l].astype(BF16), M, 2 * MEM_WIDTH)
        kv3 = kv_p.reshape(B, M, 2 * MEM_WIDTH)
        zc_p = _attention_prompt(p_q, kv3, B, T, R)
        q_s = jnp.swapaxes(p_q[n_p:].reshape(ST, SB, MEM_WIDTH), 0, 1)
        q_s = jnp.pad(q_s, ((0, 0), (0, 8 - ST), (0, 0)))
        zc_s = _attention_cached(q_s, cache_mem_k.reshape(L, SB, M * MEM_HEADS, MEM_HEAD_DIM),
                                 cache_mem_v.reshape(L, SB, M * MEM_HEADS, MEM_HEAD_DIM), l, 8)[:, :ST]
        zc_s = jnp.swapaxes(zc_s, 0, 1).reshape(n_s, MEM_WIDTH)

        h, h_pk = _merge(n_p, alpha, y_p, y_s, bonus_, g_, row2(lnx_g[l]), row2(lnx_b[l]),
                         zb_p, zb_s, zc_p, zc_s, p_gate, x,
                         w_o_a[l].astype(BF16), w_o_b[l].astype(BF16), w_o_c[l].astype(BF16),
                         w_out[l].astype(BF16), row2(ln1_g[l]), row2(ln1_b[l]))

        ws13 = jnp.concatenate([w_s1[l], w_s3[l]], axis=1).astype(BF16)
        x_new = _moe(l, alpha, h, h_pk, w_router[l].astype(BF16), row2(b_router[l]), w_e1, w_e3, w_e2,
                     ws13, w_s2[l].astype(BF16), row2(ln2_g[l]), row2(ln2_b[l]))

        outs['wkv_p'].append(s_p.astype(state_wkv.dtype))
        outs['shift_p'].append(x[T - 1:n_p:T].astype(state_shift.dtype))
        outs['conv_p'].append(conv_p.astype(state_conv.dtype))
        outs['mk'].append(kv3[..., :MEM_WIDTH].reshape(B, M, MEM_HEADS, MEM_HEAD_DIM).astype(cache_mem_k.dtype))
        outs['mv'].append(kv3[..., MEM_WIDTH:].reshape(B, M, MEM_HEADS, MEM_HEAD_DIM).astype(cache_mem_v.dtype))
        outs['wkv_s'].append(s_s.astype(state_wkv.dtype))
        outs['shift_s'].append(x[n_p + (ST - 1) * SB:].astype(state_shift.dtype))
        outs['conv_s'].append(conv_s.astype(state_conv.dtype))
        x = x_new

    y_p = x[:n_p].reshape(B, T, D)
    y_s = jnp.swapaxes(x[n_p:].reshape(ST, SB, D), 0, 1)
    return (y_p, y_s, jnp.stack(outs['wkv_p']), jnp.stack(outs['shift_p']), jnp.stack(outs['conv_p']),
            jnp.stack(outs['mk']), jnp.stack(outs['mv']), jnp.stack(outs['wkv_s']), jnp.stack(outs['shift_s']),
            jnp.stack(outs['conv_s']))
```
